```python
import math
import jax, jax.numpy as jnp
from jax import lax
import numpy as np

D_MODEL = 1024
BATCH = 8
SEQ = 4096
DEPTH = 1
DEC_BATCH = 32
DEC_SEQ = 16
PAST_LEN = 4096

CHUNK = 64
N_HEADS = 8
HEAD_DIM = 64
ATTN_W = N_HEADS * HEAD_DIM
N_IDX_HEADS = 8
IDX_DIM = 64
TOPK_MAX = 256
CONV_CH = 512
CONV_K = 31
FFN_HIDDEN = -(-8 * D_MODEL // (3 * 256)) * 256
NUM_BUCKETS = 32
MAX_DISTANCE = 128
Q_BLOCK = 128
EPS = 1e-6
NEG = -1e30
IN_SIZES = (ATTN_W, ATTN_W, ATTN_W, N_IDX_HEADS * IDX_DIM, IDX_DIM, N_IDX_HEADS, 2 * CONV_CH, D_MODEL, D_MODEL)
IN_COLS = 3 * ATTN_W + N_IDX_HEADS * IDX_DIM + IDX_DIM + N_IDX_HEADS + 2 * CONV_CH + 2 * D_MODEL

kernel_name = "hybrid_dsa_conformer_stream_step"


def _rmsnorm(x, g):
    xf = x.astype(jnp.float32)
    y = xf * lax.rsqrt(jnp.mean(xf * xf, axis=-1, keepdims=True) + EPS)
    return (y * g.astype(jnp.float32)).astype(x.dtype)


def _layernorm(x, g, b):
    xf = x.astype(jnp.float32)
    mu = jnp.mean(xf, axis=-1, keepdims=True)
    var = jnp.mean(jnp.square(xf - mu), axis=-1, keepdims=True)
    y = (xf - mu) * lax.rsqrt(var + EPS) * g.astype(jnp.float32) + b.astype(jnp.float32)
    return y.astype(x.dtype)


def _rel_bucket(rel):
    nb = NUM_BUCKETS // 2
    max_exact = nb // 2
    ret = jnp.where(rel > 0, nb, 0)
    n = jnp.abs(rel)
    nf = jnp.maximum(n, 1).astype(jnp.float32)
    large = max_exact + (jnp.log(nf / max_exact) / math.log(MAX_DISTANCE / max_exact) * (nb - max_exact)).astype(jnp.int32)
    large = jnp.minimum(large, nb - 1)
    return ret + jnp.where(n < max_exact, n, large)


def _gather_rows(a, idx):
    return jax.vmap(lambda ab, ib: ab[ib])(a, idx)


def _sparse_attention(q, iq, iw, q_pos, keys, vals, ikeys, k_pos, rel_bias, top_k):
    idx_logits = jnp.einsum('bqhd,bsd->bqhs', iq, ikeys).astype(jnp.float32) * (IDX_DIM ** -0.5)
    score = jnp.einsum('bqhs,bqh->bqs', jax.nn.relu(idx_logits), iw.astype(jnp.float32))
    admissible = (k_pos[None, :] // CHUNK) <= (q_pos[:, None] // CHUNK)
    score = jnp.where(admissible[None], score, NEG)
    _, sel = lax.top_k(score, top_k)
    k_sel = _gather_rows(keys, sel)
    v_sel = _gather_rows(vals, sel)
    sel_pos = k_pos[sel]
    valid = (sel_pos // CHUNK) <= (q_pos[None, :, None] // CHUNK)
    bias = rel_bias[_rel_bucket(sel_pos - q_pos[None, :, None])].astype(jnp.float32)
    logits = jnp.einsum('bqhd,bqkhd->bqhk', q, k_sel).astype(jnp.float32) * (HEAD_DIM ** -0.5)
    logits = logits + jnp.moveaxis(bias, -1, 2)
    logits = jnp.where(valid[:, :, None, :], logits, NEG)
    p = jax.nn.softmax(logits, axis=-1).astype(vals.dtype)
    return jnp.einsum('bqhk,bqkhd->bqhd', p, v_sel)


def _layer(x, pos0, past_k, past_v, past_ik, past_conv, rel_bias,
           norm_mix_g, w_in, q_norm_g, k_norm_g, idx_k_norm_g, conv_dw_w, conv_dw_b, conv_ln_g, conv_ln_b,
           w_conv_out, w_attn_out, w_out, norm_ffn_g, w_ffn_gate, w_ffn_up, w_ffn_down):
    B, T, _ = x.shape
    h = _rmsnorm(x, norm_mix_g)
    proj = h @ w_in
    splits = []
    off = 0
    for s in IN_SIZES[:-1]:
        off += s
        splits.append(off)
    q, k, v, iq, ik, iw, glu, ga, gc = jnp.split(proj, splits, axis=-1)
    q = _rmsnorm(q.reshape(B, T, N_HEADS, HEAD_DIM), q_norm_g)
    k = _rmsnorm(k.reshape(B, T, N_HEADS, HEAD_DIM), k_norm_g)
    v = v.reshape(B, T, N_HEADS, HEAD_DIM)
    iq = iq.reshape(B, T, N_IDX_HEADS, IDX_DIM)
    ik = _rmsnorm(ik, idx_k_norm_g)
    iw = iw * (N_IDX_HEADS ** -0.5)

    keys = jnp.concatenate([past_k, k], axis=1)
    vals = jnp.concatenate([past_v, v], axis=1)
    ikeys = jnp.concatenate([past_ik, ik], axis=1)
    L = keys.shape[1]
    top_k = min(TOPK_MAX, L // 4)
    k_pos = jnp.arange(L, dtype=jnp.int32)
    q_pos = pos0 + jnp.arange(T, dtype=jnp.int32)
    qb = min(Q_BLOCK, T)
    nb = T // qb

    def blocks(a):
        return jnp.moveaxis(a.reshape((B, nb, qb) + a.shape[2:]), 1, 0)

    attn = lax.map(
        lambda blk: _sparse_attention(blk[0], blk[1], blk[2], blk[3], keys, vals, ikeys, k_pos, rel_bias, top_k),
        (blocks(q), blocks(iq), blocks(iw), q_pos.reshape(nb, qb)))
    attn = jnp.moveaxis(attn, 0, 1).reshape(B, T, ATTN_W)
    attn_out = attn @ w_attn_out

    ga_in, gb_in = jnp.split(glu, 2, axis=-1)
    u = ga_in * jax.nn.sigmoid(gb_in)
    u_full = jnp.concatenate([past_conv, u], axis=1)
    c = lax.conv_general_dilated(u_full, conv_dw_w.astype(u_full.dtype), (1,), 'VALID',
                                 dimension_numbers=('NWC', 'WIO', 'NWC'), feature_group_count=CONV_CH)
    c = jax.nn.silu(_layernorm(c + conv_dw_b, conv_ln_g, conv_ln_b))
    conv_out = c @ w_conv_out

    merged = jax.nn.sigmoid(ga) * attn_out + jax.nn.sigmoid(gc) * conv_out
    x = x + merged @ w_out
    h2 = _rmsnorm(x, norm_ffn_g)
    x = x + (jax.nn.silu(h2 @ w_ffn_gate) * (h2 @ w_ffn_up)) @ w_ffn_down
    return x, k, v, ik, u_full[:, -(CONV_K - 1):]


def setup_inputs(seed: int = 0) -> dict:
    key = jax.random.key(seed)
    ks = jax.random.split(key, 26)
    f32 = jnp.float32

    def nrm(k, shape, scale):
        return jax.random.normal(k, shape, f32) * scale

    def gain(k, shape):
        return 1.0 + 0.05 * jax.random.normal(k, shape, f32)

    return {
        "x_prompt": nrm(ks[0], (BATCH, SEQ, D_MODEL), 1.0),
        "x_sample": nrm(ks[1], (DEC_BATCH, DEC_SEQ, D_MODEL), 1.0),
        "cache_k": nrm(ks[2], (DEPTH, DEC_BATCH, PAST_LEN, N_HEADS, HEAD_DIM), 1.0),
        "cache_v": nrm(ks[3], (DEPTH, DEC_BATCH, PAST_LEN, N_HEADS, HEAD_DIM), 1.0),
        "cache_idx_k": nrm(ks[4], (DEPTH, DEC_BATCH, PAST_LEN, IDX_DIM), 1.0),
        "state_conv": nrm(ks[5], (DEPTH, DEC_BATCH, CONV_K - 1, CONV_CH), 0.5),
        "rel_bias": nrm(ks[6], (NUM_BUCKETS, N_HEADS), 0.5),
        "norm_mix_g": gain(ks[7], (DEPTH, D_MODEL)),
        "w_in": nrm(ks[8], (DEPTH, D_MODEL, IN_COLS), D_MODEL ** -0.5),
        "q_norm_g": gain(ks[9], (DEPTH, HEAD_DIM)),
        "k_norm_g": gain(ks[10], (DEPTH, HEAD_DIM)),
        "idx_k_norm_g": gain(ks[11], (DEPTH, IDX_DIM)),
        "conv_dw_w": nrm(ks[12], (DEPTH, CONV_K, 1, CONV_CH), CONV_K ** -0.5),
        "conv_dw_b": nrm(ks[13], (DEPTH, CONV_CH), 0.02),
        "conv_ln_g": gain(ks[14], (DEPTH, CONV_CH)),
        "conv_ln_b": nrm(ks[15], (DEPTH, CONV_CH), 0.02),
        "w_conv_out": nrm(ks[16], (DEPTH, CONV_CH, D_MODEL), CONV_CH ** -0.5),
        "w_attn_out": nrm(ks[17], (DEPTH, ATTN_W, D_MODEL), ATTN_W ** -0.5),
        "w_out": nrm(ks[18], (DEPTH, D_MODEL, D_MODEL), D_MODEL ** -0.5),
        "norm_ffn_g": gain(ks[19], (DEPTH, D_MODEL)),
        "w_ffn_gate": nrm(ks[20], (DEPTH, D_MODEL, FFN_HIDDEN), D_MODEL ** -0.5),
        "w_ffn_up": nrm(ks[21], (DEPTH, D_MODEL, FFN_HIDDEN), D_MODEL ** -0.5),
        "w_ffn_down": nrm(ks[22], (DEPTH, FFN_HIDDEN, D_MODEL), FFN_HIDDEN ** -0.5),
    }


def reference(x_prompt, x_sample, cache_k, cache_v, cache_idx_k, state_conv, rel_bias,
              norm_mix_g, w_in, q_norm_g, k_norm_g, idx_k_norm_g, conv_dw_w, conv_dw_b, conv_ln_g, conv_ln_b,
              w_conv_out, w_attn_out, w_out, norm_ffn_g, w_ffn_gate, w_ffn_up, w_ffn_down):
    Bp = x_prompt.shape[0]
    past_len = cache_k.shape[2]
    yp = x_prompt
    ys = x_sample
    kp_l, vp_l, ikp_l, cp_l = [], [], [], []
    ks_l, vs_l, iks_l, cs_l = [], [], [], []
    for l in range(DEPTH):
        lw = (norm_mix_g[l], w_in[l], q_norm_g[l], k_norm_g[l], idx_k_norm_g[l], conv_dw_w[l], conv_dw_b[l],
              conv_ln_g[l], conv_ln_b[l], w_conv_out[l], w_attn_out[l], w_out[l], norm_ffn_g[l],
              w_ffn_gate[l], w_ffn_up[l], w_ffn_down[l])
        empty_kv = jnp.zeros((Bp, 0, N_HEADS, HEAD_DIM), yp.dtype)
        empty_ik = jnp.zeros((Bp, 0, IDX_DIM), yp.dtype)
        zero_conv = jnp.zeros((Bp, CONV_K - 1, CONV_CH), yp.dtype)
        yp, kp, vp, ikp, cp = _layer(yp, 0, empty_kv, empty_kv, empty_ik, zero_conv, rel_bias, *lw)
        kp_l.append(kp); vp_l.append(vp); ikp_l.append(ikp); cp_l.append(cp)
        ys, kn, vn, ikn, cn = _layer(ys, past_len, cache_k[l], cache_v[l], cache_idx_k[l], state_conv[l],
                                     rel_bias, *lw)
        ks_l.append(kn); vs_l.append(vn); iks_l.append(ikn); cs_l.append(cn)
    new_k_prompt = jnp.stack(kp_l)
    new_v_prompt = jnp.stack(vp_l)
    new_idx_k_prompt = jnp.stack(ikp_l)
    new_conv_prompt = jnp.stack(cp_l)
    new_k_sample = jnp.stack(ks_l)
    new_v_sample = jnp.stack(vs_l)
    new_idx_k_sample = jnp.stack(iks_l)
    new_conv_sample = jnp.stack(cs_l)
    return (yp, ys, new_k_prompt, new_v_prompt, new_idx_k_prompt, new_conv_prompt,
            new_k_sample, new_v_sample, new_idx_k_sample, new_conv_sample)
```

```python
import functools
import math

import numpy as np
import jax
import jax.numpy as jnp
from jax import lax
from jax.experimental import pallas as pl
from jax.experimental.pallas import tpu as pltpu

CHUNK = 64
CHUNK_SHIFT = 6
N_HEADS = 8
HEAD_DIM = 64
ATTN_W = N_HEADS * HEAD_DIM
N_IDX_HEADS = 8
IDX_DIM = 64
TOPK_MAX = 256
CONV_K = 31
NUM_BUCKETS = 32
MAX_DISTANCE = 128
EPS = 1e-6
NEG = -1e30
INT_MIN = -(2 ** 31)

LANES = 128
HALO = 32
KEY_TILE = 256
ROW_TILE = 256
VMEM_LIMIT = 56 * 1024 * 1024

F32 = jnp.float32
BF16 = jnp.bfloat16


def _const_spec(shape):
    return pl.BlockSpec(shape, lambda *_: (0,) * len(shape), pipeline_mode=pl.Buffered(1))


def _split_dot(x, m):
    hi = x.astype(BF16)
    lo = (x - hi.astype(F32)).astype(BF16)
    return jnp.dot(hi, m, preferred_element_type=F32) + jnp.dot(lo, m, preferred_element_type=F32)


def _in_proj_kernel(x_ref, g_ref, wq_ref, wk_ref, wv_ref, wiq_ref, wsm_ref, wglu_ref, wga_ref, wgc_ref,
                    qg_ref, kg_ref, ikg_ref, seg_ref, segt_ref,
                    q_ref, k_ref, kb_ref, v_ref, vb_ref, iq_ref, sm_ref, smb_ref, u_ref, sga_ref, sgc_ref):
    x = x_ref[...]
    ms = jnp.mean(x * x, axis=-1, keepdims=True)
    h = (x * lax.rsqrt(ms + EPS) * g_ref[...]).astype(BF16)

    def head_rms(y, gain):
        ss = _split_dot(y * y, seg_ref[...])
        r = lax.rsqrt(ss * (1.0 / HEAD_DIM) + EPS)
        return y * _split_dot(r, segt_ref[...]) * gain

    q = head_rms(jnp.dot(h, wq_ref[...], preferred_element_type=F32), qg_ref[...])
    q_ref[...] = (q * (HEAD_DIM ** -0.5)).astype(BF16)
    k = head_rms(jnp.dot(h, wk_ref[...], preferred_element_type=F32), kg_ref[...])
    k_ref[...] = k
    kb_ref[...] = k.astype(BF16)
    v = jnp.dot(h, wv_ref[...], preferred_element_type=F32)
    v_ref[...] = v
    vb_ref[...] = v.astype(BF16)
    iq = jnp.dot(h, wiq_ref[...], preferred_element_type=F32)
    iq_ref[...] = (iq * (IDX_DIM ** -0.5)).astype(BF16)

    sm = jnp.dot(h, wsm_ref[...], preferred_element_type=F32)
    lane = lax.broadcasted_iota(jnp.int32, sm.shape, 1)
    is_ik = lane < IDX_DIM
    ss = jnp.sum(jnp.where(is_ik, sm * sm, 0.0), axis=-1, keepdims=True)
    ikn = sm * lax.rsqrt(ss * (1.0 / IDX_DIM) + EPS) * ikg_ref[...]
    iw = sm * (N_IDX_HEADS ** -0.5)
    smo = jnp.where(is_ik, ikn, jnp.where(lane < IDX_DIM + N_IDX_HEADS, iw, 0.0))
    sm_ref[...] = smo
    smb_ref[...] = smo.astype(BF16)

    glu = jnp.dot(h, wglu_ref[...], preferred_element_type=F32)
    c = glu.shape[1] // 2
    u_ref[...] = glu[:, :c] * jax.nn.sigmoid(glu[:, c:])
    sga_ref[...] = jax.nn.sigmoid(jnp.dot(h, wga_ref[...], preferred_element_type=F32)).astype(BF16)
    sgc_ref[...] = jax.nn.sigmoid(jnp.dot(h, wgc_ref[...], preferred_element_type=F32)).astype(BF16)


def _in_proj(x, w):
    m, d = x.shape
    tm = min(ROW_TILE, m)
    assert m % tm == 0
    conv_ch = w["wglu"].shape[1] // 2
    row = lambda n: pl.BlockSpec((tm, n), lambda i: (i, 0))
    consts = [w[n] for n in ("g_mix", "wq", "wk", "wv", "wiq", "wsm", "wglu", "wga", "wgc",
                             "qg", "kg", "ikg", "seg", "segt")]
    out_shape = [
        jax.ShapeDtypeStruct((m, ATTN_W), BF16),
        jax.ShapeDtypeStruct((m, ATTN_W), F32),
        jax.ShapeDtypeStruct((m, ATTN_W), BF16),
        jax.ShapeDtypeStruct((m, ATTN_W), F32),
        jax.ShapeDtypeStruct((m, ATTN_W), BF16),
        jax.ShapeDtypeStruct((m, ATTN_W), BF16),
        jax.ShapeDtypeStruct((m, LANES), F32),
        jax.ShapeDtypeStruct((m, LANES), BF16),
        jax.ShapeDtypeStruct((m, conv_ch), F32),
        jax.ShapeDtypeStruct((m, d), BF16),
        jax.ShapeDtypeStruct((m, d), BF16),
    ]
    return pl.pallas_call(
        _in_proj_kernel,
        grid=(m // tm,),
        in_specs=[row(d)] + [_const_spec(c.shape) for c in consts],
        out_specs=[row(s.shape[1]) for s in out_shape],
        out_shape=out_shape,
        compiler_params=pltpu.CompilerParams(dimension_semantics=("arbitrary",), vmem_limit_bytes=VMEM_LIMIT),
        name="in_proj",
    )(x, *consts)


def _sparse_attn_kernel(q_ref, iq_ref, sm_ref, kt_ref, v_ref, ikt_ref, bias_ref, o_ref,
                        key_scr, m_scr, l_scr, acc_scr, *, tq, tk, nkt, pos0, n_keys, top_k):
    qi = pl.program_id(1)
    q0 = pos0 + qi * tq
    n_kt = jnp.minimum(nkt, ((q0 + tq - 1) // CHUNK * CHUNK + CHUNK + tk - 1) // tk)
    last = n_kt - 1

    iq = iq_ref[...]
    iq_h = [iq[:, h * IDX_DIM:(h + 1) * IDX_DIM] for h in range(N_IDX_HEADS)]
    iw = sm_ref[:, IDX_DIM:IDX_DIM + N_IDX_HEADS]
    iw_h = [iw[:, h:h + 1] for h in range(N_IDX_HEADS)]

    def score_tile(kt, masked):
        ik = ikt_ref[kt]
        s = None
        for h in range(N_IDX_HEADS):
            t = jnp.maximum(jnp.dot(iq_h[h], ik, preferred_element_type=F32), 0.0) * iw_h[h]
            s = t if s is None else s + t
        s = s + 0.0
        bits = pltpu.bitcast(s, jnp.int32)
        key = bits ^ ((bits >> 31) & 0x7FFFFFFF)
        if masked:
            kp = kt * tk + lax.broadcasted_iota(jnp.int32, (tq, tk), 1)
            qp = q0 + lax.broadcasted_iota(jnp.int32, (tq, tk), 0)
            adm = ((kp >> CHUNK_SHIFT) <= (qp >> CHUNK_SHIFT)) & (kp < n_keys)
            key = jnp.where(adm, key, INT_MIN)
        key_scr[kt] = key

    def score_body(kt, carry):
        score_tile(kt, False)
        return carry

    lax.fori_loop(0, last, score_body, 0)
    score_tile(last, True)

    def count(pred):
        def body(kt, acc):
            c = jnp.where(pred(key_scr[kt], kt), 1.0, 0.0)
            part = c[:, 0:LANES]
            for j in range(1, tk // LANES):
                part = part + c[:, j * LANES:(j + 1) * LANES]
            return acc + part
        acc = lax.fori_loop(0, n_kt, body, jnp.zeros((tq, LANES), F32))
        return jnp.sum(acc, axis=-1, keepdims=True)

    def thr_body(i, t):
        cand = t + (jnp.int32(1) << (31 - i))
        c = count(lambda key, kt: key >= cand)
        return jnp.where(c >= top_k, cand, t)

    thr = lax.fori_loop(0, 32, thr_body, jnp.full((tq, 1), INT_MIN, jnp.int32))
    thr = jnp.maximum(thr, INT_MIN + 1)

    n_ge = count(lambda key, kt: key >= thr)
    has_tie = jnp.max(n_ge) > top_k

    @pl.when(has_tie)
    def _():
        need_m1 = (top_k - 1) - count(lambda key, kt: key > thr)
        col = lax.broadcasted_iota(jnp.int32, (tq, tk), 1)

        def idx_body(i, cut):
            cand = cut + (jnp.int32(1) << (15 - i))
            c = count(lambda key, kt: (key == thr) & (kt * tk + col < cand))
            return jnp.where(c <= need_m1, cand, cut)

        cut = lax.fori_loop(0, 16, idx_body, jnp.zeros((tq, 1), jnp.int32))
        tie_row = n_ge > top_k

        def drop_body(kt, carry):
            key = key_scr[kt]
            drop = tie_row & (key == thr) & (kt * tk + col > cut)
            key_scr[kt] = jnp.where(drop, INT_MIN, key)
            return carry

        lax.fori_loop(0, n_kt, drop_body, 0)

    q = q_ref[...]
    q_h = [q[:, h * HEAD_DIM:(h + 1) * HEAD_DIM] for h in range(N_HEADS)]
    m_scr[...] = jnp.full(m_scr.shape, NEG, F32)
    l_scr[...] = jnp.zeros(l_scr.shape, F32)
    acc_scr[...] = jnp.zeros(acc_scr.shape, F32)

    def attn_tile(kt, bias_idx):
        madd = jnp.where(key_scr[kt] >= thr, 0.0, NEG)
        for h in range(N_HEADS):
            s = jnp.dot(q_h[h], kt_ref[kt, h * HEAD_DIM:(h + 1) * HEAD_DIM, :], preferred_element_type=F32)
            if bias_idx is not None:
                s = s + bias_ref[bias_idx, h]
            s = s + madd
            m_prev = m_scr[h]
            m_new = jnp.maximum(m_prev, jnp.max(s, axis=-1, keepdims=True))
            alpha = jnp.exp(m_prev - m_new)
            p = jnp.exp(s - m_new)
            l_scr[h] = alpha * l_scr[h] + jnp.sum(p, axis=-1, keepdims=True)
            pair = h // 2
            pv = jnp.dot(p.astype(BF16), v_ref[kt, :, pair * LANES:(pair + 1) * LANES], preferred_element_type=F32)
            acc_scr[h] = alpha * acc_scr[h] + pv
            m_scr[h] = m_new

    def far_body(kt, carry):
        attn_tile(kt, None)
        return carry

    lax.fori_loop(0, n_kt - 2, far_body, 0)

    @pl.when(n_kt >= 2)
    def _():
        attn_tile(n_kt - 2, 0)

    attn_tile(last, 1)

    for h in range(N_HEADS):
        half = (h % 2) * HEAD_DIM
        o = acc_scr[h][:, half:half + HEAD_DIM] / l_scr[h]
        o_ref[:, h * HEAD_DIM:(h + 1) * HEAD_DIM] = o.astype(o_ref.dtype)


def _rel_bucket(rel):
    nb = NUM_BUCKETS // 2
    max_exact = nb // 2
    ret = jnp.where(rel > 0, nb, 0)
    n = jnp.abs(rel)
    nf = jnp.maximum(n, 1).astype(jnp.float32)
    large = max_exact + (jnp.log(nf / max_exact) / math.log(MAX_DISTANCE / max_exact) * (nb - max_exact)).astype(jnp.int32)
    large = jnp.minimum(large, nb - 1)
    return ret + jnp.where(n < max_exact, n, large)


def _bias_tiles(rel_bias, tq, tk):
    i = jnp.arange(tq, dtype=jnp.int32)[:, None]
    j = jnp.arange(tk, dtype=jnp.int32)[None, :]
    far = rel_bias[_rel_bucket(jnp.int32(-MAX_DISTANCE))]
    tiles = []
    for d in (-tk, 0):
        b = rel_bias[_rel_bucket(j - i + d)].astype(F32) - far
        tiles.append(jnp.moveaxis(b, -1, 0))
    return jnp.stack(tiles)


def _check_tiling(t, tq, tk, nkt, pos0, n_keys):
    for qi in range(t // tq):
        q0 = pos0 + qi * tq
        n_kt = min(nkt, ((q0 + tq - 1) // CHUNK * CHUNK + CHUNK + tk - 1) // tk)
        adm_end = min(q0 // CHUNK * CHUNK + CHUNK, n_keys)
        assert (n_kt - 1) * tk <= adm_end, "only the last visited tile may hold inadmissible keys"
        assert n_kt * tk >= min((q0 + tq - 1) // CHUNK * CHUNK + CHUNK, n_keys), "visited tiles cover every admissible key"
        assert (n_kt - 1) * tk == q0, "last tile starts with the query tile"
        assert tk >= MAX_DISTANCE, "tiles before the last two are at least MAX_DISTANCE behind"


def _sparse_attn(q, iq, sm, k_b, v_b, ik_b, rel_bias, pos0):
    b, t, _ = q.shape
    n_keys = k_b.shape[1]
    top_k = min(TOPK_MAX, n_keys // 4)
    tk = KEY_TILE
    tq = min(tk, t)
    nkt = pl.cdiv(n_keys, tk)
    _check_tiling(t, tq, tk, nkt, pos0, n_keys)
    pad = nkt * tk - n_keys
    if pad:
        k_b, v_b, ik_b = (jnp.pad(a, ((0, 0), (0, pad), (0, 0))) for a in (k_b, v_b, ik_b))
    kt_t = k_b.reshape(b, nkt, tk, ATTN_W).swapaxes(2, 3)
    v_t = v_b.reshape(b, nkt, tk, ATTN_W)
    ikt_t = ik_b.reshape(b, nkt, tk, IDX_DIM).swapaxes(2, 3)
    bias = _bias_tiles(rel_bias, tq, tk)

    qspec = lambda n: pl.BlockSpec((None, tq, n), lambda bi, qi: (bi, qi, 0))
    kspec = lambda r, c: pl.BlockSpec((None, nkt, r, c), lambda bi, qi: (bi, 0, 0, 0))
    kern = functools.partial(_sparse_attn_kernel, tq=tq, tk=tk, nkt=nkt, pos0=pos0, n_keys=n_keys, top_k=top_k)
    return pl.pallas_call(
        kern,
        grid=(b, t // tq),
        in_specs=[qspec(ATTN_W), qspec(ATTN_W), qspec(LANES),
                  kspec(ATTN_W, tk), kspec(tk, ATTN_W), kspec(IDX_DIM, tk), _const_spec(bias.shape)],
        out_specs=qspec(ATTN_W),
        out_shape=jax.ShapeDtypeStruct((b, t, ATTN_W), BF16),
        scratch_shapes=[pltpu.VMEM((nkt, tq, tk), jnp.int32),
                        pltpu.VMEM((N_HEADS, tq, 1), F32),
                        pltpu.VMEM((N_HEADS, tq, 1), F32),
                        pltpu.VMEM((N_HEADS, tq, LANES), F32)],
        compiler_params=pltpu.CompilerParams(dimension_semantics=("arbitrary", "arbitrary"),
                                             vmem_limit_bytes=VMEM_LIMIT),
        name="sparse_attn",
    )(q, iq, sm, kt_t, v_t, ikt_t, bias)


def _conv_kernel(*refs, tm, has_prev):
    if has_prev:
        u_ref, prev_ref, past_ref, w_ref, b_ref, lg_ref, lb_ref, o_ref, ext_scr = refs
        halo = jnp.where(pl.program_id(1) == 0, past_ref[...], prev_ref[...])
    else:
        u_ref, past_ref, w_ref, b_ref, lg_ref, lb_ref, o_ref, ext_scr = refs
        halo = past_ref[...]
    ext_scr[0:HALO, :] = halo
    ext_scr[HALO:HALO + tm, :] = u_ref[...]
    rb = min(tm, 32)
    first = HALO - (CONV_K - 1)
    for r0 in range(0, tm, rb):
        acc = None
        for j in range(CONV_K):
            t = ext_scr[r0 + first + j:r0 + first + j + rb, :] * w_ref[j:j + 1, :]
            acc = t if acc is None else acc + t
        c = acc + b_ref[...]
        mu = jnp.mean(c, axis=-1, keepdims=True)
        cc = c - mu
        var = jnp.mean(cc * cc, axis=-1, keepdims=True)
        y = cc * lax.rsqrt(var + EPS) * lg_ref[...] + lb_ref[...]
        o_ref[r0:r0 + rb, :] = (y * jax.nn.sigmoid(y)).astype(o_ref.dtype)


def _conv_mod(u, past, w):
    b, t, c = u.shape
    tm = min(ROW_TILE, t)
    assert t % tm == 0 and tm % 8 == 0
    has_prev = t > tm
    tile = pl.BlockSpec((None, tm, c), lambda bi, i: (bi, i, 0))
    halo = pl.BlockSpec((None, HALO, c), lambda bi, i: (bi, 0, 0))
    in_specs = [tile]
    args = [u]
    if has_prev:
        r = tm // HALO
        in_specs.append(pl.BlockSpec((None, HALO, c), lambda bi, i: (bi, jnp.maximum(i * r - 1, 0), 0)))
        args.append(u)
    consts = [w["conv_w"], w["conv_b"], w["ln_g"], w["ln_b"]]
    in_specs += [halo] + [_const_spec(x.shape) for x in consts]
    args += [past] + consts
    return pl.pallas_call(
        functools.partial(_conv_kernel, tm=tm, has_prev=has_prev),
        grid=(b, t // tm),
        in_specs=in_specs,
        out_specs=tile,
        out_shape=jax.ShapeDtypeStruct((b, t, c), BF16),
        scratch_shapes=[pltpu.VMEM((HALO + tm, c), F32)],
        compiler_params=pltpu.CompilerParams(dimension_semantics=("arbitrary", "arbitrary"),
                                             vmem_limit_bytes=VMEM_LIMIT),
        name="conv_mod",
    )(*args)


def _mix_ffn_kernel(x_ref, attn_ref, cact_ref, sga_ref, sgc_ref, wao_ref, wco_ref, wo_ref, g_ref,
                    wg_ref, wu_ref, wd_ref, y_ref, *, n_chunks):
    attn_out = jnp.dot(attn_ref[...], wao_ref[...], preferred_element_type=F32)
    conv_out = jnp.dot(cact_ref[...], wco_ref[...], preferred_element_type=F32)
    merged = sga_ref[...].astype(F32) * attn_out + sgc_ref[...].astype(F32) * conv_out
    x1 = x_ref[...] + jnp.dot(merged.astype(BF16), wo_ref[...], preferred_element_type=F32)
    ms = jnp.mean(x1 * x1, axis=-1, keepdims=True)
    h2 = (x1 * lax.rsqrt(ms + EPS) * g_ref[...]).astype(BF16)
    hc = wg_ref.shape[1] // n_chunks
    y = x1
    for ci in range(n_chunks):
        sl = slice(ci * hc, (ci + 1) * hc)
        gate = jnp.dot(h2, wg_ref[:, sl], preferred_element_type=F32)
        up = jnp.dot(h2, wu_ref[:, sl], preferred_element_type=F32)
        act = (gate * jax.nn.sigmoid(gate) * up).astype(BF16)
        y = y + jnp.dot(act, wd_ref[sl, :], preferred_element_type=F32)
    y_ref[...] = y


def _mix_ffn(x, attn, cact, sga, sgc, w):
    m, d = x.shape
    tm = min(ROW_TILE, m)
    assert m % tm == 0
    hidden = w["wg"].shape[1]
    n_chunks = 2 if hidden % (2 * LANES) == 0 else 1
    row = lambda n: pl.BlockSpec((tm, n), lambda i: (i, 0))
    consts = [w[n] for n in ("wao", "wco", "wo", "g_ffn", "wg", "wu", "wd")]
    return pl.pallas_call(
        functools.partial(_mix_ffn_kernel, n_chunks=n_chunks),
        grid=(m // tm,),
        in_specs=[row(d), row(attn.shape[1]), row(cact.shape[1]), row(d), row(d)]
                 + [_const_spec(c.shape) for c in consts],
        out_specs=row(d),
        out_shape=jax.ShapeDtypeStruct((m, d), F32),
        compiler_params=pltpu.CompilerParams(dimension_semantics=("arbitrary",), vmem_limit_bytes=VMEM_LIMIT),
        name="mix_ffn",
    )(x, attn, cact, sga, sgc, *consts)


def _prep_weights(norm_mix_g, w_in, q_norm_g, k_norm_g, idx_k_norm_g, conv_dw_w, conv_dw_b, conv_ln_g, conv_ln_b,
                  w_conv_out, w_attn_out, w_out, norm_ffn_g, w_ffn_gate, w_ffn_up, w_ffn_down):
    d = w_in.shape[0]
    conv_ch = conv_dw_w.shape[-1]
    sizes = (ATTN_W, ATTN_W, ATTN_W, N_IDX_HEADS * IDX_DIM, IDX_DIM, N_IDX_HEADS, 2 * conv_ch, d, d)
    offs = np.concatenate([[0], np.cumsum(sizes)])
    assert offs[-1] == w_in.shape[1]
    col = lambda i: w_in[:, offs[i]:offs[i + 1]]
    wsm = jnp.concatenate([col(4), col(5), jnp.zeros((d, LANES - IDX_DIM - N_IDX_HEADS), w_in.dtype)], axis=1)
    head = np.arange(ATTN_W) // HEAD_DIM
    seg = (head[:, None] == np.arange(LANES)[None, :]).astype(np.float32)
    return dict(
        g_mix=norm_mix_g.reshape(1, d).astype(F32),
        wq=col(0).astype(BF16), wk=col(1).astype(BF16), wv=col(2).astype(BF16), wiq=col(3).astype(BF16),
        wsm=wsm.astype(BF16), wglu=col(6).astype(BF16), wga=col(7).astype(BF16), wgc=col(8).astype(BF16),
        qg=jnp.tile(q_norm_g.astype(F32), N_HEADS).reshape(1, ATTN_W),
        kg=jnp.tile(k_norm_g.astype(F32), N_HEADS).reshape(1, ATTN_W),
        ikg=jnp.concatenate([idx_k_norm_g.astype(F32), jnp.zeros((LANES - IDX_DIM,), F32)]).reshape(1, LANES),
        seg=jnp.asarray(seg, BF16), segt=jnp.asarray(seg.T, BF16),
        conv_w=conv_dw_w.reshape(CONV_K, conv_ch).astype(F32), conv_b=conv_dw_b.reshape(1, conv_ch).astype(F32),
        ln_g=conv_ln_g.reshape(1, conv_ch).astype(F32), ln_b=conv_ln_b.reshape(1, conv_ch).astype(F32),
        wao=w_attn_out.astype(BF16), wco=w_conv_out.astype(BF16), wo=w_out.astype(BF16),
        g_ffn=norm_ffn_g.reshape(1, d).astype(F32),
        wg=w_ffn_gate.astype(BF16), wu=w_ffn_up.astype(BF16), wd=w_ffn_down.astype(BF16),
    )


def _layer(x, pos0, past_k, past_v, past_ik, past_conv, rel_bias, w):
    b, t, d = x.shape
    m = b * t
    q, k, k_b, v, v_b, iq, sm, sm_b, u, sga, sgc = _in_proj(x.reshape(m, d), w)
    conv_ch = u.shape[1]
    u = u.reshape(b, t, conv_ch)

    k_b = k_b.reshape(b, t, ATTN_W)
    v_b = v_b.reshape(b, t, ATTN_W)
    ik_b = sm_b.reshape(b, t, LANES)[:, :, :IDX_DIM]
    if past_k is not None:
        p = past_k.shape[1]
        k_b = jnp.concatenate([past_k.reshape(b, p, ATTN_W).astype(BF16), k_b], axis=1)
        v_b = jnp.concatenate([past_v.reshape(b, p, ATTN_W).astype(BF16), v_b], axis=1)
        ik_b = jnp.concatenate([past_ik.astype(BF16), ik_b], axis=1)
    attn = _sparse_attn(q.reshape(b, t, ATTN_W), iq.reshape(b, t, ATTN_W), sm.reshape(b, t, LANES),
                        k_b, v_b, ik_b, rel_bias, pos0)

    past = jnp.pad(past_conv.astype(F32), ((0, 0), (HALO - (CONV_K - 1), 0), (0, 0)))
    cact = _conv_mod(u, past, w)
    y = _mix_ffn(x.reshape(m, d), attn.reshape(m, ATTN_W), cact.reshape(m, conv_ch), sga, sgc, w)

    new_conv = jnp.concatenate([past_conv, u], axis=1)[:, -(CONV_K - 1):]
    return (y.reshape(b, t, d), k.reshape(b, t, N_HEADS, HEAD_DIM), v.reshape(b, t, N_HEADS, HEAD_DIM),
            sm.reshape(b, t, LANES)[:, :, :IDX_DIM], new_conv)


def kernel(x_prompt, x_sample, cache_k, cache_v, cache_idx_k, state_conv, rel_bias, norm_mix_g, w_in, q_norm_g, k_norm_g, idx_k_norm_g, conv_dw_w, conv_dw_b, conv_ln_g, conv_ln_b, w_conv_out, w_attn_out, w_out, norm_ffn_g, w_ffn_gate, w_ffn_up, w_ffn_down):
    depth = w_in.shape[0]
    bp = x_prompt.shape[0]
    past_len = cache_k.shape[2]
    conv_ch = conv_dw_w.shape[-1]
    yp, ys = x_prompt, x_sample
    outs_p, outs_s = [], []
    for l in range(depth):
        w = _prep_weights(norm_mix_g[l], w_in[l], q_norm_g[l], k_norm_g[l], idx_k_norm_g[l], conv_dw_w[l],
                          conv_dw_b[l], conv_ln_g[l], conv_ln_b[l], w_conv_out[l], w_attn_out[l], w_out[l],
                          norm_ffn_g[l], w_ffn_gate[l], w_ffn_up[l], w_ffn_down[l])
        zero_conv = jnp.zeros((bp, CONV_K - 1, conv_ch), yp.dtype)
        yp, *rest_p = _layer(yp, 0, None, None, None, zero_conv, rel_bias, w)
        outs_p.append(rest_p)
        ys, *rest_s = _layer(ys, past_len, cache_k[l], cache_v[l], cache_idx_k[l], state_conv[l], rel_bias, w)
        outs_s.append(rest_s)
    stack = lambda outs, i: jnp.stack([o[i] for o in outs])
    return (yp, ys, stack(outs_p, 0), stack(outs_p, 1), stack(outs_p, 2), stack(outs_p, 3),
            stack(outs_s, 0), stack(outs_s, 1), stack(outs_s, 2), stack(outs_s, 3))
```

```python
import functools
import math

import numpy as np
import jax
import jax.numpy as jnp
from jax import lax
from jax.experimental import pallas as pl
from jax.experimental.pallas import tpu as pltpu

CHUNK = 64
CHUNK_SHIFT = 6
N_HEADS = 8
HEAD_DIM = 64
ATTN_W = N_HEADS * HEAD_DIM
N_IDX_HEADS = 8
IDX_DIM = 64
TOPK_MAX = 256
CONV_K = 31
NUM_BUCKETS = 32
MAX_DISTANCE = 128
EPS = 1e-6
NEG = -1e30
INT_MIN = -(2 ** 31)
LOG2E = math.log2(math.e)

LANES = 128
SUBLANES = 8
HALO = 32
KEY_TILE = 256
ROW_TILE = 256
VMEM_LIMIT = 56 * 1024 * 1024

F32 = jnp.float32
BF16 = jnp.bfloat16


def _const_spec(shape):
    return pl.BlockSpec(shape, lambda *_: (0,) * len(shape), pipeline_mode=pl.Buffered(1))


def _split_dot(x, m):
    hi = x.astype(BF16)
    lo = (x - hi.astype(F32)).astype(BF16)
    return jnp.dot(hi, m, preferred_element_type=F32) + jnp.dot(lo, m, preferred_element_type=F32)


def _in_proj_kernel(x_ref, g_ref, wq_ref, wk_ref, wv_ref, wiq_ref, wsm_ref, wglu_ref, wga_ref, wgc_ref,
                    qg_ref, kg_ref, ikg_ref, seg_ref, segt_ref,
                    q_ref, k_ref, kb_ref, v_ref, vb_ref, iq_ref, sm_ref, smb_ref, u_ref, sga_ref, sgc_ref):
    x = x_ref[...]
    ms = jnp.mean(x * x, axis=-1, keepdims=True)
    h = (x * lax.rsqrt(ms + EPS) * g_ref[...]).astype(BF16)

    def head_rms(y, gain):
        ss = _split_dot(y * y, seg_ref[...])
        r = lax.rsqrt(ss * (1.0 / HEAD_DIM) + EPS)
        return y * _split_dot(r, segt_ref[...]) * gain

    q = head_rms(jnp.dot(h, wq_ref[...], preferred_element_type=F32), qg_ref[...])
    q_ref[...] = (q * (HEAD_DIM ** -0.5 * LOG2E)).astype(BF16)
    k = head_rms(jnp.dot(h, wk_ref[...], preferred_element_type=F32), kg_ref[...])
    k_ref[...] = k
    kb_ref[...] = k.astype(BF16)
    v = jnp.dot(h, wv_ref[...], preferred_element_type=F32)
    v_ref[...] = v
    vb_ref[...] = v.astype(BF16)
    iq = jnp.dot(h, wiq_ref[...], preferred_element_type=F32)
    iq_ref[...] = (iq * (IDX_DIM ** -0.5)).astype(BF16)

    sm = jnp.dot(h, wsm_ref[...], preferred_element_type=F32)
    lane = lax.broadcasted_iota(jnp.int32, sm.shape, 1)
    is_ik = lane < IDX_DIM
    ss = jnp.sum(jnp.where(is_ik, sm * sm, 0.0), axis=-1, keepdims=True)
    ikn = sm * lax.rsqrt(ss * (1.0 / IDX_DIM) + EPS) * ikg_ref[...]
    iw = sm * (N_IDX_HEADS ** -0.5)
    smo = jnp.where(is_ik, ikn, jnp.where(lane < IDX_DIM + N_IDX_HEADS, iw, 0.0))
    sm_ref[...] = smo
    smb_ref[...] = smo.astype(BF16)

    glu = jnp.dot(h, wglu_ref[...], preferred_element_type=F32)
    c = glu.shape[1] // 2
    u_ref[...] = glu[:, :c] * jax.nn.sigmoid(glu[:, c:])
    sga_ref[...] = jax.nn.sigmoid(jnp.dot(h, wga_ref[...], preferred_element_type=F32)).astype(BF16)
    sgc_ref[...] = jax.nn.sigmoid(jnp.dot(h, wgc_ref[...], preferred_element_type=F32)).astype(BF16)


def _in_proj(x, w):
    m, d = x.shape
    tm = min(ROW_TILE, m)
    assert m % tm == 0
    conv_ch = w["wglu"].shape[1] // 2
    row = lambda n: pl.BlockSpec((tm, n), lambda i: (i, 0))
    consts = [w[n] for n in ("g_mix", "wq", "wk", "wv", "wiq", "wsm", "wglu", "wga", "wgc",
                             "qg", "kg", "ikg", "seg", "segt")]
    out_shape = [
        jax.ShapeDtypeStruct((m, ATTN_W), BF16),
        jax.ShapeDtypeStruct((m, ATTN_W), F32),
        jax.ShapeDtypeStruct((m, ATTN_W), BF16),
        jax.ShapeDtypeStruct((m, ATTN_W), F32),
        jax.ShapeDtypeStruct((m, ATTN_W), BF16),
        jax.ShapeDtypeStruct((m, ATTN_W), BF16),
        jax.ShapeDtypeStruct((m, LANES), F32),
        jax.ShapeDtypeStruct((m, LANES), BF16),
        jax.ShapeDtypeStruct((m, conv_ch), F32),
        jax.ShapeDtypeStruct((m, d), BF16),
        jax.ShapeDtypeStruct((m, d), BF16),
    ]
    return pl.pallas_call(
        _in_proj_kernel,
        grid=(m // tm,),
        in_specs=[row(d)] + [_const_spec(c.shape) for c in consts],
        out_specs=[row(s.shape[1]) for s in out_shape],
        out_shape=out_shape,
        compiler_params=pltpu.CompilerParams(dimension_semantics=("arbitrary",), vmem_limit_bytes=VMEM_LIMIT),
        name="in_proj",
    )(x, *consts)


def _sparse_attn_kernel(qt_ref, iqt_ref, iwt_ref, k_ref, vt_ref, ik_ref, bias_ref, o_ref,
                        key_scr, m_scr, l_scr, acc_scr, s_scr, *, tq, tk, nkt, pos0, n_keys, top_k):
    qi = pl.program_id(1)
    q0 = pos0 + qi * tq
    n_kt = jnp.minimum(nkt, ((q0 + tq - 1) // CHUNK * CHUNK + CHUNK + tk - 1) // tk)
    last = n_kt - 1

    def fold(x, op):
        return op(x.reshape(tk // SUBLANES, SUBLANES, tq), axis=0)

    iqt = iqt_ref[...]
    iqt_h = [iqt[h * IDX_DIM:(h + 1) * IDX_DIM, :] for h in range(N_IDX_HEADS)]
    iwt = iwt_ref[...]
    iw_h = [iwt[h:h + 1, :] for h in range(N_IDX_HEADS)]

    def score_tile(kt, masked):
        ik = ik_ref[kt]
        s = None
        for h in range(N_IDX_HEADS):
            t = jnp.maximum(jnp.dot(ik, iqt_h[h], preferred_element_type=F32), 0.0) * iw_h[h]
            s = t if s is None else s + t
        s = s + 0.0
        bits = pltpu.bitcast(s, jnp.int32)
        key = bits ^ ((bits >> 31) & 0x7FFFFFFF)
        if masked:
            kp = kt * tk + lax.broadcasted_iota(jnp.int32, (tk, tq), 0)
            qp = q0 + lax.broadcasted_iota(jnp.int32, (tk, tq), 1)
            adm = ((kp >> CHUNK_SHIFT) <= (qp >> CHUNK_SHIFT)) & (kp < n_keys)
            key = jnp.where(adm, key, INT_MIN)
        key_scr[kt] = key

    def score_body(kt, carry):
        score_tile(kt, False)
        return carry

    lax.fori_loop(0, last, score_body, 0)
    score_tile(last, True)

    def count(pred):
        def body(kt, acc):
            return acc + fold(jnp.where(pred(key_scr[kt], kt), 1.0, 0.0), jnp.sum)
        acc = lax.fori_loop(0, n_kt, body, jnp.zeros((SUBLANES, tq), F32))
        return jnp.sum(acc, axis=0, keepdims=True)

    def thr_body(i, t):
        cand = t + (jnp.int32(1) << (31 - i))
        c = count(lambda key, kt: key >= cand)
        return jnp.where(c >= top_k, cand, t)

    thr = lax.fori_loop(0, 32, thr_body, jnp.full((1, tq), INT_MIN, jnp.int32))
    thr = jnp.maximum(thr, INT_MIN + 1)

    n_ge = count(lambda key, kt: key >= thr)
    has_tie = jnp.max(n_ge) > top_k

    @pl.when(has_tie)
    def _():
        need_m1 = (top_k - 1) - count(lambda key, kt: key > thr)
        row = lax.broadcasted_iota(jnp.int32, (tk, tq), 0)

        def idx_body(i, cut):
            cand = cut + (jnp.int32(1) << (15 - i))
            c = count(lambda key, kt: (key == thr) & (kt * tk + row < cand))
            return jnp.where(c <= need_m1, cand, cut)

        cut = lax.fori_loop(0, 16, idx_body, jnp.zeros((1, tq), jnp.int32))
        tie_col = n_ge > top_k

        def drop_body(kt, carry):
            key = key_scr[kt]
            drop = tie_col & (key == thr) & (kt * tk + row > cut)
            key_scr[kt] = jnp.where(drop, INT_MIN, key)
            return carry

        lax.fori_loop(0, n_kt, drop_body, 0)

    qt = qt_ref[...]
    slab_row = lax.broadcasted_iota(jnp.int32, (LANES, tq), 0)
    qz = []
    for h in range(N_HEADS):
        slab = qt[(h // 2) * LANES:(h // 2 + 1) * LANES, :]
        mine = (slab_row < HEAD_DIM) if h % 2 == 0 else (slab_row >= HEAD_DIM)
        qz.append(jnp.where(mine, slab, jnp.zeros_like(slab)))
    m_scr[...] = jnp.full(m_scr.shape, NEG, F32)
    l_scr[...] = jnp.zeros(l_scr.shape, F32)
    acc_scr[...] = jnp.zeros(acc_scr.shape, F32)

    def attn_tile(kt, bias_idx):
        madd = jnp.where(key_scr[kt] >= thr, 0.0, NEG)
        m_new = []
        for h in range(N_HEADS):
            pair = h // 2
            s = jnp.dot(k_ref[kt, :, pair * LANES:(pair + 1) * LANES], qz[h], preferred_element_type=F32)
            if bias_idx is not None:
                s = s + bias_ref[bias_idx, h]
            s = s + madd
            s_scr[h] = s
            m_new.append(jnp.maximum(m_scr[h], jnp.max(fold(s, jnp.max), axis=0, keepdims=True)))
        for h in range(N_HEADS):
            alpha = jnp.exp2(m_scr[h] - m_new[h])
            p = jnp.exp2(s_scr[h] - m_new[h])
            l_scr[h] = alpha * l_scr[h] + jnp.sum(fold(p, jnp.sum), axis=0, keepdims=True)
            pv = jnp.dot(vt_ref[kt, h * HEAD_DIM:(h + 1) * HEAD_DIM, :], p.astype(BF16), preferred_element_type=F32)
            acc_scr[h] = alpha * acc_scr[h] + pv
            m_scr[h] = m_new[h]

    def far_body(kt, carry):
        attn_tile(kt, None)
        return carry

    lax.fori_loop(0, n_kt - 2, far_body, 0)

    @pl.when(n_kt >= 2)
    def _():
        attn_tile(n_kt - 2, 0)

    attn_tile(last, 1)

    for h in range(N_HEADS):
        o_ref[h * HEAD_DIM:(h + 1) * HEAD_DIM, :] = (acc_scr[h] / l_scr[h]).astype(o_ref.dtype)


def _rel_bucket(rel):
    nb = NUM_BUCKETS // 2
    max_exact = nb // 2
    ret = jnp.where(rel > 0, nb, 0)
    n = jnp.abs(rel)
    nf = jnp.maximum(n, 1).astype(jnp.float32)
    large = max_exact + (jnp.log(nf / max_exact) / math.log(MAX_DISTANCE / max_exact) * (nb - max_exact)).astype(jnp.int32)
    large = jnp.minimum(large, nb - 1)
    return ret + jnp.where(n < max_exact, n, large)


def _bias_tiles(rel_bias, tq, tk):
    j = jnp.arange(tk, dtype=jnp.int32)[:, None]
    i = jnp.arange(tq, dtype=jnp.int32)[None, :]
    rb = (rel_bias.astype(F32) - rel_bias[_rel_bucket(jnp.int32(-MAX_DISTANCE))].astype(F32)) * LOG2E
    tiles = []
    for d in (-tk, 0):
        onehot = jax.nn.one_hot(_rel_bucket(j - i + d), NUM_BUCKETS, dtype=F32)
        b = jnp.einsum("jib,bh->hji", onehot, rb, precision=lax.Precision.HIGHEST)
        tiles.append(b)
    return jnp.stack(tiles)


def _check_tiling(t, tq, tk, nkt, pos0, n_keys):
    for qi in range(t // tq):
        q0 = pos0 + qi * tq
        n_kt = min(nkt, ((q0 + tq - 1) // CHUNK * CHUNK + CHUNK + tk - 1) // tk)
        adm_end = min(q0 // CHUNK * CHUNK + CHUNK, n_keys)
        assert (n_kt - 1) * tk <= adm_end, "only the last visited tile may hold inadmissible keys"
        assert n_kt * tk >= min((q0 + tq - 1) // CHUNK * CHUNK + CHUNK, n_keys), "visited tiles cover every admissible key"
        assert (n_kt - 1) * tk == q0, "last tile starts with the query tile"
        assert tk >= MAX_DISTANCE, "tiles before the last two are at least MAX_DISTANCE behind"


def _sparse_attn(qt, iqt, iwt, k_b, v_b, ik_b, rel_bias, pos0):
    b, _, t = qt.shape
    n_keys = k_b.shape[1]
    top_k = min(TOPK_MAX, n_keys // 4)
    tk = KEY_TILE
    tq = min(tk, t)
    nkt = pl.cdiv(n_keys, tk)
    _check_tiling(t, tq, tk, nkt, pos0, n_keys)
    pad = nkt * tk - n_keys
    if pad:
        k_b, v_b, ik_b = (jnp.pad(a, ((0, 0), (0, pad), (0, 0))) for a in (k_b, v_b, ik_b))
    k_t = k_b.reshape(b, nkt, tk, ATTN_W)
    vt_t = v_b.reshape(b, nkt, tk, ATTN_W).swapaxes(2, 3)
    ik_t = ik_b.reshape(b, nkt, tk, IDX_DIM)
    bias = _bias_tiles(rel_bias, tq, tk)

    qspec = lambda n: pl.BlockSpec((None, n, tq), lambda bi, qi: (bi, 0, qi))
    kspec = lambda r, c: pl.BlockSpec((None, nkt, r, c), lambda bi, qi: (bi, 0, 0, 0))
    kern = functools.partial(_sparse_attn_kernel, tq=tq, tk=tk, nkt=nkt, pos0=pos0, n_keys=n_keys, top_k=top_k)
    return pl.pallas_call(
        kern,
        grid=(b, t // tq),
        in_specs=[qspec(ATTN_W), qspec(ATTN_W), qspec(N_IDX_HEADS),
                  kspec(tk, ATTN_W), kspec(ATTN_W, tk), kspec(tk, IDX_DIM), _const_spec(bias.shape)],
        out_specs=qspec(ATTN_W),
        out_shape=jax.ShapeDtypeStruct((b, ATTN_W, t), BF16),
        scratch_shapes=[pltpu.VMEM((nkt, tk, tq), jnp.int32),
                        pltpu.VMEM((N_HEADS, 1, tq), F32),
                        pltpu.VMEM((N_HEADS, 1, tq), F32),
                        pltpu.VMEM((N_HEADS, HEAD_DIM, tq), F32),
                        pltpu.VMEM((N_HEADS, tk, tq), F32)],
        compiler_params=pltpu.CompilerParams(dimension_semantics=("arbitrary", "arbitrary"),
                                             vmem_limit_bytes=VMEM_LIMIT),
        name="sparse_attn",
    )(qt, iqt, iwt, k_t, vt_t, ik_t, bias)


def _conv_kernel(*refs, tm, has_prev):
    if has_prev:
        u_ref, prev_ref, past_ref, w_ref, b_ref, lg_ref, lb_ref, o_ref, ext_scr = refs
        halo = jnp.where(pl.program_id(1) == 0, past_ref[...], prev_ref[...])
    else:
        u_ref, past_ref, w_ref, b_ref, lg_ref, lb_ref, o_ref, ext_scr = refs
        halo = past_ref[...]
    ext_scr[0:HALO, :] = halo
    ext_scr[HALO:HALO + tm, :] = u_ref[...]
    rb = min(tm, 32)
    first = HALO - (CONV_K - 1)
    for r0 in range(0, tm, rb):
        acc = None
        for j in range(CONV_K):
            t = ext_scr[r0 + first + j:r0 + first + j + rb, :] * w_ref[j:j + 1, :]
            acc = t if acc is None else acc + t
        c = acc + b_ref[...]
        mu = jnp.mean(c, axis=-1, keepdims=True)
        cc = c - mu
        var = jnp.mean(cc * cc, axis=-1, keepdims=True)
        y = cc * lax.rsqrt(var + EPS) * lg_ref[...] + lb_ref[...]
        o_ref[r0:r0 + rb, :] = (y * jax.nn.sigmoid(y)).astype(o_ref.dtype)


def _conv_mod(u, past, w):
    b, t, c = u.shape
    tm = min(ROW_TILE, t)
    assert t % tm == 0 and tm % 8 == 0
    has_prev = t > tm
    tile = pl.BlockSpec((None, tm, c), lambda bi, i: (bi, i, 0))
    halo = pl.BlockSpec((None, HALO, c), lambda bi, i: (bi, 0, 0))
    in_specs = [tile]
    args = [u]
    if has_prev:
        r = tm // HALO
        in_specs.append(pl.BlockSpec((None, HALO, c), lambda bi, i: (bi, jnp.maximum(i * r - 1, 0), 0)))
        args.append(u)
    consts = [w["conv_w"], w["conv_b"], w["ln_g"], w["ln_b"]]
    in_specs += [halo] + [_const_spec(x.shape) for x in consts]
    args += [past] + consts
    return pl.pallas_call(
        functools.partial(_conv_kernel, tm=tm, has_prev=has_prev),
        grid=(b, t // tm),
        in_specs=in_specs,
        out_specs=tile,
        out_shape=jax.ShapeDtypeStruct((b, t, c), BF16),
        scratch_shapes=[pltpu.VMEM((HALO + tm, c), F32)],
        compiler_params=pltpu.CompilerParams(dimension_semantics=("arbitrary", "arbitrary"),
                                             vmem_limit_bytes=VMEM_LIMIT),
        name="conv_mod",
    )(*args)


def _mix_ffn_kernel(x_ref, attn_ref, cact_ref, sga_ref, sgc_ref, wao_ref, wco_ref, wo_ref, g_ref,
                    wg_ref, wu_ref, wd_ref, y_ref, *, n_chunks):
    attn_out = jnp.dot(attn_ref[...], wao_ref[...], preferred_element_type=F32)
    conv_out = jnp.dot(cact_ref[...], wco_ref[...], preferred_element_type=F32)
    merged = sga_ref[...].astype(F32) * attn_out + sgc_ref[...].astype(F32) * conv_out
    x1 = x_ref[...] + jnp.dot(merged.astype(BF16), wo_ref[...], preferred_element_type=F32)
    ms = jnp.mean(x1 * x1, axis=-1, keepdims=True)
    h2 = (x1 * lax.rsqrt(ms + EPS) * g_ref[...]).astype(BF16)
    hc = wg_ref.shape[1] // n_chunks
    y = x1
    for ci in range(n_chunks):
        sl = slice(ci * hc, (ci + 1) * hc)
        gate = jnp.dot(h2, wg_ref[:, sl], preferred_element_type=F32)
        up = jnp.dot(h2, wu_ref[:, sl], preferred_element_type=F32)
        act = (gate * jax.nn.sigmoid(gate) * up).astype(BF16)
        y = y + jnp.dot(act, wd_ref[sl, :], preferred_element_type=F32)
    y_ref[...] = y


def _mix_ffn(x, attn, cact, sga, sgc, w):
    m, d = x.shape
    tm = min(ROW_TILE, m)
    assert m % tm == 0
    hidden = w["wg"].shape[1]
    n_chunks = 2 if hidden % (2 * LANES) == 0 else 1
    row = lambda n: pl.BlockSpec((tm, n), lambda i: (i, 0))
    consts = [w[n] for n in ("wao", "wco", "wo", "g_ffn", "wg", "wu", "wd")]
    return pl.pallas_call(
        functools.partial(_mix_ffn_kernel, n_chunks=n_chunks),
        grid=(m // tm,),
        in_specs=[row(d), row(attn.shape[1]), row(cact.shape[1]), row(d), row(d)]
                 + [_const_spec(c.shape) for c in consts],
        out_specs=row(d),
        out_shape=jax.ShapeDtypeStruct((m, d), F32),
        compiler_params=pltpu.CompilerParams(dimension_semantics=("arbitrary",), vmem_limit_bytes=VMEM_LIMIT),
        name="mix_ffn",
    )(x, attn, cact, sga, sgc, *consts)


def _prep_weights(norm_mix_g, w_in, q_norm_g, k_norm_g, idx_k_norm_g, conv_dw_w, conv_dw_b, conv_ln_g, conv_ln_b,
                  w_conv_out, w_attn_out, w_out, norm_ffn_g, w_ffn_gate, w_ffn_up, w_ffn_down):
    d = w_in.shape[0]
    conv_ch = conv_dw_w.shape[-1]
    sizes = (ATTN_W, ATTN_W, ATTN_W, N_IDX_HEADS * IDX_DIM, IDX_DIM, N_IDX_HEADS, 2 * conv_ch, d, d)
    offs = np.concatenate([[0], np.cumsum(sizes)])
    assert offs[-1] == w_in.shape[1]
    col = lambda i: w_in[:, offs[i]:offs[i + 1]]
    wsm = jnp.concatenate([col(4), col(5), jnp.zeros((d, LANES - IDX_DIM - N_IDX_HEADS), w_in.dtype)], axis=1)
    head = np.arange(ATTN_W) // HEAD_DIM
    seg = (head[:, None] == np.arange(LANES)[None, :]).astype(np.float32)
    return dict(
        g_mix=norm_mix_g.reshape(1, d).astype(F32),
        wq=col(0).astype(BF16), wk=col(1).astype(BF16), wv=col(2).astype(BF16), wiq=col(3).astype(BF16),
        wsm=wsm.astype(BF16), wglu=col(6).astype(BF16), wga=col(7).astype(BF16), wgc=col(8).astype(BF16),
        qg=jnp.tile(q_norm_g.astype(F32), N_HEADS).reshape(1, ATTN_W),
        kg=jnp.tile(k_norm_g.astype(F32), N_HEADS).reshape(1, ATTN_W),
        ikg=jnp.concatenate([idx_k_norm_g.astype(F32), jnp.zeros((LANES - IDX_DIM,), F32)]).reshape(1, LANES),
        seg=jnp.asarray(seg, BF16), segt=jnp.asarray(seg.T, BF16),
        conv_w=conv_dw_w.reshape(CONV_K, conv_ch).astype(F32), conv_b=conv_dw_b.reshape(1, conv_ch).astype(F32),
        ln_g=conv_ln_g.reshape(1, conv_ch).astype(F32), ln_b=conv_ln_b.reshape(1, conv_ch).astype(F32),
        wao=w_attn_out.astype(BF16), wco=w_conv_out.astype(BF16), wo=w_out.astype(BF16),
        g_ffn=norm_ffn_g.reshape(1, d).astype(F32),
        wg=w_ffn_gate.astype(BF16), wu=w_ffn_up.astype(BF16), wd=w_ffn_down.astype(BF16),
    )


def _layer(x, pos0, past_k, past_v, past_ik, past_conv, rel_bias, w):
    b, t, d = x.shape
    m = b * t
    q, k, k_b, v, v_b, iq, sm, sm_b, u, sga, sgc = _in_proj(x.reshape(m, d), w)
    conv_ch = u.shape[1]
    u = u.reshape(b, t, conv_ch)
    sm = sm.reshape(b, t, LANES)

    k_b = k_b.reshape(b, t, ATTN_W)
    v_b = v_b.reshape(b, t, ATTN_W)
    ik_b = sm_b.reshape(b, t, LANES)[:, :, :IDX_DIM]
    if past_k is not None:
        p = past_k.shape[1]
        k_b = jnp.concatenate([past_k.reshape(b, p, ATTN_W).astype(BF16), k_b], axis=1)
        v_b = jnp.concatenate([past_v.reshape(b, p, ATTN_W).astype(BF16), v_b], axis=1)
        ik_b = jnp.concatenate([past_ik.astype(BF16), ik_b], axis=1)
    qt = q.reshape(b, t, ATTN_W).swapaxes(1, 2)
    iqt = iq.reshape(b, t, ATTN_W).swapaxes(1, 2)
    iwt = sm[:, :, IDX_DIM:IDX_DIM + N_IDX_HEADS].swapaxes(1, 2)
    attn = _sparse_attn(qt, iqt, iwt, k_b, v_b, ik_b, rel_bias, pos0).swapaxes(1, 2)

    past = jnp.pad(past_conv.astype(F32), ((0, 0), (HALO - (CONV_K - 1), 0), (0, 0)))
    cact = _conv_mod(u, past, w)
    y = _mix_ffn(x.reshape(m, d), attn.reshape(m, ATTN_W), cact.reshape(m, conv_ch), sga, sgc, w)

    new_conv = jnp.concatenate([past_conv, u], axis=1)[:, -(CONV_K - 1):]
    return (y.reshape(b, t, d), k.reshape(b, t, N_HEADS, HEAD_DIM), v.reshape(b, t, N_HEADS, HEAD_DIM),
            sm[:, :, :IDX_DIM], new_conv)


def kernel(x_prompt, x_sample, cache_k, cache_v, cache_idx_k, state_conv, rel_bias, norm_mix_g, w_in, q_norm_g, k_norm_g, idx_k_norm_g, conv_dw_w, conv_dw_b, conv_ln_g, conv_ln_b, w_conv_out, w_attn_out, w_out, norm_ffn_g, w_ffn_gate, w_ffn_up, w_ffn_down):
    depth = w_in.shape[0]
    bp = x_prompt.shape[0]
    past_len = cache_k.shape[2]
    conv_ch = conv_dw_w.shape[-1]
    yp, ys = x_prompt, x_sample
    outs_p, outs_s = [], []
    for l in range(depth):
        w = _prep_weights(norm_mix_g[l], w_in[l], q_norm_g[l], k_norm_g[l], idx_k_norm_g[l], conv_dw_w[l],
                          conv_dw_b[l], conv_ln_g[l], conv_ln_b[l], w_conv_out[l], w_attn_out[l], w_out[l],
                          norm_ffn_g[l], w_ffn_gate[l], w_ffn_up[l], w_ffn_down[l])
        zero_conv = jnp.zeros((bp, CONV_K - 1, conv_ch), yp.dtype)
        yp, *rest_p = _layer(yp, 0, None, None, None, zero_conv, rel_bias, w)
        outs_p.append(rest_p)
        ys, *rest_s = _layer(ys, past_len, cache_k[l], cache_v[l], cache_idx_k[l], state_conv[l], rel_bias, w)
        outs_s.append(rest_s)
    stack = lambda outs, i: jnp.stack([o[i] for o in outs])
    return (yp, ys, stack(outs_p, 0), stack(outs_p, 1), stack(outs_p, 2), stack(outs_p, 3),
            stack(outs_s, 0), stack(outs_s, 1), stack(outs_s, 2), stack(outs_s, 3))
```

```python
import functools
import math

import numpy as np
import jax
import jax.numpy as jnp
from jax import lax
from jax.experimental import pallas as pl
from jax.experimental.pallas import tpu as pltpu

CHUNK = 64
CHUNK_SHIFT = 6
N_HEADS = 8
HEAD_DIM = 64
ATTN_W = N_HEADS * HEAD_DIM
N_IDX_HEADS = 8
IDX_DIM = 64
TOPK_MAX = 256
CONV_K = 31
NUM_BUCKETS = 32
MAX_DISTANCE = 128
EPS = 1e-6
NEG = -1e30
INT_MIN = -(2 ** 31)
LOG2E = math.log2(math.e)

LANES = 128
SUBLANES = 8
HALO = 32
KEY_TILE = 256
ROW_TILE = 256
VMEM_LIMIT = 56 * 1024 * 1024

F32 = jnp.float32
BF16 = jnp.bfloat16


def _const_spec(shape):
    return pl.BlockSpec(shape, lambda *_: (0,) * len(shape), pipeline_mode=pl.Buffered(1))


def _split_dot(x, m):
    hi = x.astype(BF16)
    lo = (x - hi.astype(F32)).astype(BF16)
    return jnp.dot(hi, m, preferred_element_type=F32) + jnp.dot(lo, m, preferred_element_type=F32)


def _in_proj_kernel(x_ref, g_ref, wq_ref, wk_ref, wv_ref, wiq_ref, wsm_ref, wglu_ref, wga_ref, wgc_ref,
                    qg_ref, kg_ref, ikg_ref, seg_ref, segt_ref,
                    q_ref, k_ref, kb_ref, v_ref, vb_ref, iq_ref, sm_ref, smb_ref, u_ref, sga_ref, sgc_ref):
    x = x_ref[...]
    ms = jnp.mean(x * x, axis=-1, keepdims=True)
    h = (x * lax.rsqrt(ms + EPS) * g_ref[...]).astype(BF16)

    def head_rms(y, gain):
        ss = _split_dot(y * y, seg_ref[...])
        r = lax.rsqrt(ss * (1.0 / HEAD_DIM) + EPS)
        return y * _split_dot(r, segt_ref[...]) * gain

    q = head_rms(jnp.dot(h, wq_ref[...], preferred_element_type=F32), qg_ref[...])
    q_ref[...] = (q * (HEAD_DIM ** -0.5 * LOG2E)).astype(BF16)
    k = head_rms(jnp.dot(h, wk_ref[...], preferred_element_type=F32), kg_ref[...])
    k_ref[...] = k
    kb_ref[...] = k.astype(BF16)
    v = jnp.dot(h, wv_ref[...], preferred_element_type=F32)
    v_ref[...] = v
    vb_ref[...] = v.astype(BF16)
    iq = jnp.dot(h, wiq_ref[...], preferred_element_type=F32)
    iq_ref[...] = (iq * (IDX_DIM ** -0.5)).astype(BF16)

    sm = jnp.dot(h, wsm_ref[...], preferred_element_type=F32)
    lane = lax.broadcasted_iota(jnp.int32, sm.shape, 1)
    is_ik = lane < IDX_DIM
    ss = jnp.sum(jnp.where(is_ik, sm * sm, 0.0), axis=-1, keepdims=True)
    ikn = sm * lax.rsqrt(ss * (1.0 / IDX_DIM) + EPS) * ikg_ref[...]
    iw = sm * (N_IDX_HEADS ** -0.5)
    smo = jnp.where(is_ik, ikn, jnp.where(lane < IDX_DIM + N_IDX_HEADS, iw, 0.0))
    sm_ref[...] = smo
    smb_ref[...] = smo.astype(BF16)

    glu = jnp.dot(h, wglu_ref[...], preferred_element_type=F32)
    c = glu.shape[1] // 2
    u_ref[...] = glu[:, :c] * jax.nn.sigmoid(glu[:, c:])
    sga_ref[...] = jax.nn.sigmoid(jnp.dot(h, wga_ref[...], preferred_element_type=F32)).astype(BF16)
    sgc_ref[...] = jax.nn.sigmoid(jnp.dot(h, wgc_ref[...], preferred_element_type=F32)).astype(BF16)


def _in_proj(x, w):
    m, d = x.shape
    tm = min(ROW_TILE, m)
    assert m % tm == 0
    conv_ch = w["wglu"].shape[1] // 2
    row = lambda n: pl.BlockSpec((tm, n), lambda i: (i, 0))
    consts = [w[n] for n in ("g_mix", "wq", "wk", "wv", "wiq", "wsm", "wglu", "wga", "wgc",
                             "qg", "kg", "ikg", "seg", "segt")]
    out_shape = [
        jax.ShapeDtypeStruct((m, ATTN_W), BF16),
        jax.ShapeDtypeStruct((m, ATTN_W), F32),
        jax.ShapeDtypeStruct((m, ATTN_W), BF16),
        jax.ShapeDtypeStruct((m, ATTN_W), F32),
        jax.ShapeDtypeStruct((m, ATTN_W), BF16),
        jax.ShapeDtypeStruct((m, ATTN_W), BF16),
        jax.ShapeDtypeStruct((m, LANES), F32),
        jax.ShapeDtypeStruct((m, LANES), BF16),
        jax.ShapeDtypeStruct((m, conv_ch), F32),
        jax.ShapeDtypeStruct((m, d), BF16),
        jax.ShapeDtypeStruct((m, d), BF16),
    ]
    return pl.pallas_call(
        _in_proj_kernel,
        grid=(m // tm,),
        in_specs=[row(d)] + [_const_spec(c.shape) for c in consts],
        out_specs=[row(s.shape[1]) for s in out_shape],
        out_shape=out_shape,
        compiler_params=pltpu.CompilerParams(dimension_semantics=("arbitrary",), vmem_limit_bytes=VMEM_LIMIT),
        name="in_proj",
    )(x, *consts)


def _sparse_attn_kernel(qt_ref, iqt_ref, iwt_ref, k_ref, vt_ref, ik_ref, bias_ref, o_ref,
                        key_scr, m_scr, l_scr, acc_scr, s_scr, *, tq, tk, nkt, pos0, n_keys, top_k):
    qi = pl.program_id(1)
    q0 = pos0 + qi * tq
    n_kt = jnp.minimum(nkt, ((q0 + tq - 1) // CHUNK * CHUNK + CHUNK + tk - 1) // tk)
    last = n_kt - 1

    def fold(x, op):
        return op(x.reshape(tk // SUBLANES, SUBLANES, tq), axis=0)

    iqt = iqt_ref[...]
    iqt_h = [iqt[h * IDX_DIM:(h + 1) * IDX_DIM, :] for h in range(N_IDX_HEADS)]
    iwt = iwt_ref[...]
    iw_h = [iwt[h:h + 1, :] for h in range(N_IDX_HEADS)]

    def score_tile(kt, masked):
        ik = ik_ref[kt]
        s = None
        for h in range(N_IDX_HEADS):
            t = jnp.maximum(jnp.dot(ik, iqt_h[h], preferred_element_type=F32), 0.0) * iw_h[h]
            s = t if s is None else s + t
        s = s + 0.0
        bits = pltpu.bitcast(s, jnp.int32)
        key = bits ^ ((bits >> 31) & 0x7FFFFFFF)
        if masked:
            kp = kt * tk + lax.broadcasted_iota(jnp.int32, (tk, tq), 0)
            qp = q0 + lax.broadcasted_iota(jnp.int32, (tk, tq), 1)
            adm = ((kp >> CHUNK_SHIFT) <= (qp >> CHUNK_SHIFT)) & (kp < n_keys)
            key = jnp.where(adm, key, INT_MIN)
        key_scr[kt] = key

    def score_body(kt, carry):
        score_tile(kt, False)
        return carry

    lax.fori_loop(0, last, score_body, 0)
    score_tile(last, True)

    @pl.when(n_kt % 2 == 1)
    def _():
        key_scr[n_kt] = jnp.full((tk, tq), INT_MIN, jnp.int32)

    def count(pred):
        def body(i, acc):
            for kt in (2 * i, 2 * i + 1):
                acc = acc + fold(jnp.where(pred(key_scr[kt], kt), 1.0, 0.0), jnp.sum)
            return acc
        acc = lax.fori_loop(0, (n_kt + 1) // 2, body, jnp.zeros((SUBLANES, tq), F32))
        return jnp.sum(acc, axis=0, keepdims=True)

    def thr_body(i, t):
        cand = t + (jnp.int32(1) << (31 - i))
        c = count(lambda key, kt: key >= cand)
        return jnp.where(c >= top_k, cand, t)

    thr = lax.fori_loop(0, 32, thr_body, jnp.full((1, tq), INT_MIN, jnp.int32))
    thr = jnp.maximum(thr, INT_MIN + 1)

    n_ge = count(lambda key, kt: key >= thr)
    has_tie = jnp.max(n_ge) > top_k

    @pl.when(has_tie)
    def _():
        need_m1 = (top_k - 1) - count(lambda key, kt: key > thr)
        row = lax.broadcasted_iota(jnp.int32, (tk, tq), 0)

        def idx_body(i, cut):
            cand = cut + (jnp.int32(1) << (15 - i))
            c = count(lambda key, kt: (key == thr) & (kt * tk + row < cand))
            return jnp.where(c <= need_m1, cand, cut)

        cut = lax.fori_loop(0, 16, idx_body, jnp.zeros((1, tq), jnp.int32))
        tie_col = n_ge > top_k

        def drop_body(kt, carry):
            key = key_scr[kt]
            drop = tie_col & (key == thr) & (kt * tk + row > cut)
            key_scr[kt] = jnp.where(drop, INT_MIN, key)
            return carry

        lax.fori_loop(0, n_kt, drop_body, 0)

    qt = qt_ref[...]
    slab_row = lax.broadcasted_iota(jnp.int32, (LANES, tq), 0)
    qz = []
    for h in range(N_HEADS):
        slab = qt[(h // 2) * LANES:(h // 2 + 1) * LANES, :]
        mine = (slab_row < HEAD_DIM) if h % 2 == 0 else (slab_row >= HEAD_DIM)
        qz.append(jnp.where(mine, slab, jnp.zeros_like(slab)))
    m_scr[...] = jnp.full(m_scr.shape, NEG, F32)
    l_scr[...] = jnp.zeros(l_scr.shape, F32)
    acc_scr[...] = jnp.zeros(acc_scr.shape, F32)

    def attn_tile(kt, bias_idx):
        madd = jnp.where(key_scr[kt] >= thr, 0.0, NEG)
        m_new = []
        for h in range(N_HEADS):
            pair = h // 2
            s = jnp.dot(k_ref[kt, :, pair * LANES:(pair + 1) * LANES], qz[h], preferred_element_type=F32)
            if bias_idx is not None:
                s = s + bias_ref[bias_idx, h]
            s = s + madd
            s_scr[h] = s
            m_new.append(jnp.maximum(m_scr[h], jnp.max(fold(s, jnp.max), axis=0, keepdims=True)))
        for h in range(N_HEADS):
            alpha = jnp.exp2(m_scr[h] - m_new[h])
            p = jnp.exp2(s_scr[h] - m_new[h])
            l_scr[h] = alpha * l_scr[h] + jnp.sum(fold(p, jnp.sum), axis=0, keepdims=True)
            pv = jnp.dot(vt_ref[kt, h * HEAD_DIM:(h + 1) * HEAD_DIM, :], p.astype(BF16), preferred_element_type=F32)
            acc_scr[h] = alpha * acc_scr[h] + pv
            m_scr[h] = m_new[h]

    def far_body(kt, carry):
        attn_tile(kt, None)
        return carry

    lax.fori_loop(0, n_kt - 2, far_body, 0)

    @pl.when(n_kt >= 2)
    def _():
        attn_tile(n_kt - 2, 0)

    attn_tile(last, 1)

    for h in range(N_HEADS):
        o_ref[h * HEAD_DIM:(h + 1) * HEAD_DIM, :] = (acc_scr[h] / l_scr[h]).astype(o_ref.dtype)


def _rel_bucket(rel):
    nb = NUM_BUCKETS // 2
    max_exact = nb // 2
    ret = jnp.where(rel > 0, nb, 0)
    n = jnp.abs(rel)
    nf = jnp.maximum(n, 1).astype(jnp.float32)
    large = max_exact + (jnp.log(nf / max_exact) / math.log(MAX_DISTANCE / max_exact) * (nb - max_exact)).astype(jnp.int32)
    large = jnp.minimum(large, nb - 1)
    return ret + jnp.where(n < max_exact, n, large)


def _bias_tiles(rel_bias, tq, tk):
    j = jnp.arange(tk, dtype=jnp.int32)[:, None]
    i = jnp.arange(tq, dtype=jnp.int32)[None, :]
    return jnp.stack([_shifted_bias(rel_bias, j - i + d) for d in (-tk, 0)])


def _shifted_bias(rel_bias, rel):
    rb = (rel_bias.astype(F32) - rel_bias[_rel_bucket(jnp.int32(-MAX_DISTANCE))].astype(F32)) * LOG2E
    onehot = jax.nn.one_hot(_rel_bucket(rel), NUM_BUCKETS, dtype=F32)
    return jnp.moveaxis(jnp.einsum("...b,bh->...h", onehot, rb, precision=lax.Precision.HIGHEST), -1, 0)


def _check_tiling(t, tq, tk, nkt, pos0, n_keys):
    for qi in range(t // tq):
        q0 = pos0 + qi * tq
        n_kt = min(nkt, ((q0 + tq - 1) // CHUNK * CHUNK + CHUNK + tk - 1) // tk)
        adm_end = min(q0 // CHUNK * CHUNK + CHUNK, n_keys)
        assert (n_kt - 1) * tk <= adm_end, "only the last visited tile may hold inadmissible keys"
        assert n_kt * tk >= min((q0 + tq - 1) // CHUNK * CHUNK + CHUNK, n_keys), "visited tiles cover every admissible key"
        assert (n_kt - 1) * tk == q0, "last tile starts with the query tile"
        assert tk >= MAX_DISTANCE, "tiles before the last two are at least MAX_DISTANCE behind"


def _sparse_attn(qt, iqt, iwt, k_b, v_b, ik_b, rel_bias, pos0):
    b, _, t = qt.shape
    n_keys = k_b.shape[1]
    top_k = min(TOPK_MAX, n_keys // 4)
    tk = KEY_TILE
    tq = min(tk, t)
    nkt = pl.cdiv(n_keys, tk)
    _check_tiling(t, tq, tk, nkt, pos0, n_keys)
    pad = nkt * tk - n_keys
    if pad:
        k_b, v_b, ik_b = (jnp.pad(a, ((0, 0), (0, pad), (0, 0))) for a in (k_b, v_b, ik_b))
    k_t = k_b.reshape(b, nkt, tk, ATTN_W)
    vt_t = v_b.reshape(b, nkt, tk, ATTN_W).swapaxes(2, 3)
    ik_t = ik_b.reshape(b, nkt, tk, IDX_DIM)
    bias = _bias_tiles(rel_bias, tq, tk)

    qspec = lambda n: pl.BlockSpec((None, n, tq), lambda bi, qi: (bi, 0, qi))
    kspec = lambda r, c: pl.BlockSpec((None, nkt, r, c), lambda bi, qi: (bi, 0, 0, 0))
    kern = functools.partial(_sparse_attn_kernel, tq=tq, tk=tk, nkt=nkt, pos0=pos0, n_keys=n_keys, top_k=top_k)
    return pl.pallas_call(
        kern,
        grid=(b, t // tq),
        in_specs=[qspec(ATTN_W), qspec(ATTN_W), qspec(N_IDX_HEADS),
                  kspec(tk, ATTN_W), kspec(ATTN_W, tk), kspec(tk, IDX_DIM), _const_spec(bias.shape)],
        out_specs=qspec(ATTN_W),
        out_shape=jax.ShapeDtypeStruct((b, ATTN_W, t), BF16),
        scratch_shapes=[pltpu.VMEM((nkt + 1, tk, tq), jnp.int32),
                        pltpu.VMEM((N_HEADS, 1, tq), F32),
                        pltpu.VMEM((N_HEADS, 1, tq), F32),
                        pltpu.VMEM((N_HEADS, HEAD_DIM, tq), F32),
                        pltpu.VMEM((N_HEADS, tk, tq), F32)],
        compiler_params=pltpu.CompilerParams(dimension_semantics=("arbitrary", "arbitrary"),
                                             vmem_limit_bytes=VMEM_LIMIT),
        name="sparse_attn",
    )(qt, iqt, iwt, k_t, vt_t, ik_t, bias)


DEC_ROWS = N_HEADS * 16
DEC_KEY_TILE = 1024
NT_DIMS = (((1,), (1,)), ((), ()))


def _dec_select_kernel(iq_ref, iwc_ref, ik_ref, madd_ref, key_scr, *, g, tq, n_keys, n_pad, top_k):
    n_slabs = n_pad // LANES
    col = lax.broadcasted_iota(jnp.int32, (tq, n_pad), 1)
    for s in range(g):
        iq_s = iq_ref[s * tq:(s + 1) * tq, :]
        lhs = jnp.concatenate([iq_s[:, h * IDX_DIM:(h + 1) * IDX_DIM] for h in range(N_IDX_HEADS)], axis=0)
        x = jnp.maximum(lax.dot_general(lhs, ik_ref[s], NT_DIMS, preferred_element_type=F32), 0.0) * iwc_ref[s]
        score = x[0:tq]
        for h in range(1, N_IDX_HEADS):
            score = score + x[h * tq:(h + 1) * tq]
        score = score + 0.0
        bits = pltpu.bitcast(score, jnp.int32)
        key = bits ^ ((bits >> 31) & 0x7FFFFFFF)
        key_scr[s * tq:(s + 1) * tq, :] = jnp.where(col < n_keys, key, INT_MIN)

    rows = g * tq
    lane = lax.broadcasted_iota(jnp.int32, (rows, LANES), 1)

    def count(pred):
        acc = jnp.zeros((rows, LANES), F32)
        for j in range(n_slabs):
            acc = acc + jnp.where(pred(key_scr[:, j * LANES:(j + 1) * LANES], j), 1.0, 0.0)
        return jnp.sum(acc, axis=1, keepdims=True)

    def thr_body(i, t):
        cand = t + (jnp.int32(1) << (31 - i))
        c = count(lambda key, j: key >= cand)
        return jnp.where(c >= top_k, cand, t)

    thr = lax.fori_loop(0, 32, thr_body, jnp.full((rows, 1), INT_MIN, jnp.int32))
    thr = jnp.maximum(thr, INT_MIN + 1)

    n_ge = count(lambda key, j: key >= thr)
    has_tie = jnp.max(n_ge) > top_k

    @pl.when(has_tie)
    def _():
        need_m1 = (top_k - 1) - count(lambda key, j: key > thr)

        def idx_body(i, cut):
            cand = cut + (jnp.int32(1) << (15 - i))
            c = count(lambda key, j: (key == thr) & (j * LANES + lane < cand))
            return jnp.where(c <= need_m1, cand, cut)

        cut = lax.fori_loop(0, 16, idx_body, jnp.zeros((rows, 1), jnp.int32))
        tie_row = n_ge > top_k
        for j in range(n_slabs):
            key = key_scr[:, j * LANES:(j + 1) * LANES]
            drop = tie_row & (key == thr) & (j * LANES + lane > cut)
            key_scr[:, j * LANES:(j + 1) * LANES] = jnp.where(drop, INT_MIN, key)

    madd_ref[...] = jnp.where(key_scr[...] >= thr, 0.0, NEG)


def _dec_attn_kernel(q_ref, maddc_ref, maddn_ref, k_ref, v_ref, kn_ref, vn_ref, biasc_ref, biasn_ref, o_ref,
                     m_scr, l_scr, acc_scr, *, tq):
    kt = pl.program_id(1)
    q = q_ref[...]
    lane = lax.broadcasted_iota(jnp.int32, q.shape, 1)
    in_head = [(lane >= h * HEAD_DIM) & (lane < (h + 1) * HEAD_DIM) for h in range(N_HEADS)]
    qblk = jnp.concatenate([jnp.where(in_head[h], q, jnp.zeros_like(q)) for h in range(N_HEADS)], axis=0)

    @pl.when(kt == 0)
    def _():
        m_scr[...] = jnp.full(m_scr.shape, NEG, F32)
        l_scr[...] = jnp.zeros(l_scr.shape, F32)
        acc_scr[...] = jnp.zeros(acc_scr.shape, F32)

    def step(kb, vb, madd, bias):
        s = lax.dot_general(qblk, kb, NT_DIMS, preferred_element_type=F32)
        s = s + jnp.concatenate([madd] * N_HEADS, axis=0) + bias
        m_prev = m_scr[...]
        m_new = jnp.maximum(m_prev, jnp.max(s, axis=1, keepdims=True))
        alpha = jnp.exp2(m_prev - m_new)
        p = jnp.exp2(s - m_new)
        l_scr[...] = alpha * l_scr[...] + jnp.sum(p, axis=1, keepdims=True)
        acc_scr[...] = alpha * acc_scr[...] + jnp.dot(p.astype(BF16), vb, preferred_element_type=F32)
        m_scr[...] = m_new

    step(k_ref[...].astype(BF16), v_ref[...].astype(BF16), maddc_ref[...], biasc_ref[...])

    @pl.when(kt == pl.num_programs(1) - 1)
    def _():
        step(kn_ref[...], vn_ref[...], maddn_ref[...], biasn_ref[...])
        o = acc_scr[...] / l_scr[...]
        out = jnp.zeros((tq, ATTN_W), F32)
        for h in range(N_HEADS):
            out = out + jnp.where(in_head[h], o[h * tq:(h + 1) * tq, :], 0.0)
        o_ref[...] = out.astype(o_ref.dtype)


def _dec_attn(q, iq, iw, past_k, past_v, past_ik, k_new, v_new, ik_new, rel_bias, pos0):
    b, t, _ = q.shape
    p_len = past_k.shape[1]
    n_keys = p_len + t
    top_k = min(TOPK_MAX, n_keys // 4)
    tkk = min(DEC_KEY_TILE, p_len)
    n_pad = p_len + LANES
    g = min(LANES // t, b)
    assert N_HEADS * t == DEC_ROWS and b % g == 0 and p_len % tkk == 0 and t <= LANES
    assert (pos0 + t - 1) // CHUNK == pos0 // CHUNK and (n_keys - 1) // CHUNK <= pos0 // CHUNK, "every key admissible"
    assert tkk >= MAX_DISTANCE + LANES >= MAX_DISTANCE + t, "only the last cache tile is within MAX_DISTANCE of a query"

    ik_all = jnp.concatenate([past_ik.astype(BF16), ik_new, jnp.zeros((b, LANES - t, IDX_DIM), BF16)], axis=1)
    iw_col = iw.swapaxes(1, 2).reshape(b, DEC_ROWS, 1)
    madd = pl.pallas_call(
        functools.partial(_dec_select_kernel, g=g, tq=t, n_keys=n_keys, n_pad=n_pad, top_k=top_k),
        grid=(b // g,),
        in_specs=[pl.BlockSpec((g * t, ATTN_W), lambda i: (i, 0)),
                  pl.BlockSpec((g, DEC_ROWS, 1), lambda i: (i, 0, 0)),
                  pl.BlockSpec((g, n_pad, IDX_DIM), lambda i: (i, 0, 0))],
        out_specs=pl.BlockSpec((g * t, n_pad), lambda i: (i, 0)),
        out_shape=jax.ShapeDtypeStruct((b * t, n_pad), F32),
        scratch_shapes=[pltpu.VMEM((g * t, n_pad), jnp.int32)],
        compiler_params=pltpu.CompilerParams(dimension_semantics=("arbitrary",), vmem_limit_bytes=VMEM_LIMIT),
        name="dec_select",
    )(iq.reshape(b * t, ATTN_W), iw_col, ik_all)

    n_steps = p_len // tkk
    qpos = jnp.tile(jnp.arange(t, dtype=jnp.int32), N_HEADS)[:, None]
    hsel = jnp.repeat(jnp.arange(N_HEADS), t)
    rows = jnp.arange(DEC_ROWS)
    near = MAX_DISTANCE + LANES
    rel_c = jnp.arange(near, dtype=jnp.int32)[None, :] - near - qpos
    rel_n = jnp.arange(LANES, dtype=jnp.int32)[None, :] - qpos
    bias_last = jnp.pad(_shifted_bias(rel_bias, rel_c)[hsel, rows], ((0, 0), (tkk - near, 0)))
    bias_c = jnp.concatenate([jnp.zeros((n_steps - 1, DEC_ROWS, tkk), F32), bias_last[None]], axis=0)
    bias_n = _shifted_bias(rel_bias, rel_n)[hsel, rows]
    pad_new = lambda a: jnp.pad(a, ((0, 0), (0, LANES - t), (0, 0)))
    return pl.pallas_call(
        functools.partial(_dec_attn_kernel, tq=t),
        grid=(b, n_steps),
        in_specs=[pl.BlockSpec((None, t, ATTN_W), lambda bi, kt: (bi, 0, 0)),
                  pl.BlockSpec((t, tkk), lambda bi, kt: (bi, kt)),
                  pl.BlockSpec((t, LANES), lambda bi, kt: (bi, p_len // LANES)),
                  pl.BlockSpec((None, tkk, ATTN_W), lambda bi, kt: (bi, kt, 0)),
                  pl.BlockSpec((None, tkk, ATTN_W), lambda bi, kt: (bi, kt, 0)),
                  pl.BlockSpec((None, LANES, ATTN_W), lambda bi, kt: (bi, 0, 0)),
                  pl.BlockSpec((None, LANES, ATTN_W), lambda bi, kt: (bi, 0, 0)),
                  pl.BlockSpec((None, DEC_ROWS, tkk), lambda bi, kt: (kt, 0, 0)),
                  _const_spec(bias_n.shape)],
        out_specs=pl.BlockSpec((None, t, ATTN_W), lambda bi, kt: (bi, 0, 0)),
        out_shape=jax.ShapeDtypeStruct((b, t, ATTN_W), BF16),
        scratch_shapes=[pltpu.VMEM((DEC_ROWS, 1), F32), pltpu.VMEM((DEC_ROWS, 1), F32),
                        pltpu.VMEM((DEC_ROWS, ATTN_W), F32)],
        compiler_params=pltpu.CompilerParams(dimension_semantics=("arbitrary", "arbitrary"),
                                             vmem_limit_bytes=VMEM_LIMIT),
        name="dec_attn",
    )(q, madd, madd, past_k, past_v, pad_new(k_new), pad_new(v_new), bias_c, bias_n)


def _conv_kernel(*refs, tm, has_prev):
    if has_prev:
        u_ref, prev_ref, past_ref, w_ref, b_ref, lg_ref, lb_ref, o_ref, ext_scr = refs
        halo = jnp.where(pl.program_id(1) == 0, past_ref[...], prev_ref[...])
    else:
        u_ref, past_ref, w_ref, b_ref, lg_ref, lb_ref, o_ref, ext_scr = refs
        halo = past_ref[...]
    ext_scr[0, 0:HALO, :] = halo
    ext_scr[0, HALO:HALO + tm, :] = u_ref[...]
    n_rows = tm + HALO - SUBLANES
    for r in range(1, SUBLANES):
        ext_scr[r, 0:n_rows, :] = ext_scr[0, r:r + n_rows, :]
    rb = min(tm, 32)
    first = HALO - (CONV_K - 1)
    for r0 in range(0, tm, rb):
        acc = None
        for j in range(CONV_K):
            a, r = divmod(first + j, SUBLANES)
            t = ext_scr[r, r0 + a * SUBLANES:r0 + a * SUBLANES + rb, :] * w_ref[j:j + 1, :]
            acc = t if acc is None else acc + t
        c = acc + b_ref[...]
        mu = jnp.mean(c, axis=-1, keepdims=True)
        cc = c - mu
        var = jnp.mean(cc * cc, axis=-1, keepdims=True)
        y = cc * lax.rsqrt(var + EPS) * lg_ref[...] + lb_ref[...]
        o_ref[r0:r0 + rb, :] = (y * jax.nn.sigmoid(y)).astype(o_ref.dtype)


def _conv_mod(u, past, w):
    b, t, c = u.shape
    tm = min(ROW_TILE, t)
    assert t % tm == 0 and tm % 8 == 0
    has_prev = t > tm
    tile = pl.BlockSpec((None, tm, c), lambda bi, i: (bi, i, 0))
    halo = pl.BlockSpec((None, HALO, c), lambda bi, i: (bi, 0, 0))
    in_specs = [tile]
    args = [u]
    if has_prev:
        r = tm // HALO
        in_specs.append(pl.BlockSpec((None, HALO, c), lambda bi, i: (bi, jnp.maximum(i * r - 1, 0), 0)))
        args.append(u)
    consts = [w["conv_w"], w["conv_b"], w["ln_g"], w["ln_b"]]
    in_specs += [halo] + [_const_spec(x.shape) for x in consts]
    args += [past] + consts
    return pl.pallas_call(
        functools.partial(_conv_kernel, tm=tm, has_prev=has_prev),
        grid=(b, t // tm),
        in_specs=in_specs,
        out_specs=tile,
        out_shape=jax.ShapeDtypeStruct((b, t, c), BF16),
        scratch_shapes=[pltpu.VMEM((SUBLANES, HALO + tm, c), F32)],
        compiler_params=pltpu.CompilerParams(dimension_semantics=("arbitrary", "arbitrary"),
                                             vmem_limit_bytes=VMEM_LIMIT),
        name="conv_mod",
    )(*args)


def _mix_ffn_kernel(x_ref, attn_ref, cact_ref, sga_ref, sgc_ref, wao_ref, wco_ref, wo_ref, g_ref,
                    wg_ref, wu_ref, wd_ref, y_ref, *, n_chunks):
    attn_out = jnp.dot(attn_ref[...], wao_ref[...], preferred_element_type=F32)
    conv_out = jnp.dot(cact_ref[...], wco_ref[...], preferred_element_type=F32)
    merged = sga_ref[...].astype(F32) * attn_out + sgc_ref[...].astype(F32) * conv_out
    x1 = x_ref[...] + jnp.dot(merged.astype(BF16), wo_ref[...], preferred_element_type=F32)
    ms = jnp.mean(x1 * x1, axis=-1, keepdims=True)
    h2 = (x1 * lax.rsqrt(ms + EPS) * g_ref[...]).astype(BF16)
    hc = wg_ref.shape[1] // n_chunks
    y = x1
    for ci in range(n_chunks):
        sl = slice(ci * hc, (ci + 1) * hc)
        gate = jnp.dot(h2, wg_ref[:, sl], preferred_element_type=F32)
        up = jnp.dot(h2, wu_ref[:, sl], preferred_element_type=F32)
        act = (gate * jax.nn.sigmoid(gate) * up).astype(BF16)
        y = y + jnp.dot(act, wd_ref[sl, :], preferred_element_type=F32)
    y_ref[...] = y


def _mix_ffn(x, attn, cact, sga, sgc, w):
    m, d = x.shape
    tm = min(ROW_TILE, m)
    assert m % tm == 0
    hidden = w["wg"].shape[1]
    n_chunks = 2 if hidden % (2 * LANES) == 0 else 1
    row = lambda n: pl.BlockSpec((tm, n), lambda i: (i, 0))
    consts = [w[n] for n in ("wao", "wco", "wo", "g_ffn", "wg", "wu", "wd")]
    return pl.pallas_call(
        functools.partial(_mix_ffn_kernel, n_chunks=n_chunks),
        grid=(m // tm,),
        in_specs=[row(d), row(attn.shape[1]), row(cact.shape[1]), row(d), row(d)]
                 + [_const_spec(c.shape) for c in consts],
        out_specs=row(d),
        out_shape=jax.ShapeDtypeStruct((m, d), F32),
        compiler_params=pltpu.CompilerParams(dimension_semantics=("arbitrary",), vmem_limit_bytes=VMEM_LIMIT),
        name="mix_ffn",
    )(x, attn, cact, sga, sgc, *consts)


def _prep_weights(norm_mix_g, w_in, q_norm_g, k_norm_g, idx_k_norm_g, conv_dw_w, conv_dw_b, conv_ln_g, conv_ln_b,
                  w_conv_out, w_attn_out, w_out, norm_ffn_g, w_ffn_gate, w_ffn_up, w_ffn_down):
    d = w_in.shape[0]
    conv_ch = conv_dw_w.shape[-1]
    sizes = (ATTN_W, ATTN_W, ATTN_W, N_IDX_HEADS * IDX_DIM, IDX_DIM, N_IDX_HEADS, 2 * conv_ch, d, d)
    offs = np.concatenate([[0], np.cumsum(sizes)])
    assert offs[-1] == w_in.shape[1]
    col = lambda i: w_in[:, offs[i]:offs[i + 1]]
    wsm = jnp.concatenate([col(4), col(5), jnp.zeros((d, LANES - IDX_DIM - N_IDX_HEADS), w_in.dtype)], axis=1)
    head = np.arange(ATTN_W) // HEAD_DIM
    seg = (head[:, None] == np.arange(LANES)[None, :]).astype(np.float32)
    return dict(
        g_mix=norm_mix_g.reshape(1, d).astype(F32),
        wq=col(0).astype(BF16), wk=col(1).astype(BF16), wv=col(2).astype(BF16), wiq=col(3).astype(BF16),
        wsm=wsm.astype(BF16), wglu=col(6).astype(BF16), wga=col(7).astype(BF16), wgc=col(8).astype(BF16),
        qg=jnp.tile(q_norm_g.astype(F32), N_HEADS).reshape(1, ATTN_W),
        kg=jnp.tile(k_norm_g.astype(F32), N_HEADS).reshape(1, ATTN_W),
        ikg=jnp.concatenate([idx_k_norm_g.astype(F32), jnp.zeros((LANES - IDX_DIM,), F32)]).reshape(1, LANES),
        seg=jnp.asarray(seg, BF16), segt=jnp.asarray(seg.T, BF16),
        conv_w=conv_dw_w.reshape(CONV_K, conv_ch).astype(F32), conv_b=conv_dw_b.reshape(1, conv_ch).astype(F32),
        ln_g=conv_ln_g.reshape(1, conv_ch).astype(F32), ln_b=conv_ln_b.reshape(1, conv_ch).astype(F32),
        wao=w_attn_out.astype(BF16), wco=w_conv_out.astype(BF16), wo=w_out.astype(BF16),
        g_ffn=norm_ffn_g.reshape(1, d).astype(F32),
        wg=w_ffn_gate.astype(BF16), wu=w_ffn_up.astype(BF16), wd=w_ffn_down.astype(BF16),
    )


def _layer(x, pos0, past_k, past_v, past_ik, past_conv, rel_bias, w):
    b, t, d = x.shape
    m = b * t
    q, k, k_b, v, v_b, iq, sm, sm_b, u, sga, sgc = _in_proj(x.reshape(m, d), w)
    conv_ch = u.shape[1]
    u = u.reshape(b, t, conv_ch)
    sm = sm.reshape(b, t, LANES)

    q = q.reshape(b, t, ATTN_W)
    iq = iq.reshape(b, t, ATTN_W)
    k_b = k_b.reshape(b, t, ATTN_W)
    v_b = v_b.reshape(b, t, ATTN_W)
    ik_b = sm_b.reshape(b, t, LANES)[:, :, :IDX_DIM]
    iw = sm[:, :, IDX_DIM:IDX_DIM + N_IDX_HEADS]
    if past_k is None:
        attn = _sparse_attn(q.swapaxes(1, 2), iq.swapaxes(1, 2), iw.swapaxes(1, 2),
                            k_b, v_b, ik_b, rel_bias, pos0).swapaxes(1, 2)
    else:
        p = past_k.shape[1]
        attn = _dec_attn(q, iq, iw, past_k.reshape(b, p, ATTN_W), past_v.reshape(b, p, ATTN_W), past_ik,
                         k_b, v_b, ik_b, rel_bias, pos0)

    past = jnp.pad(past_conv.astype(F32), ((0, 0), (HALO - (CONV_K - 1), 0), (0, 0)))
    cact = _conv_mod(u, past, w)
    y = _mix_ffn(x.reshape(m, d), attn.reshape(m, ATTN_W), cact.reshape(m, conv_ch), sga, sgc, w)

    new_conv = jnp.concatenate([past_conv, u], axis=1)[:, -(CONV_K - 1):]
    return (y.reshape(b, t, d), k.reshape(b, t, N_HEADS, HEAD_DIM), v.reshape(b, t, N_HEADS, HEAD_DIM),
            sm[:, :, :IDX_DIM], new_conv)


def kernel(x_prompt, x_sample, cache_k, cache_v, cache_idx_k, state_conv, rel_bias, norm_mix_g, w_in, q_norm_g, k_norm_g, idx_k_norm_g, conv_dw_w, conv_dw_b, conv_ln_g, conv_ln_b, w_conv_out, w_attn_out, w_out, norm_ffn_g, w_ffn_gate, w_ffn_up, w_ffn_down):
    depth = w_in.shape[0]
    bp = x_prompt.shape[0]
    past_len = cache_k.shape[2]
    conv_ch = conv_dw_w.shape[-1]
    yp, ys = x_prompt, x_sample
    outs_p, outs_s = [], []
    for l in range(depth):
        w = _prep_weights(norm_mix_g[l], w_in[l], q_norm_g[l], k_norm_g[l], idx_k_norm_g[l], conv_dw_w[l],
                          conv_dw_b[l], conv_ln_g[l], conv_ln_b[l], w_conv_out[l], w_attn_out[l], w_out[l],
                          norm_ffn_g[l], w_ffn_gate[l], w_ffn_up[l], w_ffn_down[l])
        zero_conv = jnp.zeros((bp, CONV_K - 1, conv_ch), yp.dtype)
        yp, *rest_p = _layer(yp, 0, None, None, None, zero_conv, rel_bias, w)
        outs_p.append(rest_p)
        ys, *rest_s = _layer(ys, past_len, cache_k[l], cache_v[l], cache_idx_k[l], state_conv[l], rel_bias, w)
        outs_s.append(rest_s)
    stack = lambda outs, i: jnp.stack([o[i] for o in outs])
    return (yp, ys, stack(outs_p, 0), stack(outs_p, 1), stack(outs_p, 2), stack(outs_p, 3),
            stack(outs_s, 0), stack(outs_s, 1), stack(outs_s, 2), stack(outs_s, 3))
```

```python
import functools
import math

import numpy as np
import jax
import jax.numpy as jnp
from jax import lax
from jax.experimental import pallas as pl
from jax.experimental.pallas import tpu as pltpu

CHUNK = 64
CHUNK_SHIFT = 6
N_HEADS = 8
HEAD_DIM = 64
ATTN_W = N_HEADS * HEAD_DIM
N_IDX_HEADS = 8
IDX_DIM = 64
TOPK_MAX = 256
CONV_K = 31
NUM_BUCKETS = 32
MAX_DISTANCE = 128
EPS = 1e-6
NEG = -1e30
INT_MIN = -(2 ** 31)
LOG2E = math.log2(math.e)

LANES = 128
SUBLANES = 8
HALO = 32
KEY_TILE = 256
V_ROWS = HEAD_DIM + 16
ROW_TILE = 256
VMEM_LIMIT = 56 * 1024 * 1024

F32 = jnp.float32
BF16 = jnp.bfloat16


def _const_spec(shape):
    return pl.BlockSpec(shape, lambda *_: (0,) * len(shape), pipeline_mode=pl.Buffered(1))


def _split_dot(x, m):
    hi = x.astype(BF16)
    lo = (x - hi.astype(F32)).astype(BF16)
    return jnp.dot(hi, m, preferred_element_type=F32) + jnp.dot(lo, m, preferred_element_type=F32)


def _in_proj_kernel(x_ref, g_ref, wq_ref, wk_ref, wv_ref, wiq_ref, wsm_ref, wglu_ref, wga_ref, wgc_ref,
                    qg_ref, kg_ref, ikg_ref, seg_ref, segt_ref,
                    q_ref, k_ref, kb_ref, v_ref, vb_ref, iq_ref, sm_ref, smb_ref, u_ref, sga_ref, sgc_ref):
    x = x_ref[...]
    ms = jnp.mean(x * x, axis=-1, keepdims=True)
    h = (x * lax.rsqrt(ms + EPS) * g_ref[...]).astype(BF16)

    def head_rms(y, gain):
        ss = _split_dot(y * y, seg_ref[...])
        r = lax.rsqrt(ss * (1.0 / HEAD_DIM) + EPS)
        return y * _split_dot(r, segt_ref[...]) * gain

    q = head_rms(jnp.dot(h, wq_ref[...], preferred_element_type=F32), qg_ref[...])
    q_ref[...] = (q * (HEAD_DIM ** -0.5 * LOG2E)).astype(BF16)
    k = head_rms(jnp.dot(h, wk_ref[...], preferred_element_type=F32), kg_ref[...])
    k_ref[...] = k
    kb_ref[...] = k.astype(BF16)
    v = jnp.dot(h, wv_ref[...], preferred_element_type=F32)
    v_ref[...] = v
    vb_ref[...] = v.astype(BF16)
    iq = jnp.dot(h, wiq_ref[...], preferred_element_type=F32)
    iq_ref[...] = (iq * (IDX_DIM ** -0.5)).astype(BF16)

    sm = jnp.dot(h, wsm_ref[...], preferred_element_type=F32)
    lane = lax.broadcasted_iota(jnp.int32, sm.shape, 1)
    is_ik = lane < IDX_DIM
    ss = jnp.sum(jnp.where(is_ik, sm * sm, 0.0), axis=-1, keepdims=True)
    ikn = sm * lax.rsqrt(ss * (1.0 / IDX_DIM) + EPS) * ikg_ref[...]
    iw = sm * (N_IDX_HEADS ** -0.5)
    smo = jnp.where(is_ik, ikn, jnp.where(lane < IDX_DIM + N_IDX_HEADS, iw, 0.0))
    sm_ref[...] = smo
    smb_ref[...] = smo.astype(BF16)

    glu = jnp.dot(h, wglu_ref[...], preferred_element_type=F32)
    c = glu.shape[1] // 2
    u_ref[...] = glu[:, :c] * jax.nn.sigmoid(glu[:, c:])
    sga_ref[...] = jax.nn.sigmoid(jnp.dot(h, wga_ref[...], preferred_element_type=F32)).astype(BF16)
    sgc_ref[...] = jax.nn.sigmoid(jnp.dot(h, wgc_ref[...], preferred_element_type=F32)).astype(BF16)


def _in_proj(x, w):
    m, d = x.shape
    tm = min(ROW_TILE, m)
    assert m % tm == 0
    conv_ch = w["wglu"].shape[1] // 2
    row = lambda n: pl.BlockSpec((tm, n), lambda i: (i, 0))
    consts = [w[n] for n in ("g_mix", "wq", "wk", "wv", "wiq", "wsm", "wglu", "wga", "wgc",
                             "qg", "kg", "ikg", "seg", "segt")]
    out_shape = [
        jax.ShapeDtypeStruct((m, ATTN_W), BF16),
        jax.ShapeDtypeStruct((m, ATTN_W), F32),
        jax.ShapeDtypeStruct((m, ATTN_W), BF16),
        jax.ShapeDtypeStruct((m, ATTN_W), F32),
        jax.ShapeDtypeStruct((m, ATTN_W), BF16),
        jax.ShapeDtypeStruct((m, ATTN_W), BF16),
        jax.ShapeDtypeStruct((m, LANES), F32),
        jax.ShapeDtypeStruct((m, LANES), BF16),
        jax.ShapeDtypeStruct((m, conv_ch), F32),
        jax.ShapeDtypeStruct((m, d), BF16),
        jax.ShapeDtypeStruct((m, d), BF16),
    ]
    return pl.pallas_call(
        _in_proj_kernel,
        grid=(m // tm,),
        in_specs=[row(d)] + [_const_spec(c.shape) for c in consts],
        out_specs=[row(s.shape[1]) for s in out_shape],
        out_shape=out_shape,
        compiler_params=pltpu.CompilerParams(dimension_semantics=("arbitrary",), vmem_limit_bytes=VMEM_LIMIT),
        name="in_proj",
    )(x, *consts)


def _sparse_attn_kernel(qt_ref, iqt_ref, iwt_ref, k_ref, vt_ref, ik_ref, bias_ref, o_ref,
                        key_scr, m_scr, acc_scr, s_scr, *, tq, tk, nkt, pos0, n_keys, top_k):
    qi = pl.program_id(1)
    q0 = pos0 + qi * tq
    n_kt = jnp.minimum(nkt, ((q0 + tq - 1) // CHUNK * CHUNK + CHUNK + tk - 1) // tk)
    last = n_kt - 1

    def fold(x, op):
        return op(x.reshape(tk // SUBLANES, SUBLANES, tq), axis=0)

    iqt = iqt_ref[...]
    iqt_h = [iqt[h * IDX_DIM:(h + 1) * IDX_DIM, :] for h in range(N_IDX_HEADS)]
    iwt = iwt_ref[...]
    iw_h = [iwt[h:h + 1, :] for h in range(N_IDX_HEADS)]

    def score_tile(kt, masked):
        ik = ik_ref[kt]
        s = None
        for h in range(N_IDX_HEADS):
            t = jnp.maximum(jnp.dot(ik, iqt_h[h], preferred_element_type=F32), 0.0) * iw_h[h]
            s = t if s is None else s + t
        s = s + 0.0
        bits = pltpu.bitcast(s, jnp.int32)
        key = bits ^ ((bits >> 31) & 0x7FFFFFFF)
        if masked:
            kp = kt * tk + lax.broadcasted_iota(jnp.int32, (tk, tq), 0)
            qp = q0 + lax.broadcasted_iota(jnp.int32, (tk, tq), 1)
            adm = ((kp >> CHUNK_SHIFT) <= (qp >> CHUNK_SHIFT)) & (kp < n_keys)
            key = jnp.where(adm, key, INT_MIN)
        key_scr[kt] = key

    def score_body(kt, carry):
        score_tile(kt, False)
        return carry

    lax.fori_loop(0, last, score_body, 0)
    score_tile(last, True)

    @pl.when(n_kt % 2 == 1)
    def _():
        key_scr[n_kt] = jnp.full((tk, tq), INT_MIN, jnp.int32)

    def count(pred):
        def body(i, acc):
            for kt in (2 * i, 2 * i + 1):
                acc = acc + fold(jnp.where(pred(key_scr[kt], kt), 1.0, 0.0), jnp.sum)
            return acc
        acc = lax.fori_loop(0, (n_kt + 1) // 2, body, jnp.zeros((SUBLANES, tq), F32))
        return jnp.sum(acc, axis=0, keepdims=True)

    def thr_body(i, t):
        cand = t + (jnp.int32(1) << (31 - i))
        c = count(lambda key, kt: key >= cand)
        return jnp.where(c >= top_k, cand, t)

    thr = lax.fori_loop(0, 32, thr_body, jnp.full((1, tq), INT_MIN, jnp.int32))
    thr = jnp.maximum(thr, INT_MIN + 1)

    n_ge = count(lambda key, kt: key >= thr)
    has_tie = jnp.max(n_ge) > top_k

    @pl.when(has_tie)
    def _():
        need_m1 = (top_k - 1) - count(lambda key, kt: key > thr)
        row = lax.broadcasted_iota(jnp.int32, (tk, tq), 0)

        def idx_body(i, cut):
            cand = cut + (jnp.int32(1) << (15 - i))
            c = count(lambda key, kt: (key == thr) & (kt * tk + row < cand))
            return jnp.where(c <= need_m1, cand, cut)

        cut = lax.fori_loop(0, 16, idx_body, jnp.zeros((1, tq), jnp.int32))
        tie_col = n_ge > top_k

        def drop_body(kt, carry):
            key = key_scr[kt]
            drop = tie_col & (key == thr) & (kt * tk + row > cut)
            key_scr[kt] = jnp.where(drop, INT_MIN, key)
            return carry

        lax.fori_loop(0, n_kt, drop_body, 0)

    qt = qt_ref[...]
    slab_row = lax.broadcasted_iota(jnp.int32, (LANES, tq), 0)
    qz = []
    for h in range(N_HEADS):
        slab = qt[(h // 2) * LANES:(h // 2 + 1) * LANES, :]
        mine = (slab_row < HEAD_DIM) if h % 2 == 0 else (slab_row >= HEAD_DIM)
        qz.append(jnp.where(mine, slab, jnp.zeros_like(slab)))
    m_scr[...] = jnp.full(m_scr.shape, NEG, F32)
    acc_scr[...] = jnp.zeros(acc_scr.shape, F32)

    def attn_tile(kt, bias_idx):
        madd = jnp.where(key_scr[kt] >= thr, 0.0, NEG)
        m_new = []
        for h in range(N_HEADS):
            pair = h // 2
            s = jnp.dot(k_ref[kt, :, pair * LANES:(pair + 1) * LANES], qz[h], preferred_element_type=F32)
            if bias_idx is not None:
                s = s + bias_ref[bias_idx, h]
            s = s + madd
            s_scr[h] = s
            m_new.append(jnp.maximum(m_scr[h], jnp.max(fold(s, jnp.max), axis=0, keepdims=True)))
        for h in range(N_HEADS):
            alpha = jnp.exp2(m_scr[h] - m_new[h])
            p = jnp.exp2(s_scr[h] - m_new[h])
            pv = jnp.dot(vt_ref[kt, h * V_ROWS:(h + 1) * V_ROWS, :], p.astype(BF16), preferred_element_type=F32)
            acc_scr[h] = alpha * acc_scr[h] + pv
            m_scr[h] = m_new[h]

    def far_body(kt, carry):
        attn_tile(kt, None)
        return carry

    lax.fori_loop(0, n_kt - 2, far_body, 0)

    @pl.when(n_kt >= 2)
    def _():
        attn_tile(n_kt - 2, 0)

    attn_tile(last, 1)

    for h in range(N_HEADS):
        acc = acc_scr[h]
        o_ref[h * HEAD_DIM:(h + 1) * HEAD_DIM, :] = (acc[:HEAD_DIM] / acc[HEAD_DIM:HEAD_DIM + 1]).astype(o_ref.dtype)


def _rel_bucket(rel):
    nb = NUM_BUCKETS // 2
    max_exact = nb // 2
    ret = jnp.where(rel > 0, nb, 0)
    n = jnp.abs(rel)
    nf = jnp.maximum(n, 1).astype(jnp.float32)
    large = max_exact + (jnp.log(nf / max_exact) / math.log(MAX_DISTANCE / max_exact) * (nb - max_exact)).astype(jnp.int32)
    large = jnp.minimum(large, nb - 1)
    return ret + jnp.where(n < max_exact, n, large)


def _bias_tiles(rel_bias, tq, tk):
    j = jnp.arange(tk, dtype=jnp.int32)[:, None]
    i = jnp.arange(tq, dtype=jnp.int32)[None, :]
    return jnp.stack([_shifted_bias(rel_bias, j - i + d) for d in (-tk, 0)])


def _shifted_bias(rel_bias, rel):
    rb = (rel_bias.astype(F32) - rel_bias[_rel_bucket(jnp.int32(-MAX_DISTANCE))].astype(F32)) * LOG2E
    onehot = jax.nn.one_hot(_rel_bucket(rel), NUM_BUCKETS, dtype=F32)
    return jnp.moveaxis(jnp.einsum("...b,bh->...h", onehot, rb, precision=lax.Precision.HIGHEST), -1, 0)


def _check_tiling(t, tq, tk, nkt, pos0, n_keys):
    for qi in range(t // tq):
        q0 = pos0 + qi * tq
        n_kt = min(nkt, ((q0 + tq - 1) // CHUNK * CHUNK + CHUNK + tk - 1) // tk)
        adm_end = min(q0 // CHUNK * CHUNK + CHUNK, n_keys)
        assert (n_kt - 1) * tk <= adm_end, "only the last visited tile may hold inadmissible keys"
        assert n_kt * tk >= min((q0 + tq - 1) // CHUNK * CHUNK + CHUNK, n_keys), "visited tiles cover every admissible key"
        assert (n_kt - 1) * tk == q0, "last tile starts with the query tile"
        assert tk >= MAX_DISTANCE, "tiles before the last two are at least MAX_DISTANCE behind"


def _sparse_attn(qt, iqt, iwt, k_b, v_b, ik_b, rel_bias, pos0):
    b, _, t = qt.shape
    n_keys = k_b.shape[1]
    top_k = min(TOPK_MAX, n_keys // 4)
    tk = KEY_TILE
    tq = min(tk, t)
    nkt = pl.cdiv(n_keys, tk)
    _check_tiling(t, tq, tk, nkt, pos0, n_keys)
    pad = nkt * tk - n_keys
    if pad:
        k_b, v_b, ik_b = (jnp.pad(a, ((0, 0), (0, pad), (0, 0))) for a in (k_b, v_b, ik_b))
    k_t = k_b.reshape(b, nkt, tk, ATTN_W)
    vt_t = v_b.reshape(b, nkt, tk, N_HEADS, HEAD_DIM).transpose(0, 1, 3, 4, 2)
    extra = jnp.zeros((b, nkt, N_HEADS, V_ROWS - HEAD_DIM, tk), BF16).at[:, :, :, 0, :].set(1.0)
    vt_t = jnp.concatenate([vt_t, extra], axis=3).reshape(b, nkt, N_HEADS * V_ROWS, tk)
    ik_t = ik_b.reshape(b, nkt, tk, IDX_DIM)
    bias = _bias_tiles(rel_bias, tq, tk)

    qspec = lambda n: pl.BlockSpec((None, n, tq), lambda bi, qi: (bi, 0, qi))
    kspec = lambda r, c: pl.BlockSpec((None, nkt, r, c), lambda bi, qi: (bi, 0, 0, 0))
    kern = functools.partial(_sparse_attn_kernel, tq=tq, tk=tk, nkt=nkt, pos0=pos0, n_keys=n_keys, top_k=top_k)
    return pl.pallas_call(
        kern,
        grid=(b, t // tq),
        in_specs=[qspec(ATTN_W), qspec(ATTN_W), qspec(N_IDX_HEADS),
                  kspec(tk, ATTN_W), kspec(N_HEADS * V_ROWS, tk), kspec(tk, IDX_DIM), _const_spec(bias.shape)],
        out_specs=qspec(ATTN_W),
        out_shape=jax.ShapeDtypeStruct((b, ATTN_W, t), BF16),
        scratch_shapes=[pltpu.VMEM((nkt + 1, tk, tq), jnp.int32),
                        pltpu.VMEM((N_HEADS, 1, tq), F32),
                        pltpu.VMEM((N_HEADS, V_ROWS, tq), F32),
                        pltpu.VMEM((N_HEADS, tk, tq), F32)],
        compiler_params=pltpu.CompilerParams(dimension_semantics=("arbitrary", "arbitrary"),
                                             vmem_limit_bytes=VMEM_LIMIT),
        name="sparse_attn",
    )(qt, iqt, iwt, k_t, vt_t, ik_t, bias)


DEC_ROWS = N_HEADS * 16
DEC_KEY_TILE = 1024
NT_DIMS = (((1,), (1,)), ((), ()))


def _dec_select_kernel(iq_ref, iwc_ref, ik_ref, madd_ref, key_scr, *, g, tq, n_keys, n_pad, top_k):
    n_slabs = n_pad // LANES
    col = lax.broadcasted_iota(jnp.int32, (tq, n_pad), 1)
    for s in range(g):
        iq_s = iq_ref[s * tq:(s + 1) * tq, :]
        lhs = jnp.concatenate([iq_s[:, h * IDX_DIM:(h + 1) * IDX_DIM] for h in range(N_IDX_HEADS)], axis=0)
        x = jnp.maximum(lax.dot_general(lhs, ik_ref[s], NT_DIMS, preferred_element_type=F32), 0.0) * iwc_ref[s]
        score = x[0:tq]
        for h in range(1, N_IDX_HEADS):
            score = score + x[h * tq:(h + 1) * tq]
        score = score + 0.0
        bits = pltpu.bitcast(score, jnp.int32)
        key = bits ^ ((bits >> 31) & 0x7FFFFFFF)
        key_scr[s * tq:(s + 1) * tq, :] = jnp.where(col < n_keys, key, INT_MIN)

    rows = g * tq
    lane = lax.broadcasted_iota(jnp.int32, (rows, LANES), 1)

    def count(pred):
        acc = jnp.zeros((rows, LANES), F32)
        for j in range(n_slabs):
            acc = acc + jnp.where(pred(key_scr[:, j * LANES:(j + 1) * LANES], j), 1.0, 0.0)
        return jnp.sum(acc, axis=1, keepdims=True)

    def thr_body(i, t):
        cand = t + (jnp.int32(1) << (31 - i))
        c = count(lambda key, j: key >= cand)
        return jnp.where(c >= top_k, cand, t)

    thr = lax.fori_loop(0, 32, thr_body, jnp.full((rows, 1), INT_MIN, jnp.int32))
    thr = jnp.maximum(thr, INT_MIN + 1)

    n_ge = count(lambda key, j: key >= thr)
    has_tie = jnp.max(n_ge) > top_k

    @pl.when(has_tie)
    def _():
        need_m1 = (top_k - 1) - count(lambda key, j: key > thr)

        def idx_body(i, cut):
            cand = cut + (jnp.int32(1) << (15 - i))
            c = count(lambda key, j: (key == thr) & (j * LANES + lane < cand))
            return jnp.where(c <= need_m1, cand, cut)

        cut = lax.fori_loop(0, 16, idx_body, jnp.zeros((rows, 1), jnp.int32))
        tie_row = n_ge > top_k
        for j in range(n_slabs):
            key = key_scr[:, j * LANES:(j + 1) * LANES]
            drop = tie_row & (key == thr) & (j * LANES + lane > cut)
            key_scr[:, j * LANES:(j + 1) * LANES] = jnp.where(drop, INT_MIN, key)

    madd_ref[...] = jnp.where(key_scr[...] >= thr, 0.0, NEG)


def _dec_attn_kernel(q_ref, maddc_ref, maddn_ref, k_ref, v_ref, kn_ref, vn_ref, biasc_ref, biasn_ref, o_ref,
                     m_scr, l_scr, acc_scr, *, tq):
    kt = pl.program_id(1)
    q = q_ref[...]
    q_h = [q[:, h * HEAD_DIM:(h + 1) * HEAD_DIM] for h in range(N_HEADS)]

    @pl.when(kt == 0)
    def _():
        m_scr[...] = jnp.full(m_scr.shape, NEG, F32)
        l_scr[...] = jnp.zeros(l_scr.shape, F32)
        acc_scr[...] = jnp.zeros(acc_scr.shape, F32)

    def step(kr, vr, madd, bias):
        n = kr.shape[0] // N_HEADS
        head = lambda r, h: r[pl.ds(h, n, stride=N_HEADS), :].astype(BF16)
        s = jnp.concatenate([lax.dot_general(q_h[h], head(kr, h), NT_DIMS, preferred_element_type=F32)
                             for h in range(N_HEADS)], axis=0)
        s = s + jnp.concatenate([madd] * N_HEADS, axis=0) + bias
        m_prev = m_scr[...]
        m_new = jnp.maximum(m_prev, jnp.max(s, axis=1, keepdims=True))
        alpha = jnp.exp2(m_prev - m_new)
        p = jnp.exp2(s - m_new)
        l_scr[...] = alpha * l_scr[...] + jnp.sum(p, axis=1, keepdims=True)
        pb = p.astype(BF16)
        pv = jnp.concatenate([jnp.dot(pb[h * tq:(h + 1) * tq, :], head(vr, h), preferred_element_type=F32)
                              for h in range(N_HEADS)], axis=0)
        acc_scr[...] = alpha * acc_scr[...] + pv
        m_scr[...] = m_new

    step(k_ref, v_ref, maddc_ref[...], biasc_ref[...])

    @pl.when(kt == pl.num_programs(1) - 1)
    def _():
        step(kn_ref, vn_ref, maddn_ref[...], biasn_ref[...])
        o = acc_scr[...] / l_scr[...]
        for h in range(N_HEADS):
            o_ref[:, h * HEAD_DIM:(h + 1) * HEAD_DIM] = o[h * tq:(h + 1) * tq, :].astype(o_ref.dtype)


def _dec_attn(q, iq, iw, past_k, past_v, past_ik, k_new, v_new, ik_new, rel_bias, pos0):
    b, t, _ = q.shape
    p_len = past_k.shape[1]
    n_keys = p_len + t
    top_k = min(TOPK_MAX, n_keys // 4)
    tkk = min(DEC_KEY_TILE, p_len)
    n_pad = p_len + LANES
    g = min(LANES // t, b)
    assert N_HEADS * t == DEC_ROWS and b % g == 0 and p_len % tkk == 0 and t <= LANES
    assert (pos0 + t - 1) // CHUNK == pos0 // CHUNK and (n_keys - 1) // CHUNK <= pos0 // CHUNK, "every key admissible"
    assert tkk >= MAX_DISTANCE + LANES >= MAX_DISTANCE + t, "only the last cache tile is within MAX_DISTANCE of a query"

    ik_all = jnp.concatenate([past_ik.astype(BF16), ik_new, jnp.zeros((b, LANES - t, IDX_DIM), BF16)], axis=1)
    iw_col = iw.swapaxes(1, 2).reshape(b, DEC_ROWS, 1)
    madd = pl.pallas_call(
        functools.partial(_dec_select_kernel, g=g, tq=t, n_keys=n_keys, n_pad=n_pad, top_k=top_k),
        grid=(b // g,),
        in_specs=[pl.BlockSpec((g * t, ATTN_W), lambda i: (i, 0)),
                  pl.BlockSpec((g, DEC_ROWS, 1), lambda i: (i, 0, 0)),
                  pl.BlockSpec((g, n_pad, IDX_DIM), lambda i: (i, 0, 0))],
        out_specs=pl.BlockSpec((g * t, n_pad), lambda i: (i, 0)),
        out_shape=jax.ShapeDtypeStruct((b * t, n_pad), F32),
        scratch_shapes=[pltpu.VMEM((g * t, n_pad), jnp.int32)],
        compiler_params=pltpu.CompilerParams(dimension_semantics=("arbitrary",), vmem_limit_bytes=VMEM_LIMIT),
        name="dec_select",
    )(iq.reshape(b * t, ATTN_W), iw_col, ik_all)

    n_steps = p_len // tkk
    qpos = jnp.tile(jnp.arange(t, dtype=jnp.int32), N_HEADS)[:, None]
    hsel = jnp.repeat(jnp.arange(N_HEADS), t)
    rows = jnp.arange(DEC_ROWS)
    near = MAX_DISTANCE + LANES
    rel_c = jnp.arange(near, dtype=jnp.int32)[None, :] - near - qpos
    rel_n = jnp.arange(LANES, dtype=jnp.int32)[None, :] - qpos
    bias_last = jnp.pad(_shifted_bias(rel_bias, rel_c)[hsel, rows], ((0, 0), (tkk - near, 0)))
    bias_c = jnp.concatenate([jnp.zeros((n_steps - 1, DEC_ROWS, tkk), F32), bias_last[None]], axis=0)
    bias_n = _shifted_bias(rel_bias, rel_n)[hsel, rows]
    rows_kh = lambda a: a.reshape(b, -1, HEAD_DIM)
    pad_new = lambda a: rows_kh(jnp.pad(a.reshape(b, t, N_HEADS, HEAD_DIM), ((0, 0), (0, LANES - t), (0, 0), (0, 0))))
    kv_tile = pl.BlockSpec((None, tkk * N_HEADS, HEAD_DIM), lambda bi, kt: (bi, kt, 0))
    kv_new = pl.BlockSpec((None, LANES * N_HEADS, HEAD_DIM), lambda bi, kt: (bi, 0, 0))
    return pl.pallas_call(
        functools.partial(_dec_attn_kernel, tq=t),
        grid=(b, n_steps),
        in_specs=[pl.BlockSpec((None, t, ATTN_W), lambda bi, kt: (bi, 0, 0)),
                  pl.BlockSpec((t, tkk), lambda bi, kt: (bi, kt)),
                  pl.BlockSpec((t, LANES), lambda bi, kt: (bi, p_len // LANES)),
                  kv_tile, kv_tile, kv_new, kv_new,
                  pl.BlockSpec((None, DEC_ROWS, tkk), lambda bi, kt: (kt, 0, 0)),
                  _const_spec(bias_n.shape)],
        out_specs=pl.BlockSpec((None, t, ATTN_W), lambda bi, kt: (bi, 0, 0)),
        out_shape=jax.ShapeDtypeStruct((b, t, ATTN_W), BF16),
        scratch_shapes=[pltpu.VMEM((DEC_ROWS, 1), F32), pltpu.VMEM((DEC_ROWS, 1), F32),
                        pltpu.VMEM((DEC_ROWS, HEAD_DIM), F32)],
        compiler_params=pltpu.CompilerParams(dimension_semantics=("arbitrary", "arbitrary"),
                                             vmem_limit_bytes=VMEM_LIMIT),
        name="dec_attn",
    )(q, madd, madd, rows_kh(past_k), rows_kh(past_v), pad_new(k_new), pad_new(v_new), bias_c, bias_n)


def _conv_kernel(*refs, tm, has_prev):
    if has_prev:
        u_ref, prev_ref, past_ref, w_ref, b_ref, lg_ref, lb_ref, o_ref, ext_scr = refs
        halo = jnp.where(pl.program_id(1) == 0, past_ref[...], prev_ref[...])
    else:
        u_ref, past_ref, w_ref, b_ref, lg_ref, lb_ref, o_ref, ext_scr = refs
        halo = past_ref[...]
    ext_scr[0, 0:HALO, :] = halo
    ext_scr[0, HALO:HALO + tm, :] = u_ref[...]
    n_rows = tm + HALO - SUBLANES
    for r in range(1, SUBLANES):
        ext_scr[r, 0:n_rows, :] = ext_scr[0, r:r + n_rows, :]
    rb = min(tm, 32)
    first = HALO - (CONV_K - 1)
    for r0 in range(0, tm, rb):
        acc = None
        for j in range(CONV_K):
            a, r = divmod(first + j, SUBLANES)
            t = ext_scr[r, r0 + a * SUBLANES:r0 + a * SUBLANES + rb, :] * w_ref[j:j + 1, :]
            acc = t if acc is None else acc + t
        c = acc + b_ref[...]
        mu = jnp.mean(c, axis=-1, keepdims=True)
        cc = c - mu
        var = jnp.mean(cc * cc, axis=-1, keepdims=True)
        y = cc * lax.rsqrt(var + EPS) * lg_ref[...] + lb_ref[...]
        o_ref[r0:r0 + rb, :] = (y * jax.nn.sigmoid(y)).astype(o_ref.dtype)


def _conv_mod(u, past, w):
    b, t, c = u.shape
    tm = min(ROW_TILE, t)
    assert t % tm == 0 and tm % 8 == 0
    has_prev = t > tm
    tile = pl.BlockSpec((None, tm, c), lambda bi, i: (bi, i, 0))
    halo = pl.BlockSpec((None, HALO, c), lambda bi, i: (bi, 0, 0))
    in_specs = [tile]
    args = [u]
    if has_prev:
        r = tm // HALO
        in_specs.append(pl.BlockSpec((None, HALO, c), lambda bi, i: (bi, jnp.maximum(i * r - 1, 0), 0)))
        args.append(u)
    consts = [w["conv_w"], w["conv_b"], w["ln_g"], w["ln_b"]]
    in_specs += [halo] + [_const_spec(x.shape) for x in consts]
    args += [past] + consts
    return pl.pallas_call(
        functools.partial(_conv_kernel, tm=tm, has_prev=has_prev),
        grid=(b, t // tm),
        in_specs=in_specs,
        out_specs=tile,
        out_shape=jax.ShapeDtypeStruct((b, t, c), BF16),
        scratch_shapes=[pltpu.VMEM((SUBLANES, HALO + tm, c), F32)],
        compiler_params=pltpu.CompilerParams(dimension_semantics=("arbitrary", "arbitrary"),
                                             vmem_limit_bytes=VMEM_LIMIT),
        name="conv_mod",
    )(*args)


def _mix_ffn_kernel(x_ref, attn_ref, cact_ref, sga_ref, sgc_ref, wao_ref, wco_ref, wo_ref, g_ref,
                    wg_ref, wu_ref, wd_ref, y_ref, *, n_chunks):
    attn_out = jnp.dot(attn_ref[...], wao_ref[...], preferred_element_type=F32)
    conv_out = jnp.dot(cact_ref[...], wco_ref[...], preferred_element_type=F32)
    merged = sga_ref[...].astype(F32) * attn_out + sgc_ref[...].astype(F32) * conv_out
    x1 = x_ref[...] + jnp.dot(merged.astype(BF16), wo_ref[...], preferred_element_type=F32)
    ms = jnp.mean(x1 * x1, axis=-1, keepdims=True)
    h2 = (x1 * lax.rsqrt(ms + EPS) * g_ref[...]).astype(BF16)
    hc = wg_ref.shape[1] // n_chunks
    y = x1
    for ci in range(n_chunks):
        sl = slice(ci * hc, (ci + 1) * hc)
        gate = jnp.dot(h2, wg_ref[:, sl], preferred_element_type=F32)
        up = jnp.dot(h2, wu_ref[:, sl], preferred_element_type=F32)
        act = (gate * jax.nn.sigmoid(gate) * up).astype(BF16)
        y = y + jnp.dot(act, wd_ref[sl, :], preferred_element_type=F32)
    y_ref[...] = y


def _mix_ffn(x, attn, cact, sga, sgc, w):
    m, d = x.shape
    tm = min(ROW_TILE, m)
    assert m % tm == 0
    hidden = w["wg"].shape[1]
    n_chunks = 2 if hidden % (2 * LANES) == 0 else 1
    row = lambda n: pl.BlockSpec((tm, n), lambda i: (i, 0))
    consts = [w[n] for n in ("wao", "wco", "wo", "g_ffn", "wg", "wu", "wd")]
    return pl.pallas_call(
        functools.partial(_mix_ffn_kernel, n_chunks=n_chunks),
        grid=(m // tm,),
        in_specs=[row(d), row(attn.shape[1]), row(cact.shape[1]), row(d), row(d)]
                 + [_const_spec(c.shape) for c in consts],
        out_specs=row(d),
        out_shape=jax.ShapeDtypeStruct((m, d), F32),
        compiler_params=pltpu.CompilerParams(dimension_semantics=("arbitrary",), vmem_limit_bytes=VMEM_LIMIT),
        name="mix_ffn",
    )(x, attn, cact, sga, sgc, *consts)


def _prep_weights(norm_mix_g, w_in, q_norm_g, k_norm_g, idx_k_norm_g, conv_dw_w, conv_dw_b, conv_ln_g, conv_ln_b,
                  w_conv_out, w_attn_out, w_out, norm_ffn_g, w_ffn_gate, w_ffn_up, w_ffn_down):
    d = w_in.shape[0]
    conv_ch = conv_dw_w.shape[-1]
    sizes = (ATTN_W, ATTN_W, ATTN_W, N_IDX_HEADS * IDX_DIM, IDX_DIM, N_IDX_HEADS, 2 * conv_ch, d, d)
    offs = np.concatenate([[0], np.cumsum(sizes)])
    assert offs[-1] == w_in.shape[1]
    col = lambda i: w_in[:, offs[i]:offs[i + 1]]
    wsm = jnp.concatenate([col(4), col(5), jnp.zeros((d, LANES - IDX_DIM - N_IDX_HEADS), w_in.dtype)], axis=1)
    head = np.arange(ATTN_W) // HEAD_DIM
    seg = (head[:, None] == np.arange(LANES)[None, :]).astype(np.float32)
    return dict(
        g_mix=norm_mix_g.reshape(1, d).astype(F32),
        wq=col(0).astype(BF16), wk=col(1).astype(BF16), wv=col(2).astype(BF16), wiq=col(3).astype(BF16),
        wsm=wsm.astype(BF16), wglu=col(6).astype(BF16), wga=col(7).astype(BF16), wgc=col(8).astype(BF16),
        qg=jnp.tile(q_norm_g.astype(F32), N_HEADS).reshape(1, ATTN_W),
        kg=jnp.tile(k_norm_g.astype(F32), N_HEADS).reshape(1, ATTN_W),
        ikg=jnp.concatenate([idx_k_norm_g.astype(F32), jnp.zeros((LANES - IDX_DIM,), F32)]).reshape(1, LANES),
        seg=jnp.asarray(seg, BF16), segt=jnp.asarray(seg.T, BF16),
        conv_w=conv_dw_w.reshape(CONV_K, conv_ch).astype(F32), conv_b=conv_dw_b.reshape(1, conv_ch).astype(F32),
        ln_g=conv_ln_g.reshape(1, conv_ch).astype(F32), ln_b=conv_ln_b.reshape(1, conv_ch).astype(F32),
        wao=w_attn_out.astype(BF16), wco=w_conv_out.astype(BF16), wo=w_out.astype(BF16),
        g_ffn=norm_ffn_g.reshape(1, d).astype(F32),
        wg=w_ffn_gate.astype(BF16), wu=w_ffn_up.astype(BF16), wd=w_ffn_down.astype(BF16),
    )


def _layer(x, pos0, past_k, past_v, past_ik, past_conv, rel_bias, w):
    b, t, d = x.shape
    m = b * t
    q, k, k_b, v, v_b, iq, sm, sm_b, u, sga, sgc = _in_proj(x.reshape(m, d), w)
    conv_ch = u.shape[1]
    u = u.reshape(b, t, conv_ch)
    sm = sm.reshape(b, t, LANES)

    q = q.reshape(b, t, ATTN_W)
    iq = iq.reshape(b, t, ATTN_W)
    k_b = k_b.reshape(b, t, ATTN_W)
    v_b = v_b.reshape(b, t, ATTN_W)
    ik_b = sm_b.reshape(b, t, LANES)[:, :, :IDX_DIM]
    iw = sm[:, :, IDX_DIM:IDX_DIM + N_IDX_HEADS]
    if past_k is None:
        attn = _sparse_attn(q.swapaxes(1, 2), iq.swapaxes(1, 2), iw.swapaxes(1, 2),
                            k_b, v_b, ik_b, rel_bias, pos0).swapaxes(1, 2)
    else:
        attn = _dec_attn(q, iq, iw, past_k, past_v, past_ik, k.reshape(b, t, ATTN_W), v.reshape(b, t, ATTN_W),
                         ik_b, rel_bias, pos0)

    past = jnp.pad(past_conv.astype(F32), ((0, 0), (HALO - (CONV_K - 1), 0), (0, 0)))
    cact = _conv_mod(u, past, w)
    y = _mix_ffn(x.reshape(m, d), attn.reshape(m, ATTN_W), cact.reshape(m, conv_ch), sga, sgc, w)

    new_conv = jnp.concatenate([past_conv, u], axis=1)[:, -(CONV_K - 1):]
    return (y.reshape(b, t, d), k.reshape(b, t, N_HEADS, HEAD_DIM), v.reshape(b, t, N_HEADS, HEAD_DIM),
            sm[:, :, :IDX_DIM], new_conv)


def kernel(x_prompt, x_sample, cache_k, cache_v, cache_idx_k, state_conv, rel_bias, norm_mix_g, w_in, q_norm_g, k_norm_g, idx_k_norm_g, conv_dw_w, conv_dw_b, conv_ln_g, conv_ln_b, w_conv_out, w_attn_out, w_out, norm_ffn_g, w_ffn_gate, w_ffn_up, w_ffn_down):
    depth = w_in.shape[0]
    bp = x_prompt.shape[0]
    past_len = cache_k.shape[2]
    conv_ch = conv_dw_w.shape[-1]
    yp, ys = x_prompt, x_sample
    outs_p, outs_s = [], []
    for l in range(depth):
        w = _prep_weights(norm_mix_g[l], w_in[l], q_norm_g[l], k_norm_g[l], idx_k_norm_g[l], conv_dw_w[l],
                          conv_dw_b[l], conv_ln_g[l], conv_ln_b[l], w_conv_out[l], w_attn_out[l], w_out[l],
                          norm_ffn_g[l], w_ffn_gate[l], w_ffn_up[l], w_ffn_down[l])
        zero_conv = jnp.zeros((bp, CONV_K - 1, conv_ch), yp.dtype)
        yp, *rest_p = _layer(yp, 0, None, None, None, zero_conv, rel_bias, w)
        outs_p.append(rest_p)
        ys, *rest_s = _layer(ys, past_len, cache_k[l], cache_v[l], cache_idx_k[l], state_conv[l], rel_bias, w)
        outs_s.append(rest_s)
    stack = lambda outs, i: jnp.stack([o[i] for o in outs])
    return (yp, ys, stack(outs_p, 0), stack(outs_p, 1), stack(outs_p, 2), stack(outs_p, 3),
            stack(outs_s, 0), stack(outs_s, 1), stack(outs_s, 2), stack(outs_s, 3))
```

```python
import functools
import math

import numpy as np
import jax
import jax.numpy as jnp
from jax import lax
from jax.experimental import pallas as pl
from jax.experimental.pallas import tpu as pltpu

CHUNK = 64
CHUNK_SHIFT = 6
N_HEADS = 8
HEAD_DIM = 64
ATTN_W = N_HEADS * HEAD_DIM
N_IDX_HEADS = 8
IDX_DIM = 64
TOPK_MAX = 256
CONV_K = 31
NUM_BUCKETS = 32
MAX_DISTANCE = 128
EPS = 1e-6
NEG = -1e30
INT_MIN = -(2 ** 31)
LOG2E = math.log2(math.e)

LANES = 128
SUBLANES = 8
HALO = 32
KEY_TILE = 256
V_ROWS = HEAD_DIM + 16
ROW_TILE = 256
VMEM_LIMIT = 56 * 1024 * 1024

F32 = jnp.float32
BF16 = jnp.bfloat16


def _const_spec(shape):
    return pl.BlockSpec(shape, lambda *_: (0,) * len(shape), pipeline_mode=pl.Buffered(1))


def _split_dot(x, m):
    hi = x.astype(BF16)
    lo = (x - hi.astype(F32)).astype(BF16)
    return jnp.dot(hi, m, preferred_element_type=F32) + jnp.dot(lo, m, preferred_element_type=F32)


def _in_proj_kernel(x_ref, g_ref, wq_ref, wk_ref, wv_ref, wiq_ref, wsm_ref, wglu_ref, wga_ref, wgc_ref,
                    qg_ref, kg_ref, ikg_ref, seg_ref, segt_ref,
                    q_ref, k_ref, kb_ref, v_ref, vb_ref, iq_ref, sm_ref, smb_ref, u_ref, sga_ref, sgc_ref):
    x = x_ref[...]
    ms = jnp.mean(x * x, axis=-1, keepdims=True)
    h = (x * lax.rsqrt(ms + EPS) * g_ref[...]).astype(BF16)

    def head_rms(y, gain):
        ss = _split_dot(y * y, seg_ref[...])
        r = lax.rsqrt(ss * (1.0 / HEAD_DIM) + EPS)
        return y * _split_dot(r, segt_ref[...]) * gain

    q = head_rms(jnp.dot(h, wq_ref[...], preferred_element_type=F32), qg_ref[...])
    q_ref[...] = (q * (HEAD_DIM ** -0.5 * LOG2E)).astype(BF16)
    k = head_rms(jnp.dot(h, wk_ref[...], preferred_element_type=F32), kg_ref[...])
    k_ref[...] = k
    kb_ref[...] = k.astype(BF16)
    v = jnp.dot(h, wv_ref[...], preferred_element_type=F32)
    v_ref[...] = v
    vb_ref[...] = v.astype(BF16)
    iq = jnp.dot(h, wiq_ref[...], preferred_element_type=F32)
    iq_ref[...] = (iq * (IDX_DIM ** -0.5)).astype(BF16)

    sm = jnp.dot(h, wsm_ref[...], preferred_element_type=F32)
    lane = lax.broadcasted_iota(jnp.int32, sm.shape, 1)
    is_ik = lane < IDX_DIM
    ss = jnp.sum(jnp.where(is_ik, sm * sm, 0.0), axis=-1, keepdims=True)
    ikn = sm * lax.rsqrt(ss * (1.0 / IDX_DIM) + EPS) * ikg_ref[...]
    iw = sm * (N_IDX_HEADS ** -0.5)
    smo = jnp.where(is_ik, ikn, jnp.where(lane < IDX_DIM + N_IDX_HEADS, iw, 0.0))
    sm_ref[...] = smo
    smb_ref[...] = smo.astype(BF16)

    glu = jnp.dot(h, wglu_ref[...], preferred_element_type=F32)
    c = glu.shape[1] // 2
    u_ref[...] = glu[:, :c] * jax.nn.sigmoid(glu[:, c:])
    sga_ref[...] = jax.nn.sigmoid(jnp.dot(h, wga_ref[...], preferred_element_type=F32)).astype(BF16)
    sgc_ref[...] = jax.nn.sigmoid(jnp.dot(h, wgc_ref[...], preferred_element_type=F32)).astype(BF16)


def _in_proj(x, w):
    m, d = x.shape
    tm = min(ROW_TILE, m)
    assert m % tm == 0
    conv_ch = w["wglu"].shape[1] // 2
    row = lambda n: pl.BlockSpec((tm, n), lambda i: (i, 0))
    consts = [w[n] for n in ("g_mix", "wq", "wk", "wv", "wiq", "wsm", "wglu", "wga", "wgc",
                             "qg", "kg", "ikg", "seg", "segt")]
    out_shape = [
        jax.ShapeDtypeStruct((m, ATTN_W), BF16),
        jax.ShapeDtypeStruct((m, ATTN_W), F32),
        jax.ShapeDtypeStruct((m, ATTN_W), BF16),
        jax.ShapeDtypeStruct((m, ATTN_W), F32),
        jax.ShapeDtypeStruct((m, ATTN_W), BF16),
        jax.ShapeDtypeStruct((m, ATTN_W), BF16),
        jax.ShapeDtypeStruct((m, LANES), F32),
        jax.ShapeDtypeStruct((m, LANES), BF16),
        jax.ShapeDtypeStruct((m, conv_ch), F32),
        jax.ShapeDtypeStruct((m, d), BF16),
        jax.ShapeDtypeStruct((m, d), BF16),
    ]
    return pl.pallas_call(
        _in_proj_kernel,
        grid=(m // tm,),
        in_specs=[row(d)] + [_const_spec(c.shape) for c in consts],
        out_specs=[row(s.shape[1]) for s in out_shape],
        out_shape=out_shape,
        compiler_params=pltpu.CompilerParams(dimension_semantics=("arbitrary",), vmem_limit_bytes=VMEM_LIMIT),
        name="in_proj",
    )(x, *consts)


def _sparse_attn_kernel(qt_ref, iqt_ref, iwt_ref, k_ref, vt_ref, ik_ref, bias_ref, o_ref,
                        key_scr, m_scr, acc_scr, s_scr, *, tq, tk, nkt, pos0, n_keys, top_k):
    qi = pl.program_id(1)
    q0 = pos0 + qi * tq
    n_kt = jnp.minimum(nkt, ((q0 + tq - 1) // CHUNK * CHUNK + CHUNK + tk - 1) // tk)
    last = n_kt - 1

    def fold(x, op):
        return op(x.reshape(tk // SUBLANES, SUBLANES, tq), axis=0)

    iqt = iqt_ref[...]
    iqt_h = [iqt[h * IDX_DIM:(h + 1) * IDX_DIM, :] for h in range(N_IDX_HEADS)]
    iwt = iwt_ref[...]
    iw_h = [iwt[h:h + 1, :] for h in range(N_IDX_HEADS)]

    def score_tile(kt, masked):
        ik = ik_ref[kt]
        s = None
        for h in range(N_IDX_HEADS):
            t = jnp.maximum(jnp.dot(ik, iqt_h[h], preferred_element_type=F32), 0.0) * iw_h[h]
            s = t if s is None else s + t
        s = s + 0.0
        bits = pltpu.bitcast(s, jnp.int32)
        key = bits ^ ((bits >> 31) & 0x7FFFFFFF)
        if masked:
            kp = kt * tk + lax.broadcasted_iota(jnp.int32, (tk, tq), 0)
            qp = q0 + lax.broadcasted_iota(jnp.int32, (tk, tq), 1)
            adm = ((kp >> CHUNK_SHIFT) <= (qp >> CHUNK_SHIFT)) & (kp < n_keys)
            key = jnp.where(adm, key, INT_MIN)
        key_scr[kt] = key

    def score_body(kt, carry):
        score_tile(kt, False)
        return carry

    lax.fori_loop(0, last, score_body, 0)
    score_tile(last, True)

    @pl.when(n_kt % 2 == 1)
    def _():
        key_scr[n_kt] = jnp.full((tk, tq), INT_MIN, jnp.int32)

    def count(pred):
        def body(i, acc):
            for kt in (2 * i, 2 * i + 1):
                acc = acc + fold(jnp.where(pred(key_scr[kt], kt), 1.0, 0.0), jnp.sum)
            return acc
        acc = lax.fori_loop(0, (n_kt + 1) // 2, body, jnp.zeros((SUBLANES, tq), F32))
        return jnp.sum(acc, axis=0, keepdims=True)

    def thr_body(i, t):
        cand = t + (jnp.int32(1) << (31 - i))
        c = count(lambda key, kt: key >= cand)
        return jnp.where(c >= top_k, cand, t)

    thr = lax.fori_loop(0, 32, thr_body, jnp.full((1, tq), INT_MIN, jnp.int32))
    thr = jnp.maximum(thr, INT_MIN + 1)

    n_ge = count(lambda key, kt: key >= thr)
    has_tie = jnp.max(n_ge) > top_k

    @pl.when(has_tie)
    def _():
        need_m1 = (top_k - 1) - count(lambda key, kt: key > thr)
        row = lax.broadcasted_iota(jnp.int32, (tk, tq), 0)

        def idx_body(i, cut):
            cand = cut + (jnp.int32(1) << (15 - i))
            c = count(lambda key, kt: (key == thr) & (kt * tk + row < cand))
            return jnp.where(c <= need_m1, cand, cut)

        cut = lax.fori_loop(0, 16, idx_body, jnp.zeros((1, tq), jnp.int32))
        tie_col = n_ge > top_k

        def drop_body(kt, carry):
            key = key_scr[kt]
            drop = tie_col & (key == thr) & (kt * tk + row > cut)
            key_scr[kt] = jnp.where(drop, INT_MIN, key)
            return carry

        lax.fori_loop(0, n_kt, drop_body, 0)

    qt = qt_ref[...]
    slab_row = lax.broadcasted_iota(jnp.int32, (LANES, tq), 0)
    qz = []
    for h in range(N_HEADS):
        slab = qt[(h // 2) * LANES:(h // 2 + 1) * LANES, :]
        mine = (slab_row < HEAD_DIM) if h % 2 == 0 else (slab_row >= HEAD_DIM)
        qz.append(jnp.where(mine, slab, jnp.zeros_like(slab)))
    m_scr[...] = jnp.full(m_scr.shape, NEG, F32)
    acc_scr[...] = jnp.zeros(acc_scr.shape, F32)

    def attn_tile(kt, bias_idx):
        madd = jnp.where(key_scr[kt] >= thr, 0.0, NEG)
        m_new = []
        for h in range(N_HEADS):
            pair = h // 2
            s = jnp.dot(k_ref[kt, :, pair * LANES:(pair + 1) * LANES], qz[h], preferred_element_type=F32)
            if bias_idx is not None:
                s = s + bias_ref[bias_idx, h]
            s = s + madd
            s_scr[h] = s
            m_new.append(jnp.maximum(m_scr[h], jnp.max(fold(s, jnp.max), axis=0, keepdims=True)))
        for h in range(N_HEADS):
            alpha = jnp.exp2(m_scr[h] - m_new[h])
            p = jnp.exp2(s_scr[h] - m_new[h])
            pv = jnp.dot(vt_ref[kt, h * V_ROWS:(h + 1) * V_ROWS, :], p.astype(BF16), preferred_element_type=F32)
            acc_scr[h] = alpha * acc_scr[h] + pv
            m_scr[h] = m_new[h]

    def far_body(kt, carry):
        attn_tile(kt, None)
        return carry

    lax.fori_loop(0, n_kt - 2, far_body, 0)

    @pl.when(n_kt >= 2)
    def _():
        attn_tile(n_kt - 2, 0)

    attn_tile(last, 1)

    for h in range(N_HEADS):
        acc = acc_scr[h]
        o_ref[h * HEAD_DIM:(h + 1) * HEAD_DIM, :] = (acc[:HEAD_DIM] / acc[HEAD_DIM:HEAD_DIM + 1]).astype(o_ref.dtype)


def _rel_bucket(rel):
    nb = NUM_BUCKETS // 2
    max_exact = nb // 2
    ret = jnp.where(rel > 0, nb, 0)
    n = jnp.abs(rel)
    nf = jnp.maximum(n, 1).astype(jnp.float32)
    large = max_exact + (jnp.log(nf / max_exact) / math.log(MAX_DISTANCE / max_exact) * (nb - max_exact)).astype(jnp.int32)
    large = jnp.minimum(large, nb - 1)
    return ret + jnp.where(n < max_exact, n, large)


def _bias_tiles(rel_bias, tq, tk):
    j = jnp.arange(tk, dtype=jnp.int32)[:, None]
    i = jnp.arange(tq, dtype=jnp.int32)[None, :]
    return jnp.stack([_shifted_bias(rel_bias, j - i + d) for d in (-tk, 0)])


def _shifted_bias(rel_bias, rel):
    rb = (rel_bias.astype(F32) - rel_bias[_rel_bucket(jnp.int32(-MAX_DISTANCE))].astype(F32)) * LOG2E
    onehot = jax.nn.one_hot(_rel_bucket(rel), NUM_BUCKETS, dtype=F32)
    return jnp.moveaxis(jnp.einsum("...b,bh->...h", onehot, rb, precision=lax.Precision.HIGHEST), -1, 0)


def _check_tiling(t, tq, tk, nkt, pos0, n_keys):
    for qi in range(t // tq):
        q0 = pos0 + qi * tq
        n_kt = min(nkt, ((q0 + tq - 1) // CHUNK * CHUNK + CHUNK + tk - 1) // tk)
        adm_end = min(q0 // CHUNK * CHUNK + CHUNK, n_keys)
        assert (n_kt - 1) * tk <= adm_end, "only the last visited tile may hold inadmissible keys"
        assert n_kt * tk >= min((q0 + tq - 1) // CHUNK * CHUNK + CHUNK, n_keys), "visited tiles cover every admissible key"
        assert (n_kt - 1) * tk == q0, "last tile starts with the query tile"
        assert tk >= MAX_DISTANCE, "tiles before the last two are at least MAX_DISTANCE behind"


def _sparse_attn(qt, iqt, iwt, k_b, v_b, ik_b, rel_bias, pos0):
    b, _, t = qt.shape
    n_keys = k_b.shape[1]
    top_k = min(TOPK_MAX, n_keys // 4)
    tk = KEY_TILE
    tq = min(tk, t)
    nkt = pl.cdiv(n_keys, tk)
    _check_tiling(t, tq, tk, nkt, pos0, n_keys)
    pad = nkt * tk - n_keys
    if pad:
        k_b, v_b, ik_b = (jnp.pad(a, ((0, 0), (0, pad), (0, 0))) for a in (k_b, v_b, ik_b))
    k_t = k_b.reshape(b, nkt, tk, ATTN_W)
    vt_t = v_b.reshape(b, nkt, tk, N_HEADS, HEAD_DIM).transpose(0, 1, 3, 4, 2)
    extra = jnp.zeros((b, nkt, N_HEADS, V_ROWS - HEAD_DIM, tk), BF16).at[:, :, :, 0, :].set(1.0)
    vt_t = jnp.concatenate([vt_t, extra], axis=3).reshape(b, nkt, N_HEADS * V_ROWS, tk)
    ik_t = ik_b.reshape(b, nkt, tk, IDX_DIM)
    bias = _bias_tiles(rel_bias, tq, tk)

    qspec = lambda n: pl.BlockSpec((None, n, tq), lambda bi, qi: (bi, 0, qi))
    kspec = lambda r, c: pl.BlockSpec((None, nkt, r, c), lambda bi, qi: (bi, 0, 0, 0))
    kern = functools.partial(_sparse_attn_kernel, tq=tq, tk=tk, nkt=nkt, pos0=pos0, n_keys=n_keys, top_k=top_k)
    return pl.pallas_call(
        kern,
        grid=(b, t // tq),
        in_specs=[qspec(ATTN_W), qspec(ATTN_W), qspec(N_IDX_HEADS),
                  kspec(tk, ATTN_W), kspec(N_HEADS * V_ROWS, tk), kspec(tk, IDX_DIM), _const_spec(bias.shape)],
        out_specs=qspec(ATTN_W),
        out_shape=jax.ShapeDtypeStruct((b, ATTN_W, t), BF16),
        scratch_shapes=[pltpu.VMEM((nkt + 1, tk, tq), jnp.int32),
                        pltpu.VMEM((N_HEADS, 1, tq), F32),
                        pltpu.VMEM((N_HEADS, V_ROWS, tq), F32),
                        pltpu.VMEM((N_HEADS, tk, tq), F32)],
        compiler_params=pltpu.CompilerParams(dimension_semantics=("arbitrary", "arbitrary"),
                                             vmem_limit_bytes=VMEM_LIMIT),
        name="sparse_attn",
    )(qt, iqt, iwt, k_t, vt_t, ik_t, bias)


DEC_ROWS = N_HEADS * 16
DEC_KEY_TILE = 1024
NT_DIMS = (((1,), (1,)), ((), ()))


def _dec_select_kernel(iq_ref, iwc_ref, ikt_ref, iktn_ref, madd_ref, key_scr, *, g, tq, n_keys, n_pad, top_k):
    n_slabs = n_pad // LANES
    col = lax.broadcasted_iota(jnp.int32, (tq, n_pad), 1)
    for s in range(g):
        iq_s = iq_ref[s * tq:(s + 1) * tq, :]
        lhs = jnp.concatenate([iq_s[:, h * IDX_DIM:(h + 1) * IDX_DIM] for h in range(N_IDX_HEADS)], axis=0)
        logit = jnp.concatenate([jnp.dot(lhs, ikt_ref[s].astype(BF16), preferred_element_type=F32),
                                 jnp.dot(lhs, iktn_ref[s], preferred_element_type=F32)], axis=1)
        x = jnp.maximum(logit, 0.0) * iwc_ref[s]
        score = x[0:tq]
        for h in range(1, N_IDX_HEADS):
            score = score + x[h * tq:(h + 1) * tq]
        score = score + 0.0
        bits = pltpu.bitcast(score, jnp.int32)
        key = bits ^ ((bits >> 31) & 0x7FFFFFFF)
        key_scr[s * tq:(s + 1) * tq, :] = jnp.where(col < n_keys, key, INT_MIN)

    rows = g * tq
    lane = lax.broadcasted_iota(jnp.int32, (rows, LANES), 1)

    def count(pred):
        acc = jnp.zeros((rows, LANES), F32)
        for j in range(n_slabs):
            acc = acc + jnp.where(pred(key_scr[:, j * LANES:(j + 1) * LANES], j), 1.0, 0.0)
        return jnp.sum(acc, axis=1, keepdims=True)

    def thr_body(i, t):
        cand = t + (jnp.int32(1) << (31 - i))
        c = count(lambda key, j: key >= cand)
        return jnp.where(c >= top_k, cand, t)

    thr = lax.fori_loop(0, 32, thr_body, jnp.full((rows, 1), INT_MIN, jnp.int32))
    thr = jnp.maximum(thr, INT_MIN + 1)

    n_ge = count(lambda key, j: key >= thr)
    has_tie = jnp.max(n_ge) > top_k

    @pl.when(has_tie)
    def _():
        need_m1 = (top_k - 1) - count(lambda key, j: key > thr)

        def idx_body(i, cut):
            cand = cut + (jnp.int32(1) << (15 - i))
            c = count(lambda key, j: (key == thr) & (j * LANES + lane < cand))
            return jnp.where(c <= need_m1, cand, cut)

        cut = lax.fori_loop(0, 16, idx_body, jnp.zeros((rows, 1), jnp.int32))
        tie_row = n_ge > top_k
        for j in range(n_slabs):
            key = key_scr[:, j * LANES:(j + 1) * LANES]
            drop = tie_row & (key == thr) & (j * LANES + lane > cut)
            key_scr[:, j * LANES:(j + 1) * LANES] = jnp.where(drop, INT_MIN, key)

    madd_ref[...] = jnp.where(key_scr[...] >= thr, 0.0, NEG)


def _dec_attn_kernel(q_ref, maddc_ref, maddn_ref, k_ref, v_ref, kn_ref, vn_ref, biasc_ref, biasn_ref, o_ref,
                     m_scr, l_scr, acc_scr, *, tq):
    kt = pl.program_id(1)
    q = q_ref[...]
    q_h = [q[:, h * HEAD_DIM:(h + 1) * HEAD_DIM] for h in range(N_HEADS)]

    @pl.when(kt == 0)
    def _():
        m_scr[...] = jnp.full(m_scr.shape, NEG, F32)
        l_scr[...] = jnp.zeros(l_scr.shape, F32)
        acc_scr[...] = jnp.zeros(acc_scr.shape, F32)

    def step(ktr, vtr, madd, bias):
        s = jnp.concatenate([jnp.dot(q_h[h], ktr[h].astype(BF16), preferred_element_type=F32)
                             for h in range(N_HEADS)], axis=0)
        s = s + jnp.concatenate([madd] * N_HEADS, axis=0) + bias
        m_prev = m_scr[...]
        m_new = jnp.maximum(m_prev, jnp.max(s, axis=1, keepdims=True))
        alpha = jnp.exp2(m_prev - m_new)
        p = jnp.exp2(s - m_new)
        l_scr[...] = alpha * l_scr[...] + jnp.sum(p, axis=1, keepdims=True)
        pb = p.astype(BF16)
        pv = jnp.concatenate([lax.dot_general(pb[h * tq:(h + 1) * tq, :], vtr[h].astype(BF16), NT_DIMS,
                                              preferred_element_type=F32) for h in range(N_HEADS)], axis=0)
        acc_scr[...] = alpha * acc_scr[...] + pv
        m_scr[...] = m_new

    step(k_ref, v_ref, maddc_ref[...], biasc_ref[...])

    @pl.when(kt == pl.num_programs(1) - 1)
    def _():
        step(kn_ref, vn_ref, maddn_ref[...], biasn_ref[...])
        o = acc_scr[...] / l_scr[...]
        for h in range(N_HEADS):
            o_ref[:, h * HEAD_DIM:(h + 1) * HEAD_DIM] = o[h * tq:(h + 1) * tq, :].astype(o_ref.dtype)


def _dec_attn(q, iq, iw, past_k, past_v, past_ik, k_new, v_new, ik_new, rel_bias, pos0):
    b, t, _ = q.shape
    p_len = past_k.shape[1]
    n_keys = p_len + t
    top_k = min(TOPK_MAX, n_keys // 4)
    tkk = min(DEC_KEY_TILE, p_len)
    n_pad = p_len + LANES
    g = min(LANES // t, b)
    assert N_HEADS * t == DEC_ROWS and b % g == 0 and p_len % tkk == 0 and t <= LANES
    assert (pos0 + t - 1) // CHUNK == pos0 // CHUNK and (n_keys - 1) // CHUNK <= pos0 // CHUNK, "every key admissible"
    assert tkk >= MAX_DISTANCE + LANES >= MAX_DISTANCE + t, "only the last cache tile is within MAX_DISTANCE of a query"

    ikt_new = jnp.pad(ik_new.swapaxes(1, 2), ((0, 0), (0, 0), (0, LANES - t)))
    iw_col = iw.swapaxes(1, 2).reshape(b, DEC_ROWS, 1)
    madd = pl.pallas_call(
        functools.partial(_dec_select_kernel, g=g, tq=t, n_keys=n_keys, n_pad=n_pad, top_k=top_k),
        grid=(b // g,),
        in_specs=[pl.BlockSpec((g * t, ATTN_W), lambda i: (i, 0)),
                  pl.BlockSpec((g, DEC_ROWS, 1), lambda i: (i, 0, 0)),
                  pl.BlockSpec((g, IDX_DIM, p_len), lambda i: (i, 0, 0)),
                  pl.BlockSpec((g, IDX_DIM, LANES), lambda i: (i, 0, 0))],
        out_specs=pl.BlockSpec((g * t, n_pad), lambda i: (i, 0)),
        out_shape=jax.ShapeDtypeStruct((b * t, n_pad), F32),
        scratch_shapes=[pltpu.VMEM((g * t, n_pad), jnp.int32)],
        compiler_params=pltpu.CompilerParams(dimension_semantics=("arbitrary",), vmem_limit_bytes=VMEM_LIMIT),
        name="dec_select",
    )(iq.reshape(b * t, ATTN_W), iw_col, past_ik.swapaxes(1, 2), ikt_new)

    n_steps = p_len // tkk
    qpos = jnp.tile(jnp.arange(t, dtype=jnp.int32), N_HEADS)[:, None]
    hsel = jnp.repeat(jnp.arange(N_HEADS), t)
    rows = jnp.arange(DEC_ROWS)
    near = MAX_DISTANCE + LANES
    rel_c = jnp.arange(near, dtype=jnp.int32)[None, :] - near - qpos
    rel_n = jnp.arange(LANES, dtype=jnp.int32)[None, :] - qpos
    bias_last = jnp.pad(_shifted_bias(rel_bias, rel_c)[hsel, rows], ((0, 0), (tkk - near, 0)))
    bias_c = jnp.concatenate([jnp.zeros((n_steps - 1, DEC_ROWS, tkk), F32), bias_last[None]], axis=0)
    bias_n = _shifted_bias(rel_bias, rel_n)[hsel, rows]
    heads_t = lambda a: a.transpose(0, 2, 3, 1)
    pad_new = lambda a: jnp.pad(heads_t(a.reshape(b, t, N_HEADS, HEAD_DIM)), ((0, 0), (0, 0), (0, 0), (0, LANES - t)))
    kv_tile = pl.BlockSpec((None, N_HEADS, HEAD_DIM, tkk), lambda bi, kt: (bi, 0, 0, kt))
    kv_new = pl.BlockSpec((None, N_HEADS, HEAD_DIM, LANES), lambda bi, kt: (bi, 0, 0, 0))
    return pl.pallas_call(
        functools.partial(_dec_attn_kernel, tq=t),
        grid=(b, n_steps),
        in_specs=[pl.BlockSpec((None, t, ATTN_W), lambda bi, kt: (bi, 0, 0)),
                  pl.BlockSpec((t, tkk), lambda bi, kt: (bi, kt)),
                  pl.BlockSpec((t, LANES), lambda bi, kt: (bi, p_len // LANES)),
                  kv_tile, kv_tile, kv_new, kv_new,
                  pl.BlockSpec((None, DEC_ROWS, tkk), lambda bi, kt: (kt, 0, 0)),
                  _const_spec(bias_n.shape)],
        out_specs=pl.BlockSpec((None, t, ATTN_W), lambda bi, kt: (bi, 0, 0)),
        out_shape=jax.ShapeDtypeStruct((b, t, ATTN_W), BF16),
        scratch_shapes=[pltpu.VMEM((DEC_ROWS, 1), F32), pltpu.VMEM((DEC_ROWS, 1), F32),
                        pltpu.VMEM((DEC_ROWS, HEAD_DIM), F32)],
        compiler_params=pltpu.CompilerParams(dimension_semantics=("arbitrary", "arbitrary"),
                                             vmem_limit_bytes=VMEM_LIMIT),
        name="dec_attn",
    )(q, madd, madd, heads_t(past_k), heads_t(past_v), pad_new(k_new), pad_new(v_new), bias_c, bias_n)


def _conv_kernel(*refs, tm, has_prev):
    if has_prev:
        u_ref, prev_ref, past_ref, w_ref, b_ref, lg_ref, lb_ref, o_ref, ext_scr = refs
        halo = jnp.where(pl.program_id(1) == 0, past_ref[...], prev_ref[...])
    else:
        u_ref, past_ref, w_ref, b_ref, lg_ref, lb_ref, o_ref, ext_scr = refs
        halo = past_ref[...]
    ext_scr[0, 0:HALO, :] = halo
    ext_scr[0, HALO:HALO + tm, :] = u_ref[...]
    n_rows = tm + HALO - SUBLANES
    for r in range(1, SUBLANES):
        ext_scr[r, 0:n_rows, :] = ext_scr[0, r:r + n_rows, :]
    rb = min(tm, 32)
    first = HALO - (CONV_K - 1)
    for r0 in range(0, tm, rb):
        acc = None
        for j in range(CONV_K):
            a, r = divmod(first + j, SUBLANES)
            t = ext_scr[r, r0 + a * SUBLANES:r0 + a * SUBLANES + rb, :] * w_ref[j:j + 1, :]
            acc = t if acc is None else acc + t
        c = acc + b_ref[...]
        mu = jnp.mean(c, axis=-1, keepdims=True)
        cc = c - mu
        var = jnp.mean(cc * cc, axis=-1, keepdims=True)
        y = cc * lax.rsqrt(var + EPS) * lg_ref[...] + lb_ref[...]
        o_ref[r0:r0 + rb, :] = (y * jax.nn.sigmoid(y)).astype(o_ref.dtype)


def _conv_mod(u, past, w):
    b, t, c = u.shape
    tm = min(ROW_TILE, t)
    assert t % tm == 0 and tm % 8 == 0
    has_prev = t > tm
    tile = pl.BlockSpec((None, tm, c), lambda bi, i: (bi, i, 0))
    halo = pl.BlockSpec((None, HALO, c), lambda bi, i: (bi, 0, 0))
    in_specs = [tile]
    args = [u]
    if has_prev:
        r = tm // HALO
        in_specs.append(pl.BlockSpec((None, HALO, c), lambda bi, i: (bi, jnp.maximum(i * r - 1, 0), 0)))
        args.append(u)
    consts = [w["conv_w"], w["conv_b"], w["ln_g"], w["ln_b"]]
    in_specs += [halo] + [_const_spec(x.shape) for x in consts]
    args += [past] + consts
    return pl.pallas_call(
        functools.partial(_conv_kernel, tm=tm, has_prev=has_prev),
        grid=(b, t // tm),
        in_specs=in_specs,
        out_specs=tile,
        out_shape=jax.ShapeDtypeStruct((b, t, c), BF16),
        scratch_shapes=[pltpu.VMEM((SUBLANES, HALO + tm, c), F32)],
        compiler_params=pltpu.CompilerParams(dimension_semantics=("arbitrary", "arbitrary"),
                                             vmem_limit_bytes=VMEM_LIMIT),
        name="conv_mod",
    )(*args)


def _mix_ffn_kernel(x_ref, attn_ref, cact_ref, sga_ref, sgc_ref, wao_ref, wco_ref, wo_ref, g_ref,
                    wg_ref, wu_ref, wd_ref, y_ref, *, n_chunks):
    attn_out = jnp.dot(attn_ref[...], wao_ref[...], preferred_element_type=F32)
    conv_out = jnp.dot(cact_ref[...], wco_ref[...], preferred_element_type=F32)
    merged = sga_ref[...].astype(F32) * attn_out + sgc_ref[...].astype(F32) * conv_out
    x1 = x_ref[...] + jnp.dot(merged.astype(BF16), wo_ref[...], preferred_element_type=F32)
    ms = jnp.mean(x1 * x1, axis=-1, keepdims=True)
    h2 = (x1 * lax.rsqrt(ms + EPS) * g_ref[...]).astype(BF16)
    hc = wg_ref.shape[1] // n_chunks
    y = x1
    for ci in range(n_chunks):
        sl = slice(ci * hc, (ci + 1) * hc)
        gate = jnp.dot(h2, wg_ref[:, sl], preferred_element_type=F32)
        up = jnp.dot(h2, wu_ref[:, sl], preferred_element_type=F32)
        act = (gate * jax.nn.sigmoid(gate) * up).astype(BF16)
        y = y + jnp.dot(act, wd_ref[sl, :], preferred_element_type=F32)
    y_ref[...] = y


def _mix_ffn(x, attn, cact, sga, sgc, w):
    m, d = x.shape
    tm = min(ROW_TILE, m)
    assert m % tm == 0
    hidden = w["wg"].shape[1]
    n_chunks = 2 if hidden % (2 * LANES) == 0 else 1
    row = lambda n: pl.BlockSpec((tm, n), lambda i: (i, 0))
    consts = [w[n] for n in ("wao", "wco", "wo", "g_ffn", "wg", "wu", "wd")]
    return pl.pallas_call(
        functools.partial(_mix_ffn_kernel, n_chunks=n_chunks),
        grid=(m // tm,),
        in_specs=[row(d), row(attn.shape[1]), row(cact.shape[1]), row(d), row(d)]
                 + [_const_spec(c.shape) for c in consts],
        out_specs=row(d),
        out_shape=jax.ShapeDtypeStruct((m, d), F32),
        compiler_params=pltpu.CompilerParams(dimension_semantics=("arbitrary",), vmem_limit_bytes=VMEM_LIMIT),
        name="mix_ffn",
    )(x, attn, cact, sga, sgc, *consts)


def _prep_weights(norm_mix_g, w_in, q_norm_g, k_norm_g, idx_k_norm_g, conv_dw_w, conv_dw_b, conv_ln_g, conv_ln_b,
                  w_conv_out, w_attn_out, w_out, norm_ffn_g, w_ffn_gate, w_ffn_up, w_ffn_down):
    d = w_in.shape[0]
    conv_ch = conv_dw_w.shape[-1]
    sizes = (ATTN_W, ATTN_W, ATTN_W, N_IDX_HEADS * IDX_DIM, IDX_DIM, N_IDX_HEADS, 2 * conv_ch, d, d)
    offs = np.concatenate([[0], np.cumsum(sizes)])
    assert offs[-1] == w_in.shape[1]
    col = lambda i: w_in[:, offs[i]:offs[i + 1]]
    wsm = jnp.concatenate([col(4), col(5), jnp.zeros((d, LANES - IDX_DIM - N_IDX_HEADS), w_in.dtype)], axis=1)
    head = np.arange(ATTN_W) // HEAD_DIM
    seg = (head[:, None] == np.arange(LANES)[None, :]).astype(np.float32)
    return dict(
        g_mix=norm_mix_g.reshape(1, d).astype(F32),
        wq=col(0).astype(BF16), wk=col(1).astype(BF16), wv=col(2).astype(BF16), wiq=col(3).astype(BF16),
        wsm=wsm.astype(BF16), wglu=col(6).astype(BF16), wga=col(7).astype(BF16), wgc=col(8).astype(BF16),
        qg=jnp.tile(q_norm_g.astype(F32), N_HEADS).reshape(1, ATTN_W),
        kg=jnp.tile(k_norm_g.astype(F32), N_HEADS).reshape(1, ATTN_W),
        ikg=jnp.concatenate([idx_k_norm_g.astype(F32), jnp.zeros((LANES - IDX_DIM,), F32)]).reshape(1, LANES),
        seg=jnp.asarray(seg, BF16), segt=jnp.asarray(seg.T, BF16),
        conv_w=conv_dw_w.reshape(CONV_K, conv_ch).astype(F32), conv_b=conv_dw_b.reshape(1, conv_ch).astype(F32),
        ln_g=conv_ln_g.reshape(1, conv_ch).astype(F32), ln_b=conv_ln_b.reshape(1, conv_ch).astype(F32),
        wao=w_attn_out.astype(BF16), wco=w_conv_out.astype(BF16), wo=w_out.astype(BF16),
        g_ffn=norm_ffn_g.reshape(1, d).astype(F32),
        wg=w_ffn_gate.astype(BF16), wu=w_ffn_up.astype(BF16), wd=w_ffn_down.astype(BF16),
    )


def _layer(x, pos0, past_k, past_v, past_ik, past_conv, rel_bias, w):
    b, t, d = x.shape
    m = b * t
    q, k, k_b, v, v_b, iq, sm, sm_b, u, sga, sgc = _in_proj(x.reshape(m, d), w)
    conv_ch = u.shape[1]
    u = u.reshape(b, t, conv_ch)
    sm = sm.reshape(b, t, LANES)

    q = q.reshape(b, t, ATTN_W)
    iq = iq.reshape(b, t, ATTN_W)
    k_b = k_b.reshape(b, t, ATTN_W)
    v_b = v_b.reshape(b, t, ATTN_W)
    ik_b = sm_b.reshape(b, t, LANES)[:, :, :IDX_DIM]
    iw = sm[:, :, IDX_DIM:IDX_DIM + N_IDX_HEADS]
    if past_k is None:
        attn = _sparse_attn(q.swapaxes(1, 2), iq.swapaxes(1, 2), iw.swapaxes(1, 2),
                            k_b, v_b, ik_b, rel_bias, pos0).swapaxes(1, 2)
    else:
        attn = _dec_attn(q, iq, iw, past_k, past_v, past_ik, k.reshape(b, t, ATTN_W), v.reshape(b, t, ATTN_W),
                         ik_b, rel_bias, pos0)

    past = jnp.pad(past_conv.astype(F32), ((0, 0), (HALO - (CONV_K - 1), 0), (0, 0)))
    cact = _conv_mod(u, past, w)
    y = _mix_ffn(x.reshape(m, d), attn.reshape(m, ATTN_W), cact.reshape(m, conv_ch), sga, sgc, w)

    new_conv = jnp.concatenate([past_conv, u], axis=1)[:, -(CONV_K - 1):]
    return (y.reshape(b, t, d), k.reshape(b, t, N_HEADS, HEAD_DIM), v.reshape(b, t, N_HEADS, HEAD_DIM),
            sm[:, :, :IDX_DIM], new_conv)


def kernel(x_prompt, x_sample, cache_k, cache_v, cache_idx_k, state_conv, rel_bias, norm_mix_g, w_in, q_norm_g, k_norm_g, idx_k_norm_g, conv_dw_w, conv_dw_b, conv_ln_g, conv_ln_b, w_conv_out, w_attn_out, w_out, norm_ffn_g, w_ffn_gate, w_ffn_up, w_ffn_down):
    depth = w_in.shape[0]
    bp = x_prompt.shape[0]
    past_len = cache_k.shape[2]
    conv_ch = conv_dw_w.shape[-1]
    yp, ys = x_prompt, x_sample
    outs_p, outs_s = [], []
    for l in range(depth):
        w = _prep_weights(norm_mix_g[l], w_in[l], q_norm_g[l], k_norm_g[l], idx_k_norm_g[l], conv_dw_w[l],
                          conv_dw_b[l], conv_ln_g[l], conv_ln_b[l], w_conv_out[l], w_attn_out[l], w_out[l],
                          norm_ffn_g[l], w_ffn_gate[l], w_ffn_up[l], w_ffn_down[l])
        zero_conv = jnp.zeros((bp, CONV_K - 1, conv_ch), yp.dtype)
        yp, *rest_p = _layer(yp, 0, None, None, None, zero_conv, rel_bias, w)
        outs_p.append(rest_p)
        ys, *rest_s = _layer(ys, past_len, cache_k[l], cache_v[l], cache_idx_k[l], state_conv[l], rel_bias, w)
        outs_s.append(rest_s)
    stack = lambda outs, i: jnp.stack([o[i] for o in outs])
    return (yp, ys, stack(outs_p, 0), stack(outs_p, 1), stack(outs_p, 2), stack(outs_p, 3),
            stack(outs_s, 0), stack(outs_s, 1), stack(outs_s, 2), stack(outs_s, 3))
```

```python
import functools
import math

import numpy as np
import jax
import jax.numpy as jnp
from jax import lax
from jax.experimental import pallas as pl
from jax.experimental.pallas import tpu as pltpu

CHUNK = 64
CHUNK_SHIFT = 6
N_HEADS = 8
HEAD_DIM = 64
ATTN_W = N_HEADS * HEAD_DIM
N_IDX_HEADS = 8
IDX_DIM = 64
TOPK_MAX = 256
CONV_K = 31
NUM_BUCKETS = 32
MAX_DISTANCE = 128
EPS = 1e-6
NEG = -1e30
INT_MIN = -(2 ** 31)
LOG2E = math.log2(math.e)

LANES = 128
SUBLANES = 8
HALO = 32
KEY_TILE = 256
V_ROWS = HEAD_DIM + 16
ROW_TILE = 256
VMEM_LIMIT = 56 * 1024 * 1024

F32 = jnp.float32
BF16 = jnp.bfloat16


def _const_spec(shape):
    return pl.BlockSpec(shape, lambda *_: (0,) * len(shape), pipeline_mode=pl.Buffered(1))


def _split_dot(x, m):
    hi = x.astype(BF16)
    lo = (x - hi.astype(F32)).astype(BF16)
    return jnp.dot(hi, m, preferred_element_type=F32) + jnp.dot(lo, m, preferred_element_type=F32)


def _in_proj_kernel(x_ref, g_ref, wq_ref, wk_ref, wv_ref, wiq_ref, wsm_ref, wglu_ref, wga_ref, wgc_ref,
                    qg_ref, kg_ref, ikg_ref, seg_ref, segt_ref,
                    q_ref, k_ref, kb_ref, v_ref, vb_ref, iq_ref, sm_ref, smb_ref, u_ref, sga_ref, sgc_ref,
                    *, transposed):
    x = x_ref[...]
    ms = jnp.mean(x * x, axis=-1, keepdims=True)
    h = (x * lax.rsqrt(ms + EPS) * g_ref[...]).astype(BF16)

    def put(ref, val):
        ref[...] = (val.T if transposed else val).astype(ref.dtype)

    def head_rms(y, gain):
        ss = _split_dot(y * y, seg_ref[...])
        r = lax.rsqrt(ss * (1.0 / HEAD_DIM) + EPS)
        return y * _split_dot(r, segt_ref[...]) * gain

    q = head_rms(jnp.dot(h, wq_ref[...], preferred_element_type=F32), qg_ref[...])
    put(q_ref, q * (HEAD_DIM ** -0.5 * LOG2E))
    k = head_rms(jnp.dot(h, wk_ref[...], preferred_element_type=F32), kg_ref[...])
    put(k_ref, k)
    kb_ref[...] = k.astype(BF16)
    v = jnp.dot(h, wv_ref[...], preferred_element_type=F32)
    put(v_ref, v)
    if transposed:
        vt = v.T.astype(BF16)
        pad_rows = lax.broadcasted_iota(jnp.int32, (V_ROWS - HEAD_DIM, vt.shape[1]), 0)
        ones_row = jnp.where(pad_rows == 0, 1.0, 0.0).astype(BF16)
        for hd in range(N_HEADS):
            vb_ref[hd * V_ROWS:hd * V_ROWS + HEAD_DIM, :] = vt[hd * HEAD_DIM:(hd + 1) * HEAD_DIM, :]
            vb_ref[hd * V_ROWS + HEAD_DIM:(hd + 1) * V_ROWS, :] = ones_row
    else:
        vb_ref[...] = v.astype(BF16)
    iq = jnp.dot(h, wiq_ref[...], preferred_element_type=F32)
    put(iq_ref, iq * (IDX_DIM ** -0.5))

    sm = jnp.dot(h, wsm_ref[...], preferred_element_type=F32)
    lane = lax.broadcasted_iota(jnp.int32, sm.shape, 1)
    is_ik = lane < IDX_DIM
    ss = jnp.sum(jnp.where(is_ik, sm * sm, 0.0), axis=-1, keepdims=True)
    ikn = sm * lax.rsqrt(ss * (1.0 / IDX_DIM) + EPS) * ikg_ref[...]
    iw = sm * (N_IDX_HEADS ** -0.5)
    smo = jnp.where(is_ik, ikn, jnp.where(lane < IDX_DIM + N_IDX_HEADS, iw, 0.0))
    put(sm_ref, smo)
    smb_ref[...] = smo.astype(BF16)

    glu = jnp.dot(h, wglu_ref[...], preferred_element_type=F32)
    c = glu.shape[1] // 2
    u_ref[...] = glu[:, :c] * jax.nn.sigmoid(glu[:, c:])
    sga_ref[...] = jax.nn.sigmoid(jnp.dot(h, wga_ref[...], preferred_element_type=F32)).astype(BF16)
    sgc_ref[...] = jax.nn.sigmoid(jnp.dot(h, wgc_ref[...], preferred_element_type=F32)).astype(BF16)


def _in_proj(x, w, transposed):
    b, t, d = x.shape
    m = b * t
    tm = min(ROW_TILE, t if transposed else m)
    assert m % tm == 0 and (not transposed or t % tm == 0)
    nt = t // tm
    conv_ch = w["wglu"].shape[1] // 2
    consts = [w[n] for n in ("g_mix", "wq", "wk", "wv", "wiq", "wsm", "wglu", "wga", "wgc",
                             "qg", "kg", "ikg", "seg", "segt")]

    def rows(n, dtype):
        return jax.ShapeDtypeStruct((m, n), dtype), pl.BlockSpec((tm, n), lambda i: (i, 0))

    def cols(n, dtype):
        if not transposed:
            return rows(n, dtype)
        return jax.ShapeDtypeStruct((b, n, t), dtype), pl.BlockSpec((None, n, tm), lambda i: (i // nt, 0, i % nt))

    if transposed:
        v_slab = (jax.ShapeDtypeStruct((b, nt, N_HEADS * V_ROWS, tm), BF16),
                  pl.BlockSpec((None, None, N_HEADS * V_ROWS, tm), lambda i: (i // nt, i % nt, 0, 0)))
    else:
        v_slab = rows(ATTN_W, BF16)
    outs = [
        cols(ATTN_W, BF16),
        cols(ATTN_W, F32),
        rows(ATTN_W, BF16),
        cols(ATTN_W, F32),
        v_slab,
        cols(ATTN_W, BF16),
        cols(LANES, F32),
        rows(LANES, BF16),
        rows(conv_ch, F32),
        rows(d, BF16),
        rows(d, BF16),
    ]
    out_shape = [o[0] for o in outs]
    return pl.pallas_call(
        functools.partial(_in_proj_kernel, transposed=transposed),
        grid=(m // tm,),
        in_specs=[pl.BlockSpec((tm, d), lambda i: (i, 0))] + [_const_spec(c.shape) for c in consts],
        out_specs=[o[1] for o in outs],
        out_shape=out_shape,
        compiler_params=pltpu.CompilerParams(dimension_semantics=("arbitrary",), vmem_limit_bytes=VMEM_LIMIT),
        name="in_proj",
    )(x.reshape(m, d), *consts)


def _sparse_attn_kernel(qt_ref, iqt_ref, iwt_ref, k_ref, vt_ref, ik_ref, bias_ref, o_ref,
                        key_scr, m_scr, acc_scr, s_scr, *, tq, tk, nkt, pos0, n_keys, top_k):
    qi = pl.program_id(1)
    q0 = pos0 + qi * tq
    n_kt = jnp.minimum(nkt, ((q0 + tq - 1) // CHUNK * CHUNK + CHUNK + tk - 1) // tk)
    last = n_kt - 1

    def fold(x, op):
        return op(x.reshape(tk // SUBLANES, SUBLANES, tq), axis=0)

    iqt = iqt_ref[...]
    iqt_h = [iqt[h * IDX_DIM:(h + 1) * IDX_DIM, :] for h in range(N_IDX_HEADS)]
    iwt = iwt_ref[...]
    iw_h = [iwt[h:h + 1, :] for h in range(N_IDX_HEADS)]

    def score_tile(kt, masked):
        ik = ik_ref[kt]
        s = None
        for h in range(N_IDX_HEADS):
            t = jnp.maximum(jnp.dot(ik, iqt_h[h], preferred_element_type=F32), 0.0) * iw_h[h]
            s = t if s is None else s + t
        s = s + 0.0
        bits = pltpu.bitcast(s, jnp.int32)
        key = bits ^ ((bits >> 31) & 0x7FFFFFFF)
        if masked:
            kp = kt * tk + lax.broadcasted_iota(jnp.int32, (tk, tq), 0)
            qp = q0 + lax.broadcasted_iota(jnp.int32, (tk, tq), 1)
            adm = ((kp >> CHUNK_SHIFT) <= (qp >> CHUNK_SHIFT)) & (kp < n_keys)
            key = jnp.where(adm, key, INT_MIN)
        key_scr[kt] = key

    def score_body(kt, carry):
        score_tile(kt, False)
        return carry

    lax.fori_loop(0, last, score_body, 0)
    score_tile(last, True)

    @pl.when(n_kt % 2 == 1)
    def _():
        key_scr[n_kt] = jnp.full((tk, tq), INT_MIN, jnp.int32)

    def count(pred):
        def body(i, acc):
            for kt in (2 * i, 2 * i + 1):
                acc = acc + fold(jnp.where(pred(key_scr[kt], kt), 1.0, 0.0), jnp.sum)
            return acc
        acc = lax.fori_loop(0, (n_kt + 1) // 2, body, jnp.zeros((SUBLANES, tq), F32))
        return jnp.sum(acc, axis=0, keepdims=True)

    def thr_body(i, t):
        cand = t + (jnp.int32(1) << (31 - i))
        c = count(lambda key, kt: key >= cand)
        return jnp.where(c >= top_k, cand, t)

    thr = lax.fori_loop(0, 32, thr_body, jnp.full((1, tq), INT_MIN, jnp.int32))
    thr = jnp.maximum(thr, INT_MIN + 1)

    n_ge = count(lambda key, kt: key >= thr)
    has_tie = jnp.max(n_ge) > top_k

    @pl.when(has_tie)
    def _():
        need_m1 = (top_k - 1) - count(lambda key, kt: key > thr)
        row = lax.broadcasted_iota(jnp.int32, (tk, tq), 0)

        def idx_body(i, cut):
            cand = cut + (jnp.int32(1) << (15 - i))
            c = count(lambda key, kt: (key == thr) & (kt * tk + row < cand))
            return jnp.where(c <= need_m1, cand, cut)

        cut = lax.fori_loop(0, 16, idx_body, jnp.zeros((1, tq), jnp.int32))
        tie_col = n_ge > top_k

        def drop_body(kt, carry):
            key = key_scr[kt]
            drop = tie_col & (key == thr) & (kt * tk + row > cut)
            key_scr[kt] = jnp.where(drop, INT_MIN, key)
            return carry

        lax.fori_loop(0, n_kt, drop_body, 0)

    qt = qt_ref[...]
    slab_row = lax.broadcasted_iota(jnp.int32, (LANES, tq), 0)
    qz = []
    for h in range(N_HEADS):
        slab = qt[(h // 2) * LANES:(h // 2 + 1) * LANES, :]
        mine = (slab_row < HEAD_DIM) if h % 2 == 0 else (slab_row >= HEAD_DIM)
        qz.append(jnp.where(mine, slab, jnp.zeros_like(slab)))
    m_scr[...] = jnp.full(m_scr.shape, NEG, F32)
    acc_scr[...] = jnp.zeros(acc_scr.shape, F32)

    def attn_tile(kt, bias_idx):
        madd = jnp.where(key_scr[kt] >= thr, 0.0, NEG)
        m_new = []
        for h in range(N_HEADS):
            pair = h // 2
            s = jnp.dot(k_ref[kt, :, pair * LANES:(pair + 1) * LANES], qz[h], preferred_element_type=F32)
            if bias_idx is not None:
                s = s + bias_ref[bias_idx, h]
            s = s + madd
            s_scr[h] = s
            m_new.append(jnp.maximum(m_scr[h], jnp.max(fold(s, jnp.max), axis=0, keepdims=True)))
        for h in range(N_HEADS):
            alpha = jnp.exp2(m_scr[h] - m_new[h])
            p = jnp.exp2(s_scr[h] - m_new[h])
            pv = jnp.dot(vt_ref[kt, h * V_ROWS:(h + 1) * V_ROWS, :], p.astype(BF16), preferred_element_type=F32)
            acc_scr[h] = alpha * acc_scr[h] + pv
            m_scr[h] = m_new[h]

    def far_body(kt, carry):
        attn_tile(kt, None)
        return carry

    lax.fori_loop(0, n_kt - 2, far_body, 0)

    @pl.when(n_kt >= 2)
    def _():
        attn_tile(n_kt - 2, 0)

    attn_tile(last, 1)

    for h in range(N_HEADS):
        acc = acc_scr[h]
        o_ref[h * HEAD_DIM:(h + 1) * HEAD_DIM, :] = (acc[:HEAD_DIM] / acc[HEAD_DIM:HEAD_DIM + 1]).astype(o_ref.dtype)


def _rel_bucket(rel):
    nb = NUM_BUCKETS // 2
    max_exact = nb // 2
    ret = jnp.where(rel > 0, nb, 0)
    n = jnp.abs(rel)
    nf = jnp.maximum(n, 1).astype(jnp.float32)
    large = max_exact + (jnp.log(nf / max_exact) / math.log(MAX_DISTANCE / max_exact) * (nb - max_exact)).astype(jnp.int32)
    large = jnp.minimum(large, nb - 1)
    return ret + jnp.where(n < max_exact, n, large)


def _bias_tiles(rel_bias, tq, tk):
    j = jnp.arange(tk, dtype=jnp.int32)[:, None]
    i = jnp.arange(tq, dtype=jnp.int32)[None, :]
    return jnp.stack([_shifted_bias(rel_bias, j - i + d) for d in (-tk, 0)])


def _shifted_bias(rel_bias, rel):
    rb = (rel_bias.astype(F32) - rel_bias[_rel_bucket(jnp.int32(-MAX_DISTANCE))].astype(F32)) * LOG2E
    onehot = jax.nn.one_hot(_rel_bucket(rel), NUM_BUCKETS, dtype=F32)
    return jnp.moveaxis(jnp.einsum("...b,bh->...h", onehot, rb, precision=lax.Precision.HIGHEST), -1, 0)


def _check_tiling(t, tq, tk, nkt, pos0, n_keys):
    for qi in range(t // tq):
        q0 = pos0 + qi * tq
        n_kt = min(nkt, ((q0 + tq - 1) // CHUNK * CHUNK + CHUNK + tk - 1) // tk)
        adm_end = min(q0 // CHUNK * CHUNK + CHUNK, n_keys)
        assert (n_kt - 1) * tk <= adm_end, "only the last visited tile may hold inadmissible keys"
        assert n_kt * tk >= min((q0 + tq - 1) // CHUNK * CHUNK + CHUNK, n_keys), "visited tiles cover every admissible key"
        assert (n_kt - 1) * tk == q0, "last tile starts with the query tile"
        assert tk >= MAX_DISTANCE, "tiles before the last two are at least MAX_DISTANCE behind"


def _sparse_attn(qt, iqt, iwt, k_b, vt_t, ik_b, rel_bias, pos0):
    b, _, t = qt.shape
    n_keys = k_b.shape[1]
    top_k = min(TOPK_MAX, n_keys // 4)
    tk = KEY_TILE
    tq = min(tk, t)
    nkt = n_keys // tk
    assert n_keys % tk == 0 and vt_t.shape == (b, nkt, N_HEADS * V_ROWS, tk)
    _check_tiling(t, tq, tk, nkt, pos0, n_keys)
    k_t = k_b.reshape(b, nkt, tk, ATTN_W)
    ik_t = ik_b.reshape(b, nkt, tk, IDX_DIM)
    bias = _bias_tiles(rel_bias, tq, tk)

    qspec = lambda n: pl.BlockSpec((None, n, tq), lambda bi, qi: (bi, 0, qi))
    kspec = lambda r, c: pl.BlockSpec((None, nkt, r, c), lambda bi, qi: (bi, 0, 0, 0))
    kern = functools.partial(_sparse_attn_kernel, tq=tq, tk=tk, nkt=nkt, pos0=pos0, n_keys=n_keys, top_k=top_k)
    return pl.pallas_call(
        kern,
        grid=(b, t // tq),
        in_specs=[qspec(ATTN_W), qspec(ATTN_W), qspec(N_IDX_HEADS),
                  kspec(tk, ATTN_W), kspec(N_HEADS * V_ROWS, tk), kspec(tk, IDX_DIM), _const_spec(bias.shape)],
        out_specs=qspec(ATTN_W),
        out_shape=jax.ShapeDtypeStruct((b, ATTN_W, t), BF16),
        scratch_shapes=[pltpu.VMEM((nkt + 1, tk, tq), jnp.int32),
                        pltpu.VMEM((N_HEADS, 1, tq), F32),
                        pltpu.VMEM((N_HEADS, V_ROWS, tq), F32),
                        pltpu.VMEM((N_HEADS, tk, tq), F32)],
        compiler_params=pltpu.CompilerParams(dimension_semantics=("arbitrary", "arbitrary"),
                                             vmem_limit_bytes=VMEM_LIMIT),
        name="sparse_attn",
    )(qt, iqt, iwt, k_t, vt_t, ik_t, bias)


DEC_ROWS = N_HEADS * 16
DEC_KEY_TILE = 1024
NT_DIMS = (((1,), (1,)), ((), ()))


def _dec_select_kernel(iq_ref, iwc_ref, ikt_ref, iktn_ref, madd_ref, key_scr, *, g, tq, n_keys, n_pad, top_k):
    n_slabs = n_pad // LANES
    col = lax.broadcasted_iota(jnp.int32, (tq, n_pad), 1)
    for s in range(g):
        iq_s = iq_ref[s * tq:(s + 1) * tq, :]
        lhs = jnp.concatenate([iq_s[:, h * IDX_DIM:(h + 1) * IDX_DIM] for h in range(N_IDX_HEADS)], axis=0)
        logit = jnp.concatenate([jnp.dot(lhs, ikt_ref[s].astype(BF16), preferred_element_type=F32),
                                 jnp.dot(lhs, iktn_ref[s], preferred_element_type=F32)], axis=1)
        x = jnp.maximum(logit, 0.0) * iwc_ref[s]
        score = x[0:tq]
        for h in range(1, N_IDX_HEADS):
            score = score + x[h * tq:(h + 1) * tq]
        score = score + 0.0
        bits = pltpu.bitcast(score, jnp.int32)
        key = bits ^ ((bits >> 31) & 0x7FFFFFFF)
        key_scr[s * tq:(s + 1) * tq, :] = jnp.where(col < n_keys, key, INT_MIN)

    rows = g * tq
    lane = lax.broadcasted_iota(jnp.int32, (rows, LANES), 1)

    def count(pred):
        acc = jnp.zeros((rows, LANES), F32)
        for j in range(n_slabs):
            acc = acc + jnp.where(pred(key_scr[:, j * LANES:(j + 1) * LANES], j), 1.0, 0.0)
        return jnp.sum(acc, axis=1, keepdims=True)

    def thr_body(i, t):
        cand = t + (jnp.int32(1) << (31 - i))
        c = count(lambda key, j: key >= cand)
        return jnp.where(c >= top_k, cand, t)

    thr = lax.fori_loop(0, 32, thr_body, jnp.full((rows, 1), INT_MIN, jnp.int32))
    thr = jnp.maximum(thr, INT_MIN + 1)

    n_ge = count(lambda key, j: key >= thr)
    has_tie = jnp.max(n_ge) > top_k

    @pl.when(has_tie)
    def _():
        need_m1 = (top_k - 1) - count(lambda key, j: key > thr)

        def idx_body(i, cut):
            cand = cut + (jnp.int32(1) << (15 - i))
            c = count(lambda key, j: (key == thr) & (j * LANES + lane < cand))
            return jnp.where(c <= need_m1, cand, cut)

        cut = lax.fori_loop(0, 16, idx_body, jnp.zeros((rows, 1), jnp.int32))
        tie_row = n_ge > top_k
        for j in range(n_slabs):
            key = key_scr[:, j * LANES:(j + 1) * LANES]
            drop = tie_row & (key == thr) & (j * LANES + lane > cut)
            key_scr[:, j * LANES:(j + 1) * LANES] = jnp.where(drop, INT_MIN, key)

    madd_ref[...] = jnp.where(key_scr[...] >= thr, 0.0, NEG)


def _dec_attn_kernel(q_ref, maddc_ref, maddn_ref, k_ref, v_ref, kn_ref, vn_ref, biasc_ref, biasn_ref, o_ref,
                     m_scr, l_scr, acc_scr, *, tq):
    kt = pl.program_id(1)
    q = q_ref[...]
    q_h = [q[:, h * HEAD_DIM:(h + 1) * HEAD_DIM] for h in range(N_HEADS)]

    @pl.when(kt == 0)
    def _():
        m_scr[...] = jnp.full(m_scr.shape, NEG, F32)
        l_scr[...] = jnp.zeros(l_scr.shape, F32)
        acc_scr[...] = jnp.zeros(acc_scr.shape, F32)

    def step(ktr, vtr, madd, bias):
        s = jnp.concatenate([jnp.dot(q_h[h], ktr[h].astype(BF16), preferred_element_type=F32)
                             for h in range(N_HEADS)], axis=0)
        s = s + jnp.concatenate([madd] * N_HEADS, axis=0) + bias
        m_prev = m_scr[...]
        m_new = jnp.maximum(m_prev, jnp.max(s, axis=1, keepdims=True))
        alpha = jnp.exp2(m_prev - m_new)
        p = jnp.exp2(s - m_new)
        l_scr[...] = alpha * l_scr[...] + jnp.sum(p, axis=1, keepdims=True)
        pb = p.astype(BF16)
        pv = jnp.concatenate([lax.dot_general(pb[h * tq:(h + 1) * tq, :], vtr[h].astype(BF16), NT_DIMS,
                                              preferred_element_type=F32) for h in range(N_HEADS)], axis=0)
        acc_scr[...] = alpha * acc_scr[...] + pv
        m_scr[...] = m_new

    step(k_ref, v_ref, maddc_ref[...], biasc_ref[...])

    @pl.when(kt == pl.num_programs(1) - 1)
    def _():
        step(kn_ref, vn_ref, maddn_ref[...], biasn_ref[...])
        o = acc_scr[...] / l_scr[...]
        for h in range(N_HEADS):
            o_ref[:, h * HEAD_DIM:(h + 1) * HEAD_DIM] = o[h * tq:(h + 1) * tq, :].astype(o_ref.dtype)


def _dec_attn(q, iq, iw, past_k, past_v, past_ik, k_new, v_new, ik_new, rel_bias, pos0):
    b, t, _ = q.shape
    p_len = past_k.shape[1]
    n_keys = p_len + t
    top_k = min(TOPK_MAX, n_keys // 4)
    tkk = min(DEC_KEY_TILE, p_len)
    n_pad = p_len + LANES
    g = min(LANES // t, b)
    assert N_HEADS * t == DEC_ROWS and b % g == 0 and p_len % tkk == 0 and t <= LANES
    assert (pos0 + t - 1) // CHUNK == pos0 // CHUNK and (n_keys - 1) // CHUNK <= pos0 // CHUNK, "every key admissible"
    assert tkk >= MAX_DISTANCE + LANES >= MAX_DISTANCE + t, "only the last cache tile is within MAX_DISTANCE of a query"

    ikt_new = jnp.pad(ik_new.swapaxes(1, 2), ((0, 0), (0, 0), (0, LANES - t)))
    iw_col = iw.swapaxes(1, 2).reshape(b, DEC_ROWS, 1)
    madd = pl.pallas_call(
        functools.partial(_dec_select_kernel, g=g, tq=t, n_keys=n_keys, n_pad=n_pad, top_k=top_k),
        grid=(b // g,),
        in_specs=[pl.BlockSpec((g * t, ATTN_W), lambda i: (i, 0)),
                  pl.BlockSpec((g, DEC_ROWS, 1), lambda i: (i, 0, 0)),
                  pl.BlockSpec((g, IDX_DIM, p_len), lambda i: (i, 0, 0)),
                  pl.BlockSpec((g, IDX_DIM, LANES), lambda i: (i, 0, 0))],
        out_specs=pl.BlockSpec((g * t, n_pad), lambda i: (i, 0)),
        out_shape=jax.ShapeDtypeStruct((b * t, n_pad), F32),
        scratch_shapes=[pltpu.VMEM((g * t, n_pad), jnp.int32)],
        compiler_params=pltpu.CompilerParams(dimension_semantics=("arbitrary",), vmem_limit_bytes=VMEM_LIMIT),
        name="dec_select",
    )(iq.reshape(b * t, ATTN_W), iw_col, past_ik.swapaxes(1, 2), ikt_new)

    n_steps = p_len // tkk
    qpos = jnp.tile(jnp.arange(t, dtype=jnp.int32), N_HEADS)[:, None]
    hsel = jnp.repeat(jnp.arange(N_HEADS), t)
    rows = jnp.arange(DEC_ROWS)
    near = MAX_DISTANCE + LANES
    rel_c = jnp.arange(near, dtype=jnp.int32)[None, :] - near - qpos
    rel_n = jnp.arange(LANES, dtype=jnp.int32)[None, :] - qpos
    bias_last = jnp.pad(_shifted_bias(rel_bias, rel_c)[hsel, rows], ((0, 0), (tkk - near, 0)))
    bias_c = jnp.concatenate([jnp.zeros((n_steps - 1, DEC_ROWS, tkk), F32), bias_last[None]], axis=0)
    bias_n = _shifted_bias(rel_bias, rel_n)[hsel, rows]
    heads_t = lambda a: a.transpose(0, 2, 3, 1)
    pad_new = lambda a: jnp.pad(heads_t(a.reshape(b, t, N_HEADS, HEAD_DIM)), ((0, 0), (0, 0), (0, 0), (0, LANES - t)))
    kv_tile = pl.BlockSpec((None, N_HEADS, HEAD_DIM, tkk), lambda bi, kt: (bi, 0, 0, kt))
    kv_new = pl.BlockSpec((None, N_HEADS, HEAD_DIM, LANES), lambda bi, kt: (bi, 0, 0, 0))
    return pl.pallas_call(
        functools.partial(_dec_attn_kernel, tq=t),
        grid=(b, n_steps),
        in_specs=[pl.BlockSpec((None, t, ATTN_W), lambda bi, kt: (bi, 0, 0)),
                  pl.BlockSpec((t, tkk), lambda bi, kt: (bi, kt)),
                  pl.BlockSpec((t, LANES), lambda bi, kt: (bi, p_len // LANES)),
                  kv_tile, kv_tile, kv_new, kv_new,
                  pl.BlockSpec((None, DEC_ROWS, tkk), lambda bi, kt: (kt, 0, 0)),
                  _const_spec(bias_n.shape)],
        out_specs=pl.BlockSpec((None, t, ATTN_W), lambda bi, kt: (bi, 0, 0)),
        out_shape=jax.ShapeDtypeStruct((b, t, ATTN_W), BF16),
        scratch_shapes=[pltpu.VMEM((DEC_ROWS, 1), F32), pltpu.VMEM((DEC_ROWS, 1), F32),
                        pltpu.VMEM((DEC_ROWS, HEAD_DIM), F32)],
        compiler_params=pltpu.CompilerParams(dimension_semantics=("arbitrary", "arbitrary"),
                                             vmem_limit_bytes=VMEM_LIMIT),
        name="dec_attn",
    )(q, madd, madd, heads_t(past_k), heads_t(past_v), pad_new(k_new), pad_new(v_new), bias_c, bias_n)


def _conv_kernel(*refs, tm, has_prev):
    if has_prev:
        u_ref, prev_ref, past_ref, w_ref, b_ref, lg_ref, lb_ref, o_ref, ext_scr = refs
        halo = jnp.where(pl.program_id(1) == 0, past_ref[...], prev_ref[...])
    else:
        u_ref, past_ref, w_ref, b_ref, lg_ref, lb_ref, o_ref, ext_scr = refs
        halo = past_ref[...]
    ext_scr[0, 0:HALO, :] = halo
    ext_scr[0, HALO:HALO + tm, :] = u_ref[...]
    n_rows = tm + HALO - SUBLANES
    for r in range(1, SUBLANES):
        ext_scr[r, 0:n_rows, :] = ext_scr[0, r:r + n_rows, :]
    rb = min(tm, 32)
    first = HALO - (CONV_K - 1)
    for r0 in range(0, tm, rb):
        acc = None
        for j in range(CONV_K):
            a, r = divmod(first + j, SUBLANES)
            t = ext_scr[r, r0 + a * SUBLANES:r0 + a * SUBLANES + rb, :] * w_ref[j:j + 1, :]
            acc = t if acc is None else acc + t
        c = acc + b_ref[...]
        mu = jnp.mean(c, axis=-1, keepdims=True)
        cc = c - mu
        var = jnp.mean(cc * cc, axis=-1, keepdims=True)
        y = cc * lax.rsqrt(var + EPS) * lg_ref[...] + lb_ref[...]
        o_ref[r0:r0 + rb, :] = (y * jax.nn.sigmoid(y)).astype(o_ref.dtype)


def _conv_mod(u, past, w):
    b, t, c = u.shape
    tm = min(ROW_TILE, t)
    assert t % tm == 0 and tm % 8 == 0
    has_prev = t > tm
    tile = pl.BlockSpec((None, tm, c), lambda bi, i: (bi, i, 0))
    halo = pl.BlockSpec((None, HALO, c), lambda bi, i: (bi, 0, 0))
    in_specs = [tile]
    args = [u]
    if has_prev:
        r = tm // HALO
        in_specs.append(pl.BlockSpec((None, HALO, c), lambda bi, i: (bi, jnp.maximum(i * r - 1, 0), 0)))
        args.append(u)
    consts = [w["conv_w"], w["conv_b"], w["ln_g"], w["ln_b"]]
    in_specs += [halo] + [_const_spec(x.shape) for x in consts]
    args += [past] + consts
    return pl.pallas_call(
        functools.partial(_conv_kernel, tm=tm, has_prev=has_prev),
        grid=(b, t // tm),
        in_specs=in_specs,
        out_specs=tile,
        out_shape=jax.ShapeDtypeStruct((b, t, c), BF16),
        scratch_shapes=[pltpu.VMEM((SUBLANES, HALO + tm, c), F32)],
        compiler_params=pltpu.CompilerParams(dimension_semantics=("arbitrary", "arbitrary"),
                                             vmem_limit_bytes=VMEM_LIMIT),
        name="conv_mod",
    )(*args)


def _mix_ffn_kernel(x_ref, attn_ref, cact_ref, sga_ref, sgc_ref, wao_ref, wco_ref, wo_ref, g_ref,
                    wg_ref, wu_ref, wd_ref, y_ref, *, n_chunks):
    attn_out = jnp.dot(attn_ref[...], wao_ref[...], preferred_element_type=F32)
    conv_out = jnp.dot(cact_ref[...], wco_ref[...], preferred_element_type=F32)
    merged = sga_ref[...].astype(F32) * attn_out + sgc_ref[...].astype(F32) * conv_out
    x1 = x_ref[...] + jnp.dot(merged.astype(BF16), wo_ref[...], preferred_element_type=F32)
    ms = jnp.mean(x1 * x1, axis=-1, keepdims=True)
    h2 = (x1 * lax.rsqrt(ms + EPS) * g_ref[...]).astype(BF16)
    hc = wg_ref.shape[1] // n_chunks
    y = x1
    for ci in range(n_chunks):
        sl = slice(ci * hc, (ci + 1) * hc)
        gate = jnp.dot(h2, wg_ref[:, sl], preferred_element_type=F32)
        up = jnp.dot(h2, wu_ref[:, sl], preferred_element_type=F32)
        act = (gate * jax.nn.sigmoid(gate) * up).astype(BF16)
        y = y + jnp.dot(act, wd_ref[sl, :], preferred_element_type=F32)
    y_ref[...] = y


def _mix_ffn(x, attn, cact, sga, sgc, w):
    m, d = x.shape
    tm = min(ROW_TILE, m)
    assert m % tm == 0
    hidden = w["wg"].shape[1]
    n_chunks = 2 if hidden % (2 * LANES) == 0 else 1
    row = lambda n: pl.BlockSpec((tm, n), lambda i: (i, 0))
    consts = [w[n] for n in ("wao", "wco", "wo", "g_ffn", "wg", "wu", "wd")]
    return pl.pallas_call(
        functools.partial(_mix_ffn_kernel, n_chunks=n_chunks),
        grid=(m // tm,),
        in_specs=[row(d), row(attn.shape[1]), row(cact.shape[1]), row(d), row(d)]
                 + [_const_spec(c.shape) for c in consts],
        out_specs=row(d),
        out_shape=jax.ShapeDtypeStruct((m, d), F32),
        compiler_params=pltpu.CompilerParams(dimension_semantics=("arbitrary",), vmem_limit_bytes=VMEM_LIMIT),
        name="mix_ffn",
    )(x, attn, cact, sga, sgc, *consts)


def _prep_weights(norm_mix_g, w_in, q_norm_g, k_norm_g, idx_k_norm_g, conv_dw_w, conv_dw_b, conv_ln_g, conv_ln_b,
                  w_conv_out, w_attn_out, w_out, norm_ffn_g, w_ffn_gate, w_ffn_up, w_ffn_down):
    d = w_in.shape[0]
    conv_ch = conv_dw_w.shape[-1]
    sizes = (ATTN_W, ATTN_W, ATTN_W, N_IDX_HEADS * IDX_DIM, IDX_DIM, N_IDX_HEADS, 2 * conv_ch, d, d)
    offs = np.concatenate([[0], np.cumsum(sizes)])
    assert offs[-1] == w_in.shape[1]
    col = lambda i: w_in[:, offs[i]:offs[i + 1]]
    wsm = jnp.concatenate([col(4), col(5), jnp.zeros((d, LANES - IDX_DIM - N_IDX_HEADS), w_in.dtype)], axis=1)
    head = np.arange(ATTN_W) // HEAD_DIM
    seg = (head[:, None] == np.arange(LANES)[None, :]).astype(np.float32)
    return dict(
        g_mix=norm_mix_g.reshape(1, d).astype(F32),
        wq=col(0).astype(BF16), wk=col(1).astype(BF16), wv=col(2).astype(BF16), wiq=col(3).astype(BF16),
        wsm=wsm.astype(BF16), wglu=col(6).astype(BF16), wga=col(7).astype(BF16), wgc=col(8).astype(BF16),
        qg=jnp.tile(q_norm_g.astype(F32), N_HEADS).reshape(1, ATTN_W),
        kg=jnp.tile(k_norm_g.astype(F32), N_HEADS).reshape(1, ATTN_W),
        ikg=jnp.concatenate([idx_k_norm_g.astype(F32), jnp.zeros((LANES - IDX_DIM,), F32)]).reshape(1, LANES),
        seg=jnp.asarray(seg, BF16), segt=jnp.asarray(seg.T, BF16),
        conv_w=conv_dw_w.reshape(CONV_K, conv_ch).astype(F32), conv_b=conv_dw_b.reshape(1, conv_ch).astype(F32),
        ln_g=conv_ln_g.reshape(1, conv_ch).astype(F32), ln_b=conv_ln_b.reshape(1, conv_ch).astype(F32),
        wao=w_attn_out.astype(BF16), wco=w_conv_out.astype(BF16), wo=w_out.astype(BF16),
        g_ffn=norm_ffn_g.reshape(1, d).astype(F32),
        wg=w_ffn_gate.astype(BF16), wu=w_ffn_up.astype(BF16), wd=w_ffn_down.astype(BF16),
    )


def _layer(x, pos0, past_k, past_v, past_ik, past_conv, rel_bias, w):
    b, t, d = x.shape
    m = b * t
    prompt = past_k is None
    q, k, k_b, v, v_b, iq, sm, sm_b, u, sga, sgc = _in_proj(x, w, transposed=prompt)
    conv_ch = u.shape[1]
    u = u.reshape(b, t, conv_ch)
    k_b = k_b.reshape(b, t, ATTN_W)
    ik_b = sm_b.reshape(b, t, LANES)[:, :, :IDX_DIM]
    if prompt:
        attn = _sparse_attn(q, iq, sm[:, IDX_DIM:IDX_DIM + N_IDX_HEADS, :], k_b, v_b, ik_b, rel_bias,
                            pos0).swapaxes(1, 2)
        heads = lambda a: a.reshape(b, N_HEADS, HEAD_DIM, t).transpose(0, 3, 1, 2)
        new_k, new_v, new_ik = heads(k), heads(v), sm[:, :IDX_DIM, :].swapaxes(1, 2)
    else:
        sm = sm.reshape(b, t, LANES)
        k, v = k.reshape(b, t, ATTN_W), v.reshape(b, t, ATTN_W)
        attn = _dec_attn(q.reshape(b, t, ATTN_W), iq.reshape(b, t, ATTN_W), sm[:, :, IDX_DIM:IDX_DIM + N_IDX_HEADS],
                         past_k, past_v, past_ik, k, v, ik_b, rel_bias, pos0)
        heads = lambda a: a.reshape(b, t, N_HEADS, HEAD_DIM)
        new_k, new_v, new_ik = heads(k), heads(v), sm[:, :, :IDX_DIM]

    past = jnp.pad(past_conv.astype(F32), ((0, 0), (HALO - (CONV_K - 1), 0), (0, 0)))
    cact = _conv_mod(u, past, w)
    y = _mix_ffn(x.reshape(m, d), attn.reshape(m, ATTN_W), cact.reshape(m, conv_ch), sga, sgc, w)

    new_conv = jnp.concatenate([past_conv, u], axis=1)[:, -(CONV_K - 1):]
    return y.reshape(b, t, d), new_k, new_v, new_ik, new_conv


def kernel(x_prompt, x_sample, cache_k, cache_v, cache_idx_k, state_conv, rel_bias, norm_mix_g, w_in, q_norm_g, k_norm_g, idx_k_norm_g, conv_dw_w, conv_dw_b, conv_ln_g, conv_ln_b, w_conv_out, w_attn_out, w_out, norm_ffn_g, w_ffn_gate, w_ffn_up, w_ffn_down):
    depth = w_in.shape[0]
    bp = x_prompt.shape[0]
    past_len = cache_k.shape[2]
    conv_ch = conv_dw_w.shape[-1]
    yp, ys = x_prompt, x_sample
    outs_p, outs_s = [], []
    for l in range(depth):
        w = _prep_weights(norm_mix_g[l], w_in[l], q_norm_g[l], k_norm_g[l], idx_k_norm_g[l], conv_dw_w[l],
                          conv_dw_b[l], conv_ln_g[l], conv_ln_b[l], w_conv_out[l], w_attn_out[l], w_out[l],
                          norm_ffn_g[l], w_ffn_gate[l], w_ffn_up[l], w_ffn_down[l])
        zero_conv = jnp.zeros((bp, CONV_K - 1, conv_ch), yp.dtype)
        yp, *rest_p = _layer(yp, 0, None, None, None, zero_conv, rel_bias, w)
        outs_p.append(rest_p)
        ys, *rest_s = _layer(ys, past_len, cache_k[l], cache_v[l], cache_idx_k[l], state_conv[l], rel_bias, w)
        outs_s.append(rest_s)
    stack = lambda outs, i: jnp.stack([o[i] for o in outs])
    return (yp, ys, stack(outs_p, 0), stack(outs_p, 1), stack(outs_p, 2), stack(outs_p, 3),
            stack(outs_s, 0), stack(outs_s, 1), stack(outs_s, 2), stack(outs_s, 3))
```

```python
import functools
import math

import numpy as np
import jax
import jax.numpy as jnp
from jax import lax
from jax.experimental import pallas as pl
from jax.experimental.pallas import tpu as pltpu

CHUNK = 64
CHUNK_SHIFT = 6
N_HEADS = 8
HEAD_DIM = 64
ATTN_W = N_HEADS * HEAD_DIM
N_IDX_HEADS = 8
IDX_DIM = 64
TOPK_MAX = 256
CONV_K = 31
NUM_BUCKETS = 32
MAX_DISTANCE = 128
EPS = 1e-6
NEG = -1e30
INT_MIN = -(2 ** 31)
BF16_INF_BITS = 0x7F80
BF16_MIN_NORMAL_BITS = 0x0080
N_BF16_NEG = BF16_INF_BITS - BF16_MIN_NORMAL_BITS + 1
LOG2E = math.log2(math.e)

LANES = 128
SUBLANES = 8
HALO = 32
KEY_TILE = 256
V_ROWS = HEAD_DIM + 16
ROW_TILE = 256
VMEM_LIMIT = 56 * 1024 * 1024

F32 = jnp.float32
BF16 = jnp.bfloat16


def _const_spec(shape):
    return pl.BlockSpec(shape, lambda *_: (0,) * len(shape), pipeline_mode=pl.Buffered(1))


def _split_dot(x, m):
    hi = x.astype(BF16)
    lo = (x - hi.astype(F32)).astype(BF16)
    return jnp.dot(hi, m, preferred_element_type=F32) + jnp.dot(lo, m, preferred_element_type=F32)


def _in_proj_kernel(x_ref, g_ref, wq_ref, wk_ref, wv_ref, wiq_ref, wsm_ref, wglu_ref, wga_ref, wgc_ref,
                    qg_ref, kg_ref, ikg_ref, seg_ref, segt_ref,
                    q_ref, k_ref, kb_ref, v_ref, vb_ref, iq_ref, sm_ref, smb_ref, u_ref, sga_ref, sgc_ref,
                    *, transposed):
    x = x_ref[...]
    ms = jnp.mean(x * x, axis=-1, keepdims=True)
    h = (x * lax.rsqrt(ms + EPS) * g_ref[...]).astype(BF16)

    def put(ref, val):
        ref[...] = (val.T if transposed else val).astype(ref.dtype)

    def head_rms(y, gain):
        ss = _split_dot(y * y, seg_ref[...])
        r = lax.rsqrt(ss * (1.0 / HEAD_DIM) + EPS)
        return y * _split_dot(r, segt_ref[...]) * gain

    q = head_rms(jnp.dot(h, wq_ref[...], preferred_element_type=F32), qg_ref[...])
    put(q_ref, q * (HEAD_DIM ** -0.5 * LOG2E))
    k = head_rms(jnp.dot(h, wk_ref[...], preferred_element_type=F32), kg_ref[...])
    put(k_ref, k)
    kb_ref[...] = k.astype(BF16)
    v = jnp.dot(h, wv_ref[...], preferred_element_type=F32)
    put(v_ref, v)
    if transposed:
        vt = v.T.astype(BF16)
        pad_rows = lax.broadcasted_iota(jnp.int32, (V_ROWS - HEAD_DIM, vt.shape[1]), 0)
        ones_row = jnp.where(pad_rows == 0, 1.0, 0.0).astype(BF16)
        for hd in range(N_HEADS):
            vb_ref[hd * V_ROWS:hd * V_ROWS + HEAD_DIM, :] = vt[hd * HEAD_DIM:(hd + 1) * HEAD_DIM, :]
            vb_ref[hd * V_ROWS + HEAD_DIM:(hd + 1) * V_ROWS, :] = ones_row
    else:
        vb_ref[...] = v.astype(BF16)
    iq = jnp.dot(h, wiq_ref[...], preferred_element_type=F32)
    put(iq_ref, iq * (IDX_DIM ** -0.5))

    sm = jnp.dot(h, wsm_ref[...], preferred_element_type=F32)
    lane = lax.broadcasted_iota(jnp.int32, sm.shape, 1)
    is_ik = lane < IDX_DIM
    ss = jnp.sum(jnp.where(is_ik, sm * sm, 0.0), axis=-1, keepdims=True)
    ikn = sm * lax.rsqrt(ss * (1.0 / IDX_DIM) + EPS) * ikg_ref[...]
    iw = sm * (N_IDX_HEADS ** -0.5)
    smo = jnp.where(is_ik, ikn, jnp.where(lane < IDX_DIM + N_IDX_HEADS, iw, 0.0))
    put(sm_ref, smo)
    smb_ref[...] = smo.astype(BF16)

    glu = jnp.dot(h, wglu_ref[...], preferred_element_type=F32)
    c = glu.shape[1] // 2
    u_ref[...] = glu[:, :c] * jax.nn.sigmoid(glu[:, c:])
    sga_ref[...] = jax.nn.sigmoid(jnp.dot(h, wga_ref[...], preferred_element_type=F32)).astype(BF16)
    sgc_ref[...] = jax.nn.sigmoid(jnp.dot(h, wgc_ref[...], preferred_element_type=F32)).astype(BF16)


def _in_proj(x, w, transposed):
    b, t, d = x.shape
    m = b * t
    tm = min(ROW_TILE, t if transposed else m)
    assert m % tm == 0 and (not transposed or t % tm == 0)
    nt = t // tm
    conv_ch = w["wglu"].shape[1] // 2
    consts = [w[n] for n in ("g_mix", "wq", "wk", "wv", "wiq", "wsm", "wglu", "wga", "wgc",
                             "qg", "kg", "ikg", "seg", "segt")]

    def rows(n, dtype):
        return jax.ShapeDtypeStruct((m, n), dtype), pl.BlockSpec((tm, n), lambda i: (i, 0))

    def cols(n, dtype):
        if not transposed:
            return rows(n, dtype)
        return jax.ShapeDtypeStruct((b, n, t), dtype), pl.BlockSpec((None, n, tm), lambda i: (i // nt, 0, i % nt))

    if transposed:
        v_slab = (jax.ShapeDtypeStruct((b, nt, N_HEADS * V_ROWS, tm), BF16),
                  pl.BlockSpec((None, None, N_HEADS * V_ROWS, tm), lambda i: (i // nt, i % nt, 0, 0)))
    else:
        v_slab = rows(ATTN_W, BF16)
    outs = [
        cols(ATTN_W, BF16),
        cols(ATTN_W, F32),
        rows(ATTN_W, BF16),
        cols(ATTN_W, F32),
        v_slab,
        cols(ATTN_W, BF16),
        cols(LANES, F32),
        rows(LANES, BF16),
        rows(conv_ch, F32),
        rows(d, BF16),
        rows(d, BF16),
    ]
    out_shape = [o[0] for o in outs]
    return pl.pallas_call(
        functools.partial(_in_proj_kernel, transposed=transposed),
        grid=(m // tm,),
        in_specs=[pl.BlockSpec((tm, d), lambda i: (i, 0))] + [_const_spec(c.shape) for c in consts],
        out_specs=[o[1] for o in outs],
        out_shape=out_shape,
        compiler_params=pltpu.CompilerParams(dimension_semantics=("arbitrary",), vmem_limit_bytes=VMEM_LIMIT),
        name="in_proj",
    )(x.reshape(m, d), *consts)


def _fold(x, op):
    return op(x.reshape(x.shape[0] // SUBLANES, SUBLANES, x.shape[1]), axis=0)


def _select_top_k(key_scr, part_scr, n_kt, *, tk, tq, top_k):
    @pl.when(n_kt % 2 == 1)
    def _():
        key_scr[n_kt] = jnp.full((tk, tq), INT_MIN, jnp.int32)
        part_scr[n_kt] = jnp.full((tk, tq), -jnp.inf, BF16)

    n_pairs = (n_kt + 1) // 2

    def count(pred):
        def body(i, acc):
            for kt in (2 * i, 2 * i + 1):
                acc = acc + _fold(jnp.where(pred(key_scr[kt], kt), 1.0, 0.0), jnp.sum)
            return acc
        acc = lax.fori_loop(0, n_pairs, body, jnp.zeros((SUBLANES, tq), F32))
        return jnp.sum(acc, axis=0, keepdims=True)

    rows16 = 2 * SUBLANES
    one_b, zero_b = jnp.ones((), BF16), jnp.zeros((), BF16)

    def count_part(pred):
        def body(i, acc):
            for kt in (2 * i, 2 * i + 1):
                c = jnp.where(pred(part_scr[kt]), one_b, zero_b)
                part = c[0:rows16]
                for j in range(1, tk // rows16):
                    part = part + c[j * rows16:(j + 1) * rows16]
                acc = acc + part.astype(F32)
            return acc
        acc = lax.fori_loop(0, n_pairs, body, jnp.zeros((rows16, tq), F32))
        return jnp.sum(acc, axis=0, keepdims=True)

    def search(n_bits, base, to_image, limit):
        def body(i, t):
            cand = t + (jnp.int32(1) << (n_bits - 1 - i))
            img = to_image(cand)
            c = base + count_part(lambda a: a >= img)
            return jnp.where((c >= top_k) & (cand <= limit), cand, t)
        return lax.fori_loop(0, n_bits, body, jnp.zeros((1, tq), jnp.int32))

    def refine(eq_img, shift):
        def body(i, carry):
            for kt in (2 * i, 2 * i + 1):
                nxt = ((key_scr[kt] >> shift) & 0xFF).astype(F32).astype(BF16)
                part_scr[kt] = jnp.where(part_scr[kt] == eq_img, nxt, -one_b)
            return carry
        lax.fori_loop(0, n_pairs, body, 0)

    def top_bits(c):
        neg = 0x8000 | (BF16_INF_BITS - c)
        pos = BF16_MIN_NORMAL_BITS + (c - N_BF16_NEG - 1)
        return jnp.where(c < N_BF16_NEG, neg, jnp.where(c == N_BF16_NEG, 0, pos))

    def top_image(c):
        return pltpu.bitcast(top_bits(c) << 16, F32).astype(BF16)

    small_image = lambda c: c.astype(F32).astype(BF16)

    t1 = search(16, 0.0, top_image, 2 * N_BF16_NEG)
    img1 = top_image(t1)
    above = count_part(lambda a: a > img1)
    refine(img1, 8)
    t2 = search(8, above, small_image, 255)
    img2 = small_image(t2)
    above = above + count_part(lambda a: a > img2)
    refine(img2, 0)
    t3 = search(8, above, small_image, 255)
    hi = top_bits(t1)
    hi = hi - ((hi >> 15) << 16)
    hi = hi ^ ((hi >> 15) & 0x7FFF)
    thr = (hi << 16) + (t2 << 8) + t3
    thr = jnp.where(t1 == 0, INT_MIN + 1, jnp.maximum(thr, INT_MIN + 1))

    n_ge = count(lambda key, kt: key >= thr)
    has_tie = jnp.max(n_ge) > top_k

    @pl.when(has_tie)
    def _():
        need_m1 = (top_k - 1) - count(lambda key, kt: key > thr)
        row = lax.broadcasted_iota(jnp.int32, (tk, tq), 0)

        def idx_body(i, cut):
            cand = cut + (jnp.int32(1) << (15 - i))
            c = count(lambda key, kt: (key == thr) & (kt * tk + row < cand))
            return jnp.where(c <= need_m1, cand, cut)

        cut = lax.fori_loop(0, 16, idx_body, jnp.zeros((1, tq), jnp.int32))
        tie_col = n_ge > top_k

        def drop_body(kt, carry):
            key = key_scr[kt]
            drop = tie_col & (key == thr) & (kt * tk + row > cut)
            key_scr[kt] = jnp.where(drop, INT_MIN, key)
            return carry

        lax.fori_loop(0, n_kt, drop_body, 0)
    return thr


def _sparse_attn_kernel(qt_ref, iqt_ref, iwt_ref, k_ref, vt_ref, ik_ref, bias_ref, o_ref,
                        key_scr, part_scr, m_scr, acc_scr, s_scr, *, tq, tk, nkt, pos0, n_keys, top_k):
    qi = pl.program_id(1)
    q0 = pos0 + qi * tq
    n_kt = jnp.minimum(nkt, ((q0 + tq - 1) // CHUNK * CHUNK + CHUNK + tk - 1) // tk)
    last = n_kt - 1

    fold = _fold

    iqt = iqt_ref[...]
    iqt_h = [iqt[h * IDX_DIM:(h + 1) * IDX_DIM, :] for h in range(N_IDX_HEADS)]
    iwt = iwt_ref[...]
    iw_h = [iwt[h:h + 1, :] for h in range(N_IDX_HEADS)]

    def score_tile(kt, masked):
        ik = ik_ref[kt]
        s = None
        for h in range(N_IDX_HEADS):
            t = jnp.maximum(jnp.dot(ik, iqt_h[h], preferred_element_type=F32), 0.0) * iw_h[h]
            s = t if s is None else s + t
        s = s + 0.0
        bits = pltpu.bitcast(s, jnp.int32)
        key = bits ^ ((bits >> 31) & 0x7FFFFFFF)
        top = pltpu.bitcast(bits & -65536, F32)
        if masked:
            kp = kt * tk + lax.broadcasted_iota(jnp.int32, (tk, tq), 0)
            qp = q0 + lax.broadcasted_iota(jnp.int32, (tk, tq), 1)
            adm = ((kp >> CHUNK_SHIFT) <= (qp >> CHUNK_SHIFT)) & (kp < n_keys)
            key = jnp.where(adm, key, INT_MIN)
            top = jnp.where(adm, top, -jnp.inf)
        key_scr[kt] = key
        part_scr[kt] = top.astype(BF16)

    def score_body(kt, carry):
        score_tile(kt, False)
        return carry

    lax.fori_loop(0, last, score_body, 0)
    score_tile(last, True)

    thr = _select_top_k(key_scr, part_scr, n_kt, tk=tk, tq=tq, top_k=top_k)

    qt = qt_ref[...]
    slab_row = lax.broadcasted_iota(jnp.int32, (LANES, tq), 0)
    qz = []
    for h in range(N_HEADS):
        slab = qt[(h // 2) * LANES:(h // 2 + 1) * LANES, :]
        mine = (slab_row < HEAD_DIM) if h % 2 == 0 else (slab_row >= HEAD_DIM)
        qz.append(jnp.where(mine, slab, jnp.zeros_like(slab)))
    m_scr[...] = jnp.full(m_scr.shape, NEG, F32)
    acc_scr[...] = jnp.zeros(acc_scr.shape, F32)

    def attn_tile(kt, bias_idx):
        madd = jnp.where(key_scr[kt] >= thr, 0.0, NEG)
        m_new = []
        for h in range(N_HEADS):
            pair = h // 2
            s = jnp.dot(k_ref[kt, :, pair * LANES:(pair + 1) * LANES], qz[h], preferred_element_type=F32)
            if bias_idx is not None:
                s = s + bias_ref[bias_idx, h]
            s = s + madd
            s_scr[h] = s
            m_new.append(jnp.maximum(m_scr[h], jnp.max(fold(s, jnp.max), axis=0, keepdims=True)))
        for h in range(N_HEADS):
            alpha = jnp.exp2(m_scr[h] - m_new[h])
            p = jnp.exp2(s_scr[h] - m_new[h])
            pv = jnp.dot(vt_ref[kt, h * V_ROWS:(h + 1) * V_ROWS, :], p.astype(BF16), preferred_element_type=F32)
            acc_scr[h] = alpha * acc_scr[h] + pv
            m_scr[h] = m_new[h]

    def far_body(kt, carry):
        attn_tile(kt, None)
        return carry

    lax.fori_loop(0, n_kt - 2, far_body, 0)

    @pl.when(n_kt >= 2)
    def _():
        attn_tile(n_kt - 2, 0)

    attn_tile(last, 1)

    for h in range(N_HEADS):
        acc = acc_scr[h]
        o_ref[h * HEAD_DIM:(h + 1) * HEAD_DIM, :] = (acc[:HEAD_DIM] / acc[HEAD_DIM:HEAD_DIM + 1]).astype(o_ref.dtype)


def _rel_bucket(rel):
    nb = NUM_BUCKETS // 2
    max_exact = nb // 2
    ret = jnp.where(rel > 0, nb, 0)
    n = jnp.abs(rel)
    nf = jnp.maximum(n, 1).astype(jnp.float32)
    large = max_exact + (jnp.log(nf / max_exact) / math.log(MAX_DISTANCE / max_exact) * (nb - max_exact)).astype(jnp.int32)
    large = jnp.minimum(large, nb - 1)
    return ret + jnp.where(n < max_exact, n, large)


def _bias_tiles(rel_bias, tq, tk):
    j = jnp.arange(tk, dtype=jnp.int32)[:, None]
    i = jnp.arange(tq, dtype=jnp.int32)[None, :]
    return jnp.stack([_shifted_bias(rel_bias, j - i + d) for d in (-tk, 0)])


def _shifted_bias(rel_bias, rel):
    rb = (rel_bias.astype(F32) - rel_bias[_rel_bucket(jnp.int32(-MAX_DISTANCE))].astype(F32)) * LOG2E
    onehot = jax.nn.one_hot(_rel_bucket(rel), NUM_BUCKETS, dtype=F32)
    return jnp.moveaxis(jnp.einsum("...b,bh->...h", onehot, rb, precision=lax.Precision.HIGHEST), -1, 0)


def _check_tiling(t, tq, tk, nkt, pos0, n_keys):
    for qi in range(t // tq):
        q0 = pos0 + qi * tq
        n_kt = min(nkt, ((q0 + tq - 1) // CHUNK * CHUNK + CHUNK + tk - 1) // tk)
        adm_end = min(q0 // CHUNK * CHUNK + CHUNK, n_keys)
        assert (n_kt - 1) * tk <= adm_end, "only the last visited tile may hold inadmissible keys"
        assert n_kt * tk >= min((q0 + tq - 1) // CHUNK * CHUNK + CHUNK, n_keys), "visited tiles cover every admissible key"
        assert (n_kt - 1) * tk == q0, "last tile starts with the query tile"
        assert tk >= MAX_DISTANCE, "tiles before the last two are at least MAX_DISTANCE behind"


def _sparse_attn(qt, iqt, iwt, k_b, vt_t, ik_b, rel_bias, pos0):
    b, _, t = qt.shape
    n_keys = k_b.shape[1]
    top_k = min(TOPK_MAX, n_keys // 4)
    tk = KEY_TILE
    tq = min(tk, t)
    nkt = n_keys // tk
    assert n_keys % tk == 0 and vt_t.shape == (b, nkt, N_HEADS * V_ROWS, tk)
    _check_tiling(t, tq, tk, nkt, pos0, n_keys)
    k_t = k_b.reshape(b, nkt, tk, ATTN_W)
    ik_t = ik_b.reshape(b, nkt, tk, IDX_DIM)
    bias = _bias_tiles(rel_bias, tq, tk)

    qspec = lambda n: pl.BlockSpec((None, n, tq), lambda bi, qi: (bi, 0, qi))
    kspec = lambda r, c: pl.BlockSpec((None, nkt, r, c), lambda bi, qi: (bi, 0, 0, 0))
    kern = functools.partial(_sparse_attn_kernel, tq=tq, tk=tk, nkt=nkt, pos0=pos0, n_keys=n_keys, top_k=top_k)
    return pl.pallas_call(
        kern,
        grid=(b, t // tq),
        in_specs=[qspec(ATTN_W), qspec(ATTN_W), qspec(N_IDX_HEADS),
                  kspec(tk, ATTN_W), kspec(N_HEADS * V_ROWS, tk), kspec(tk, IDX_DIM), _const_spec(bias.shape)],
        out_specs=qspec(ATTN_W),
        out_shape=jax.ShapeDtypeStruct((b, ATTN_W, t), BF16),
        scratch_shapes=[pltpu.VMEM((nkt + 1, tk, tq), jnp.int32),
                        pltpu.VMEM((nkt + 1, tk, tq), BF16),
                        pltpu.VMEM((N_HEADS, 1, tq), F32),
                        pltpu.VMEM((N_HEADS, V_ROWS, tq), F32),
                        pltpu.VMEM((N_HEADS, tk, tq), F32)],
        compiler_params=pltpu.CompilerParams(dimension_semantics=("arbitrary", "arbitrary"),
                                             vmem_limit_bytes=VMEM_LIMIT),
        name="sparse_attn",
    )(qt, iqt, iwt, k_t, vt_t, ik_t, bias)


DEC_ROWS = N_HEADS * 16
DEC_KEY_TILE = 1024
NT_DIMS = (((1,), (1,)), ((), ()))


def _dec_select_kernel(iq_ref, iwc_ref, ikt_ref, iktn_ref, madd_ref, key_scr, *, g, tq, n_keys, n_pad, top_k):
    n_slabs = n_pad // LANES
    col = lax.broadcasted_iota(jnp.int32, (tq, n_pad), 1)
    for s in range(g):
        iq_s = iq_ref[s * tq:(s + 1) * tq, :]
        lhs = jnp.concatenate([iq_s[:, h * IDX_DIM:(h + 1) * IDX_DIM] for h in range(N_IDX_HEADS)], axis=0)
        logit = jnp.concatenate([jnp.dot(lhs, ikt_ref[s].astype(BF16), preferred_element_type=F32),
                                 jnp.dot(lhs, iktn_ref[s], preferred_element_type=F32)], axis=1)
        x = jnp.maximum(logit, 0.0) * iwc_ref[s]
        score = x[0:tq]
        for h in range(1, N_IDX_HEADS):
            score = score + x[h * tq:(h + 1) * tq]
        score = score + 0.0
        bits = pltpu.bitcast(score, jnp.int32)
        key = bits ^ ((bits >> 31) & 0x7FFFFFFF)
        key_scr[s * tq:(s + 1) * tq, :] = jnp.where(col < n_keys, key, INT_MIN)

    rows = g * tq
    lane = lax.broadcasted_iota(jnp.int32, (rows, LANES), 1)

    def count(pred):
        acc = jnp.zeros((rows, LANES), F32)
        for j in range(n_slabs):
            acc = acc + jnp.where(pred(key_scr[:, j * LANES:(j + 1) * LANES], j), 1.0, 0.0)
        return jnp.sum(acc, axis=1, keepdims=True)

    def thr_body(i, t):
        cand = t + (jnp.int32(1) << (31 - i))
        c = count(lambda key, j: key >= cand)
        return jnp.where(c >= top_k, cand, t)

    thr = lax.fori_loop(0, 32, thr_body, jnp.full((rows, 1), INT_MIN, jnp.int32))
    thr = jnp.maximum(thr, INT_MIN + 1)

    n_ge = count(lambda key, j: key >= thr)
    has_tie = jnp.max(n_ge) > top_k

    @pl.when(has_tie)
    def _():
        need_m1 = (top_k - 1) - count(lambda key, j: key > thr)

        def idx_body(i, cut):
            cand = cut + (jnp.int32(1) << (15 - i))
            c = count(lambda key, j: (key == thr) & (j * LANES + lane < cand))
            return jnp.where(c <= need_m1, cand, cut)

        cut = lax.fori_loop(0, 16, idx_body, jnp.zeros((rows, 1), jnp.int32))
        tie_row = n_ge > top_k
        for j in range(n_slabs):
            key = key_scr[:, j * LANES:(j + 1) * LANES]
            drop = tie_row & (key == thr) & (j * LANES + lane > cut)
            key_scr[:, j * LANES:(j + 1) * LANES] = jnp.where(drop, INT_MIN, key)

    madd_ref[...] = jnp.where(key_scr[...] >= thr, 0.0, NEG)


def _dec_attn_kernel(q_ref, maddc_ref, maddn_ref, k_ref, v_ref, kn_ref, vn_ref, biasc_ref, biasn_ref, o_ref,
                     m_scr, l_scr, acc_scr, *, tq):
    kt = pl.program_id(1)
    q = q_ref[...]
    q_h = [q[:, h * HEAD_DIM:(h + 1) * HEAD_DIM] for h in range(N_HEADS)]

    @pl.when(kt == 0)
    def _():
        m_scr[...] = jnp.full(m_scr.shape, NEG, F32)
        l_scr[...] = jnp.zeros(l_scr.shape, F32)
        acc_scr[...] = jnp.zeros(acc_scr.shape, F32)

    def step(ktr, vtr, madd, bias):
        s = jnp.concatenate([jnp.dot(q_h[h], ktr[h].astype(BF16), preferred_element_type=F32)
                             for h in range(N_HEADS)], axis=0)
        s = s + jnp.concatenate([madd] * N_HEADS, axis=0) + bias
        m_prev = m_scr[...]
        m_new = jnp.maximum(m_prev, jnp.max(s, axis=1, keepdims=True))
        alpha = jnp.exp2(m_prev - m_new)
        p = jnp.exp2(s - m_new)
        l_scr[...] = alpha * l_scr[...] + jnp.sum(p, axis=1, keepdims=True)
        pb = p.astype(BF16)
        pv = jnp.concatenate([lax.dot_general(pb[h * tq:(h + 1) * tq, :], vtr[h].astype(BF16), NT_DIMS,
                                              preferred_element_type=F32) for h in range(N_HEADS)], axis=0)
        acc_scr[...] = alpha * acc_scr[...] + pv
        m_scr[...] = m_new

    step(k_ref, v_ref, maddc_ref[...], biasc_ref[...])

    @pl.when(kt == pl.num_programs(1) - 1)
    def _():
        step(kn_ref, vn_ref, maddn_ref[...], biasn_ref[...])
        o = acc_scr[...] / l_scr[...]
        for h in range(N_HEADS):
            o_ref[:, h * HEAD_DIM:(h + 1) * HEAD_DIM] = o[h * tq:(h + 1) * tq, :].astype(o_ref.dtype)


def _dec_attn(q, iq, iw, past_k, past_v, past_ik, k_new, v_new, ik_new, rel_bias, pos0):
    b, t, _ = q.shape
    p_len = past_k.shape[1]
    n_keys = p_len + t
    top_k = min(TOPK_MAX, n_keys // 4)
    tkk = min(DEC_KEY_TILE, p_len)
    n_pad = p_len + LANES
    g = min(LANES // t, b)
    assert N_HEADS * t == DEC_ROWS and b % g == 0 and p_len % tkk == 0 and t <= LANES
    assert (pos0 + t - 1) // CHUNK == pos0 // CHUNK and (n_keys - 1) // CHUNK <= pos0 // CHUNK, "every key admissible"
    assert tkk >= MAX_DISTANCE + LANES >= MAX_DISTANCE + t, "only the last cache tile is within MAX_DISTANCE of a query"

    ikt_new = jnp.pad(ik_new.swapaxes(1, 2), ((0, 0), (0, 0), (0, LANES - t)))
    iw_col = iw.swapaxes(1, 2).reshape(b, DEC_ROWS, 1)
    madd = pl.pallas_call(
        functools.partial(_dec_select_kernel, g=g, tq=t, n_keys=n_keys, n_pad=n_pad, top_k=top_k),
        grid=(b // g,),
        in_specs=[pl.BlockSpec((g * t, ATTN_W), lambda i: (i, 0)),
                  pl.BlockSpec((g, DEC_ROWS, 1), lambda i: (i, 0, 0)),
                  pl.BlockSpec((g, IDX_DIM, p_len), lambda i: (i, 0, 0)),
                  pl.BlockSpec((g, IDX_DIM, LANES), lambda i: (i, 0, 0))],
        out_specs=pl.BlockSpec((g * t, n_pad), lambda i: (i, 0)),
        out_shape=jax.ShapeDtypeStruct((b * t, n_pad), F32),
        scratch_shapes=[pltpu.VMEM((g * t, n_pad), jnp.int32)],
        compiler_params=pltpu.CompilerParams(dimension_semantics=("arbitrary",), vmem_limit_bytes=VMEM_LIMIT),
        name="dec_select",
    )(iq.reshape(b * t, ATTN_W), iw_col, past_ik.swapaxes(1, 2), ikt_new)

    n_steps = p_len // tkk
    qpos = jnp.tile(jnp.arange(t, dtype=jnp.int32), N_HEADS)[:, None]
    hsel = jnp.repeat(jnp.arange(N_HEADS), t)
    rows = jnp.arange(DEC_ROWS)
    near = MAX_DISTANCE + LANES
    rel_c = jnp.arange(near, dtype=jnp.int32)[None, :] - near - qpos
    rel_n = jnp.arange(LANES, dtype=jnp.int32)[None, :] - qpos
    bias_last = jnp.pad(_shifted_bias(rel_bias, rel_c)[hsel, rows], ((0, 0), (tkk - near, 0)))
    bias_c = jnp.concatenate([jnp.zeros((n_steps - 1, DEC_ROWS, tkk), F32), bias_last[None]], axis=0)
    bias_n = _shifted_bias(rel_bias, rel_n)[hsel, rows]
    heads_t = lambda a: a.transpose(0, 2, 3, 1)
    pad_new = lambda a: jnp.pad(heads_t(a.reshape(b, t, N_HEADS, HEAD_DIM)), ((0, 0), (0, 0), (0, 0), (0, LANES - t)))
    kv_tile = pl.BlockSpec((None, N_HEADS, HEAD_DIM, tkk), lambda bi, kt: (bi, 0, 0, kt))
    kv_new = pl.BlockSpec((None, N_HEADS, HEAD_DIM, LANES), lambda bi, kt: (bi, 0, 0, 0))
    return pl.pallas_call(
        functools.partial(_dec_attn_kernel, tq=t),
        grid=(b, n_steps),
        in_specs=[pl.BlockSpec((None, t, ATTN_W), lambda bi, kt: (bi, 0, 0)),
                  pl.BlockSpec((t, tkk), lambda bi, kt: (bi, kt)),
                  pl.BlockSpec((t, LANES), lambda bi, kt: (bi, p_len // LANES)),
                  kv_tile, kv_tile, kv_new, kv_new,
                  pl.BlockSpec((None, DEC_ROWS, tkk), lambda bi, kt: (kt, 0, 0)),
                  _const_spec(bias_n.shape)],
        out_specs=pl.BlockSpec((None, t, ATTN_W), lambda bi, kt: (bi, 0, 0)),
        out_shape=jax.ShapeDtypeStruct((b, t, ATTN_W), BF16),
        scratch_shapes=[pltpu.VMEM((DEC_ROWS, 1), F32), pltpu.VMEM((DEC_ROWS, 1), F32),
                        pltpu.VMEM((DEC_ROWS, HEAD_DIM), F32)],
        compiler_params=pltpu.CompilerParams(dimension_semantics=("arbitrary", "arbitrary"),
                                             vmem_limit_bytes=VMEM_LIMIT),
        name="dec_attn",
    )(q, madd, madd, heads_t(past_k), heads_t(past_v), pad_new(k_new), pad_new(v_new), bias_c, bias_n)


def _conv_kernel(*refs, tm, has_prev):
    if has_prev:
        u_ref, prev_ref, past_ref, w_ref, b_ref, lg_ref, lb_ref, o_ref, ext_scr = refs
        halo = jnp.where(pl.program_id(1) == 0, past_ref[...], prev_ref[...])
    else:
        u_ref, past_ref, w_ref, b_ref, lg_ref, lb_ref, o_ref, ext_scr = refs
        halo = past_ref[...]
    ext_scr[0, 0:HALO, :] = halo
    ext_scr[0, HALO:HALO + tm, :] = u_ref[...]
    n_rows = tm + HALO - SUBLANES
    for r in range(1, SUBLANES):
        ext_scr[r, 0:n_rows, :] = ext_scr[0, r:r + n_rows, :]
    rb = min(tm, 32)
    first = HALO - (CONV_K - 1)
    for r0 in range(0, tm, rb):
        acc = None
        for j in range(CONV_K):
            a, r = divmod(first + j, SUBLANES)
            t = ext_scr[r, r0 + a * SUBLANES:r0 + a * SUBLANES + rb, :] * w_ref[j:j + 1, :]
            acc = t if acc is None else acc + t
        c = acc + b_ref[...]
        mu = jnp.mean(c, axis=-1, keepdims=True)
        cc = c - mu
        var = jnp.mean(cc * cc, axis=-1, keepdims=True)
        y = cc * lax.rsqrt(var + EPS) * lg_ref[...] + lb_ref[...]
        o_ref[r0:r0 + rb, :] = (y * jax.nn.sigmoid(y)).astype(o_ref.dtype)


def _conv_mod(u, past, w):
    b, t, c = u.shape
    tm = min(ROW_TILE, t)
    assert t % tm == 0 and tm % 8 == 0
    has_prev = t > tm
    tile = pl.BlockSpec((None, tm, c), lambda bi, i: (bi, i, 0))
    halo = pl.BlockSpec((None, HALO, c), lambda bi, i: (bi, 0, 0))
    in_specs = [tile]
    args = [u]
    if has_prev:
        r = tm // HALO
        in_specs.append(pl.BlockSpec((None, HALO, c), lambda bi, i: (bi, jnp.maximum(i * r - 1, 0), 0)))
        args.append(u)
    consts = [w["conv_w"], w["conv_b"], w["ln_g"], w["ln_b"]]
    in_specs += [halo] + [_const_spec(x.shape) for x in consts]
    args += [past] + consts
    return pl.pallas_call(
        functools.partial(_conv_kernel, tm=tm, has_prev=has_prev),
        grid=(b, t // tm),
        in_specs=in_specs,
        out_specs=tile,
        out_shape=jax.ShapeDtypeStruct((b, t, c), BF16),
        scratch_shapes=[pltpu.VMEM((SUBLANES, HALO + tm, c), F32)],
        compiler_params=pltpu.CompilerParams(dimension_semantics=("arbitrary", "arbitrary"),
                                             vmem_limit_bytes=VMEM_LIMIT),
        name="conv_mod",
    )(*args)


def _mix_ffn_kernel(x_ref, attn_ref, cact_ref, sga_ref, sgc_ref, wao_ref, wco_ref, wo_ref, g_ref,
                    wg_ref, wu_ref, wd_ref, y_ref, *, n_chunks):
    attn_out = jnp.dot(attn_ref[...], wao_ref[...], preferred_element_type=F32)
    conv_out = jnp.dot(cact_ref[...], wco_ref[...], preferred_element_type=F32)
    merged = sga_ref[...].astype(F32) * attn_out + sgc_ref[...].astype(F32) * conv_out
    x1 = x_ref[...] + jnp.dot(merged.astype(BF16), wo_ref[...], preferred_element_type=F32)
    ms = jnp.mean(x1 * x1, axis=-1, keepdims=True)
    h2 = (x1 * lax.rsqrt(ms + EPS) * g_ref[...]).astype(BF16)
    hc = wg_ref.shape[1] // n_chunks
    y = x1
    for ci in range(n_chunks):
        sl = slice(ci * hc, (ci + 1) * hc)
        gate = jnp.dot(h2, wg_ref[:, sl], preferred_element_type=F32)
        up = jnp.dot(h2, wu_ref[:, sl], preferred_element_type=F32)
        act = (gate * jax.nn.sigmoid(gate) * up).astype(BF16)
        y = y + jnp.dot(act, wd_ref[sl, :], preferred_element_type=F32)
    y_ref[...] = y


def _mix_ffn(x, attn, cact, sga, sgc, w):
    m, d = x.shape
    tm = min(ROW_TILE, m)
    assert m % tm == 0
    hidden = w["wg"].shape[1]
    n_chunks = 2 if hidden % (2 * LANES) == 0 else 1
    row = lambda n: pl.BlockSpec((tm, n), lambda i: (i, 0))
    consts = [w[n] for n in ("wao", "wco", "wo", "g_ffn", "wg", "wu", "wd")]
    return pl.pallas_call(
        functools.partial(_mix_ffn_kernel, n_chunks=n_chunks),
        grid=(m // tm,),
        in_specs=[row(d), row(attn.shape[1]), row(cact.shape[1]), row(d), row(d)]
                 + [_const_spec(c.shape) for c in consts],
        out_specs=row(d),
        out_shape=jax.ShapeDtypeStruct((m, d), F32),
        compiler_params=pltpu.CompilerParams(dimension_semantics=("arbitrary",), vmem_limit_bytes=VMEM_LIMIT),
        name="mix_ffn",
    )(x, attn, cact, sga, sgc, *consts)


def _prep_weights(norm_mix_g, w_in, q_norm_g, k_norm_g, idx_k_norm_g, conv_dw_w, conv_dw_b, conv_ln_g, conv_ln_b,
                  w_conv_out, w_attn_out, w_out, norm_ffn_g, w_ffn_gate, w_ffn_up, w_ffn_down):
    d = w_in.shape[0]
    conv_ch = conv_dw_w.shape[-1]
    sizes = (ATTN_W, ATTN_W, ATTN_W, N_IDX_HEADS * IDX_DIM, IDX_DIM, N_IDX_HEADS, 2 * conv_ch, d, d)
    offs = np.concatenate([[0], np.cumsum(sizes)])
    assert offs[-1] == w_in.shape[1]
    col = lambda i: w_in[:, offs[i]:offs[i + 1]]
    wsm = jnp.concatenate([col(4), col(5), jnp.zeros((d, LANES - IDX_DIM - N_IDX_HEADS), w_in.dtype)], axis=1)
    head = np.arange(ATTN_W) // HEAD_DIM
    seg = (head[:, None] == np.arange(LANES)[None, :]).astype(np.float32)
    return dict(
        g_mix=norm_mix_g.reshape(1, d).astype(F32),
        wq=col(0).astype(BF16), wk=col(1).astype(BF16), wv=col(2).astype(BF16), wiq=col(3).astype(BF16),
        wsm=wsm.astype(BF16), wglu=col(6).astype(BF16), wga=col(7).astype(BF16), wgc=col(8).astype(BF16),
        qg=jnp.tile(q_norm_g.astype(F32), N_HEADS).reshape(1, ATTN_W),
        kg=jnp.tile(k_norm_g.astype(F32), N_HEADS).reshape(1, ATTN_W),
        ikg=jnp.concatenate([idx_k_norm_g.astype(F32), jnp.zeros((LANES - IDX_DIM,), F32)]).reshape(1, LANES),
        seg=jnp.asarray(seg, BF16), segt=jnp.asarray(seg.T, BF16),
        conv_w=conv_dw_w.reshape(CONV_K, conv_ch).astype(F32), conv_b=conv_dw_b.reshape(1, conv_ch).astype(F32),
        ln_g=conv_ln_g.reshape(1, conv_ch).astype(F32), ln_b=conv_ln_b.reshape(1, conv_ch).astype(F32),
        wao=w_attn_out.astype(BF16), wco=w_conv_out.astype(BF16), wo=w_out.astype(BF16),
        g_ffn=norm_ffn_g.reshape(1, d).astype(F32),
        wg=w_ffn_gate.astype(BF16), wu=w_ffn_up.astype(BF16), wd=w_ffn_down.astype(BF16),
    )


def _layer(x, pos0, past_k, past_v, past_ik, past_conv, rel_bias, w):
    b, t, d = x.shape
    m = b * t
    prompt = past_k is None
    q, k, k_b, v, v_b, iq, sm, sm_b, u, sga, sgc = _in_proj(x, w, transposed=prompt)
    conv_ch = u.shape[1]
    u = u.reshape(b, t, conv_ch)
    k_b = k_b.reshape(b, t, ATTN_W)
    ik_b = sm_b.reshape(b, t, LANES)[:, :, :IDX_DIM]
    if prompt:
        attn = _sparse_attn(q, iq, sm[:, IDX_DIM:IDX_DIM + N_IDX_HEADS, :], k_b, v_b, ik_b, rel_bias,
                            pos0).swapaxes(1, 2)
        heads = lambda a: a.reshape(b, N_HEADS, HEAD_DIM, t).transpose(0, 3, 1, 2)
        new_k, new_v, new_ik = heads(k), heads(v), sm[:, :IDX_DIM, :].swapaxes(1, 2)
    else:
        sm = sm.reshape(b, t, LANES)
        k, v = k.reshape(b, t, ATTN_W), v.reshape(b, t, ATTN_W)
        attn = _dec_attn(q.reshape(b, t, ATTN_W), iq.reshape(b, t, ATTN_W), sm[:, :, IDX_DIM:IDX_DIM + N_IDX_HEADS],
                         past_k, past_v, past_ik, k, v, ik_b, rel_bias, pos0)
        heads = lambda a: a.reshape(b, t, N_HEADS, HEAD_DIM)
        new_k, new_v, new_ik = heads(k), heads(v), sm[:, :, :IDX_DIM]

    past = jnp.pad(past_conv.astype(F32), ((0, 0), (HALO - (CONV_K - 1), 0), (0, 0)))
    cact = _conv_mod(u, past, w)
    y = _mix_ffn(x.reshape(m, d), attn.reshape(m, ATTN_W), cact.reshape(m, conv_ch), sga, sgc, w)

    new_conv = jnp.concatenate([past_conv, u], axis=1)[:, -(CONV_K - 1):]
    return y.reshape(b, t, d), new_k, new_v, new_ik, new_conv


def kernel(x_prompt, x_sample, cache_k, cache_v, cache_idx_k, state_conv, rel_bias, norm_mix_g, w_in, q_norm_g, k_norm_g, idx_k_norm_g, conv_dw_w, conv_dw_b, conv_ln_g, conv_ln_b, w_conv_out, w_attn_out, w_out, norm_ffn_g, w_ffn_gate, w_ffn_up, w_ffn_down):
    depth = w_in.shape[0]
    bp = x_prompt.shape[0]
    past_len = cache_k.shape[2]
    conv_ch = conv_dw_w.shape[-1]
    yp, ys = x_prompt, x_sample
    outs_p, outs_s = [], []
    for l in range(depth):
        w = _prep_weights(norm_mix_g[l], w_in[l], q_norm_g[l], k_norm_g[l], idx_k_norm_g[l], conv_dw_w[l],
                          conv_dw_b[l], conv_ln_g[l], conv_ln_b[l], w_conv_out[l], w_attn_out[l], w_out[l],
                          norm_ffn_g[l], w_ffn_gate[l], w_ffn_up[l], w_ffn_down[l])
        zero_conv = jnp.zeros((bp, CONV_K - 1, conv_ch), yp.dtype)
        yp, *rest_p = _layer(yp, 0, None, None, None, zero_conv, rel_bias, w)
        outs_p.append(rest_p)
        ys, *rest_s = _layer(ys, past_len, cache_k[l], cache_v[l], cache_idx_k[l], state_conv[l], rel_bias, w)
        outs_s.append(rest_s)
    stack = lambda outs, i: jnp.stack([o[i] for o in outs])
    return (yp, ys, stack(outs_p, 0), stack(outs_p, 1), stack(outs_p, 2), stack(outs_p, 3),
            stack(outs_s, 0), stack(outs_s, 1), stack(outs_s, 2), stack(outs_s, 3))
```

```python
import functools
import math

import numpy as np
import jax
import jax.numpy as jnp
from jax import lax
from jax.experimental import pallas as pl
from jax.experimental.pallas import tpu as pltpu

CHUNK = 64
CHUNK_SHIFT = 6
N_HEADS = 8
HEAD_DIM = 64
ATTN_W = N_HEADS * HEAD_DIM
N_IDX_HEADS = 8
IDX_DIM = 64
TOPK_MAX = 256
CONV_K = 31
NUM_BUCKETS = 32
MAX_DISTANCE = 128
EPS = 1e-6
NEG = -1e30
INT_MIN = -(2 ** 31)
BF16_INF_BITS = 0x7F80
BF16_MIN_NORMAL_BITS = 0x0080
N_BF16_NEG = BF16_INF_BITS - BF16_MIN_NORMAL_BITS + 1
LOG2E = math.log2(math.e)

LANES = 128
SUBLANES = 8
HALO = 32
KEY_TILE = 256
V_ROWS = HEAD_DIM + 16
ROW_TILE = 256
VMEM_LIMIT = 56 * 1024 * 1024

F32 = jnp.float32
BF16 = jnp.bfloat16


def _const_spec(shape):
    return pl.BlockSpec(shape, lambda *_: (0,) * len(shape), pipeline_mode=pl.Buffered(1))


def _split_dot(x, m):
    hi = x.astype(BF16)
    lo = (x - hi.astype(F32)).astype(BF16)
    return jnp.dot(hi, m, preferred_element_type=F32) + jnp.dot(lo, m, preferred_element_type=F32)


def _in_proj_kernel(x_ref, g_ref, wq_ref, wk_ref, wv_ref, wiq_ref, wsm_ref, wglu_ref, wga_ref, wgc_ref,
                    qg_ref, kg_ref, ikg_ref, seg_ref, segt_ref,
                    q_ref, k_ref, kb_ref, v_ref, vb_ref, iq_ref, sm_ref, smb_ref, u_ref, sga_ref, sgc_ref,
                    *, transposed):
    x = x_ref[...]
    ms = jnp.mean(x * x, axis=-1, keepdims=True)
    h = (x * lax.rsqrt(ms + EPS) * g_ref[...]).astype(BF16)

    def put(ref, val):
        ref[...] = (val.T if transposed else val).astype(ref.dtype)

    glu = jnp.dot(h, wglu_ref[...], preferred_element_type=F32)
    c = glu.shape[1] // 2
    u_ref[...] = glu[:, :c] * jax.nn.sigmoid(glu[:, c:])

    def head_rms(y, gain):
        ss = _split_dot(y * y, seg_ref[...])
        r = lax.rsqrt(ss * (1.0 / HEAD_DIM) + EPS)
        return y * _split_dot(r, segt_ref[...]) * gain

    q = head_rms(jnp.dot(h, wq_ref[...], preferred_element_type=F32), qg_ref[...])
    put(q_ref, q * (HEAD_DIM ** -0.5 * LOG2E))
    k = head_rms(jnp.dot(h, wk_ref[...], preferred_element_type=F32), kg_ref[...])
    put(k_ref, k)
    kb_ref[...] = k.astype(BF16)
    v = jnp.dot(h, wv_ref[...], preferred_element_type=F32)
    put(v_ref, v)
    if transposed:
        vt = v.T.astype(BF16)
        pad_rows = lax.broadcasted_iota(jnp.int32, (V_ROWS - HEAD_DIM, vt.shape[1]), 0)
        ones_row = jnp.where(pad_rows == 0, 1.0, 0.0).astype(BF16)
        for hd in range(N_HEADS):
            vb_ref[hd * V_ROWS:hd * V_ROWS + HEAD_DIM, :] = vt[hd * HEAD_DIM:(hd + 1) * HEAD_DIM, :]
            vb_ref[hd * V_ROWS + HEAD_DIM:(hd + 1) * V_ROWS, :] = ones_row
    else:
        vb_ref[...] = v.astype(BF16)
    iq = jnp.dot(h, wiq_ref[...], preferred_element_type=F32)
    put(iq_ref, iq * (IDX_DIM ** -0.5))

    sm = jnp.dot(h, wsm_ref[...], preferred_element_type=F32)
    lane = lax.broadcasted_iota(jnp.int32, sm.shape, 1)
    is_ik = lane < IDX_DIM
    ss = jnp.sum(jnp.where(is_ik, sm * sm, 0.0), axis=-1, keepdims=True)
    ikn = sm * lax.rsqrt(ss * (1.0 / IDX_DIM) + EPS) * ikg_ref[...]
    iw = sm * (N_IDX_HEADS ** -0.5)
    smo = jnp.where(is_ik, ikn, jnp.where(lane < IDX_DIM + N_IDX_HEADS, iw, 0.0))
    put(sm_ref, smo)
    smb_ref[...] = smo.astype(BF16)

    sga_ref[...] = jax.nn.sigmoid(jnp.dot(h, wga_ref[...], preferred_element_type=F32)).astype(BF16)
    sgc_ref[...] = jax.nn.sigmoid(jnp.dot(h, wgc_ref[...], preferred_element_type=F32)).astype(BF16)


def _in_proj(x, w, transposed):
    b, t, d = x.shape
    m = b * t
    tm = min(ROW_TILE, t if transposed else m)
    assert m % tm == 0 and (not transposed or t % tm == 0)
    nt = t // tm
    conv_ch = w["wglu"].shape[1] // 2
    consts = [w[n] for n in ("g_mix", "wq", "wk", "wv", "wiq", "wsm", "wglu", "wga", "wgc",
                             "qg", "kg", "ikg", "seg", "segt")]

    def rows(n, dtype):
        return jax.ShapeDtypeStruct((m, n), dtype), pl.BlockSpec((tm, n), lambda i: (i, 0))

    def cols(n, dtype):
        if not transposed:
            return rows(n, dtype)
        return jax.ShapeDtypeStruct((b, n, t), dtype), pl.BlockSpec((None, n, tm), lambda i: (i // nt, 0, i % nt))

    if transposed:
        v_slab = (jax.ShapeDtypeStruct((b, nt, N_HEADS * V_ROWS, tm), BF16),
                  pl.BlockSpec((None, None, N_HEADS * V_ROWS, tm), lambda i: (i // nt, i % nt, 0, 0)))
    else:
        v_slab = rows(ATTN_W, BF16)
    outs = [
        cols(ATTN_W, BF16),
        cols(ATTN_W, F32),
        rows(ATTN_W, BF16),
        cols(ATTN_W, F32),
        v_slab,
        cols(ATTN_W, BF16),
        cols(LANES, F32),
        rows(LANES, BF16),
        rows(conv_ch, F32),
        rows(d, BF16),
        rows(d, BF16),
    ]
    return pl.pallas_call(
        functools.partial(_in_proj_kernel, transposed=transposed),
        grid=(m // tm,),
        in_specs=[pl.BlockSpec((tm, d), lambda i: (i, 0))] + [_const_spec(c.shape) for c in consts],
        out_specs=[o[1] for o in outs],
        out_shape=[o[0] for o in outs],
        compiler_params=pltpu.CompilerParams(dimension_semantics=("arbitrary",), vmem_limit_bytes=VMEM_LIMIT),
        name="in_proj",
    )(x.reshape(m, d), *consts)


def _fold(x, op):
    return op(x.reshape(x.shape[0] // SUBLANES, SUBLANES, x.shape[1]), axis=0)


def _select_top_k(key_scr, part_scr, n_kt, *, tk, tq, top_k):
    @pl.when(n_kt % 2 == 1)
    def _():
        key_scr[n_kt] = jnp.full((tk, tq), INT_MIN, jnp.int32)
        part_scr[n_kt] = jnp.full((tk, tq), -jnp.inf, BF16)

    n_pairs = (n_kt + 1) // 2

    def count(pred):
        def body(i, acc):
            for kt in (2 * i, 2 * i + 1):
                acc = acc + _fold(jnp.where(pred(key_scr[kt], kt), 1.0, 0.0), jnp.sum)
            return acc
        acc = lax.fori_loop(0, n_pairs, body, jnp.zeros((SUBLANES, tq), F32))
        return jnp.sum(acc, axis=0, keepdims=True)

    rows16 = 2 * SUBLANES
    one_b, zero_b = jnp.ones((), BF16), jnp.zeros((), BF16)

    def count_part(pred):
        def body(i, acc):
            for kt in (2 * i, 2 * i + 1):
                c = jnp.where(pred(part_scr[kt]), one_b, zero_b)
                part = c[0:rows16]
                for j in range(1, tk // rows16):
                    part = part + c[j * rows16:(j + 1) * rows16]
                acc = acc + part.astype(F32)
            return acc
        acc = lax.fori_loop(0, n_pairs, body, jnp.zeros((rows16, tq), F32))
        return jnp.sum(acc, axis=0, keepdims=True)

    def search(n_bits, base, to_image, limit):
        def body(i, t):
            cand = t + (jnp.int32(1) << (n_bits - 1 - i))
            img = to_image(cand)
            c = base + count_part(lambda a: a >= img)
            return jnp.where((c >= top_k) & (cand <= limit), cand, t)
        return lax.fori_loop(0, n_bits, body, jnp.zeros((1, tq), jnp.int32))

    def refine(eq_img, shift):
        def body(i, carry):
            for kt in (2 * i, 2 * i + 1):
                nxt = ((key_scr[kt] >> shift) & 0xFF).astype(F32).astype(BF16)
                part_scr[kt] = jnp.where(part_scr[kt] == eq_img, nxt, -one_b)
            return carry
        lax.fori_loop(0, n_pairs, body, 0)

    def top_bits(c):
        neg = 0x8000 | (BF16_INF_BITS - c)
        pos = BF16_MIN_NORMAL_BITS + (c - N_BF16_NEG - 1)
        return jnp.where(c < N_BF16_NEG, neg, jnp.where(c == N_BF16_NEG, 0, pos))

    def top_image(c):
        return pltpu.bitcast(top_bits(c) << 16, F32).astype(BF16)

    small_image = lambda c: c.astype(F32).astype(BF16)

    t1 = search(16, 0.0, top_image, 2 * N_BF16_NEG)
    img1 = top_image(t1)
    above = count_part(lambda a: a > img1)
    refine(img1, 8)
    t2 = search(8, above, small_image, 255)
    img2 = small_image(t2)
    above = above + count_part(lambda a: a > img2)
    refine(img2, 0)
    t3 = search(8, above, small_image, 255)
    hi = top_bits(t1)
    hi = hi - ((hi >> 15) << 16)
    hi = hi ^ ((hi >> 15) & 0x7FFF)
    thr = (hi << 16) + (t2 << 8) + t3
    thr = jnp.where(t1 == 0, INT_MIN + 1, jnp.maximum(thr, INT_MIN + 1))

    n_ge = count(lambda key, kt: key >= thr)
    has_tie = jnp.max(n_ge) > top_k

    @pl.when(has_tie)
    def _():
        need_m1 = (top_k - 1) - count(lambda key, kt: key > thr)
        row = lax.broadcasted_iota(jnp.int32, (tk, tq), 0)

        def idx_body(i, cut):
            cand = cut + (jnp.int32(1) << (15 - i))
            c = count(lambda key, kt: (key == thr) & (kt * tk + row < cand))
            return jnp.where(c <= need_m1, cand, cut)

        cut = lax.fori_loop(0, 16, idx_body, jnp.zeros((1, tq), jnp.int32))
        tie_col = n_ge > top_k

        def drop_body(kt, carry):
            key = key_scr[kt]
            drop = tie_col & (key == thr) & (kt * tk + row > cut)
            key_scr[kt] = jnp.where(drop, INT_MIN, key)
            return carry

        lax.fori_loop(0, n_kt, drop_body, 0)
    return thr


def _sparse_attn_kernel(qt_ref, iqt_ref, iwt_ref, k_ref, vt_ref, ik_ref, bias_ref, o_ref,
                        key_scr, part_scr, m_scr, acc_scr, s_scr, *, tq, tk, nkt, pos0, n_keys, top_k):
    qi = pl.program_id(1)
    q0 = pos0 + qi * tq
    n_kt = jnp.minimum(nkt, ((q0 + tq - 1) // CHUNK * CHUNK + CHUNK + tk - 1) // tk)
    last = n_kt - 1

    fold = _fold

    iqt = iqt_ref[...]
    iqt_h = [iqt[h * IDX_DIM:(h + 1) * IDX_DIM, :] for h in range(N_IDX_HEADS)]
    iwt = iwt_ref[...]
    iw_h = [iwt[h:h + 1, :] for h in range(N_IDX_HEADS)]

    def score_tile(kt, masked):
        ik = ik_ref[kt]
        s = None
        for h in range(N_IDX_HEADS):
            t = jnp.maximum(jnp.dot(ik, iqt_h[h], preferred_element_type=F32), 0.0) * iw_h[h]
            s = t if s is None else s + t
        s = s + 0.0
        bits = pltpu.bitcast(s, jnp.int32)
        key = bits ^ ((bits >> 31) & 0x7FFFFFFF)
        top = pltpu.bitcast(bits & -65536, F32)
        if masked:
            kp = kt * tk + lax.broadcasted_iota(jnp.int32, (tk, tq), 0)
            qp = q0 + lax.broadcasted_iota(jnp.int32, (tk, tq), 1)
            adm = ((kp >> CHUNK_SHIFT) <= (qp >> CHUNK_SHIFT)) & (kp < n_keys)
            key = jnp.where(adm, key, INT_MIN)
            top = jnp.where(adm, top, -jnp.inf)
        key_scr[kt] = key
        part_scr[kt] = top.astype(BF16)

    def score_body(i, carry):
        score_tile(2 * i, False)
        score_tile(2 * i + 1, False)
        return carry

    lax.fori_loop(0, last // 2, score_body, 0)

    @pl.when(last % 2 == 1)
    def _():
        score_tile(last - 1, False)

    score_tile(last, True)

    thr = _select_top_k(key_scr, part_scr, n_kt, tk=tk, tq=tq, top_k=top_k)

    qt = qt_ref[...]
    slab_row = lax.broadcasted_iota(jnp.int32, (LANES, tq), 0)
    qz = []
    for h in range(N_HEADS):
        slab = qt[(h // 2) * LANES:(h // 2 + 1) * LANES, :]
        mine = (slab_row < HEAD_DIM) if h % 2 == 0 else (slab_row >= HEAD_DIM)
        qz.append(jnp.where(mine, slab, jnp.zeros_like(slab)))
    m_scr[...] = jnp.full(m_scr.shape, NEG, F32)
    acc_scr[...] = jnp.zeros(acc_scr.shape, F32)

    def park(kt, slot):
        madd = jnp.where(key_scr[kt] >= thr, 0.0, NEG)
        for h in range(N_HEADS):
            pair = h // 2
            s = jnp.dot(k_ref[kt, :, pair * LANES:(pair + 1) * LANES], qz[h], preferred_element_type=F32)
            s_scr[slot, h] = s + madd

    def absorb(kt, slot, bias_idx):
        for h in range(N_HEADS):
            s = s_scr[slot, h]
            if bias_idx is not None:
                s = s + bias_ref[bias_idx, h]
            m_new = jnp.maximum(m_scr[h], jnp.max(fold(s, jnp.max), axis=0, keepdims=True))
            alpha = jnp.exp2(m_scr[h] - m_new)
            p = jnp.exp2(s - m_new)
            pv = jnp.dot(vt_ref[kt, h * V_ROWS:(h + 1) * V_ROWS, :], p.astype(BF16), preferred_element_type=F32)
            acc_scr[h] = alpha * acc_scr[h] + pv
            m_scr[h] = m_new

    park(0, 0)
    n_trips = (n_kt - 2) // 2

    def far_body(i, carry):
        kt = 2 * i
        park(kt + 1, 1)
        absorb(kt, 0, None)
        park(kt + 2, 0)
        absorb(kt + 1, 1, None)
        return carry

    lax.fori_loop(0, n_trips, far_body, 0)
    done = 2 * jnp.maximum(n_trips, 0)

    @pl.when(n_kt == 1)
    def _():
        absorb(0, 0, 1)

    @pl.when((n_kt >= 2) & (n_kt % 2 == 0))
    def _():
        park(done + 1, 1)
        absorb(done, 0, 0)
        absorb(done + 1, 1, 1)

    @pl.when((n_kt >= 3) & (n_kt % 2 == 1))
    def _():
        park(done + 1, 1)
        absorb(done, 0, None)
        park(done + 2, 0)
        absorb(done + 1, 1, 0)
        absorb(done + 2, 0, 1)

    for h in range(N_HEADS):
        acc = acc_scr[h]
        o_ref[h * HEAD_DIM:(h + 1) * HEAD_DIM, :] = (acc[:HEAD_DIM] / acc[HEAD_DIM:HEAD_DIM + 1]).astype(o_ref.dtype)


def _rel_bucket(rel):
    nb = NUM_BUCKETS // 2
    max_exact = nb // 2
    ret = jnp.where(rel > 0, nb, 0)
    n = jnp.abs(rel)
    nf = jnp.maximum(n, 1).astype(jnp.float32)
    large = max_exact + (jnp.log(nf / max_exact) / math.log(MAX_DISTANCE / max_exact) * (nb - max_exact)).astype(jnp.int32)
    large = jnp.minimum(large, nb - 1)
    return ret + jnp.where(n < max_exact, n, large)


def _bias_tiles(rel_bias, tq, tk):
    j = jnp.arange(tk, dtype=jnp.int32)[:, None]
    i = jnp.arange(tq, dtype=jnp.int32)[None, :]
    return jnp.stack([_shifted_bias(rel_bias, j - i + d) for d in (-tk, 0)])


def _shifted_bias(rel_bias, rel):
    rb = (rel_bias.astype(F32) - rel_bias[_rel_bucket(jnp.int32(-MAX_DISTANCE))].astype(F32)) * LOG2E
    onehot = jax.nn.one_hot(_rel_bucket(rel), NUM_BUCKETS, dtype=F32)
    return jnp.moveaxis(jnp.einsum("...b,bh->...h", onehot, rb, precision=lax.Precision.HIGHEST), -1, 0)


def _check_tiling(t, tq, tk, nkt, pos0, n_keys):
    for qi in range(t // tq):
        q0 = pos0 + qi * tq
        n_kt = min(nkt, ((q0 + tq - 1) // CHUNK * CHUNK + CHUNK + tk - 1) // tk)
        adm_end = min(q0 // CHUNK * CHUNK + CHUNK, n_keys)
        assert (n_kt - 1) * tk <= adm_end, "only the last visited tile may hold inadmissible keys"
        assert n_kt * tk >= min((q0 + tq - 1) // CHUNK * CHUNK + CHUNK, n_keys), "visited tiles cover every admissible key"
        assert (n_kt - 1) * tk == q0, "last tile starts with the query tile"
        assert tk >= MAX_DISTANCE, "tiles before the last two are at least MAX_DISTANCE behind"


def _sparse_attn(qt, iqt, iwt, k_b, vt_t, ik_b, rel_bias, pos0):
    b, _, t = qt.shape
    n_keys = k_b.shape[1]
    top_k = min(TOPK_MAX, n_keys // 4)
    tk = KEY_TILE
    tq = min(tk, t)
    nkt = n_keys // tk
    assert n_keys % tk == 0 and vt_t.shape == (b, nkt, N_HEADS * V_ROWS, tk)
    _check_tiling(t, tq, tk, nkt, pos0, n_keys)
    k_t = k_b.reshape(b, nkt, tk, ATTN_W)
    ik_t = ik_b.reshape(b, nkt, tk, IDX_DIM)
    bias = _bias_tiles(rel_bias, tq, tk)

    qspec = lambda n: pl.BlockSpec((None, n, tq), lambda bi, qi: (bi, 0, qi))
    kspec = lambda r, c: pl.BlockSpec((None, nkt, r, c), lambda bi, qi: (bi, 0, 0, 0))
    kern = functools.partial(_sparse_attn_kernel, tq=tq, tk=tk, nkt=nkt, pos0=pos0, n_keys=n_keys, top_k=top_k)
    return pl.pallas_call(
        kern,
        grid=(b, t // tq),
        in_specs=[qspec(ATTN_W), qspec(ATTN_W), qspec(N_IDX_HEADS),
                  kspec(tk, ATTN_W), kspec(N_HEADS * V_ROWS, tk), kspec(tk, IDX_DIM), _const_spec(bias.shape)],
        out_specs=qspec(ATTN_W),
        out_shape=jax.ShapeDtypeStruct((b, ATTN_W, t), BF16),
        scratch_shapes=[pltpu.VMEM((nkt + 1, tk, tq), jnp.int32),
                        pltpu.VMEM((nkt + 1, tk, tq), BF16),
                        pltpu.VMEM((N_HEADS, 1, tq), F32),
                        pltpu.VMEM((N_HEADS, V_ROWS, tq), F32),
                        pltpu.VMEM((2, N_HEADS, tk, tq), F32)],
        compiler_params=pltpu.CompilerParams(dimension_semantics=("arbitrary", "arbitrary"),
                                             vmem_limit_bytes=VMEM_LIMIT),
        name="sparse_attn",
    )(qt, iqt, iwt, k_t, vt_t, ik_t, bias)


DEC_ROWS = N_HEADS * 16
DEC_KEY_TILE = 1024
NT_DIMS = (((1,), (1,)), ((), ()))


def _dec_select_kernel(iq_ref, iwc_ref, ikt_ref, iktn_ref, madd_ref, key_scr, *, g, tq, n_keys, n_pad, top_k):
    n_slabs = n_pad // LANES
    col = lax.broadcasted_iota(jnp.int32, (tq, n_pad), 1)
    for s in range(g):
        iq_s = iq_ref[s * tq:(s + 1) * tq, :]
        lhs = jnp.concatenate([iq_s[:, h * IDX_DIM:(h + 1) * IDX_DIM] for h in range(N_IDX_HEADS)], axis=0)
        logit = jnp.concatenate([jnp.dot(lhs, ikt_ref[s].astype(BF16), preferred_element_type=F32),
                                 jnp.dot(lhs, iktn_ref[s], preferred_element_type=F32)], axis=1)
        x = jnp.maximum(logit, 0.0) * iwc_ref[s]
        score = x[0:tq]
        for h in range(1, N_IDX_HEADS):
            score = score + x[h * tq:(h + 1) * tq]
        score = score + 0.0
        bits = pltpu.bitcast(score, jnp.int32)
        key = bits ^ ((bits >> 31) & 0x7FFFFFFF)
        key_scr[s * tq:(s + 1) * tq, :] = jnp.where(col < n_keys, key, INT_MIN)

    rows = g * tq
    lane = lax.broadcasted_iota(jnp.int32, (rows, LANES), 1)

    def count(pred):
        acc = jnp.zeros((rows, LANES), F32)
        for j in range(n_slabs):
            acc = acc + jnp.where(pred(key_scr[:, j * LANES:(j + 1) * LANES], j), 1.0, 0.0)
        return jnp.sum(acc, axis=1, keepdims=True)

    def thr_body(i, t):
        cand = t + (jnp.int32(1) << (31 - i))
        c = count(lambda key, j: key >= cand)
        return jnp.where(c >= top_k, cand, t)

    thr = lax.fori_loop(0, 32, thr_body, jnp.full((rows, 1), INT_MIN, jnp.int32))
    thr = jnp.maximum(thr, INT_MIN + 1)

    n_ge = count(lambda key, j: key >= thr)
    has_tie = jnp.max(n_ge) > top_k

    @pl.when(has_tie)
    def _():
        need_m1 = (top_k - 1) - count(lambda key, j: key > thr)

        def idx_body(i, cut):
            cand = cut + (jnp.int32(1) << (15 - i))
            c = count(lambda key, j: (key == thr) & (j * LANES + lane < cand))
            return jnp.where(c <= need_m1, cand, cut)

        cut = lax.fori_loop(0, 16, idx_body, jnp.zeros((rows, 1), jnp.int32))
        tie_row = n_ge > top_k
        for j in range(n_slabs):
            key = key_scr[:, j * LANES:(j + 1) * LANES]
            drop = tie_row & (key == thr) & (j * LANES + lane > cut)
            key_scr[:, j * LANES:(j + 1) * LANES] = jnp.where(drop, INT_MIN, key)

    madd_ref[...] = jnp.where(key_scr[...] >= thr, 0.0, NEG)


def _dec_attn_kernel(q_ref, maddc_ref, maddn_ref, k_ref, v_ref, kn_ref, vn_ref, biasc_ref, biasn_ref, o_ref,
                     m_scr, l_scr, acc_scr, *, tq):
    kt = pl.program_id(1)
    q = q_ref[...]
    q_h = [q[:, h * HEAD_DIM:(h + 1) * HEAD_DIM] for h in range(N_HEADS)]

    @pl.when(kt == 0)
    def _():
        m_scr[...] = jnp.full(m_scr.shape, NEG, F32)
        l_scr[...] = jnp.zeros(l_scr.shape, F32)
        acc_scr[...] = jnp.zeros(acc_scr.shape, F32)

    def step(ktr, vtr, madd, bias):
        s = jnp.concatenate([jnp.dot(q_h[h], ktr[h].astype(BF16), preferred_element_type=F32)
                             for h in range(N_HEADS)], axis=0)
        s = s + jnp.concatenate([madd] * N_HEADS, axis=0) + bias
        m_prev = m_scr[...]
        m_new = jnp.maximum(m_prev, jnp.max(s, axis=1, keepdims=True))
        alpha = jnp.exp2(m_prev - m_new)
        p = jnp.exp2(s - m_new)
        l_scr[...] = alpha * l_scr[...] + jnp.sum(p, axis=1, keepdims=True)
        pb = p.astype(BF16)
        pv = jnp.concatenate([lax.dot_general(pb[h * tq:(h + 1) * tq, :], vtr[h].astype(BF16), NT_DIMS,
                                              preferred_element_type=F32) for h in range(N_HEADS)], axis=0)
        acc_scr[...] = alpha * acc_scr[...] + pv
        m_scr[...] = m_new

    step(k_ref, v_ref, maddc_ref[...], biasc_ref[...])

    @pl.when(kt == pl.num_programs(1) - 1)
    def _():
        step(kn_ref, vn_ref, maddn_ref[...], biasn_ref[...])
        o = acc_scr[...] / l_scr[...]
        for h in range(N_HEADS):
            o_ref[:, h * HEAD_DIM:(h + 1) * HEAD_DIM] = o[h * tq:(h + 1) * tq, :].astype(o_ref.dtype)


def _dec_attn(q, iq, iw, past_k, past_v, past_ik, k_new, v_new, ik_new, rel_bias, pos0):
    b, t, _ = q.shape
    p_len = past_k.shape[1]
    n_keys = p_len + t
    top_k = min(TOPK_MAX, n_keys // 4)
    tkk = min(DEC_KEY_TILE, p_len)
    n_pad = p_len + LANES
    g = min(LANES // t, b)
    assert N_HEADS * t == DEC_ROWS and b % g == 0 and p_len % tkk == 0 and t <= LANES
    assert (pos0 + t - 1) // CHUNK == pos0 // CHUNK and (n_keys - 1) // CHUNK <= pos0 // CHUNK, "every key admissible"
    assert tkk >= MAX_DISTANCE + LANES >= MAX_DISTANCE + t, "only the last cache tile is within MAX_DISTANCE of a query"

    ikt_new = jnp.pad(ik_new.swapaxes(1, 2), ((0, 0), (0, 0), (0, LANES - t)))
    iw_col = iw.swapaxes(1, 2).reshape(b, DEC_ROWS, 1)
    madd = pl.pallas_call(
        functools.partial(_dec_select_kernel, g=g, tq=t, n_keys=n_keys, n_pad=n_pad, top_k=top_k),
        grid=(b // g,),
        in_specs=[pl.BlockSpec((g * t, ATTN_W), lambda i: (i, 0)),
                  pl.BlockSpec((g, DEC_ROWS, 1), lambda i: (i, 0, 0)),
                  pl.BlockSpec((g, IDX_DIM, p_len), lambda i: (i, 0, 0)),
                  pl.BlockSpec((g, IDX_DIM, LANES), lambda i: (i, 0, 0))],
        out_specs=pl.BlockSpec((g * t, n_pad), lambda i: (i, 0)),
        out_shape=jax.ShapeDtypeStruct((b * t, n_pad), F32),
        scratch_shapes=[pltpu.VMEM((g * t, n_pad), jnp.int32)],
        compiler_params=pltpu.CompilerParams(dimension_semantics=("arbitrary",), vmem_limit_bytes=VMEM_LIMIT),
        name="dec_select",
    )(iq.reshape(b * t, ATTN_W), iw_col, past_ik.swapaxes(1, 2), ikt_new)

    n_steps = p_len // tkk
    qpos = jnp.tile(jnp.arange(t, dtype=jnp.int32), N_HEADS)[:, None]
    hsel = jnp.repeat(jnp.arange(N_HEADS), t)
    rows = jnp.arange(DEC_ROWS)
    near = MAX_DISTANCE + LANES
    rel_c = jnp.arange(near, dtype=jnp.int32)[None, :] - near - qpos
    rel_n = jnp.arange(LANES, dtype=jnp.int32)[None, :] - qpos
    bias_last = jnp.pad(_shifted_bias(rel_bias, rel_c)[hsel, rows], ((0, 0), (tkk - near, 0)))
    bias_c = jnp.concatenate([jnp.zeros((n_steps - 1, DEC_ROWS, tkk), F32), bias_last[None]], axis=0)
    bias_n = _shifted_bias(rel_bias, rel_n)[hsel, rows]
    heads_t = lambda a: a.transpose(0, 2, 3, 1)
    pad_new = lambda a: jnp.pad(heads_t(a.reshape(b, t, N_HEADS, HEAD_DIM)), ((0, 0), (0, 0), (0, 0), (0, LANES - t)))
    kv_tile = pl.BlockSpec((None, N_HEADS, HEAD_DIM, tkk), lambda bi, kt: (bi, 0, 0, kt))
    kv_new = pl.BlockSpec((None, N_HEADS, HEAD_DIM, LANES), lambda bi, kt: (bi, 0, 0, 0))
    return pl.pallas_call(
        functools.partial(_dec_attn_kernel, tq=t),
        grid=(b, n_steps),
        in_specs=[pl.BlockSpec((None, t, ATTN_W), lambda bi, kt: (bi, 0, 0)),
                  pl.BlockSpec((t, tkk), lambda bi, kt: (bi, kt)),
                  pl.BlockSpec((t, LANES), lambda bi, kt: (bi, p_len // LANES)),
                  kv_tile, kv_tile, kv_new, kv_new,
                  pl.BlockSpec((None, DEC_ROWS, tkk), lambda bi, kt: (kt, 0, 0)),
                  _const_spec(bias_n.shape)],
        out_specs=pl.BlockSpec((None, t, ATTN_W), lambda bi, kt: (bi, 0, 0)),
        out_shape=jax.ShapeDtypeStruct((b, t, ATTN_W), BF16),
        scratch_shapes=[pltpu.VMEM((DEC_ROWS, 1), F32), pltpu.VMEM((DEC_ROWS, 1), F32),
                        pltpu.VMEM((DEC_ROWS, HEAD_DIM), F32)],
        compiler_params=pltpu.CompilerParams(dimension_semantics=("arbitrary", "arbitrary"),
                                             vmem_limit_bytes=VMEM_LIMIT),
        name="dec_attn",
    )(q, madd, madd, heads_t(past_k), heads_t(past_v), pad_new(k_new), pad_new(v_new), bias_c, bias_n)


def _conv_kernel(*refs, tm, has_prev):
    if has_prev:
        u_ref, prev_ref, past_ref, w_ref, b_ref, lg_ref, lb_ref, o_ref, ext_scr = refs
        halo = jnp.where(pl.program_id(1) == 0, past_ref[...], prev_ref[...])
    else:
        u_ref, past_ref, w_ref, b_ref, lg_ref, lb_ref, o_ref, ext_scr = refs
        halo = past_ref[...]
    _conv_act(halo, u_ref[...], w_ref, b_ref, lg_ref, lb_ref, o_ref, ext_scr)


def _conv_act(halo, u, w_ref, b_ref, lg_ref, lb_ref, o_ref, ext_scr):
    tm = u.shape[0]
    ext_scr[0, 0:HALO, :] = halo
    ext_scr[0, HALO:HALO + tm, :] = u
    n_rows = tm + HALO - SUBLANES
    for r in range(1, SUBLANES):
        ext_scr[r, 0:n_rows, :] = ext_scr[0, r:r + n_rows, :]
    rb = min(tm, 32)
    first = HALO - (CONV_K - 1)
    for r0 in range(0, tm, rb):
        acc = None
        for j in range(CONV_K):
            a, r = divmod(first + j, SUBLANES)
            t = ext_scr[r, r0 + a * SUBLANES:r0 + a * SUBLANES + rb, :] * w_ref[j:j + 1, :]
            acc = t if acc is None else acc + t
        c = acc + b_ref[...]
        mu = jnp.mean(c, axis=-1, keepdims=True)
        cc = c - mu
        var = jnp.mean(cc * cc, axis=-1, keepdims=True)
        y = cc * lax.rsqrt(var + EPS) * lg_ref[...] + lb_ref[...]
        o_ref[r0:r0 + rb, :] = (y * jax.nn.sigmoid(y)).astype(o_ref.dtype)


def _conv_mod(u, past, w):
    b, t, c = u.shape
    tm = min(ROW_TILE, t)
    assert t % tm == 0 and tm % 8 == 0
    has_prev = t > tm
    tile = pl.BlockSpec((None, tm, c), lambda bi, i: (bi, i, 0))
    halo = pl.BlockSpec((None, HALO, c), lambda bi, i: (bi, 0, 0))
    in_specs = [tile]
    args = [u]
    if has_prev:
        r = tm // HALO
        in_specs.append(pl.BlockSpec((None, HALO, c), lambda bi, i: (bi, jnp.maximum(i * r - 1, 0), 0)))
        args.append(u)
    consts = [w["conv_w"], w["conv_b"], w["ln_g"], w["ln_b"]]
    in_specs += [halo] + [_const_spec(x.shape) for x in consts]
    args += [past] + consts
    return pl.pallas_call(
        functools.partial(_conv_kernel, tm=tm, has_prev=has_prev),
        grid=(b, t // tm),
        in_specs=in_specs,
        out_specs=tile,
        out_shape=jax.ShapeDtypeStruct((b, t, c), BF16),
        scratch_shapes=[pltpu.VMEM((SUBLANES, HALO + tm, c), F32)],
        compiler_params=pltpu.CompilerParams(dimension_semantics=("arbitrary", "arbitrary"),
                                             vmem_limit_bytes=VMEM_LIMIT),
        name="conv_mod",
    )(*args)


def _mix_ffn_kernel(x_ref, attn_ref, cact_ref, sga_ref, sgc_ref, wao_ref, wco_ref, wo_ref, g_ref,
                    wg_ref, wu_ref, wd_ref, y_ref, *, n_chunks):
    attn_out = jnp.dot(attn_ref[...], wao_ref[...], preferred_element_type=F32)
    conv_out = jnp.dot(cact_ref[...], wco_ref[...], preferred_element_type=F32)
    merged = sga_ref[...].astype(F32) * attn_out + sgc_ref[...].astype(F32) * conv_out
    x1 = x_ref[...] + jnp.dot(merged.astype(BF16), wo_ref[...], preferred_element_type=F32)
    ms = jnp.mean(x1 * x1, axis=-1, keepdims=True)
    h2 = (x1 * lax.rsqrt(ms + EPS) * g_ref[...]).astype(BF16)
    hc = wg_ref.shape[1] // n_chunks
    y = x1
    for ci in range(n_chunks):
        sl = slice(ci * hc, (ci + 1) * hc)
        gate = jnp.dot(h2, wg_ref[:, sl], preferred_element_type=F32)
        up = jnp.dot(h2, wu_ref[:, sl], preferred_element_type=F32)
        act = (gate * jax.nn.sigmoid(gate) * up).astype(BF16)
        y = y + jnp.dot(act, wd_ref[sl, :], preferred_element_type=F32)
    y_ref[...] = y


def _mix_ffn(x, attn, cact, sga, sgc, w):
    m, d = x.shape
    tm = min(ROW_TILE, m)
    assert m % tm == 0
    hidden = w["wg"].shape[1]
    n_chunks = 2 if hidden % (2 * LANES) == 0 else 1
    row = lambda n: pl.BlockSpec((tm, n), lambda i: (i, 0))
    consts = [w[n] for n in ("wao", "wco", "wo", "g_ffn", "wg", "wu", "wd")]
    return pl.pallas_call(
        functools.partial(_mix_ffn_kernel, n_chunks=n_chunks),
        grid=(m // tm,),
        in_specs=[row(d), row(attn.shape[1]), row(cact.shape[1]), row(d), row(d)]
                 + [_const_spec(c.shape) for c in consts],
        out_specs=row(d),
        out_shape=jax.ShapeDtypeStruct((m, d), F32),
        compiler_params=pltpu.CompilerParams(dimension_semantics=("arbitrary",), vmem_limit_bytes=VMEM_LIMIT),
        name="mix_ffn",
    )(x, attn, cact, sga, sgc, *consts)


def _prep_weights(norm_mix_g, w_in, q_norm_g, k_norm_g, idx_k_norm_g, conv_dw_w, conv_dw_b, conv_ln_g, conv_ln_b,
                  w_conv_out, w_attn_out, w_out, norm_ffn_g, w_ffn_gate, w_ffn_up, w_ffn_down):
    d = w_in.shape[0]
    conv_ch = conv_dw_w.shape[-1]
    sizes = (ATTN_W, ATTN_W, ATTN_W, N_IDX_HEADS * IDX_DIM, IDX_DIM, N_IDX_HEADS, 2 * conv_ch, d, d)
    offs = np.concatenate([[0], np.cumsum(sizes)])
    assert offs[-1] == w_in.shape[1]
    col = lambda i: w_in[:, offs[i]:offs[i + 1]]
    wsm = jnp.concatenate([col(4), col(5), jnp.zeros((d, LANES - IDX_DIM - N_IDX_HEADS), w_in.dtype)], axis=1)
    head = np.arange(ATTN_W) // HEAD_DIM
    seg = (head[:, None] == np.arange(LANES)[None, :]).astype(np.float32)
    return dict(
        g_mix=norm_mix_g.reshape(1, d).astype(F32),
        wq=col(0).astype(BF16), wk=col(1).astype(BF16), wv=col(2).astype(BF16), wiq=col(3).astype(BF16),
        wsm=wsm.astype(BF16), wglu=col(6).astype(BF16), wga=col(7).astype(BF16), wgc=col(8).astype(BF16),
        qg=jnp.tile(q_norm_g.astype(F32), N_HEADS).reshape(1, ATTN_W),
        kg=jnp.tile(k_norm_g.astype(F32), N_HEADS).reshape(1, ATTN_W),
        ikg=jnp.concatenate([idx_k_norm_g.astype(F32), jnp.zeros((LANES - IDX_DIM,), F32)]).reshape(1, LANES),
        seg=jnp.asarray(seg, BF16), segt=jnp.asarray(seg.T, BF16),
        conv_w=conv_dw_w.reshape(CONV_K, conv_ch).astype(F32), conv_b=conv_dw_b.reshape(1, conv_ch).astype(F32),
        ln_g=conv_ln_g.reshape(1, conv_ch).astype(F32), ln_b=conv_ln_b.reshape(1, conv_ch).astype(F32),
        wao=w_attn_out.astype(BF16), wco=w_conv_out.astype(BF16), wo=w_out.astype(BF16),
        g_ffn=norm_ffn_g.reshape(1, d).astype(F32),
        wg=w_ffn_gate.astype(BF16), wu=w_ffn_up.astype(BF16), wd=w_ffn_down.astype(BF16),
    )


def _layer(x, pos0, past_k, past_v, past_ik, past_conv, rel_bias, w):
    b, t, d = x.shape
    m = b * t
    prompt = past_k is None
    q, k, k_b, v, v_b, iq, sm, sm_b, u, sga, sgc = _in_proj(x, w, transposed=prompt)
    conv_ch = u.shape[1]
    u = u.reshape(b, t, conv_ch)
    k_b = k_b.reshape(b, t, ATTN_W)
    ik_b = sm_b.reshape(b, t, LANES)[:, :, :IDX_DIM]
    if prompt:
        attn = _sparse_attn(q, iq, sm[:, IDX_DIM:IDX_DIM + N_IDX_HEADS, :], k_b, v_b, ik_b, rel_bias,
                            pos0).swapaxes(1, 2)
        heads = lambda a: a.reshape(b, N_HEADS, HEAD_DIM, t).transpose(0, 3, 1, 2)
        new_k, new_v, new_ik = heads(k), heads(v), sm[:, :IDX_DIM, :].swapaxes(1, 2)
    else:
        sm = sm.reshape(b, t, LANES)
        k, v = k.reshape(b, t, ATTN_W), v.reshape(b, t, ATTN_W)
        attn = _dec_attn(q.reshape(b, t, ATTN_W), iq.reshape(b, t, ATTN_W), sm[:, :, IDX_DIM:IDX_DIM + N_IDX_HEADS],
                         past_k, past_v, past_ik, k, v, ik_b, rel_bias, pos0)
        heads = lambda a: a.reshape(b, t, N_HEADS, HEAD_DIM)
        new_k, new_v, new_ik = heads(k), heads(v), sm[:, :, :IDX_DIM]

    past = jnp.pad(past_conv.astype(F32), ((0, 0), (HALO - (CONV_K - 1), 0), (0, 0)))
    cact = _conv_mod(u, past, w)
    y = _mix_ffn(x.reshape(m, d), attn.reshape(m, ATTN_W), cact.reshape(m, conv_ch), sga, sgc, w)

    new_conv = jnp.concatenate([past_conv, u], axis=1)[:, -(CONV_K - 1):]
    return y.reshape(b, t, d), new_k, new_v, new_ik, new_conv


def kernel(x_prompt, x_sample, cache_k, cache_v, cache_idx_k, state_conv, rel_bias, norm_mix_g, w_in, q_norm_g, k_norm_g, idx_k_norm_g, conv_dw_w, conv_dw_b, conv_ln_g, conv_ln_b, w_conv_out, w_attn_out, w_out, norm_ffn_g, w_ffn_gate, w_ffn_up, w_ffn_down):
    depth = w_in.shape[0]
    bp = x_prompt.shape[0]
    past_len = cache_k.shape[2]
    conv_ch = conv_dw_w.shape[-1]
    yp, ys = x_prompt, x_sample
    outs_p, outs_s = [], []
    for l in range(depth):
        w = _prep_weights(norm_mix_g[l], w_in[l], q_norm_g[l], k_norm_g[l], idx_k_norm_g[l], conv_dw_w[l],
                          conv_dw_b[l], conv_ln_g[l], conv_ln_b[l], w_conv_out[l], w_attn_out[l], w_out[l],
                          norm_ffn_g[l], w_ffn_gate[l], w_ffn_up[l], w_ffn_down[l])
        zero_conv = jnp.zeros((bp, CONV_K - 1, conv_ch), yp.dtype)
        yp, *rest_p = _layer(yp, 0, None, None, None, zero_conv, rel_bias, w)
        outs_p.append(rest_p)
        ys, *rest_s = _layer(ys, past_len, cache_k[l], cache_v[l], cache_idx_k[l], state_conv[l], rel_bias, w)
        outs_s.append(rest_s)
    stack = lambda outs, i: jnp.stack([o[i] for o in outs])
    return (yp, ys, stack(outs_p, 0), stack(outs_p, 1), stack(outs_p, 2), stack(outs_p, 3),
            stack(outs_s, 0), stack(outs_s, 1), stack(outs_s, 2), stack(outs_s, 3))
```

```python
import functools
import math

import numpy as np
import jax
import jax.numpy as jnp
from jax import lax
from jax.experimental import pallas as pl
from jax.experimental.pallas import tpu as pltpu

CHUNK = 64
CHUNK_SHIFT = 6
N_HEADS = 8
HEAD_DIM = 64
ATTN_W = N_HEADS * HEAD_DIM
N_IDX_HEADS = 8
IDX_DIM = 64
TOPK_MAX = 256
CONV_K = 31
NUM_BUCKETS = 32
MAX_DISTANCE = 128
EPS = 1e-6
NEG = -1e30
INT_MIN = -(2 ** 31)
BF16_INF_BITS = 0x7F80
BF16_MIN_NORMAL_BITS = 0x0080
N_BF16_NEG = BF16_INF_BITS - BF16_MIN_NORMAL_BITS + 1
LOG2E = math.log2(math.e)

LANES = 128
SUBLANES = 8
HALO = 32
CONV_ROWS = 32
KEY_TILE = 256
V_ROWS = HEAD_DIM + 16
ROW_TILE = 256
VMEM_LIMIT = 56 * 1024 * 1024

F32 = jnp.float32
BF16 = jnp.bfloat16


def _const_spec(shape):
    return pl.BlockSpec(shape, lambda *_: (0,) * len(shape), pipeline_mode=pl.Buffered(1))


def _split_dot(x, m):
    hi = x.astype(BF16)
    lo = (x - hi.astype(F32)).astype(BF16)
    return jnp.dot(hi, m, preferred_element_type=F32) + jnp.dot(lo, m, preferred_element_type=F32)


def _in_proj_kernel(x_ref, g_ref, wq_ref, wk_ref, wv_ref, wiq_ref, wsm_ref, wglu_ref, wga_ref, wgc_ref,
                    qg_ref, kg_ref, ikg_ref, seg_ref, segt_ref,
                    q_ref, k_ref, kb_ref, v_ref, vb_ref, iq_ref, sm_ref, smb_ref, u_ref, sga_ref, sgc_ref,
                    *, transposed):
    x = x_ref[...]
    ms = jnp.mean(x * x, axis=-1, keepdims=True)
    h = (x * lax.rsqrt(ms + EPS) * g_ref[...]).astype(BF16)

    def put(ref, val):
        ref[...] = (val.T if transposed else val).astype(ref.dtype)

    glu = jnp.dot(h, wglu_ref[...], preferred_element_type=F32)
    c = glu.shape[1] // 2
    u_ref[...] = glu[:, :c] * jax.nn.sigmoid(glu[:, c:])

    def head_rms(y, gain):
        ss = _split_dot(y * y, seg_ref[...])
        r = lax.rsqrt(ss * (1.0 / HEAD_DIM) + EPS)
        return y * _split_dot(r, segt_ref[...]) * gain

    q = head_rms(jnp.dot(h, wq_ref[...], preferred_element_type=F32), qg_ref[...])
    put(q_ref, q * (HEAD_DIM ** -0.5 * LOG2E))
    k = head_rms(jnp.dot(h, wk_ref[...], preferred_element_type=F32), kg_ref[...])
    put(k_ref, k)
    kb_ref[...] = k.astype(BF16)
    v = jnp.dot(h, wv_ref[...], preferred_element_type=F32)
    put(v_ref, v)
    if transposed:
        vt = v.T.astype(BF16)
        pad_rows = lax.broadcasted_iota(jnp.int32, (V_ROWS - HEAD_DIM, vt.shape[1]), 0)
        ones_row = jnp.where(pad_rows == 0, 1.0, 0.0).astype(BF16)
        for hd in range(N_HEADS):
            vb_ref[hd * V_ROWS:hd * V_ROWS + HEAD_DIM, :] = vt[hd * HEAD_DIM:(hd + 1) * HEAD_DIM, :]
            vb_ref[hd * V_ROWS + HEAD_DIM:(hd + 1) * V_ROWS, :] = ones_row
    else:
        vb_ref[...] = v.astype(BF16)
    iq = jnp.dot(h, wiq_ref[...], preferred_element_type=F32)
    put(iq_ref, iq * (IDX_DIM ** -0.5))

    sm = jnp.dot(h, wsm_ref[...], preferred_element_type=F32)
    lane = lax.broadcasted_iota(jnp.int32, sm.shape, 1)
    is_ik = lane < IDX_DIM
    ss = jnp.sum(jnp.where(is_ik, sm * sm, 0.0), axis=-1, keepdims=True)
    ikn = sm * lax.rsqrt(ss * (1.0 / IDX_DIM) + EPS) * ikg_ref[...]
    iw = sm * (N_IDX_HEADS ** -0.5)
    smo = jnp.where(is_ik, ikn, jnp.where(lane < IDX_DIM + N_IDX_HEADS, iw, 0.0))
    put(sm_ref, smo)
    smb_ref[...] = smo.astype(BF16)

    sga_ref[...] = jax.nn.sigmoid(jnp.dot(h, wga_ref[...], preferred_element_type=F32)).astype(BF16)
    sgc_ref[...] = jax.nn.sigmoid(jnp.dot(h, wgc_ref[...], preferred_element_type=F32)).astype(BF16)


def _in_proj(x, w, transposed):
    b, t, d = x.shape
    m = b * t
    tm = min(ROW_TILE, t if transposed else m)
    assert m % tm == 0 and (not transposed or t % tm == 0)
    nt = t // tm
    conv_ch = w["wglu"].shape[1] // 2
    consts = [w[n] for n in ("g_mix", "wq", "wk", "wv", "wiq", "wsm", "wglu", "wga", "wgc",
                             "qg", "kg", "ikg", "seg", "segt")]

    def rows(n, dtype):
        return jax.ShapeDtypeStruct((m, n), dtype), pl.BlockSpec((tm, n), lambda i: (i, 0))

    def cols(n, dtype):
        if not transposed:
            return rows(n, dtype)
        return jax.ShapeDtypeStruct((b, n, t), dtype), pl.BlockSpec((None, n, tm), lambda i: (i // nt, 0, i % nt))

    if transposed:
        v_slab = (jax.ShapeDtypeStruct((b, nt, N_HEADS * V_ROWS, tm), BF16),
                  pl.BlockSpec((None, None, N_HEADS * V_ROWS, tm), lambda i: (i // nt, i % nt, 0, 0)))
    else:
        v_slab = rows(ATTN_W, BF16)
    outs = [
        cols(ATTN_W, BF16),
        cols(ATTN_W, F32),
        rows(ATTN_W, BF16),
        cols(ATTN_W, F32),
        v_slab,
        cols(ATTN_W, BF16),
        cols(LANES, F32),
        rows(LANES, BF16),
        rows(conv_ch, F32),
        rows(d, BF16),
        rows(d, BF16),
    ]
    return pl.pallas_call(
        functools.partial(_in_proj_kernel, transposed=transposed),
        grid=(m // tm,),
        in_specs=[pl.BlockSpec((tm, d), lambda i: (i, 0))] + [_const_spec(c.shape) for c in consts],
        out_specs=[o[1] for o in outs],
        out_shape=[o[0] for o in outs],
        compiler_params=pltpu.CompilerParams(dimension_semantics=("arbitrary",), vmem_limit_bytes=VMEM_LIMIT),
        name="in_proj",
    )(x.reshape(m, d), *consts)


def _fold(x, op):
    return op(x.reshape(x.shape[0] // SUBLANES, SUBLANES, x.shape[1]), axis=0)


def _select_top_k(key_scr, part_scr, n_kt, *, tk, tq, top_k):
    @pl.when(n_kt % 2 == 1)
    def _():
        key_scr[n_kt] = jnp.full((tk, tq), INT_MIN, jnp.int32)
        part_scr[n_kt] = jnp.full((tk, tq), -jnp.inf, BF16)

    n_pairs = (n_kt + 1) // 2

    def count(pred):
        def body(i, acc):
            for kt in (2 * i, 2 * i + 1):
                acc = acc + _fold(jnp.where(pred(key_scr[kt], kt), 1.0, 0.0), jnp.sum)
            return acc
        acc = lax.fori_loop(0, n_pairs, body, jnp.zeros((SUBLANES, tq), F32))
        return jnp.sum(acc, axis=0, keepdims=True)

    rows16 = 2 * SUBLANES
    one_b, zero_b = jnp.ones((), BF16), jnp.zeros((), BF16)

    def count_part(pred):
        def body(i, acc):
            for kt in (2 * i, 2 * i + 1):
                c = jnp.where(pred(part_scr[kt]), one_b, zero_b)
                part = c[0:rows16]
                for j in range(1, tk // rows16):
                    part = part + c[j * rows16:(j + 1) * rows16]
                acc = acc + part.astype(F32)
            return acc
        acc = lax.fori_loop(0, n_pairs, body, jnp.zeros((rows16, tq), F32))
        return jnp.sum(acc, axis=0, keepdims=True)

    def search(n_bits, base, to_image, limit):
        def body(i, t):
            cand = t + (jnp.int32(1) << (n_bits - 1 - i))
            img = to_image(cand)
            c = base + count_part(lambda a: a >= img)
            return jnp.where((c >= top_k) & (cand <= limit), cand, t)
        return lax.fori_loop(0, n_bits, body, jnp.zeros((1, tq), jnp.int32))

    def refine(eq_img, shift):
        def body(i, carry):
            for kt in (2 * i, 2 * i + 1):
                nxt = ((key_scr[kt] >> shift) & 0xFF).astype(F32).astype(BF16)
                part_scr[kt] = jnp.where(part_scr[kt] == eq_img, nxt, -one_b)
            return carry
        lax.fori_loop(0, n_pairs, body, 0)

    def top_bits(c):
        neg = 0x8000 | (BF16_INF_BITS - c)
        pos = BF16_MIN_NORMAL_BITS + (c - N_BF16_NEG - 1)
        return jnp.where(c < N_BF16_NEG, neg, jnp.where(c == N_BF16_NEG, 0, pos))

    def top_image(c):
        return pltpu.bitcast(top_bits(c) << 16, F32).astype(BF16)

    small_image = lambda c: c.astype(F32).astype(BF16)

    t1 = search(16, 0.0, top_image, 2 * N_BF16_NEG)
    img1 = top_image(t1)
    above = count_part(lambda a: a > img1)
    refine(img1, 8)
    t2 = search(8, above, small_image, 255)
    img2 = small_image(t2)
    above = above + count_part(lambda a: a > img2)
    refine(img2, 0)
    t3 = search(8, above, small_image, 255)
    hi = top_bits(t1)
    hi = hi - ((hi >> 15) << 16)
    hi = hi ^ ((hi >> 15) & 0x7FFF)
    thr = (hi << 16) + (t2 << 8) + t3
    thr = jnp.where(t1 == 0, INT_MIN + 1, jnp.maximum(thr, INT_MIN + 1))

    n_ge = count(lambda key, kt: key >= thr)
    has_tie = jnp.max(n_ge) > top_k

    @pl.when(has_tie)
    def _():
        need_m1 = (top_k - 1) - count(lambda key, kt: key > thr)
        row = lax.broadcasted_iota(jnp.int32, (tk, tq), 0)

        def idx_body(i, cut):
            cand = cut + (jnp.int32(1) << (15 - i))
            c = count(lambda key, kt: (key == thr) & (kt * tk + row < cand))
            return jnp.where(c <= need_m1, cand, cut)

        cut = lax.fori_loop(0, 16, idx_body, jnp.zeros((1, tq), jnp.int32))
        tie_col = n_ge > top_k

        def drop_body(kt, carry):
            key = key_scr[kt]
            drop = tie_col & (key == thr) & (kt * tk + row > cut)
            key_scr[kt] = jnp.where(drop, INT_MIN, key)
            return carry

        lax.fori_loop(0, n_kt, drop_body, 0)
    return thr


def _sparse_attn_kernel(qt_ref, iqt_ref, iwt_ref, k_ref, vt_ref, ik_ref, bias_ref, o_ref,
                        key_scr, part_scr, m_scr, acc_scr, s_scr, *, tq, tk, nkt, pos0, n_keys, top_k):
    qi = pl.program_id(1)
    q0 = pos0 + qi * tq
    n_kt = jnp.minimum(nkt, ((q0 + tq - 1) // CHUNK * CHUNK + CHUNK + tk - 1) // tk)
    last = n_kt - 1

    fold = _fold

    iqt = iqt_ref[...]
    iqt_h = [iqt[h * IDX_DIM:(h + 1) * IDX_DIM, :] for h in range(N_IDX_HEADS)]
    iwt = iwt_ref[...]
    iw_h = [iwt[h:h + 1, :] for h in range(N_IDX_HEADS)]

    def score_tile(kt, masked):
        ik = ik_ref[kt]
        s = None
        for h in range(N_IDX_HEADS):
            t = jnp.maximum(jnp.dot(ik, iqt_h[h], preferred_element_type=F32), 0.0) * iw_h[h]
            s = t if s is None else s + t
        s = s + 0.0
        bits = pltpu.bitcast(s, jnp.int32)
        key = bits ^ ((bits >> 31) & 0x7FFFFFFF)
        top = pltpu.bitcast(bits & -65536, F32)
        if masked:
            kp = kt * tk + lax.broadcasted_iota(jnp.int32, (tk, tq), 0)
            qp = q0 + lax.broadcasted_iota(jnp.int32, (tk, tq), 1)
            adm = ((kp >> CHUNK_SHIFT) <= (qp >> CHUNK_SHIFT)) & (kp < n_keys)
            key = jnp.where(adm, key, INT_MIN)
            top = jnp.where(adm, top, -jnp.inf)
        key_scr[kt] = key
        part_scr[kt] = top.astype(BF16)

    def score_body(i, carry):
        score_tile(2 * i, False)
        score_tile(2 * i + 1, False)
        return carry

    lax.fori_loop(0, last // 2, score_body, 0)

    @pl.when(last % 2 == 1)
    def _():
        score_tile(last - 1, False)

    score_tile(last, True)

    thr = _select_top_k(key_scr, part_scr, n_kt, tk=tk, tq=tq, top_k=top_k)

    qt = qt_ref[...]
    slab_row = lax.broadcasted_iota(jnp.int32, (LANES, tq), 0)
    qz = []
    for h in range(N_HEADS):
        slab = qt[(h // 2) * LANES:(h // 2 + 1) * LANES, :]
        mine = (slab_row < HEAD_DIM) if h % 2 == 0 else (slab_row >= HEAD_DIM)
        qz.append(jnp.where(mine, slab, jnp.zeros_like(slab)))
    m_scr[...] = jnp.full(m_scr.shape, NEG, F32)
    acc_scr[...] = jnp.zeros(acc_scr.shape, F32)

    def park(kt, slot):
        madd = jnp.where(key_scr[kt] >= thr, 0.0, NEG)
        for h in range(N_HEADS):
            pair = h // 2
            s = jnp.dot(k_ref[kt, :, pair * LANES:(pair + 1) * LANES], qz[h], preferred_element_type=F32)
            s_scr[slot, h] = s + madd

    def absorb(kt, slot, bias_idx):
        for h in range(N_HEADS):
            s = s_scr[slot, h]
            if bias_idx is not None:
                s = s + bias_ref[bias_idx, h]
            m_new = jnp.maximum(m_scr[h], jnp.max(fold(s, jnp.max), axis=0, keepdims=True))
            alpha = jnp.exp2(m_scr[h] - m_new)
            p = jnp.exp2(s - m_new)
            pv = jnp.dot(vt_ref[kt, h * V_ROWS:(h + 1) * V_ROWS, :], p.astype(BF16), preferred_element_type=F32)
            acc_scr[h] = alpha * acc_scr[h] + pv
            m_scr[h] = m_new

    park(0, 0)
    n_trips = (n_kt - 2) // 2

    def far_body(i, carry):
        kt = 2 * i
        park(kt + 1, 1)
        absorb(kt, 0, None)
        park(kt + 2, 0)
        absorb(kt + 1, 1, None)
        return carry

    lax.fori_loop(0, n_trips, far_body, 0)
    done = 2 * jnp.maximum(n_trips, 0)

    @pl.when(n_kt == 1)
    def _():
        absorb(0, 0, 1)

    @pl.when((n_kt >= 2) & (n_kt % 2 == 0))
    def _():
        park(done + 1, 1)
        absorb(done, 0, 0)
        absorb(done + 1, 1, 1)

    @pl.when((n_kt >= 3) & (n_kt % 2 == 1))
    def _():
        park(done + 1, 1)
        absorb(done, 0, None)
        park(done + 2, 0)
        absorb(done + 1, 1, 0)
        absorb(done + 2, 0, 1)

    for h in range(N_HEADS):
        acc = acc_scr[h]
        o_ref[h * HEAD_DIM:(h + 1) * HEAD_DIM, :] = (acc[:HEAD_DIM] / acc[HEAD_DIM:HEAD_DIM + 1]).astype(o_ref.dtype)


def _rel_bucket(rel):
    nb = NUM_BUCKETS // 2
    max_exact = nb // 2
    ret = jnp.where(rel > 0, nb, 0)
    n = jnp.abs(rel)
    nf = jnp.maximum(n, 1).astype(jnp.float32)
    large = max_exact + (jnp.log(nf / max_exact) / math.log(MAX_DISTANCE / max_exact) * (nb - max_exact)).astype(jnp.int32)
    large = jnp.minimum(large, nb - 1)
    return ret + jnp.where(n < max_exact, n, large)


def _bias_tiles(rel_bias, tq, tk):
    j = jnp.arange(tk, dtype=jnp.int32)[:, None]
    i = jnp.arange(tq, dtype=jnp.int32)[None, :]
    return jnp.stack([_shifted_bias(rel_bias, j - i + d) for d in (-tk, 0)])


def _shifted_bias(rel_bias, rel):
    rb = (rel_bias.astype(F32) - rel_bias[_rel_bucket(jnp.int32(-MAX_DISTANCE))].astype(F32)) * LOG2E
    onehot = jax.nn.one_hot(_rel_bucket(rel), NUM_BUCKETS, dtype=F32)
    return jnp.moveaxis(jnp.einsum("...b,bh->...h", onehot, rb, precision=lax.Precision.HIGHEST), -1, 0)


def _check_tiling(t, tq, tk, nkt, pos0, n_keys):
    for qi in range(t // tq):
        q0 = pos0 + qi * tq
        n_kt = min(nkt, ((q0 + tq - 1) // CHUNK * CHUNK + CHUNK + tk - 1) // tk)
        adm_end = min(q0 // CHUNK * CHUNK + CHUNK, n_keys)
        assert (n_kt - 1) * tk <= adm_end, "only the last visited tile may hold inadmissible keys"
        assert n_kt * tk >= min((q0 + tq - 1) // CHUNK * CHUNK + CHUNK, n_keys), "visited tiles cover every admissible key"
        assert (n_kt - 1) * tk == q0, "last tile starts with the query tile"
        assert tk >= MAX_DISTANCE, "tiles before the last two are at least MAX_DISTANCE behind"


def _sparse_attn(qt, iqt, iwt, k_b, vt_t, ik_b, rel_bias, pos0):
    b, _, t = qt.shape
    n_keys = k_b.shape[1]
    top_k = min(TOPK_MAX, n_keys // 4)
    tk = KEY_TILE
    tq = min(tk, t)
    nkt = n_keys // tk
    assert n_keys % tk == 0 and vt_t.shape == (b, nkt, N_HEADS * V_ROWS, tk)
    _check_tiling(t, tq, tk, nkt, pos0, n_keys)
    k_t = k_b.reshape(b, nkt, tk, ATTN_W)
    ik_t = ik_b.reshape(b, nkt, tk, IDX_DIM)
    bias = _bias_tiles(rel_bias, tq, tk)

    qspec = lambda n: pl.BlockSpec((None, n, tq), lambda bi, qi: (bi, 0, qi))
    kspec = lambda r, c: pl.BlockSpec((None, nkt, r, c), lambda bi, qi: (bi, 0, 0, 0))
    kern = functools.partial(_sparse_attn_kernel, tq=tq, tk=tk, nkt=nkt, pos0=pos0, n_keys=n_keys, top_k=top_k)
    return pl.pallas_call(
        kern,
        grid=(b, t // tq),
        in_specs=[qspec(ATTN_W), qspec(ATTN_W), qspec(N_IDX_HEADS),
                  kspec(tk, ATTN_W), kspec(N_HEADS * V_ROWS, tk), kspec(tk, IDX_DIM), _const_spec(bias.shape)],
        out_specs=qspec(ATTN_W),
        out_shape=jax.ShapeDtypeStruct((b, ATTN_W, t), BF16),
        scratch_shapes=[pltpu.VMEM((nkt + 1, tk, tq), jnp.int32),
                        pltpu.VMEM((nkt + 1, tk, tq), BF16),
                        pltpu.VMEM((N_HEADS, 1, tq), F32),
                        pltpu.VMEM((N_HEADS, V_ROWS, tq), F32),
                        pltpu.VMEM((2, N_HEADS, tk, tq), F32)],
        compiler_params=pltpu.CompilerParams(dimension_semantics=("arbitrary", "arbitrary"),
                                             vmem_limit_bytes=VMEM_LIMIT),
        name="sparse_attn",
    )(qt, iqt, iwt, k_t, vt_t, ik_t, bias)


DEC_ROWS = N_HEADS * 16
DEC_KEY_TILE = 1024
NT_DIMS = (((1,), (1,)), ((), ()))


def _dec_select_kernel(iq_ref, iwc_ref, ikt_ref, iktn_ref, madd_ref, key_scr, *, g, tq, n_keys, n_pad, top_k):
    n_slabs = n_pad // LANES
    col = lax.broadcasted_iota(jnp.int32, (tq, n_pad), 1)
    for s in range(g):
        iq_s = iq_ref[s * tq:(s + 1) * tq, :]
        lhs = jnp.concatenate([iq_s[:, h * IDX_DIM:(h + 1) * IDX_DIM] for h in range(N_IDX_HEADS)], axis=0)
        logit = jnp.concatenate([jnp.dot(lhs, ikt_ref[s].astype(BF16), preferred_element_type=F32),
                                 jnp.dot(lhs, iktn_ref[s], preferred_element_type=F32)], axis=1)
        x = jnp.maximum(logit, 0.0) * iwc_ref[s]
        score = x[0:tq]
        for h in range(1, N_IDX_HEADS):
            score = score + x[h * tq:(h + 1) * tq]
        score = score + 0.0
        bits = pltpu.bitcast(score, jnp.int32)
        key = bits ^ ((bits >> 31) & 0x7FFFFFFF)
        key_scr[s * tq:(s + 1) * tq, :] = jnp.where(col < n_keys, key, INT_MIN)

    rows = g * tq
    lane = lax.broadcasted_iota(jnp.int32, (rows, LANES), 1)

    def count(pred):
        acc = jnp.zeros((rows, LANES), F32)
        for j in range(n_slabs):
            acc = acc + jnp.where(pred(key_scr[:, j * LANES:(j + 1) * LANES], j), 1.0, 0.0)
        return jnp.sum(acc, axis=1, keepdims=True)

    def thr_body(i, t):
        cand = t + (jnp.int32(1) << (31 - i))
        c = count(lambda key, j: key >= cand)
        return jnp.where(c >= top_k, cand, t)

    thr = lax.fori_loop(0, 32, thr_body, jnp.full((rows, 1), INT_MIN, jnp.int32))
    thr = jnp.maximum(thr, INT_MIN + 1)

    n_ge = count(lambda key, j: key >= thr)
    has_tie = jnp.max(n_ge) > top_k

    @pl.when(has_tie)
    def _():
        need_m1 = (top_k - 1) - count(lambda key, j: key > thr)

        def idx_body(i, cut):
            cand = cut + (jnp.int32(1) << (15 - i))
            c = count(lambda key, j: (key == thr) & (j * LANES + lane < cand))
            return jnp.where(c <= need_m1, cand, cut)

        cut = lax.fori_loop(0, 16, idx_body, jnp.zeros((rows, 1), jnp.int32))
        tie_row = n_ge > top_k
        for j in range(n_slabs):
            key = key_scr[:, j * LANES:(j + 1) * LANES]
            drop = tie_row & (key == thr) & (j * LANES + lane > cut)
            key_scr[:, j * LANES:(j + 1) * LANES] = jnp.where(drop, INT_MIN, key)

    madd_ref[...] = jnp.where(key_scr[...] >= thr, 0.0, NEG)


def _dec_attn_kernel(q_ref, maddc_ref, maddn_ref, k_ref, v_ref, kn_ref, vn_ref, biasc_ref, biasn_ref, o_ref,
                     m_scr, l_scr, acc_scr, *, tq):
    kt = pl.program_id(1)
    q = q_ref[...]
    q_h = [q[:, h * HEAD_DIM:(h + 1) * HEAD_DIM] for h in range(N_HEADS)]

    @pl.when(kt == 0)
    def _():
        m_scr[...] = jnp.full(m_scr.shape, NEG, F32)
        l_scr[...] = jnp.zeros(l_scr.shape, F32)
        acc_scr[...] = jnp.zeros(acc_scr.shape, F32)

    def step(ktr, vtr, madd, bias):
        s = jnp.concatenate([jnp.dot(q_h[h], ktr[h].astype(BF16), preferred_element_type=F32)
                             for h in range(N_HEADS)], axis=0)
        s = s + jnp.concatenate([madd] * N_HEADS, axis=0) + bias
        m_prev = m_scr[...]
        m_new = jnp.maximum(m_prev, jnp.max(s, axis=1, keepdims=True))
        alpha = jnp.exp2(m_prev - m_new)
        p = jnp.exp2(s - m_new)
        l_scr[...] = alpha * l_scr[...] + jnp.sum(p, axis=1, keepdims=True)
        pb = p.astype(BF16)
        pv = jnp.concatenate([lax.dot_general(pb[h * tq:(h + 1) * tq, :], vtr[h].astype(BF16), NT_DIMS,
                                              preferred_element_type=F32) for h in range(N_HEADS)], axis=0)
        acc_scr[...] = alpha * acc_scr[...] + pv
        m_scr[...] = m_new

    step(k_ref, v_ref, maddc_ref[...], biasc_ref[...])

    @pl.when(kt == pl.num_programs(1) - 1)
    def _():
        step(kn_ref, vn_ref, maddn_ref[...], biasn_ref[...])
        o = acc_scr[...] / l_scr[...]
        for h in range(N_HEADS):
            o_ref[:, h * HEAD_DIM:(h + 1) * HEAD_DIM] = o[h * tq:(h + 1) * tq, :].astype(o_ref.dtype)


def _dec_attn(q, iq, iw, past_k, past_v, past_ik, k_new, v_new, ik_new, rel_bias, pos0):
    b, t, _ = q.shape
    p_len = past_k.shape[1]
    n_keys = p_len + t
    top_k = min(TOPK_MAX, n_keys // 4)
    tkk = min(DEC_KEY_TILE, p_len)
    n_pad = p_len + LANES
    g = min(LANES // t, b)
    assert N_HEADS * t == DEC_ROWS and b % g == 0 and p_len % tkk == 0 and t <= LANES
    assert (pos0 + t - 1) // CHUNK == pos0 // CHUNK and (n_keys - 1) // CHUNK <= pos0 // CHUNK, "every key admissible"
    assert tkk >= MAX_DISTANCE + LANES >= MAX_DISTANCE + t, "only the last cache tile is within MAX_DISTANCE of a query"

    ikt_new = jnp.pad(ik_new.swapaxes(1, 2), ((0, 0), (0, 0), (0, LANES - t)))
    iw_col = iw.swapaxes(1, 2).reshape(b, DEC_ROWS, 1)
    madd = pl.pallas_call(
        functools.partial(_dec_select_kernel, g=g, tq=t, n_keys=n_keys, n_pad=n_pad, top_k=top_k),
        grid=(b // g,),
        in_specs=[pl.BlockSpec((g * t, ATTN_W), lambda i: (i, 0)),
                  pl.BlockSpec((g, DEC_ROWS, 1), lambda i: (i, 0, 0)),
                  pl.BlockSpec((g, IDX_DIM, p_len), lambda i: (i, 0, 0)),
                  pl.BlockSpec((g, IDX_DIM, LANES), lambda i: (i, 0, 0))],
        out_specs=pl.BlockSpec((g * t, n_pad), lambda i: (i, 0)),
        out_shape=jax.ShapeDtypeStruct((b * t, n_pad), F32),
        scratch_shapes=[pltpu.VMEM((g * t, n_pad), jnp.int32)],
        compiler_params=pltpu.CompilerParams(dimension_semantics=("arbitrary",), vmem_limit_bytes=VMEM_LIMIT),
        name="dec_select",
    )(iq.reshape(b * t, ATTN_W), iw_col, past_ik.swapaxes(1, 2), ikt_new)

    n_steps = p_len // tkk
    qpos = jnp.tile(jnp.arange(t, dtype=jnp.int32), N_HEADS)[:, None]
    hsel = jnp.repeat(jnp.arange(N_HEADS), t)
    rows = jnp.arange(DEC_ROWS)
    near = MAX_DISTANCE + LANES
    rel_c = jnp.arange(near, dtype=jnp.int32)[None, :] - near - qpos
    rel_n = jnp.arange(LANES, dtype=jnp.int32)[None, :] - qpos
    bias_last = jnp.pad(_shifted_bias(rel_bias, rel_c)[hsel, rows], ((0, 0), (tkk - near, 0)))
    bias_c = jnp.concatenate([jnp.zeros((n_steps - 1, DEC_ROWS, tkk), F32), bias_last[None]], axis=0)
    bias_n = _shifted_bias(rel_bias, rel_n)[hsel, rows]
    heads_t = lambda a: a.transpose(0, 2, 3, 1)
    pad_new = lambda a: jnp.pad(heads_t(a.reshape(b, t, N_HEADS, HEAD_DIM)), ((0, 0), (0, 0), (0, 0), (0, LANES - t)))
    kv_tile = pl.BlockSpec((None, N_HEADS, HEAD_DIM, tkk), lambda bi, kt: (bi, 0, 0, kt))
    kv_new = pl.BlockSpec((None, N_HEADS, HEAD_DIM, LANES), lambda bi, kt: (bi, 0, 0, 0))
    return pl.pallas_call(
        functools.partial(_dec_attn_kernel, tq=t),
        grid=(b, n_steps),
        in_specs=[pl.BlockSpec((None, t, ATTN_W), lambda bi, kt: (bi, 0, 0)),
                  pl.BlockSpec((t, tkk), lambda bi, kt: (bi, kt)),
                  pl.BlockSpec((t, LANES), lambda bi, kt: (bi, p_len // LANES)),
                  kv_tile, kv_tile, kv_new, kv_new,
                  pl.BlockSpec((None, DEC_ROWS, tkk), lambda bi, kt: (kt, 0, 0)),
                  _const_spec(bias_n.shape)],
        out_specs=pl.BlockSpec((None, t, ATTN_W), lambda bi, kt: (bi, 0, 0)),
        out_shape=jax.ShapeDtypeStruct((b, t, ATTN_W), BF16),
        scratch_shapes=[pltpu.VMEM((DEC_ROWS, 1), F32), pltpu.VMEM((DEC_ROWS, 1), F32),
                        pltpu.VMEM((DEC_ROWS, HEAD_DIM), F32)],
        compiler_params=pltpu.CompilerParams(dimension_semantics=("arbitrary", "arbitrary"),
                                             vmem_limit_bytes=VMEM_LIMIT),
        name="dec_attn",
    )(q, madd, madd, heads_t(past_k), heads_t(past_v), pad_new(k_new), pad_new(v_new), bias_c, bias_n)


def _conv_kernel(*refs, tm, has_prev):
    if has_prev:
        u_ref, prev_ref, past_ref, w_ref, b_ref, lg_ref, lb_ref, o_ref, ext_scr = refs
        halo = jnp.where(pl.program_id(1) == 0, past_ref[...], prev_ref[...])
    else:
        u_ref, past_ref, w_ref, b_ref, lg_ref, lb_ref, o_ref, ext_scr = refs
        halo = past_ref[...]
    _conv_act(halo, u_ref[...], w_ref, b_ref, lg_ref, lb_ref, o_ref, ext_scr)


def _conv_act(halo, u, w_ref, b_ref, lg_ref, lb_ref, o_ref, ext_scr):
    tm = u.shape[0]
    ext_scr[0, 0:HALO, :] = halo
    ext_scr[0, HALO:HALO + tm, :] = u
    n_rows = tm + HALO - SUBLANES
    for r in range(1, SUBLANES):
        ext_scr[r, 0:n_rows, :] = ext_scr[0, r:r + n_rows, :]
    rb = min(tm, CONV_ROWS)
    first = HALO - (CONV_K - 1)
    for r0 in range(0, tm, rb):
        acc = None
        for j in range(CONV_K):
            a, r = divmod(first + j, SUBLANES)
            t = ext_scr[r, r0 + a * SUBLANES:r0 + a * SUBLANES + rb, :] * w_ref[j:j + 1, :]
            acc = t if acc is None else acc + t
        c = acc + b_ref[...]
        mu = jnp.mean(c, axis=-1, keepdims=True)
        cc = c - mu
        var = jnp.mean(cc * cc, axis=-1, keepdims=True)
        y = cc * lax.rsqrt(var + EPS) * lg_ref[...] + lb_ref[...]
        o_ref[r0:r0 + rb, :] = (y * jax.nn.sigmoid(y)).astype(o_ref.dtype)


def _conv_mod(u, past, w):
    b, t, c = u.shape
    tm = min(ROW_TILE, t)
    assert t % tm == 0 and tm % 8 == 0
    has_prev = t > tm
    tile = pl.BlockSpec((None, tm, c), lambda bi, i: (bi, i, 0))
    halo = pl.BlockSpec((None, HALO, c), lambda bi, i: (bi, 0, 0))
    in_specs = [tile]
    args = [u]
    if has_prev:
        r = tm // HALO
        in_specs.append(pl.BlockSpec((None, HALO, c), lambda bi, i: (bi, jnp.maximum(i * r - 1, 0), 0)))
        args.append(u)
    consts = [w["conv_w"], w["conv_b"], w["ln_g"], w["ln_b"]]
    in_specs += [halo] + [_const_spec(x.shape) for x in consts]
    args += [past] + consts
    return pl.pallas_call(
        functools.partial(_conv_kernel, tm=tm, has_prev=has_prev),
        grid=(b, t // tm),
        in_specs=in_specs,
        out_specs=tile,
        out_shape=jax.ShapeDtypeStruct((b, t, c), BF16),
        scratch_shapes=[pltpu.VMEM((SUBLANES, HALO + tm, c), F32)],
        compiler_params=pltpu.CompilerParams(dimension_semantics=("arbitrary", "arbitrary"),
                                             vmem_limit_bytes=VMEM_LIMIT),
        name="conv_mod",
    )(*args)


def _mix_ffn_conv_kernel(x_ref, attn_ref, u0_ref, unext_ref, utail_ref, past0_ref, pastn_ref, sga_ref, sgc_ref,
                         wao_ref, wco_ref, wo_ref, g_ref, wg_ref, wu_ref, wd_ref, cw_ref, cb_ref, lg_ref, lb_ref,
                         y_ref, cact_scr, ext_scr, *, n_chunks, tiles_per_seq):
    i = pl.program_id(0)

    @pl.when(i == 0)
    def _():
        _conv_act(past0_ref[...], u0_ref[...], cw_ref, cb_ref, lg_ref, lb_ref, cact_scr.at[0], ext_scr)

    _mix_ffn_kernel(x_ref, attn_ref, cact_scr.at[i % 2], sga_ref, sgc_ref, wao_ref, wco_ref, wo_ref, g_ref,
                    wg_ref, wu_ref, wd_ref, y_ref, n_chunks=n_chunks)
    halo = jnp.where((i + 1) % tiles_per_seq == 0, pastn_ref[...], utail_ref[...])
    _conv_act(halo, unext_ref[...], cw_ref, cb_ref, lg_ref, lb_ref, cact_scr.at[(i + 1) % 2], ext_scr)


def _mix_ffn_kernel(x_ref, attn_ref, cact_ref, sga_ref, sgc_ref, wao_ref, wco_ref, wo_ref, g_ref,
                    wg_ref, wu_ref, wd_ref, y_ref, *, n_chunks):
    attn_out = jnp.dot(attn_ref[...], wao_ref[...], preferred_element_type=F32)
    conv_out = jnp.dot(cact_ref[...], wco_ref[...], preferred_element_type=F32)
    merged = sga_ref[...].astype(F32) * attn_out + sgc_ref[...].astype(F32) * conv_out
    x1 = x_ref[...] + jnp.dot(merged.astype(BF16), wo_ref[...], preferred_element_type=F32)
    ms = jnp.mean(x1 * x1, axis=-1, keepdims=True)
    h2 = (x1 * lax.rsqrt(ms + EPS) * g_ref[...]).astype(BF16)
    hc = wg_ref.shape[1] // n_chunks
    y = x1
    for ci in range(n_chunks):
        sl = slice(ci * hc, (ci + 1) * hc)
        gate = jnp.dot(h2, wg_ref[:, sl], preferred_element_type=F32)
        up = jnp.dot(h2, wu_ref[:, sl], preferred_element_type=F32)
        act = (gate * jax.nn.sigmoid(gate) * up).astype(BF16)
        y = y + jnp.dot(act, wd_ref[sl, :], preferred_element_type=F32)
    y_ref[...] = y


def _mix_ffn_conv(x, attn, u, past, sga, sgc, w, seq_len):
    m, d = x.shape
    c = u.shape[1]
    tm = min(ROW_TILE, seq_len)
    assert m % tm == 0 and seq_len % tm == 0 and tm % HALO == 0
    nt, n = seq_len // tm, m // tm
    hidden = w["wg"].shape[1]
    n_chunks = 2 if hidden % (2 * LANES) == 0 else 1
    row = lambda k: pl.BlockSpec((tm, k), lambda i: (i, 0))
    nxt = lambda i: jnp.minimum(i + 1, n - 1)
    consts = [w[k] for k in ("wao", "wco", "wo", "g_ffn", "wg", "wu", "wd", "conv_w", "conv_b", "ln_g", "ln_b")]
    return pl.pallas_call(
        functools.partial(_mix_ffn_conv_kernel, n_chunks=n_chunks, tiles_per_seq=nt),
        grid=(n,),
        in_specs=[row(d), row(attn.shape[1]),
                  pl.BlockSpec((tm, c), lambda i: (0, 0)),
                  pl.BlockSpec((tm, c), lambda i: (nxt(i), 0)),
                  pl.BlockSpec((HALO, c), lambda i: ((i + 1) * (tm // HALO) - 1, 0)),
                  pl.BlockSpec((None, HALO, c), lambda i: (0, 0, 0)),
                  pl.BlockSpec((None, HALO, c), lambda i: (nxt(i) // nt, 0, 0)),
                  row(d), row(d)] + [_const_spec(k.shape) for k in consts],
        out_specs=row(d),
        out_shape=jax.ShapeDtypeStruct((m, d), F32),
        scratch_shapes=[pltpu.VMEM((2, tm, c), BF16), pltpu.VMEM((SUBLANES, HALO + tm, c), F32)],
        compiler_params=pltpu.CompilerParams(dimension_semantics=("arbitrary",), vmem_limit_bytes=VMEM_LIMIT),
        name="mix_ffn_conv",
    )(x, attn, u, u, u, past, past, sga, sgc, *consts)


def _mix_ffn(x, attn, cact, sga, sgc, w):
    m, d = x.shape
    tm = min(ROW_TILE, m)
    assert m % tm == 0
    hidden = w["wg"].shape[1]
    n_chunks = 2 if hidden % (2 * LANES) == 0 else 1
    row = lambda n: pl.BlockSpec((tm, n), lambda i: (i, 0))
    consts = [w[n] for n in ("wao", "wco", "wo", "g_ffn", "wg", "wu", "wd")]
    return pl.pallas_call(
        functools.partial(_mix_ffn_kernel, n_chunks=n_chunks),
        grid=(m // tm,),
        in_specs=[row(d), row(attn.shape[1]), row(cact.shape[1]), row(d), row(d)]
                 + [_const_spec(c.shape) for c in consts],
        out_specs=row(d),
        out_shape=jax.ShapeDtypeStruct((m, d), F32),
        compiler_params=pltpu.CompilerParams(dimension_semantics=("arbitrary",), vmem_limit_bytes=VMEM_LIMIT),
        name="mix_ffn",
    )(x, attn, cact, sga, sgc, *consts)


def _prep_weights(norm_mix_g, w_in, q_norm_g, k_norm_g, idx_k_norm_g, conv_dw_w, conv_dw_b, conv_ln_g, conv_ln_b,
                  w_conv_out, w_attn_out, w_out, norm_ffn_g, w_ffn_gate, w_ffn_up, w_ffn_down):
    d = w_in.shape[0]
    conv_ch = conv_dw_w.shape[-1]
    sizes = (ATTN_W, ATTN_W, ATTN_W, N_IDX_HEADS * IDX_DIM, IDX_DIM, N_IDX_HEADS, 2 * conv_ch, d, d)
    offs = np.concatenate([[0], np.cumsum(sizes)])
    assert offs[-1] == w_in.shape[1]
    col = lambda i: w_in[:, offs[i]:offs[i + 1]]
    wsm = jnp.concatenate([col(4), col(5), jnp.zeros((d, LANES - IDX_DIM - N_IDX_HEADS), w_in.dtype)], axis=1)
    head = np.arange(ATTN_W) // HEAD_DIM
    seg = (head[:, None] == np.arange(LANES)[None, :]).astype(np.float32)
    return dict(
        g_mix=norm_mix_g.reshape(1, d).astype(F32),
        wq=col(0).astype(BF16), wk=col(1).astype(BF16), wv=col(2).astype(BF16), wiq=col(3).astype(BF16),
        wsm=wsm.astype(BF16), wglu=col(6).astype(BF16), wga=col(7).astype(BF16), wgc=col(8).astype(BF16),
        qg=jnp.tile(q_norm_g.astype(F32), N_HEADS).reshape(1, ATTN_W),
        kg=jnp.tile(k_norm_g.astype(F32), N_HEADS).reshape(1, ATTN_W),
        ikg=jnp.concatenate([idx_k_norm_g.astype(F32), jnp.zeros((LANES - IDX_DIM,), F32)]).reshape(1, LANES),
        seg=jnp.asarray(seg, BF16), segt=jnp.asarray(seg.T, BF16),
        conv_w=conv_dw_w.reshape(CONV_K, conv_ch).astype(F32), conv_b=conv_dw_b.reshape(1, conv_ch).astype(F32),
        ln_g=conv_ln_g.reshape(1, conv_ch).astype(F32), ln_b=conv_ln_b.reshape(1, conv_ch).astype(F32),
        wao=w_attn_out.astype(BF16), wco=w_conv_out.astype(BF16), wo=w_out.astype(BF16),
        g_ffn=norm_ffn_g.reshape(1, d).astype(F32),
        wg=w_ffn_gate.astype(BF16), wu=w_ffn_up.astype(BF16), wd=w_ffn_down.astype(BF16),
    )


def _layer(x, pos0, past_k, past_v, past_ik, past_conv, rel_bias, w):
    b, t, d = x.shape
    m = b * t
    prompt = past_k is None
    q, k, k_b, v, v_b, iq, sm, sm_b, u, sga, sgc = _in_proj(x, w, transposed=prompt)
    conv_ch = u.shape[1]
    u = u.reshape(b, t, conv_ch)
    k_b = k_b.reshape(b, t, ATTN_W)
    ik_b = sm_b.reshape(b, t, LANES)[:, :, :IDX_DIM]
    if prompt:
        attn = _sparse_attn(q, iq, sm[:, IDX_DIM:IDX_DIM + N_IDX_HEADS, :], k_b, v_b, ik_b, rel_bias,
                            pos0).swapaxes(1, 2)
        heads = lambda a: a.reshape(b, N_HEADS, HEAD_DIM, t).transpose(0, 3, 1, 2)
        new_k, new_v, new_ik = heads(k), heads(v), sm[:, :IDX_DIM, :].swapaxes(1, 2)
    else:
        sm = sm.reshape(b, t, LANES)
        k, v = k.reshape(b, t, ATTN_W), v.reshape(b, t, ATTN_W)
        attn = _dec_attn(q.reshape(b, t, ATTN_W), iq.reshape(b, t, ATTN_W), sm[:, :, IDX_DIM:IDX_DIM + N_IDX_HEADS],
                         past_k, past_v, past_ik, k, v, ik_b, rel_bias, pos0)
        heads = lambda a: a.reshape(b, t, N_HEADS, HEAD_DIM)
        new_k, new_v, new_ik = heads(k), heads(v), sm[:, :, :IDX_DIM]

    past = jnp.pad(past_conv.astype(F32), ((0, 0), (HALO - (CONV_K - 1), 0), (0, 0)))
    if prompt:
        y = _mix_ffn_conv(x.reshape(m, d), attn.reshape(m, ATTN_W), u.reshape(m, conv_ch), past, sga, sgc, w, t)
    else:
        cact = _conv_mod(u, past, w)
        y = _mix_ffn(x.reshape(m, d), attn.reshape(m, ATTN_W), cact.reshape(m, conv_ch), sga, sgc, w)

    new_conv = jnp.concatenate([past_conv, u], axis=1)[:, -(CONV_K - 1):]
    return y.reshape(b, t, d), new_k, new_v, new_ik, new_conv


def kernel(x_prompt, x_sample, cache_k, cache_v, cache_idx_k, state_conv, rel_bias, norm_mix_g, w_in, q_norm_g, k_norm_g, idx_k_norm_g, conv_dw_w, conv_dw_b, conv_ln_g, conv_ln_b, w_conv_out, w_attn_out, w_out, norm_ffn_g, w_ffn_gate, w_ffn_up, w_ffn_down):
    depth = w_in.shape[0]
    bp = x_prompt.shape[0]
    past_len = cache_k.shape[2]
    conv_ch = conv_dw_w.shape[-1]
    yp, ys = x_prompt, x_sample
    outs_p, outs_s = [], []
    for l in range(depth):
        w = _prep_weights(norm_mix_g[l], w_in[l], q_norm_g[l], k_norm_g[l], idx_k_norm_g[l], conv_dw_w[l],
                          conv_dw_b[l], conv_ln_g[l], conv_ln_b[l], w_conv_out[l], w_attn_out[l], w_out[l],
                          norm_ffn_g[l], w_ffn_gate[l], w_ffn_up[l], w_ffn_down[l])
        zero_conv = jnp.zeros((bp, CONV_K - 1, conv_ch), yp.dtype)
        yp, *rest_p = _layer(yp, 0, None, None, None, zero_conv, rel_bias, w)
        outs_p.append(rest_p)
        ys, *rest_s = _layer(ys, past_len, cache_k[l], cache_v[l], cache_idx_k[l], state_conv[l], rel_bias, w)
        outs_s.append(rest_s)
    stack = lambda outs, i: jnp.stack([o[i] for o in outs])
    return (yp, ys, stack(outs_p, 0), stack(outs_p, 1), stack(outs_p, 2), stack(outs_p, 3),
            stack(outs_s, 0), stack(outs_s, 1), stack(outs_s, 2), stack(outs_s, 3))
```

```python
import functools
import math

import numpy as np
import jax
import jax.numpy as jnp
from jax import lax
from jax.experimental import pallas as pl
from jax.experimental.pallas import tpu as pltpu

CHUNK = 64
CHUNK_SHIFT = 6
N_HEADS = 8
HEAD_DIM = 64
ATTN_W = N_HEADS * HEAD_DIM
N_IDX_HEADS = 8
IDX_DIM = 64
TOPK_MAX = 256
CONV_K = 31
NUM_BUCKETS = 32
MAX_DISTANCE = 128
EPS = 1e-6
NEG = -1e30
INT_MIN = -(2 ** 31)
BF16_INF_BITS = 0x7F80
BF16_MIN_NORMAL_BITS = 0x0080
N_BF16_NEG = BF16_INF_BITS - BF16_MIN_NORMAL_BITS + 1
LOG2E = math.log2(math.e)

LANES = 128
SUBLANES = 8
HALO = 32
CONV_ROWS = 32
KEY_TILE = 256
SEARCH_TRIP = 2
V_ROWS = HEAD_DIM + 16
ROW_TILE = 256
V7X_VMEM_BYTES = 64 * 1024 * 1024
VMEM_LIMIT = V7X_VMEM_BYTES - 8 * 1024 * 1024

F32 = jnp.float32
BF16 = jnp.bfloat16


def _const_spec(shape):
    return pl.BlockSpec(shape, lambda *_: (0,) * len(shape), pipeline_mode=pl.Buffered(1))


def _split_dot(x, m):
    hi = x.astype(BF16)
    lo = (x - hi.astype(F32)).astype(BF16)
    return jnp.dot(hi, m, preferred_element_type=F32) + jnp.dot(lo, m, preferred_element_type=F32)


def _in_proj_kernel(x_ref, g_ref, wq_ref, wk_ref, wv_ref, wiq_ref, wsm_ref, wglu_ref, wga_ref, wgc_ref,
                    qg_ref, kg_ref, ikg_ref, seg_ref, segt_ref,
                    q_ref, k_ref, kb_ref, v_ref, vb_ref, iq_ref, sm_ref, smb_ref, u_ref, sga_ref, sgc_ref,
                    *, transposed):
    x = x_ref[...]
    ms = jnp.mean(x * x, axis=-1, keepdims=True)
    h = (x * lax.rsqrt(ms + EPS) * g_ref[...]).astype(BF16)

    def put(ref, val):
        ref[...] = (val.T if transposed else val).astype(ref.dtype)

    glu = jnp.dot(h, wglu_ref[...], preferred_element_type=F32)
    c = glu.shape[1] // 2
    u_ref[...] = glu[:, :c] * jax.nn.sigmoid(glu[:, c:])

    def head_rms(y, gain):
        ss = _split_dot(y * y, seg_ref[...])
        r = lax.rsqrt(ss * (1.0 / HEAD_DIM) + EPS)
        return y * _split_dot(r, segt_ref[...]) * gain

    q = head_rms(jnp.dot(h, wq_ref[...], preferred_element_type=F32), qg_ref[...])
    put(q_ref, q * (HEAD_DIM ** -0.5 * LOG2E))
    k = head_rms(jnp.dot(h, wk_ref[...], preferred_element_type=F32), kg_ref[...])
    put(k_ref, k)
    kb_ref[...] = k.astype(BF16)
    v = jnp.dot(h, wv_ref[...], preferred_element_type=F32)
    put(v_ref, v)
    if transposed:
        vt = v.T.astype(BF16)
        pad_rows = lax.broadcasted_iota(jnp.int32, (V_ROWS - HEAD_DIM, vt.shape[1]), 0)
        ones_row = jnp.where(pad_rows == 0, 1.0, 0.0).astype(BF16)
        for hd in range(N_HEADS):
            vb_ref[hd * V_ROWS:hd * V_ROWS + HEAD_DIM, :] = vt[hd * HEAD_DIM:(hd + 1) * HEAD_DIM, :]
            vb_ref[hd * V_ROWS + HEAD_DIM:(hd + 1) * V_ROWS, :] = ones_row
    else:
        vb_ref[...] = v.astype(BF16)
    iq = jnp.dot(h, wiq_ref[...], preferred_element_type=F32)
    put(iq_ref, iq * (IDX_DIM ** -0.5))

    sm = jnp.dot(h, wsm_ref[...], preferred_element_type=F32)
    lane = lax.broadcasted_iota(jnp.int32, sm.shape, 1)
    is_ik = lane < IDX_DIM
    ss = jnp.sum(jnp.where(is_ik, sm * sm, 0.0), axis=-1, keepdims=True)
    ikn = sm * lax.rsqrt(ss * (1.0 / IDX_DIM) + EPS) * ikg_ref[...]
    iw = sm * (N_IDX_HEADS ** -0.5)
    smo = jnp.where(is_ik, ikn, jnp.where(lane < IDX_DIM + N_IDX_HEADS, iw, 0.0))
    put(sm_ref, smo)
    smb_ref[...] = smo.astype(BF16)

    sga_ref[...] = jax.nn.sigmoid(jnp.dot(h, wga_ref[...], preferred_element_type=F32)).astype(BF16)
    sgc_ref[...] = jax.nn.sigmoid(jnp.dot(h, wgc_ref[...], preferred_element_type=F32)).astype(BF16)


def _in_proj(x, w, transposed):
    b, t, d = x.shape
    m = b * t
    tm = min(ROW_TILE, t if transposed else m)
    assert m % tm == 0 and (not transposed or t % tm == 0)
    nt = t // tm
    conv_ch = w["wglu"].shape[1] // 2
    consts = [w[n] for n in ("g_mix", "wq", "wk", "wv", "wiq", "wsm", "wglu", "wga", "wgc",
                             "qg", "kg", "ikg", "seg", "segt")]

    def rows(n, dtype):
        return jax.ShapeDtypeStruct((m, n), dtype), pl.BlockSpec((tm, n), lambda i: (i, 0))

    def cols(n, dtype):
        if not transposed:
            return rows(n, dtype)
        return jax.ShapeDtypeStruct((b, n, t), dtype), pl.BlockSpec((None, n, tm), lambda i: (i // nt, 0, i % nt))

    if transposed:
        v_slab = (jax.ShapeDtypeStruct((b, nt, N_HEADS * V_ROWS, tm), BF16),
                  pl.BlockSpec((None, None, N_HEADS * V_ROWS, tm), lambda i: (i // nt, i % nt, 0, 0)))
    else:
        v_slab = rows(ATTN_W, BF16)
    outs = [
        cols(ATTN_W, BF16),
        cols(ATTN_W, F32),
        rows(ATTN_W, BF16),
        cols(ATTN_W, F32),
        v_slab,
        cols(ATTN_W, BF16),
        cols(LANES, F32),
        rows(LANES, BF16),
        rows(conv_ch, F32),
        rows(d, BF16),
        rows(d, BF16),
    ]
    return pl.pallas_call(
        functools.partial(_in_proj_kernel, transposed=transposed),
        grid=(m // tm,),
        in_specs=[pl.BlockSpec((tm, d), lambda i: (i, 0))] + [_const_spec(c.shape) for c in consts],
        out_specs=[o[1] for o in outs],
        out_shape=[o[0] for o in outs],
        compiler_params=pltpu.CompilerParams(dimension_semantics=("arbitrary",), vmem_limit_bytes=VMEM_LIMIT),
        name="in_proj",
    )(x.reshape(m, d), *consts)


def _fold(x, op):
    return op(x.reshape(x.shape[0] // SUBLANES, SUBLANES, x.shape[1]), axis=0)


def _select_top_k(key_scr, part_scr, n_kt, *, tk, tq, top_k):
    n_trips = (n_kt + SEARCH_TRIP - 1) // SEARCH_TRIP
    for j in range(SEARCH_TRIP - 1):
        @pl.when(n_kt + j < n_trips * SEARCH_TRIP)
        def _():
            key_scr[n_kt + j] = jnp.full((tk, tq), INT_MIN, jnp.int32)
            part_scr[n_kt + j] = jnp.full((tk, tq), -jnp.inf, BF16)

    trip = lambda i: [SEARCH_TRIP * i + j for j in range(SEARCH_TRIP)]

    def count(pred):
        def body(i, acc):
            for kt in trip(i):
                acc = acc + _fold(jnp.where(pred(key_scr[kt], kt), 1.0, 0.0), jnp.sum)
            return acc
        acc = lax.fori_loop(0, n_trips, body, jnp.zeros((SUBLANES, tq), F32))
        return jnp.sum(acc, axis=0, keepdims=True)

    rows16 = 2 * SUBLANES
    one_b, zero_b = jnp.ones((), BF16), jnp.zeros((), BF16)

    def count_part(pred):
        def body(i, acc):
            for kt in trip(i):
                c = jnp.where(pred(part_scr[kt]), one_b, zero_b)
                parts = [c[j * rows16:(j + 1) * rows16] for j in range(4)]
                for j in range(4, tk // rows16):
                    parts[j % 4] = parts[j % 4] + c[j * rows16:(j + 1) * rows16]
                acc = acc + ((parts[0] + parts[1]) + (parts[2] + parts[3])).astype(F32)
            return acc
        acc = lax.fori_loop(0, n_trips, body, jnp.zeros((rows16, tq), F32))
        return jnp.sum(acc, axis=0, keepdims=True)

    def search(n_bits, base, to_image, limit):
        def body(i, t):
            cand = t + (jnp.int32(1) << (n_bits - 1 - i))
            img = to_image(cand)
            c = base + count_part(lambda a: a >= img)
            return jnp.where((c >= top_k) & (cand <= limit), cand, t)
        return lax.fori_loop(0, n_bits, body, jnp.zeros((1, tq), jnp.int32))

    def refine(eq_img, shift):
        def body(i, carry):
            for kt in trip(i):
                nxt = ((key_scr[kt] >> shift) & 0xFF).astype(F32).astype(BF16)
                part_scr[kt] = jnp.where(part_scr[kt] == eq_img, nxt, -one_b)
            return carry
        lax.fori_loop(0, n_trips, body, 0)

    def top_bits(c):
        neg = 0x8000 | (BF16_INF_BITS - c)
        pos = BF16_MIN_NORMAL_BITS + (c - N_BF16_NEG - 1)
        return jnp.where(c < N_BF16_NEG, neg, jnp.where(c == N_BF16_NEG, 0, pos))

    def top_image(c):
        return pltpu.bitcast(top_bits(c) << 16, F32).astype(BF16)

    small_image = lambda c: c.astype(F32).astype(BF16)

    t1 = search(16, 0.0, top_image, 2 * N_BF16_NEG)
    img1 = top_image(t1)
    above = count_part(lambda a: a > img1)
    refine(img1, 8)
    t2 = search(8, above, small_image, 255)
    img2 = small_image(t2)
    above = above + count_part(lambda a: a > img2)
    refine(img2, 0)
    t3 = search(8, above, small_image, 255)
    hi = top_bits(t1)
    hi = hi - ((hi >> 15) << 16)
    hi = hi ^ ((hi >> 15) & 0x7FFF)
    thr = (hi << 16) + (t2 << 8) + t3
    thr = jnp.where(t1 == 0, INT_MIN + 1, jnp.maximum(thr, INT_MIN + 1))

    n_ge = count(lambda key, kt: key >= thr)
    has_tie = jnp.max(n_ge) > top_k

    @pl.when(has_tie)
    def _():
        need_m1 = (top_k - 1) - count(lambda key, kt: key > thr)
        row = lax.broadcasted_iota(jnp.int32, (tk, tq), 0)

        def idx_body(i, cut):
            cand = cut + (jnp.int32(1) << (15 - i))
            c = count(lambda key, kt: (key == thr) & (kt * tk + row < cand))
            return jnp.where(c <= need_m1, cand, cut)

        cut = lax.fori_loop(0, 16, idx_body, jnp.zeros((1, tq), jnp.int32))
        tie_col = n_ge > top_k

        def drop_body(kt, carry):
            key = key_scr[kt]
            drop = tie_col & (key == thr) & (kt * tk + row > cut)
            key_scr[kt] = jnp.where(drop, INT_MIN, key)
            return carry

        lax.fori_loop(0, n_kt, drop_body, 0)
    return thr


def _sparse_attn_kernel(qt_ref, iqt_ref, iwt_ref, k_ref, vt_ref, ik_ref, bias_ref, o_ref,
                        key_scr, part_scr, m_scr, acc_scr, s_scr, *, tq, tk, nkt, pos0, n_keys, top_k):
    qi = pl.program_id(1)
    q0 = pos0 + qi * tq
    n_kt = jnp.minimum(nkt, ((q0 + tq - 1) // CHUNK * CHUNK + CHUNK + tk - 1) // tk)
    last = n_kt - 1

    fold = _fold

    iqt = iqt_ref[...]
    iqt_h = [iqt[h * IDX_DIM:(h + 1) * IDX_DIM, :] for h in range(N_IDX_HEADS)]
    iwt = iwt_ref[...]
    iw_h = [iwt[h:h + 1, :] for h in range(N_IDX_HEADS)]

    def score_tile(kt, masked):
        ik = ik_ref[kt]
        s = None
        for h in range(N_IDX_HEADS):
            t = jnp.maximum(jnp.dot(ik, iqt_h[h], preferred_element_type=F32), 0.0) * iw_h[h]
            s = t if s is None else s + t
        s = s + 0.0
        bits = pltpu.bitcast(s, jnp.int32)
        key = bits ^ ((bits >> 31) & 0x7FFFFFFF)
        top = pltpu.bitcast(bits & -65536, F32)
        if masked:
            kp = kt * tk + lax.broadcasted_iota(jnp.int32, (tk, tq), 0)
            qp = q0 + lax.broadcasted_iota(jnp.int32, (tk, tq), 1)
            adm = ((kp >> CHUNK_SHIFT) <= (qp >> CHUNK_SHIFT)) & (kp < n_keys)
            key = jnp.where(adm, key, INT_MIN)
            top = jnp.where(adm, top, -jnp.inf)
        key_scr[kt] = key
        part_scr[kt] = top.astype(BF16)

    def score_body(i, carry):
        score_tile(2 * i, False)
        score_tile(2 * i + 1, False)
        return carry

    lax.fori_loop(0, last // 2, score_body, 0)

    @pl.when(last % 2 == 1)
    def _():
        score_tile(last - 1, False)

    score_tile(last, True)

    thr = _select_top_k(key_scr, part_scr, n_kt, tk=tk, tq=tq, top_k=top_k)

    qt = qt_ref[...]
    slab_row = lax.broadcasted_iota(jnp.int32, (LANES, tq), 0)
    qz = []
    for h in range(N_HEADS):
        slab = qt[(h // 2) * LANES:(h // 2 + 1) * LANES, :]
        mine = (slab_row < HEAD_DIM) if h % 2 == 0 else (slab_row >= HEAD_DIM)
        qz.append(jnp.where(mine, slab, jnp.zeros_like(slab)))
    m_scr[...] = jnp.full(m_scr.shape, NEG, F32)
    acc_scr[...] = jnp.zeros(acc_scr.shape, F32)

    def park(kt, slot):
        madd = jnp.where(key_scr[kt] >= thr, 0.0, NEG)
        for h in range(N_HEADS):
            pair = h // 2
            s = jnp.dot(k_ref[kt, :, pair * LANES:(pair + 1) * LANES], qz[h], preferred_element_type=F32)
            s_scr[slot, h] = s + madd

    def absorb(kt, slot, bias_idx):
        for h in range(N_HEADS):
            s = s_scr[slot, h]
            if bias_idx is not None:
                s = s + bias_ref[bias_idx, h]
            m_new = jnp.maximum(m_scr[h], jnp.max(fold(s, jnp.max), axis=0, keepdims=True))
            alpha = jnp.exp2(m_scr[h] - m_new)
            p = jnp.exp2(s - m_new)
            pv = jnp.dot(vt_ref[kt, h * V_ROWS:(h + 1) * V_ROWS, :], p.astype(BF16), preferred_element_type=F32)
            acc_scr[h] = alpha * acc_scr[h] + pv
            m_scr[h] = m_new

    park(0, 0)
    n_trips = (n_kt - 2) // 2

    def far_body(i, carry):
        kt = 2 * i
        park(kt + 1, 1)
        absorb(kt, 0, None)
        park(kt + 2, 0)
        absorb(kt + 1, 1, None)
        return carry

    lax.fori_loop(0, n_trips, far_body, 0)
    done = 2 * jnp.maximum(n_trips, 0)

    @pl.when(n_kt == 1)
    def _():
        absorb(0, 0, 1)

    @pl.when((n_kt >= 2) & (n_kt % 2 == 0))
    def _():
        park(done + 1, 1)
        absorb(done, 0, 0)
        absorb(done + 1, 1, 1)

    @pl.when((n_kt >= 3) & (n_kt % 2 == 1))
    def _():
        park(done + 1, 1)
        absorb(done, 0, None)
        park(done + 2, 0)
        absorb(done + 1, 1, 0)
        absorb(done + 2, 0, 1)

    for h in range(N_HEADS):
        acc = acc_scr[h]
        o_ref[h * HEAD_DIM:(h + 1) * HEAD_DIM, :] = (acc[:HEAD_DIM] / acc[HEAD_DIM:HEAD_DIM + 1]).astype(o_ref.dtype)


def _rel_bucket(rel):
    nb = NUM_BUCKETS // 2
    max_exact = nb // 2
    ret = jnp.where(rel > 0, nb, 0)
    n = jnp.abs(rel)
    nf = jnp.maximum(n, 1).astype(jnp.float32)
    large = max_exact + (jnp.log(nf / max_exact) / math.log(MAX_DISTANCE / max_exact) * (nb - max_exact)).astype(jnp.int32)
    large = jnp.minimum(large, nb - 1)
    return ret + jnp.where(n < max_exact, n, large)


def _bias_tiles(rel_bias, tq, tk):
    j = jnp.arange(tk, dtype=jnp.int32)[:, None]
    i = jnp.arange(tq, dtype=jnp.int32)[None, :]
    return jnp.stack([_shifted_bias(rel_bias, j - i + d) for d in (-tk, 0)])


def _shifted_bias(rel_bias, rel):
    rb = (rel_bias.astype(F32) - rel_bias[_rel_bucket(jnp.int32(-MAX_DISTANCE))].astype(F32)) * LOG2E
    onehot = jax.nn.one_hot(_rel_bucket(rel), NUM_BUCKETS, dtype=F32)
    return jnp.moveaxis(jnp.einsum("...b,bh->...h", onehot, rb, precision=lax.Precision.HIGHEST), -1, 0)


def _check_tiling(t, tq, tk, nkt, pos0, n_keys):
    for qi in range(t // tq):
        q0 = pos0 + qi * tq
        n_kt = min(nkt, ((q0 + tq - 1) // CHUNK * CHUNK + CHUNK + tk - 1) // tk)
        adm_end = min(q0 // CHUNK * CHUNK + CHUNK, n_keys)
        assert (n_kt - 1) * tk <= adm_end, "only the last visited tile may hold inadmissible keys"
        assert n_kt * tk >= min((q0 + tq - 1) // CHUNK * CHUNK + CHUNK, n_keys), "visited tiles cover every admissible key"
        assert (n_kt - 1) * tk == q0, "last tile starts with the query tile"
        assert tk >= MAX_DISTANCE, "tiles before the last two are at least MAX_DISTANCE behind"


def _sparse_attn(qt, iqt, iwt, k_b, vt_t, ik_b, rel_bias, pos0):
    b, _, t = qt.shape
    n_keys = k_b.shape[1]
    top_k = min(TOPK_MAX, n_keys // 4)
    tk = KEY_TILE
    tq = min(tk, t)
    nkt = n_keys // tk
    assert n_keys % tk == 0 and vt_t.shape == (b, nkt, N_HEADS * V_ROWS, tk)
    _check_tiling(t, tq, tk, nkt, pos0, n_keys)
    k_t = k_b.reshape(b, nkt, tk, ATTN_W)
    ik_t = ik_b.reshape(b, nkt, tk, IDX_DIM)
    bias = _bias_tiles(rel_bias, tq, tk)

    qspec = lambda n: pl.BlockSpec((None, n, tq), lambda bi, qi: (bi, 0, qi))
    kspec = lambda r, c: pl.BlockSpec((None, nkt, r, c), lambda bi, qi: (bi, 0, 0, 0))
    kern = functools.partial(_sparse_attn_kernel, tq=tq, tk=tk, nkt=nkt, pos0=pos0, n_keys=n_keys, top_k=top_k)
    return pl.pallas_call(
        kern,
        grid=(b, t // tq),
        in_specs=[qspec(ATTN_W), qspec(ATTN_W), qspec(N_IDX_HEADS),
                  kspec(tk, ATTN_W), kspec(N_HEADS * V_ROWS, tk), kspec(tk, IDX_DIM), _const_spec(bias.shape)],
        out_specs=qspec(ATTN_W),
        out_shape=jax.ShapeDtypeStruct((b, ATTN_W, t), BF16),
        scratch_shapes=[pltpu.VMEM((nkt + SEARCH_TRIP - 1, tk, tq), jnp.int32),
                        pltpu.VMEM((nkt + SEARCH_TRIP - 1, tk, tq), BF16),
                        pltpu.VMEM((N_HEADS, 1, tq), F32),
                        pltpu.VMEM((N_HEADS, V_ROWS, tq), F32),
                        pltpu.VMEM((2, N_HEADS, tk, tq), F32)],
        compiler_params=pltpu.CompilerParams(dimension_semantics=("arbitrary", "arbitrary"),
                                             vmem_limit_bytes=VMEM_LIMIT),
        name="sparse_attn",
    )(qt, iqt, iwt, k_t, vt_t, ik_t, bias)


DEC_ROWS = N_HEADS * 16
DEC_KEY_TILE = 1024
NT_DIMS = (((1,), (1,)), ((), ()))


def _dec_select_kernel(iq_ref, iwc_ref, ikt_ref, iktn_ref, madd_ref, key_scr, *, g, tq, n_keys, n_pad, top_k):
    n_slabs = n_pad // LANES
    col = lax.broadcasted_iota(jnp.int32, (tq, n_pad), 1)
    for s in range(g):
        iq_s = iq_ref[s * tq:(s + 1) * tq, :]
        lhs = jnp.concatenate([iq_s[:, h * IDX_DIM:(h + 1) * IDX_DIM] for h in range(N_IDX_HEADS)], axis=0)
        logit = jnp.concatenate([jnp.dot(lhs, ikt_ref[s].astype(BF16), preferred_element_type=F32),
                                 jnp.dot(lhs, iktn_ref[s], preferred_element_type=F32)], axis=1)
        x = jnp.maximum(logit, 0.0) * iwc_ref[s]
        score = x[0:tq]
        for h in range(1, N_IDX_HEADS):
            score = score + x[h * tq:(h + 1) * tq]
        score = score + 0.0
        bits = pltpu.bitcast(score, jnp.int32)
        key = bits ^ ((bits >> 31) & 0x7FFFFFFF)
        key_scr[s * tq:(s + 1) * tq, :] = jnp.where(col < n_keys, key, INT_MIN)

    rows = g * tq
    lane = lax.broadcasted_iota(jnp.int32, (rows, LANES), 1)

    def count(pred):
        acc = jnp.zeros((rows, LANES), F32)
        for j in range(n_slabs):
            acc = acc + jnp.where(pred(key_scr[:, j * LANES:(j + 1) * LANES], j), 1.0, 0.0)
        return jnp.sum(acc, axis=1, keepdims=True)

    def thr_body(i, t):
        cand = t + (jnp.int32(1) << (31 - i))
        c = count(lambda key, j: key >= cand)
        return jnp.where(c >= top_k, cand, t)

    thr = lax.fori_loop(0, 32, thr_body, jnp.full((rows, 1), INT_MIN, jnp.int32))
    thr = jnp.maximum(thr, INT_MIN + 1)

    n_ge = count(lambda key, j: key >= thr)
    has_tie = jnp.max(n_ge) > top_k

    @pl.when(has_tie)
    def _():
        need_m1 = (top_k - 1) - count(lambda key, j: key > thr)

        def idx_body(i, cut):
            cand = cut + (jnp.int32(1) << (15 - i))
            c = count(lambda key, j: (key == thr) & (j * LANES + lane < cand))
            return jnp.where(c <= need_m1, cand, cut)

        cut = lax.fori_loop(0, 16, idx_body, jnp.zeros((rows, 1), jnp.int32))
        tie_row = n_ge > top_k
        for j in range(n_slabs):
            key = key_scr[:, j * LANES:(j + 1) * LANES]
            drop = tie_row & (key == thr) & (j * LANES + lane > cut)
            key_scr[:, j * LANES:(j + 1) * LANES] = jnp.where(drop, INT_MIN, key)

    madd_ref[...] = jnp.where(key_scr[...] >= thr, 0.0, NEG)


def _dec_attn_kernel(q_ref, maddc_ref, maddn_ref, k_ref, v_ref, kn_ref, vn_ref, biasc_ref, biasn_ref, o_ref,
                     m_scr, l_scr, acc_scr, *, tq):
    kt = pl.program_id(1)
    q = q_ref[...]
    q_h = [q[:, h * HEAD_DIM:(h + 1) * HEAD_DIM] for h in range(N_HEADS)]

    @pl.when(kt == 0)
    def _():
        m_scr[...] = jnp.full(m_scr.shape, NEG, F32)
        l_scr[...] = jnp.zeros(l_scr.shape, F32)
        acc_scr[...] = jnp.zeros(acc_scr.shape, F32)

    def step(ktr, vtr, madd, bias):
        s = jnp.concatenate([jnp.dot(q_h[h], ktr[h].astype(BF16), preferred_element_type=F32)
                             for h in range(N_HEADS)], axis=0)
        s = s + jnp.concatenate([madd] * N_HEADS, axis=0) + bias
        m_prev = m_scr[...]
        m_new = jnp.maximum(m_prev, jnp.max(s, axis=1, keepdims=True))
        alpha = jnp.exp2(m_prev - m_new)
        p = jnp.exp2(s - m_new)
        l_scr[...] = alpha * l_scr[...] + jnp.sum(p, axis=1, keepdims=True)
        pb = p.astype(BF16)
        pv = jnp.concatenate([lax.dot_general(pb[h * tq:(h + 1) * tq, :], vtr[h].astype(BF16), NT_DIMS,
                                              preferred_element_type=F32) for h in range(N_HEADS)], axis=0)
        acc_scr[...] = alpha * acc_scr[...] + pv
        m_scr[...] = m_new

    step(k_ref, v_ref, maddc_ref[...], biasc_ref[...])

    @pl.when(kt == pl.num_programs(1) - 1)
    def _():
        step(kn_ref, vn_ref, maddn_ref[...], biasn_ref[...])
        o = acc_scr[...] / l_scr[...]
        for h in range(N_HEADS):
            o_ref[:, h * HEAD_DIM:(h + 1) * HEAD_DIM] = o[h * tq:(h + 1) * tq, :].astype(o_ref.dtype)


def _dec_attn(q, iq, iw, past_k, past_v, past_ik, k_new, v_new, ik_new, rel_bias, pos0):
    b, t, _ = q.shape
    p_len = past_k.shape[1]
    n_keys = p_len + t
    top_k = min(TOPK_MAX, n_keys // 4)
    tkk = min(DEC_KEY_TILE, p_len)
    n_pad = p_len + LANES
    g = min(LANES // t, b)
    assert N_HEADS * t == DEC_ROWS and b % g == 0 and p_len % tkk == 0 and t <= LANES
    assert (pos0 + t - 1) // CHUNK == pos0 // CHUNK and (n_keys - 1) // CHUNK <= pos0 // CHUNK, "every key admissible"
    assert tkk >= MAX_DISTANCE + LANES >= MAX_DISTANCE + t, "only the last cache tile is within MAX_DISTANCE of a query"

    ikt_new = jnp.pad(ik_new.swapaxes(1, 2), ((0, 0), (0, 0), (0, LANES - t)))
    iw_col = iw.swapaxes(1, 2).reshape(b, DEC_ROWS, 1)
    madd = pl.pallas_call(
        functools.partial(_dec_select_kernel, g=g, tq=t, n_keys=n_keys, n_pad=n_pad, top_k=top_k),
        grid=(b // g,),
        in_specs=[pl.BlockSpec((g * t, ATTN_W), lambda i: (i, 0)),
                  pl.BlockSpec((g, DEC_ROWS, 1), lambda i: (i, 0, 0)),
                  pl.BlockSpec((g, IDX_DIM, p_len), lambda i: (i, 0, 0)),
                  pl.BlockSpec((g, IDX_DIM, LANES), lambda i: (i, 0, 0))],
        out_specs=pl.BlockSpec((g * t, n_pad), lambda i: (i, 0)),
        out_shape=jax.ShapeDtypeStruct((b * t, n_pad), F32),
        scratch_shapes=[pltpu.VMEM((g * t, n_pad), jnp.int32)],
        compiler_params=pltpu.CompilerParams(dimension_semantics=("arbitrary",), vmem_limit_bytes=VMEM_LIMIT),
        name="dec_select",
    )(iq.reshape(b * t, ATTN_W), iw_col, past_ik.swapaxes(1, 2), ikt_new)

    n_steps = p_len // tkk
    qpos = jnp.tile(jnp.arange(t, dtype=jnp.int32), N_HEADS)[:, None]
    hsel = jnp.repeat(jnp.arange(N_HEADS), t)
    rows = jnp.arange(DEC_ROWS)
    near = MAX_DISTANCE + LANES
    rel_c = jnp.arange(near, dtype=jnp.int32)[None, :] - near - qpos
    rel_n = jnp.arange(LANES, dtype=jnp.int32)[None, :] - qpos
    bias_last = jnp.pad(_shifted_bias(rel_bias, rel_c)[hsel, rows], ((0, 0), (tkk - near, 0)))
    bias_c = jnp.concatenate([jnp.zeros((n_steps - 1, DEC_ROWS, tkk), F32), bias_last[None]], axis=0)
    bias_n = _shifted_bias(rel_bias, rel_n)[hsel, rows]
    heads_t = lambda a: a.transpose(0, 2, 3, 1)
    pad_new = lambda a: jnp.pad(heads_t(a.reshape(b, t, N_HEADS, HEAD_DIM)), ((0, 0), (0, 0), (0, 0), (0, LANES - t)))
    kv_tile = pl.BlockSpec((None, N_HEADS, HEAD_DIM, tkk), lambda bi, kt: (bi, 0, 0, kt))
    kv_new = pl.BlockSpec((None, N_HEADS, HEAD_DIM, LANES), lambda bi, kt: (bi, 0, 0, 0))
    return pl.pallas_call(
        functools.partial(_dec_attn_kernel, tq=t),
        grid=(b, n_steps),
        in_specs=[pl.BlockSpec((None, t, ATTN_W), lambda bi, kt: (bi, 0, 0)),
                  pl.BlockSpec((t, tkk), lambda bi, kt: (bi, kt)),
                  pl.BlockSpec((t, LANES), lambda bi, kt: (bi, p_len // LANES)),
                  kv_tile, kv_tile, kv_new, kv_new,
                  pl.BlockSpec((None, DEC_ROWS, tkk), lambda bi, kt: (kt, 0, 0)),
                  _const_spec(bias_n.shape)],
        out_specs=pl.BlockSpec((None, t, ATTN_W), lambda bi, kt: (bi, 0, 0)),
        out_shape=jax.ShapeDtypeStruct((b, t, ATTN_W), BF16),
        scratch_shapes=[pltpu.VMEM((DEC_ROWS, 1), F32), pltpu.VMEM((DEC_ROWS, 1), F32),
                        pltpu.VMEM((DEC_ROWS, HEAD_DIM), F32)],
        compiler_params=pltpu.CompilerParams(dimension_semantics=("arbitrary", "arbitrary"),
                                             vmem_limit_bytes=VMEM_LIMIT),
        name="dec_attn",
    )(q, madd, madd, heads_t(past_k), heads_t(past_v), pad_new(k_new), pad_new(v_new), bias_c, bias_n)


def _conv_kernel(*refs, tm, has_prev):
    if has_prev:
        u_ref, prev_ref, past_ref, w_ref, b_ref, lg_ref, lb_ref, o_ref, ext_scr = refs
        halo = jnp.where(pl.program_id(1) == 0, past_ref[...], prev_ref[...])
    else:
        u_ref, past_ref, w_ref, b_ref, lg_ref, lb_ref, o_ref, ext_scr = refs
        halo = past_ref[...]
    _conv_act(halo, u_ref[...], w_ref, b_ref, lg_ref, lb_ref, o_ref, ext_scr)


def _conv_act(halo, u, w_ref, b_ref, lg_ref, lb_ref, o_ref, ext_scr):
    tm = u.shape[0]
    ext_scr[0, 0:HALO, :] = halo
    ext_scr[0, HALO:HALO + tm, :] = u
    n_rows = tm + HALO - SUBLANES
    for r in range(1, SUBLANES):
        ext_scr[r, 0:n_rows, :] = ext_scr[0, r:r + n_rows, :]
    rb = min(tm, CONV_ROWS)
    first = HALO - (CONV_K - 1)
    for r0 in range(0, tm, rb):
        acc = None
        for j in range(CONV_K):
            a, r = divmod(first + j, SUBLANES)
            t = ext_scr[r, r0 + a * SUBLANES:r0 + a * SUBLANES + rb, :] * w_ref[j:j + 1, :]
            acc = t if acc is None else acc + t
        c = acc + b_ref[...]
        mu = jnp.mean(c, axis=-1, keepdims=True)
        cc = c - mu
        var = jnp.mean(cc * cc, axis=-1, keepdims=True)
        y = cc * lax.rsqrt(var + EPS) * lg_ref[...] + lb_ref[...]
        o_ref[r0:r0 + rb, :] = (y * jax.nn.sigmoid(y)).astype(o_ref.dtype)


def _conv_mod(u, past, w):
    b, t, c = u.shape
    tm = min(ROW_TILE, t)
    assert t % tm == 0 and tm % 8 == 0
    has_prev = t > tm
    tile = pl.BlockSpec((None, tm, c), lambda bi, i: (bi, i, 0))
    halo = pl.BlockSpec((None, HALO, c), lambda bi, i: (bi, 0, 0))
    in_specs = [tile]
    args = [u]
    if has_prev:
        r = tm // HALO
        in_specs.append(pl.BlockSpec((None, HALO, c), lambda bi, i: (bi, jnp.maximum(i * r - 1, 0), 0)))
        args.append(u)
    consts = [w["conv_w"], w["conv_b"], w["ln_g"], w["ln_b"]]
    in_specs += [halo] + [_const_spec(x.shape) for x in consts]
    args += [past] + consts
    return pl.pallas_call(
        functools.partial(_conv_kernel, tm=tm, has_prev=has_prev),
        grid=(b, t // tm),
        in_specs=in_specs,
        out_specs=tile,
        out_shape=jax.ShapeDtypeStruct((b, t, c), BF16),
        scratch_shapes=[pltpu.VMEM((SUBLANES, HALO + tm, c), F32)],
        compiler_params=pltpu.CompilerParams(dimension_semantics=("arbitrary", "arbitrary"),
                                             vmem_limit_bytes=VMEM_LIMIT),
        name="conv_mod",
    )(*args)


def _mix_ffn_conv_kernel(x_ref, attn_ref, u0_ref, unext_ref, utail_ref, past0_ref, pastn_ref, sga_ref, sgc_ref,
                         wao_ref, wco_ref, wo_ref, g_ref, wg_ref, wu_ref, wd_ref, cw_ref, cb_ref, lg_ref, lb_ref,
                         y_ref, cact_scr, ext_scr, *, n_chunks, tiles_per_seq):
    i = pl.program_id(0)

    @pl.when(i == 0)
    def _():
        _conv_act(past0_ref[...], u0_ref[...], cw_ref, cb_ref, lg_ref, lb_ref, cact_scr.at[0], ext_scr)

    _mix_ffn_kernel(x_ref, attn_ref, cact_scr.at[i % 2], sga_ref, sgc_ref, wao_ref, wco_ref, wo_ref, g_ref,
                    wg_ref, wu_ref, wd_ref, y_ref, n_chunks=n_chunks)
    halo = jnp.where((i + 1) % tiles_per_seq == 0, pastn_ref[...], utail_ref[...])
    _conv_act(halo, unext_ref[...], cw_ref, cb_ref, lg_ref, lb_ref, cact_scr.at[(i + 1) % 2], ext_scr)


def _mix_ffn_kernel(x_ref, attn_ref, cact_ref, sga_ref, sgc_ref, wao_ref, wco_ref, wo_ref, g_ref,
                    wg_ref, wu_ref, wd_ref, y_ref, *, n_chunks):
    attn_out = jnp.dot(attn_ref[...], wao_ref[...], preferred_element_type=F32)
    conv_out = jnp.dot(cact_ref[...], wco_ref[...], preferred_element_type=F32)
    merged = sga_ref[...].astype(F32) * attn_out + sgc_ref[...].astype(F32) * conv_out
    x1 = x_ref[...] + jnp.dot(merged.astype(BF16), wo_ref[...], preferred_element_type=F32)
    ms = jnp.mean(x1 * x1, axis=-1, keepdims=True)
    h2 = (x1 * lax.rsqrt(ms + EPS) * g_ref[...]).astype(BF16)
    hc = wg_ref.shape[1] // n_chunks
    y = x1
    for ci in range(n_chunks):
        sl = slice(ci * hc, (ci + 1) * hc)
        gate = jnp.dot(h2, wg_ref[:, sl], preferred_element_type=F32)
        up = jnp.dot(h2, wu_ref[:, sl], preferred_element_type=F32)
        act = (gate * jax.nn.sigmoid(gate) * up).astype(BF16)
        y = y + jnp.dot(act, wd_ref[sl, :], preferred_element_type=F32)
    y_ref[...] = y


def _mix_ffn_conv(x, attn, u, past, sga, sgc, w, seq_len):
    m, d = x.shape
    c = u.shape[1]
    tm = min(ROW_TILE, seq_len)
    assert m % tm == 0 and seq_len % tm == 0 and tm % HALO == 0
    nt, n = seq_len // tm, m // tm
    hidden = w["wg"].shape[1]
    n_chunks = 2 if hidden % (2 * LANES) == 0 else 1
    row = lambda k: pl.BlockSpec((tm, k), lambda i: (i, 0))
    nxt = lambda i: jnp.minimum(i + 1, n - 1)
    consts = [w[k] for k in ("wao", "wco", "wo", "g_ffn", "wg", "wu", "wd", "conv_w", "conv_b", "ln_g", "ln_b")]
    return pl.pallas_call(
        functools.partial(_mix_ffn_conv_kernel, n_chunks=n_chunks, tiles_per_seq=nt),
        grid=(n,),
        in_specs=[row(d), row(attn.shape[1]),
                  pl.BlockSpec((tm, c), lambda i: (0, 0)),
                  pl.BlockSpec((tm, c), lambda i: (nxt(i), 0)),
                  pl.BlockSpec((HALO, c), lambda i: ((i + 1) * (tm // HALO) - 1, 0)),
                  pl.BlockSpec((None, HALO, c), lambda i: (0, 0, 0)),
                  pl.BlockSpec((None, HALO, c), lambda i: (nxt(i) // nt, 0, 0)),
                  row(d), row(d)] + [_const_spec(k.shape) for k in consts],
        out_specs=row(d),
        out_shape=jax.ShapeDtypeStruct((m, d), F32),
        scratch_shapes=[pltpu.VMEM((2, tm, c), BF16), pltpu.VMEM((SUBLANES, HALO + tm, c), F32)],
        compiler_params=pltpu.CompilerParams(dimension_semantics=("arbitrary",), vmem_limit_bytes=VMEM_LIMIT),
        name="mix_ffn_conv",
    )(x, attn, u, u, u, past, past, sga, sgc, *consts)


def _mix_ffn(x, attn, cact, sga, sgc, w):
    m, d = x.shape
    tm = min(ROW_TILE, m)
    assert m % tm == 0
    hidden = w["wg"].shape[1]
    n_chunks = 2 if hidden % (2 * LANES) == 0 else 1
    row = lambda n: pl.BlockSpec((tm, n), lambda i: (i, 0))
    consts = [w[n] for n in ("wao", "wco", "wo", "g_ffn", "wg", "wu", "wd")]
    return pl.pallas_call(
        functools.partial(_mix_ffn_kernel, n_chunks=n_chunks),
        grid=(m // tm,),
        in_specs=[row(d), row(attn.shape[1]), row(cact.shape[1]), row(d), row(d)]
                 + [_const_spec(c.shape) for c in consts],
        out_specs=row(d),
        out_shape=jax.ShapeDtypeStruct((m, d), F32),
        compiler_params=pltpu.CompilerParams(dimension_semantics=("arbitrary",), vmem_limit_bytes=VMEM_LIMIT),
        name="mix_ffn",
    )(x, attn, cact, sga, sgc, *consts)


def _prep_weights(norm_mix_g, w_in, q_norm_g, k_norm_g, idx_k_norm_g, conv_dw_w, conv_dw_b, conv_ln_g, conv_ln_b,
                  w_conv_out, w_attn_out, w_out, norm_ffn_g, w_ffn_gate, w_ffn_up, w_ffn_down):
    d = w_in.shape[0]
    conv_ch = conv_dw_w.shape[-1]
    sizes = (ATTN_W, ATTN_W, ATTN_W, N_IDX_HEADS * IDX_DIM, IDX_DIM, N_IDX_HEADS, 2 * conv_ch, d, d)
    offs = np.concatenate([[0], np.cumsum(sizes)])
    assert offs[-1] == w_in.shape[1]
    col = lambda i: w_in[:, offs[i]:offs[i + 1]]
    wsm = jnp.concatenate([col(4), col(5), jnp.zeros((d, LANES - IDX_DIM - N_IDX_HEADS), w_in.dtype)], axis=1)
    head = np.arange(ATTN_W) // HEAD_DIM
    seg = (head[:, None] == np.arange(LANES)[None, :]).astype(np.float32)
    return dict(
        g_mix=norm_mix_g.reshape(1, d).astype(F32),
        wq=col(0).astype(BF16), wk=col(1).astype(BF16), wv=col(2).astype(BF16), wiq=col(3).astype(BF16),
        wsm=wsm.astype(BF16), wglu=col(6).astype(BF16), wga=col(7).astype(BF16), wgc=col(8).astype(BF16),
        qg=jnp.tile(q_norm_g.astype(F32), N_HEADS).reshape(1, ATTN_W),
        kg=jnp.tile(k_norm_g.astype(F32), N_HEADS).reshape(1, ATTN_W),
        ikg=jnp.concatenate([idx_k_norm_g.astype(F32), jnp.zeros((LANES - IDX_DIM,), F32)]).reshape(1, LANES),
        seg=jnp.asarray(seg, BF16), segt=jnp.asarray(seg.T, BF16),
        conv_w=conv_dw_w.reshape(CONV_K, conv_ch).astype(F32), conv_b=conv_dw_b.reshape(1, conv_ch).astype(F32),
        ln_g=conv_ln_g.reshape(1, conv_ch).astype(F32), ln_b=conv_ln_b.reshape(1, conv_ch).astype(F32),
        wao=w_attn_out.astype(BF16), wco=w_conv_out.astype(BF16), wo=w_out.astype(BF16),
        g_ffn=norm_ffn_g.reshape(1, d).astype(F32),
        wg=w_ffn_gate.astype(BF16), wu=w_ffn_up.astype(BF16), wd=w_ffn_down.astype(BF16),
    )


def _layer(x, pos0, past_k, past_v, past_ik, past_conv, rel_bias, w):
    b, t, d = x.shape
    m = b * t
    prompt = past_k is None
    q, k, k_b, v, v_b, iq, sm, sm_b, u, sga, sgc = _in_proj(x, w, transposed=prompt)
    conv_ch = u.shape[1]
    u = u.reshape(b, t, conv_ch)
    k_b = k_b.reshape(b, t, ATTN_W)
    ik_b = sm_b.reshape(b, t, LANES)[:, :, :IDX_DIM]
    if prompt:
        attn = _sparse_attn(q, iq, sm[:, IDX_DIM:IDX_DIM + N_IDX_HEADS, :], k_b, v_b, ik_b, rel_bias,
                            pos0).swapaxes(1, 2)
        heads = lambda a: a.reshape(b, N_HEADS, HEAD_DIM, t).transpose(0, 3, 1, 2)
        new_k, new_v, new_ik = heads(k), heads(v), sm[:, :IDX_DIM, :].swapaxes(1, 2)
    else:
        sm = sm.reshape(b, t, LANES)
        k, v = k.reshape(b, t, ATTN_W), v.reshape(b, t, ATTN_W)
        attn = _dec_attn(q.reshape(b, t, ATTN_W), iq.reshape(b, t, ATTN_W), sm[:, :, IDX_DIM:IDX_DIM + N_IDX_HEADS],
                         past_k, past_v, past_ik, k, v, ik_b, rel_bias, pos0)
        heads = lambda a: a.reshape(b, t, N_HEADS, HEAD_DIM)
        new_k, new_v, new_ik = heads(k), heads(v), sm[:, :, :IDX_DIM]

    past = jnp.pad(past_conv.astype(F32), ((0, 0), (HALO - (CONV_K - 1), 0), (0, 0)))
    if prompt:
        y = _mix_ffn_conv(x.reshape(m, d), attn.reshape(m, ATTN_W), u.reshape(m, conv_ch), past, sga, sgc, w, t)
    else:
        cact = _conv_mod(u, past, w)
        y = _mix_ffn(x.reshape(m, d), attn.reshape(m, ATTN_W), cact.reshape(m, conv_ch), sga, sgc, w)

    new_conv = jnp.concatenate([past_conv, u], axis=1)[:, -(CONV_K - 1):]
    return y.reshape(b, t, d), new_k, new_v, new_ik, new_conv


def kernel(x_prompt, x_sample, cache_k, cache_v, cache_idx_k, state_conv, rel_bias, norm_mix_g, w_in, q_norm_g, k_norm_g, idx_k_norm_g, conv_dw_w, conv_dw_b, conv_ln_g, conv_ln_b, w_conv_out, w_attn_out, w_out, norm_ffn_g, w_ffn_gate, w_ffn_up, w_ffn_down):
    depth = w_in.shape[0]
    bp = x_prompt.shape[0]
    past_len = cache_k.shape[2]
    conv_ch = conv_dw_w.shape[-1]
    yp, ys = x_prompt, x_sample
    outs_p, outs_s = [], []
    for l in range(depth):
        w = _prep_weights(norm_mix_g[l], w_in[l], q_norm_g[l], k_norm_g[l], idx_k_norm_g[l], conv_dw_w[l],
                          conv_dw_b[l], conv_ln_g[l], conv_ln_b[l], w_conv_out[l], w_attn_out[l], w_out[l],
                          norm_ffn_g[l], w_ffn_gate[l], w_ffn_up[l], w_ffn_down[l])
        zero_conv = jnp.zeros((bp, CONV_K - 1, conv_ch), yp.dtype)
        yp, *rest_p = _layer(yp, 0, None, None, None, zero_conv, rel_bias, w)
        outs_p.append(rest_p)
        ys, *rest_s = _layer(ys, past_len, cache_k[l], cache_v[l], cache_idx_k[l], state_conv[l], rel_bias, w)
        outs_s.append(rest_s)
    stack = lambda outs, i: jnp.stack([o[i] for o in outs])
    return (yp, ys, stack(outs_p, 0), stack(outs_p, 1), stack(outs_p, 2), stack(outs_p, 3),
            stack(outs_s, 0), stack(outs_s, 1), stack(outs_s, 2), stack(outs_s, 3))
```

```python
import functools
import math

import numpy as np
import jax
import jax.numpy as jnp
from jax import lax
from jax.experimental import pallas as pl
from jax.experimental.pallas import tpu as pltpu

CHUNK = 64
CHUNK_SHIFT = 6
N_HEADS = 8
HEAD_DIM = 64
ATTN_W = N_HEADS * HEAD_DIM
N_IDX_HEADS = 8
IDX_DIM = 64
TOPK_MAX = 256
CONV_K = 31
NUM_BUCKETS = 32
MAX_DISTANCE = 128
EPS = 1e-6
NEG = -1e30
INT_MIN = -(2 ** 31)
BF16_INF_BITS = 0x7F80
BF16_MIN_NORMAL_BITS = 0x0080
N_BF16_NEG = BF16_INF_BITS - BF16_MIN_NORMAL_BITS + 1
LOG2E = math.log2(math.e)

LANES = 128
SUBLANES = 8
HALO = 32
CONV_ROWS = 32
KEY_TILE = 256
SCORE_TRIP = 4
SEARCH_TRIP = 2
V_ROWS = HEAD_DIM + 16
ROW_TILE = 256
V7X_VMEM_BYTES = 64 * 1024 * 1024
VMEM_LIMIT = V7X_VMEM_BYTES - 8 * 1024 * 1024

F32 = jnp.float32
BF16 = jnp.bfloat16


def _const_spec(shape):
    return pl.BlockSpec(shape, lambda *_: (0,) * len(shape), pipeline_mode=pl.Buffered(1))


def _split_dot(x, m):
    hi = x.astype(BF16)
    lo = (x - hi.astype(F32)).astype(BF16)
    return jnp.dot(hi, m, preferred_element_type=F32) + jnp.dot(lo, m, preferred_element_type=F32)


def _in_proj_kernel(x_ref, g_ref, wq_ref, wk_ref, wv_ref, wiq_ref, wsm_ref, wglu_ref, wga_ref, wgc_ref,
                    qg_ref, kg_ref, ikg_ref, seg_ref, segt_ref,
                    q_ref, k_ref, kb_ref, v_ref, vb_ref, iq_ref, sm_ref, smb_ref, u_ref, sga_ref, sgc_ref,
                    *, transposed):
    x = x_ref[...]
    ms = jnp.mean(x * x, axis=-1, keepdims=True)
    h = (x * lax.rsqrt(ms + EPS) * g_ref[...]).astype(BF16)

    def put(ref, val):
        ref[...] = (val.T if transposed else val).astype(ref.dtype)

    glu = jnp.dot(h, wglu_ref[...], preferred_element_type=F32)
    c = glu.shape[1] // 2
    u_ref[...] = glu[:, :c] * jax.nn.sigmoid(glu[:, c:])

    def head_rms(y, gain):
        ss = _split_dot(y * y, seg_ref[...])
        r = lax.rsqrt(ss * (1.0 / HEAD_DIM) + EPS)
        return y * _split_dot(r, segt_ref[...]) * gain

    q = head_rms(jnp.dot(h, wq_ref[...], preferred_element_type=F32), qg_ref[...])
    put(q_ref, q * (HEAD_DIM ** -0.5 * LOG2E))
    k = head_rms(jnp.dot(h, wk_ref[...], preferred_element_type=F32), kg_ref[...])
    put(k_ref, k)
    kb_ref[...] = k.astype(BF16)
    v = jnp.dot(h, wv_ref[...], preferred_element_type=F32)
    put(v_ref, v)
    if transposed:
        vt = v.T.astype(BF16)
        pad_rows = lax.broadcasted_iota(jnp.int32, (V_ROWS - HEAD_DIM, vt.shape[1]), 0)
        ones_row = jnp.where(pad_rows == 0, 1.0, 0.0).astype(BF16)
        for hd in range(N_HEADS):
            vb_ref[hd * V_ROWS:hd * V_ROWS + HEAD_DIM, :] = vt[hd * HEAD_DIM:(hd + 1) * HEAD_DIM, :]
            vb_ref[hd * V_ROWS + HEAD_DIM:(hd + 1) * V_ROWS, :] = ones_row
    else:
        vb_ref[...] = v.astype(BF16)
    iq = jnp.dot(h, wiq_ref[...], preferred_element_type=F32)
    put(iq_ref, iq * (IDX_DIM ** -0.5))

    sm = jnp.dot(h, wsm_ref[...], preferred_element_type=F32)
    lane = lax.broadcasted_iota(jnp.int32, sm.shape, 1)
    is_ik = lane < IDX_DIM
    ss = jnp.sum(jnp.where(is_ik, sm * sm, 0.0), axis=-1, keepdims=True)
    ikn = sm * lax.rsqrt(ss * (1.0 / IDX_DIM) + EPS) * ikg_ref[...]
    iw = sm * (N_IDX_HEADS ** -0.5)
    smo = jnp.where(is_ik, ikn, jnp.where(lane < IDX_DIM + N_IDX_HEADS, iw, 0.0))
    put(sm_ref, smo)
    smb_ref[...] = smo.astype(BF16)

    sga_ref[...] = jax.nn.sigmoid(jnp.dot(h, wga_ref[...], preferred_element_type=F32)).astype(BF16)
    sgc_ref[...] = jax.nn.sigmoid(jnp.dot(h, wgc_ref[...], preferred_element_type=F32)).astype(BF16)


def _in_proj(x, w, transposed):
    b, t, d = x.shape
    m = b * t
    tm = min(ROW_TILE, t if transposed else m)
    assert m % tm == 0 and (not transposed or t % tm == 0)
    nt = t // tm
    conv_ch = w["wglu"].shape[1] // 2
    consts = [w[n] for n in ("g_mix", "wq", "wk", "wv", "wiq", "wsm", "wglu", "wga", "wgc",
                             "qg", "kg", "ikg", "seg", "segt")]

    def rows(n, dtype):
        return jax.ShapeDtypeStruct((m, n), dtype), pl.BlockSpec((tm, n), lambda i: (i, 0))

    def cols(n, dtype):
        if not transposed:
            return rows(n, dtype)
        return jax.ShapeDtypeStruct((b, n, t), dtype), pl.BlockSpec((None, n, tm), lambda i: (i // nt, 0, i % nt))

    if transposed:
        v_slab = (jax.ShapeDtypeStruct((b, nt, N_HEADS * V_ROWS, tm), BF16),
                  pl.BlockSpec((None, None, N_HEADS * V_ROWS, tm), lambda i: (i // nt, i % nt, 0, 0)))
    else:
        v_slab = rows(ATTN_W, BF16)
    outs = [
        cols(ATTN_W, BF16),
        cols(ATTN_W, F32),
        rows(ATTN_W, BF16),
        cols(ATTN_W, F32),
        v_slab,
        cols(ATTN_W, BF16),
        cols(LANES, F32),
        rows(LANES, BF16),
        rows(conv_ch, F32),
        rows(d, BF16),
        rows(d, BF16),
    ]
    return pl.pallas_call(
        functools.partial(_in_proj_kernel, transposed=transposed),
        grid=(m // tm,),
        in_specs=[pl.BlockSpec((tm, d), lambda i: (i, 0))] + [_const_spec(c.shape) for c in consts],
        out_specs=[o[1] for o in outs],
        out_shape=[o[0] for o in outs],
        compiler_params=pltpu.CompilerParams(dimension_semantics=("arbitrary",), vmem_limit_bytes=VMEM_LIMIT),
        name="in_proj",
    )(x.reshape(m, d), *consts)


def _fold(x, op):
    return op(x.reshape(x.shape[0] // SUBLANES, SUBLANES, x.shape[1]), axis=0)


def _select_top_k(key_scr, part_scr, n_kt, *, tk, tq, top_k):
    n_trips = (n_kt + SEARCH_TRIP - 1) // SEARCH_TRIP
    for j in range(SEARCH_TRIP - 1):
        @pl.when(n_kt + j < n_trips * SEARCH_TRIP)
        def _():
            key_scr[n_kt + j] = jnp.full((tk, tq), INT_MIN, jnp.int32)
            part_scr[n_kt + j] = jnp.full((tk, tq), -jnp.inf, BF16)

    trip = lambda i: [SEARCH_TRIP * i + j for j in range(SEARCH_TRIP)]

    def count(pred):
        def body(i, acc):
            for kt in trip(i):
                acc = acc + _fold(jnp.where(pred(key_scr[kt], kt), 1.0, 0.0), jnp.sum)
            return acc
        acc = lax.fori_loop(0, n_trips, body, jnp.zeros((SUBLANES, tq), F32))
        return jnp.sum(acc, axis=0, keepdims=True)

    rows16 = 2 * SUBLANES
    one_b, zero_b = jnp.ones((), BF16), jnp.zeros((), BF16)

    def count_part(pred):
        def body(i, acc):
            for kt in trip(i):
                c = jnp.where(pred(part_scr[kt]), one_b, zero_b)
                parts = [c[j * rows16:(j + 1) * rows16] for j in range(4)]
                for j in range(4, tk // rows16):
                    parts[j % 4] = parts[j % 4] + c[j * rows16:(j + 1) * rows16]
                acc = acc + ((parts[0] + parts[1]) + (parts[2] + parts[3])).astype(F32)
            return acc
        acc = lax.fori_loop(0, n_trips, body, jnp.zeros((rows16, tq), F32))
        return jnp.sum(acc, axis=0, keepdims=True)

    def search(n_bits, base, to_image, limit):
        def body(i, t):
            cand = t + (jnp.int32(1) << (n_bits - 1 - i))
            img = to_image(cand)
            c = base + count_part(lambda a: a >= img)
            return jnp.where((c >= top_k) & (cand <= limit), cand, t)
        return lax.fori_loop(0, n_bits, body, jnp.zeros((1, tq), jnp.int32))

    def refine(eq_img, shift):
        def body(i, carry):
            for kt in trip(i):
                nxt = ((key_scr[kt] >> shift) & 0xFF).astype(F32).astype(BF16)
                part_scr[kt] = jnp.where(part_scr[kt] == eq_img, nxt, -one_b)
            return carry
        lax.fori_loop(0, n_trips, body, 0)

    def top_bits(c):
        neg = 0x8000 | (BF16_INF_BITS - c)
        pos = BF16_MIN_NORMAL_BITS + (c - N_BF16_NEG - 1)
        return jnp.where(c < N_BF16_NEG, neg, jnp.where(c == N_BF16_NEG, 0, pos))

    def top_image(c):
        return pltpu.bitcast(top_bits(c) << 16, F32).astype(BF16)

    small_image = lambda c: c.astype(F32).astype(BF16)

    t1 = search(16, 0.0, top_image, 2 * N_BF16_NEG)
    img1 = top_image(t1)
    above = count_part(lambda a: a > img1)
    refine(img1, 8)
    t2 = search(8, above, small_image, 255)
    img2 = small_image(t2)
    above = above + count_part(lambda a: a > img2)
    refine(img2, 0)
    t3 = search(8, above, small_image, 255)
    hi = top_bits(t1)
    hi = hi - ((hi >> 15) << 16)
    hi = hi ^ ((hi >> 15) & 0x7FFF)
    thr = (hi << 16) + (t2 << 8) + t3
    thr = jnp.where(t1 == 0, INT_MIN + 1, jnp.maximum(thr, INT_MIN + 1))

    n_ge = count(lambda key, kt: key >= thr)
    has_tie = jnp.max(n_ge) > top_k

    @pl.when(has_tie)
    def _():
        need_m1 = (top_k - 1) - count(lambda key, kt: key > thr)
        row = lax.broadcasted_iota(jnp.int32, (tk, tq), 0)

        def idx_body(i, cut):
            cand = cut + (jnp.int32(1) << (15 - i))
            c = count(lambda key, kt: (key == thr) & (kt * tk + row < cand))
            return jnp.where(c <= need_m1, cand, cut)

        cut = lax.fori_loop(0, 16, idx_body, jnp.zeros((1, tq), jnp.int32))
        tie_col = n_ge > top_k

        def drop_body(kt, carry):
            key = key_scr[kt]
            drop = tie_col & (key == thr) & (kt * tk + row > cut)
            key_scr[kt] = jnp.where(drop, INT_MIN, key)
            return carry

        lax.fori_loop(0, n_kt, drop_body, 0)
    return thr


def _sparse_attn_kernel(qt_ref, iqt_ref, iwt_ref, k_ref, vt_ref, ik_ref, bias_ref, o_ref,
                        key_scr, part_scr, m_scr, acc_scr, s_scr, *, tq, tk, nkt, pos0, n_keys, top_k):
    qi = pl.program_id(1)
    q0 = pos0 + qi * tq
    n_kt = jnp.minimum(nkt, ((q0 + tq - 1) // CHUNK * CHUNK + CHUNK + tk - 1) // tk)
    last = n_kt - 1

    fold = _fold

    iqt = iqt_ref[...]
    iqt_h = [iqt[h * IDX_DIM:(h + 1) * IDX_DIM, :] for h in range(N_IDX_HEADS)]
    iwt = iwt_ref[...]
    iw_h = [iwt[h:h + 1, :] for h in range(N_IDX_HEADS)]

    def score_tile(kt, masked):
        ik = ik_ref[kt]
        s = None
        for h in range(N_IDX_HEADS):
            t = jnp.maximum(jnp.dot(ik, iqt_h[h], preferred_element_type=F32), 0.0) * iw_h[h]
            s = t if s is None else s + t
        s = s + 0.0
        bits = pltpu.bitcast(s, jnp.int32)
        key = bits ^ ((bits >> 31) & 0x7FFFFFFF)
        top = pltpu.bitcast(bits & -65536, F32)
        if masked:
            kp = kt * tk + lax.broadcasted_iota(jnp.int32, (tk, tq), 0)
            qp = q0 + lax.broadcasted_iota(jnp.int32, (tk, tq), 1)
            adm = ((kp >> CHUNK_SHIFT) <= (qp >> CHUNK_SHIFT)) & (kp < n_keys)
            key = jnp.where(adm, key, INT_MIN)
            top = jnp.where(adm, top, -jnp.inf)
        key_scr[kt] = key
        part_scr[kt] = top.astype(BF16)

    def score_body(i, carry):
        for j in range(SCORE_TRIP):
            score_tile(SCORE_TRIP * i + j, False)
        return carry

    lax.fori_loop(0, last // SCORE_TRIP, score_body, 0)
    for j in range(SCORE_TRIP - 1):
        @pl.when(j < last % SCORE_TRIP)
        def _():
            score_tile(last // SCORE_TRIP * SCORE_TRIP + j, False)

    score_tile(last, True)

    thr = _select_top_k(key_scr, part_scr, n_kt, tk=tk, tq=tq, top_k=top_k)

    qt = qt_ref[...]
    slab_row = lax.broadcasted_iota(jnp.int32, (LANES, tq), 0)
    qz = []
    for h in range(N_HEADS):
        slab = qt[(h // 2) * LANES:(h // 2 + 1) * LANES, :]
        mine = (slab_row < HEAD_DIM) if h % 2 == 0 else (slab_row >= HEAD_DIM)
        qz.append(jnp.where(mine, slab, jnp.zeros_like(slab)))
    m_scr[...] = jnp.full(m_scr.shape, NEG, F32)
    acc_scr[...] = jnp.zeros(acc_scr.shape, F32)

    def park(kt, slot, heads, madd):
        for h in heads:
            pair = h // 2
            s = jnp.dot(k_ref[kt, :, pair * LANES:(pair + 1) * LANES], qz[h], preferred_element_type=F32)
            s_scr[slot, h] = s + madd

    def absorb(kt, slot, bias_idx, heads):
        for h in heads:
            s = s_scr[slot, h]
            if bias_idx is not None:
                s = s + bias_ref[bias_idx, h]
            m_new = jnp.maximum(m_scr[h], jnp.max(fold(s, jnp.max), axis=0, keepdims=True))
            alpha = jnp.exp2(m_scr[h] - m_new)
            p = jnp.exp2(s - m_new)
            pv = jnp.dot(vt_ref[kt, h * V_ROWS:(h + 1) * V_ROWS, :], p.astype(BF16), preferred_element_type=F32)
            acc_scr[h] = alpha * acc_scr[h] + pv
            m_scr[h] = m_new

    all_heads = range(N_HEADS)
    head_groups = (range(0, N_HEADS // 2), range(N_HEADS // 2, N_HEADS))

    def step(kt_absorb, slot, bias_idx, kt_park):
        if kt_park is None:
            absorb(kt_absorb, slot, bias_idx, all_heads)
            return
        madd = jnp.where(key_scr[kt_park] >= thr, 0.0, NEG)
        for hs in head_groups:
            park(kt_park, 1 - slot, hs, madd)
            absorb(kt_absorb, slot, bias_idx, hs)

    park(0, 0, all_heads, jnp.where(key_scr[0] >= thr, 0.0, NEG))
    n_trips = (n_kt - 2) // 2

    def far_body(i, carry):
        kt = 2 * i
        step(kt, 0, None, kt + 1)
        step(kt + 1, 1, None, kt + 2)
        return carry

    lax.fori_loop(0, n_trips, far_body, 0)
    done = 2 * jnp.maximum(n_trips, 0)

    @pl.when(n_kt == 1)
    def _():
        step(0, 0, 1, None)

    @pl.when((n_kt >= 2) & (n_kt % 2 == 0))
    def _():
        step(done, 0, 0, done + 1)
        step(done + 1, 1, 1, None)

    @pl.when((n_kt >= 3) & (n_kt % 2 == 1))
    def _():
        step(done, 0, None, done + 1)
        step(done + 1, 1, 0, done + 2)
        step(done + 2, 0, 1, None)

    for h in range(N_HEADS):
        acc = acc_scr[h]
        o_ref[h * HEAD_DIM:(h + 1) * HEAD_DIM, :] = (acc[:HEAD_DIM] / acc[HEAD_DIM:HEAD_DIM + 1]).astype(o_ref.dtype)


def _rel_bucket(rel):
    nb = NUM_BUCKETS // 2
    max_exact = nb // 2
    ret = jnp.where(rel > 0, nb, 0)
    n = jnp.abs(rel)
    nf = jnp.maximum(n, 1).astype(jnp.float32)
    large = max_exact + (jnp.log(nf / max_exact) / math.log(MAX_DISTANCE / max_exact) * (nb - max_exact)).astype(jnp.int32)
    large = jnp.minimum(large, nb - 1)
    return ret + jnp.where(n < max_exact, n, large)


def _bias_tiles(rel_bias, tq, tk):
    j = jnp.arange(tk, dtype=jnp.int32)[:, None]
    i = jnp.arange(tq, dtype=jnp.int32)[None, :]
    return jnp.stack([_shifted_bias(rel_bias, j - i + d) for d in (-tk, 0)])


def _shifted_bias(rel_bias, rel):
    rb = (rel_bias.astype(F32) - rel_bias[_rel_bucket(jnp.int32(-MAX_DISTANCE))].astype(F32)) * LOG2E
    onehot = jax.nn.one_hot(_rel_bucket(rel), NUM_BUCKETS, dtype=F32)
    return jnp.moveaxis(jnp.einsum("...b,bh->...h", onehot, rb, precision=lax.Precision.HIGHEST), -1, 0)


def _check_tiling(t, tq, tk, nkt, pos0, n_keys):
    for qi in range(t // tq):
        q0 = pos0 + qi * tq
        n_kt = min(nkt, ((q0 + tq - 1) // CHUNK * CHUNK + CHUNK + tk - 1) // tk)
        adm_end = min(q0 // CHUNK * CHUNK + CHUNK, n_keys)
        assert (n_kt - 1) * tk <= adm_end, "only the last visited tile may hold inadmissible keys"
        assert n_kt * tk >= min((q0 + tq - 1) // CHUNK * CHUNK + CHUNK, n_keys), "visited tiles cover every admissible key"
        assert (n_kt - 1) * tk == q0, "last tile starts with the query tile"
        assert tk >= MAX_DISTANCE, "tiles before the last two are at least MAX_DISTANCE behind"


def _sparse_attn(qt, iqt, iwt, k_b, vt_t, ik_b, rel_bias, pos0):
    b, _, t = qt.shape
    n_keys = k_b.shape[1]
    top_k = min(TOPK_MAX, n_keys // 4)
    tk = KEY_TILE
    tq = min(tk, t)
    nkt = n_keys // tk
    assert n_keys % tk == 0 and vt_t.shape == (b, nkt, N_HEADS * V_ROWS, tk)
    _check_tiling(t, tq, tk, nkt, pos0, n_keys)
    k_t = k_b.reshape(b, nkt, tk, ATTN_W)
    ik_t = ik_b.reshape(b, nkt, tk, IDX_DIM)
    bias = _bias_tiles(rel_bias, tq, tk)

    qspec = lambda n: pl.BlockSpec((None, n, tq), lambda bi, qi: (bi, 0, qi))
    kspec = lambda r, c: pl.BlockSpec((None, nkt, r, c), lambda bi, qi: (bi, 0, 0, 0))
    kern = functools.partial(_sparse_attn_kernel, tq=tq, tk=tk, nkt=nkt, pos0=pos0, n_keys=n_keys, top_k=top_k)
    return pl.pallas_call(
        kern,
        grid=(b, t // tq),
        in_specs=[qspec(ATTN_W), qspec(ATTN_W), qspec(N_IDX_HEADS),
                  kspec(tk, ATTN_W), kspec(N_HEADS * V_ROWS, tk), kspec(tk, IDX_DIM), _const_spec(bias.shape)],
        out_specs=qspec(ATTN_W),
        out_shape=jax.ShapeDtypeStruct((b, ATTN_W, t), BF16),
        scratch_shapes=[pltpu.VMEM((nkt + SEARCH_TRIP - 1, tk, tq), jnp.int32),
                        pltpu.VMEM((nkt + SEARCH_TRIP - 1, tk, tq), BF16),
                        pltpu.VMEM((N_HEADS, 1, tq), F32),
                        pltpu.VMEM((N_HEADS, V_ROWS, tq), F32),
                        pltpu.VMEM((2, N_HEADS, tk, tq), F32)],
        compiler_params=pltpu.CompilerParams(dimension_semantics=("arbitrary", "arbitrary"),
                                             vmem_limit_bytes=VMEM_LIMIT),
        name="sparse_attn",
    )(qt, iqt, iwt, k_t, vt_t, ik_t, bias)


DEC_ROWS = N_HEADS * 16
DEC_KEY_TILE = 1024
NT_DIMS = (((1,), (1,)), ((), ()))


def _dec_select_kernel(iq_ref, iwc_ref, ikt_ref, iktn_ref, madd_ref, key_scr, *, g, tq, n_keys, n_pad, top_k):
    n_slabs = n_pad // LANES
    col = lax.broadcasted_iota(jnp.int32, (tq, n_pad), 1)
    for s in range(g):
        iq_s = iq_ref[s * tq:(s + 1) * tq, :]
        lhs = jnp.concatenate([iq_s[:, h * IDX_DIM:(h + 1) * IDX_DIM] for h in range(N_IDX_HEADS)], axis=0)
        logit = jnp.concatenate([jnp.dot(lhs, ikt_ref[s].astype(BF16), preferred_element_type=F32),
                                 jnp.dot(lhs, iktn_ref[s], preferred_element_type=F32)], axis=1)
        x = jnp.maximum(logit, 0.0) * iwc_ref[s]
        score = x[0:tq]
        for h in range(1, N_IDX_HEADS):
            score = score + x[h * tq:(h + 1) * tq]
        score = score + 0.0
        bits = pltpu.bitcast(score, jnp.int32)
        key = bits ^ ((bits >> 31) & 0x7FFFFFFF)
        key_scr[s * tq:(s + 1) * tq, :] = jnp.where(col < n_keys, key, INT_MIN)

    rows = g * tq
    lane = lax.broadcasted_iota(jnp.int32, (rows, LANES), 1)

    def count(pred):
        acc = jnp.zeros((rows, LANES), F32)
        for j in range(n_slabs):
            acc = acc + jnp.where(pred(key_scr[:, j * LANES:(j + 1) * LANES], j), 1.0, 0.0)
        return jnp.sum(acc, axis=1, keepdims=True)

    def thr_body(i, t):
        cand = t + (jnp.int32(1) << (31 - i))
        c = count(lambda key, j: key >= cand)
        return jnp.where(c >= top_k, cand, t)

    thr = lax.fori_loop(0, 32, thr_body, jnp.full((rows, 1), INT_MIN, jnp.int32))
    thr = jnp.maximum(thr, INT_MIN + 1)

    n_ge = count(lambda key, j: key >= thr)
    has_tie = jnp.max(n_ge) > top_k

    @pl.when(has_tie)
    def _():
        need_m1 = (top_k - 1) - count(lambda key, j: key > thr)

        def idx_body(i, cut):
            cand = cut + (jnp.int32(1) << (15 - i))
            c = count(lambda key, j: (key == thr) & (j * LANES + lane < cand))
            return jnp.where(c <= need_m1, cand, cut)

        cut = lax.fori_loop(0, 16, idx_body, jnp.zeros((rows, 1), jnp.int32))
        tie_row = n_ge > top_k
        for j in range(n_slabs):
            key = key_scr[:, j * LANES:(j + 1) * LANES]
            drop = tie_row & (key == thr) & (j * LANES + lane > cut)
            key_scr[:, j * LANES:(j + 1) * LANES] = jnp.where(drop, INT_MIN, key)

    madd_ref[...] = jnp.where(key_scr[...] >= thr, 0.0, NEG)


def _dec_attn_kernel(q_ref, maddc_ref, maddn_ref, k_ref, v_ref, kn_ref, vn_ref, biasc_ref, biasn_ref, o_ref,
                     m_scr, l_scr, acc_scr, *, tq):
    kt = pl.program_id(1)
    q = q_ref[...]
    q_h = [q[:, h * HEAD_DIM:(h + 1) * HEAD_DIM] for h in range(N_HEADS)]

    @pl.when(kt == 0)
    def _():
        m_scr[...] = jnp.full(m_scr.shape, NEG, F32)
        l_scr[...] = jnp.zeros(l_scr.shape, F32)
        acc_scr[...] = jnp.zeros(acc_scr.shape, F32)

    def step(ktr, vtr, madd, bias):
        s = jnp.concatenate([jnp.dot(q_h[h], ktr[h].astype(BF16), preferred_element_type=F32)
                             for h in range(N_HEADS)], axis=0)
        s = s + jnp.concatenate([madd] * N_HEADS, axis=0) + bias
        m_prev = m_scr[...]
        m_new = jnp.maximum(m_prev, jnp.max(s, axis=1, keepdims=True))
        alpha = jnp.exp2(m_prev - m_new)
        p = jnp.exp2(s - m_new)
        l_scr[...] = alpha * l_scr[...] + jnp.sum(p, axis=1, keepdims=True)
        pb = p.astype(BF16)
        pv = jnp.concatenate([lax.dot_general(pb[h * tq:(h + 1) * tq, :], vtr[h].astype(BF16), NT_DIMS,
                                              preferred_element_type=F32) for h in range(N_HEADS)], axis=0)
        acc_scr[...] = alpha * acc_scr[...] + pv
        m_scr[...] = m_new

    step(k_ref, v_ref, maddc_ref[...], biasc_ref[...])

    @pl.when(kt == pl.num_programs(1) - 1)
    def _():
        step(kn_ref, vn_ref, maddn_ref[...], biasn_ref[...])
        o = acc_scr[...] / l_scr[...]
        for h in range(N_HEADS):
            o_ref[:, h * HEAD_DIM:(h + 1) * HEAD_DIM] = o[h * tq:(h + 1) * tq, :].astype(o_ref.dtype)


def _dec_attn(q, iq, iw, past_k, past_v, past_ik, k_new, v_new, ik_new, rel_bias, pos0):
    b, t, _ = q.shape
    p_len = past_k.shape[1]
    n_keys = p_len + t
    top_k = min(TOPK_MAX, n_keys // 4)
    tkk = min(DEC_KEY_TILE, p_len)
    n_pad = p_len + LANES
    g = min(LANES // t, b)
    assert N_HEADS * t == DEC_ROWS and b % g == 0 and p_len % tkk == 0 and t <= LANES
    assert (pos0 + t - 1) // CHUNK == pos0 // CHUNK and (n_keys - 1) // CHUNK <= pos0 // CHUNK, "every key admissible"
    assert tkk >= MAX_DISTANCE + LANES >= MAX_DISTANCE + t, "only the last cache tile is within MAX_DISTANCE of a query"

    ikt_new = jnp.pad(ik_new.swapaxes(1, 2), ((0, 0), (0, 0), (0, LANES - t)))
    iw_col = iw.swapaxes(1, 2).reshape(b, DEC_ROWS, 1)
    madd = pl.pallas_call(
        functools.partial(_dec_select_kernel, g=g, tq=t, n_keys=n_keys, n_pad=n_pad, top_k=top_k),
        grid=(b // g,),
        in_specs=[pl.BlockSpec((g * t, ATTN_W), lambda i: (i, 0)),
                  pl.BlockSpec((g, DEC_ROWS, 1), lambda i: (i, 0, 0)),
                  pl.BlockSpec((g, IDX_DIM, p_len), lambda i: (i, 0, 0)),
                  pl.BlockSpec((g, IDX_DIM, LANES), lambda i: (i, 0, 0))],
        out_specs=pl.BlockSpec((g * t, n_pad), lambda i: (i, 0)),
        out_shape=jax.ShapeDtypeStruct((b * t, n_pad), F32),
        scratch_shapes=[pltpu.VMEM((g * t, n_pad), jnp.int32)],
        compiler_params=pltpu.CompilerParams(dimension_semantics=("arbitrary",), vmem_limit_bytes=VMEM_LIMIT),
        name="dec_select",
    )(iq.reshape(b * t, ATTN_W), iw_col, past_ik.swapaxes(1, 2), ikt_new)

    n_steps = p_len // tkk
    qpos = jnp.tile(jnp.arange(t, dtype=jnp.int32), N_HEADS)[:, None]
    hsel = jnp.repeat(jnp.arange(N_HEADS), t)
    rows = jnp.arange(DEC_ROWS)
    near = MAX_DISTANCE + LANES
    rel_c = jnp.arange(near, dtype=jnp.int32)[None, :] - near - qpos
    rel_n = jnp.arange(LANES, dtype=jnp.int32)[None, :] - qpos
    bias_last = jnp.pad(_shifted_bias(rel_bias, rel_c)[hsel, rows], ((0, 0), (tkk - near, 0)))
    bias_c = jnp.concatenate([jnp.zeros((n_steps - 1, DEC_ROWS, tkk), F32), bias_last[None]], axis=0)
    bias_n = _shifted_bias(rel_bias, rel_n)[hsel, rows]
    heads_t = lambda a: a.transpose(0, 2, 3, 1)
    pad_new = lambda a: jnp.pad(heads_t(a.reshape(b, t, N_HEADS, HEAD_DIM)), ((0, 0), (0, 0), (0, 0), (0, LANES - t)))
    kv_tile = pl.BlockSpec((None, N_HEADS, HEAD_DIM, tkk), lambda bi, kt: (bi, 0, 0, kt))
    kv_new = pl.BlockSpec((None, N_HEADS, HEAD_DIM, LANES), lambda bi, kt: (bi, 0, 0, 0))
    return pl.pallas_call(
        functools.partial(_dec_attn_kernel, tq=t),
        grid=(b, n_steps),
        in_specs=[pl.BlockSpec((None, t, ATTN_W), lambda bi, kt: (bi, 0, 0)),
                  pl.BlockSpec((t, tkk), lambda bi, kt: (bi, kt)),
                  pl.BlockSpec((t, LANES), lambda bi, kt: (bi, p_len // LANES)),
                  kv_tile, kv_tile, kv_new, kv_new,
                  pl.BlockSpec((None, DEC_ROWS, tkk), lambda bi, kt: (kt, 0, 0)),
                  _const_spec(bias_n.shape)],
        out_specs=pl.BlockSpec((None, t, ATTN_W), lambda bi, kt: (bi, 0, 0)),
        out_shape=jax.ShapeDtypeStruct((b, t, ATTN_W), BF16),
        scratch_shapes=[pltpu.VMEM((DEC_ROWS, 1), F32), pltpu.VMEM((DEC_ROWS, 1), F32),
                        pltpu.VMEM((DEC_ROWS, HEAD_DIM), F32)],
        compiler_params=pltpu.CompilerParams(dimension_semantics=("arbitrary", "arbitrary"),
                                             vmem_limit_bytes=VMEM_LIMIT),
        name="dec_attn",
    )(q, madd, madd, heads_t(past_k), heads_t(past_v), pad_new(k_new), pad_new(v_new), bias_c, bias_n)


def _conv_kernel(*refs, tm, has_prev):
    if has_prev:
        u_ref, prev_ref, past_ref, w_ref, b_ref, lg_ref, lb_ref, o_ref, ext_scr = refs
        halo = jnp.where(pl.program_id(1) == 0, past_ref[...], prev_ref[...])
    else:
        u_ref, past_ref, w_ref, b_ref, lg_ref, lb_ref, o_ref, ext_scr = refs
        halo = past_ref[...]
    _conv_act(halo, u_ref[...], w_ref, b_ref, lg_ref, lb_ref, o_ref, ext_scr)


def _conv_act(halo, u, w_ref, b_ref, lg_ref, lb_ref, o_ref, ext_scr):
    tm = u.shape[0]
    ext_scr[0, 0:HALO, :] = halo
    ext_scr[0, HALO:HALO + tm, :] = u
    n_rows = tm + HALO - SUBLANES
    for r in range(1, SUBLANES):
        ext_scr[r, 0:n_rows, :] = ext_scr[0, r:r + n_rows, :]
    rb = min(tm, CONV_ROWS)
    first = HALO - (CONV_K - 1)
    for r0 in range(0, tm, rb):
        acc = None
        for j in range(CONV_K):
            a, r = divmod(first + j, SUBLANES)
            t = ext_scr[r, r0 + a * SUBLANES:r0 + a * SUBLANES + rb, :] * w_ref[j:j + 1, :]
            acc = t if acc is None else acc + t
        c = acc + b_ref[...]
        mu = jnp.mean(c, axis=-1, keepdims=True)
        cc = c - mu
        var = jnp.mean(cc * cc, axis=-1, keepdims=True)
        y = cc * lax.rsqrt(var + EPS) * lg_ref[...] + lb_ref[...]
        o_ref[r0:r0 + rb, :] = (y * jax.nn.sigmoid(y)).astype(o_ref.dtype)


def _conv_mod(u, past, w):
    b, t, c = u.shape
    tm = min(ROW_TILE, t)
    assert t % tm == 0 and tm % 8 == 0
    has_prev = t > tm
    tile = pl.BlockSpec((None, tm, c), lambda bi, i: (bi, i, 0))
    halo = pl.BlockSpec((None, HALO, c), lambda bi, i: (bi, 0, 0))
    in_specs = [tile]
    args = [u]
    if has_prev:
        r = tm // HALO
        in_specs.append(pl.BlockSpec((None, HALO, c), lambda bi, i: (bi, jnp.maximum(i * r - 1, 0), 0)))
        args.append(u)
    consts = [w["conv_w"], w["conv_b"], w["ln_g"], w["ln_b"]]
    in_specs += [halo] + [_const_spec(x.shape) for x in consts]
    args += [past] + consts
    return pl.pallas_call(
        functools.partial(_conv_kernel, tm=tm, has_prev=has_prev),
        grid=(b, t // tm),
        in_specs=in_specs,
        out_specs=tile,
        out_shape=jax.ShapeDtypeStruct((b, t, c), BF16),
        scratch_shapes=[pltpu.VMEM((SUBLANES, HALO + tm, c), F32)],
        compiler_params=pltpu.CompilerParams(dimension_semantics=("arbitrary", "arbitrary"),
                                             vmem_limit_bytes=VMEM_LIMIT),
        name="conv_mod",
    )(*args)


def _mix_ffn_conv_kernel(x_ref, attn_ref, u0_ref, unext_ref, utail_ref, past0_ref, pastn_ref, sga_ref, sgc_ref,
                         wao_ref, wco_ref, wo_ref, g_ref, wg_ref, wu_ref, wd_ref, cw_ref, cb_ref, lg_ref, lb_ref,
                         y_ref, cact_scr, ext_scr, *, n_chunks, tiles_per_seq):
    i = pl.program_id(0)

    @pl.when(i == 0)
    def _():
        _conv_act(past0_ref[...], u0_ref[...], cw_ref, cb_ref, lg_ref, lb_ref, cact_scr.at[0], ext_scr)

    _mix_ffn_kernel(x_ref, attn_ref, cact_scr.at[i % 2], sga_ref, sgc_ref, wao_ref, wco_ref, wo_ref, g_ref,
                    wg_ref, wu_ref, wd_ref, y_ref, n_chunks=n_chunks)
    halo = jnp.where((i + 1) % tiles_per_seq == 0, pastn_ref[...], utail_ref[...])
    _conv_act(halo, unext_ref[...], cw_ref, cb_ref, lg_ref, lb_ref, cact_scr.at[(i + 1) % 2], ext_scr)


def _mix_ffn_kernel(x_ref, attn_ref, cact_ref, sga_ref, sgc_ref, wao_ref, wco_ref, wo_ref, g_ref,
                    wg_ref, wu_ref, wd_ref, y_ref, *, n_chunks):
    attn_out = jnp.dot(attn_ref[...], wao_ref[...], preferred_element_type=F32)
    conv_out = jnp.dot(cact_ref[...], wco_ref[...], preferred_element_type=F32)
    merged = sga_ref[...].astype(F32) * attn_out + sgc_ref[...].astype(F32) * conv_out
    x1 = x_ref[...] + jnp.dot(merged.astype(BF16), wo_ref[...], preferred_element_type=F32)
    ms = jnp.mean(x1 * x1, axis=-1, keepdims=True)
    h2 = (x1 * lax.rsqrt(ms + EPS) * g_ref[...]).astype(BF16)
    hc = wg_ref.shape[1] // n_chunks
    y = x1
    for ci in range(n_chunks):
        sl = slice(ci * hc, (ci + 1) * hc)
        gate = jnp.dot(h2, wg_ref[:, sl], preferred_element_type=F32)
        up = jnp.dot(h2, wu_ref[:, sl], preferred_element_type=F32)
        act = (gate * jax.nn.sigmoid(gate) * up).astype(BF16)
        y = y + jnp.dot(act, wd_ref[sl, :], preferred_element_type=F32)
    y_ref[...] = y


def _mix_ffn_conv(x, attn, u, past, sga, sgc, w, seq_len):
    m, d = x.shape
    c = u.shape[1]
    tm = min(ROW_TILE, seq_len)
    assert m % tm == 0 and seq_len % tm == 0 and tm % HALO == 0
    nt, n = seq_len // tm, m // tm
    hidden = w["wg"].shape[1]
    n_chunks = 2 if hidden % (2 * LANES) == 0 else 1
    row = lambda k: pl.BlockSpec((tm, k), lambda i: (i, 0))
    nxt = lambda i: jnp.minimum(i + 1, n - 1)
    consts = [w[k] for k in ("wao", "wco", "wo", "g_ffn", "wg", "wu", "wd", "conv_w", "conv_b", "ln_g", "ln_b")]
    return pl.pallas_call(
        functools.partial(_mix_ffn_conv_kernel, n_chunks=n_chunks, tiles_per_seq=nt),
        grid=(n,),
        in_specs=[row(d), row(attn.shape[1]),
                  pl.BlockSpec((tm, c), lambda i: (0, 0)),
                  pl.BlockSpec((tm, c), lambda i: (nxt(i), 0)),
                  pl.BlockSpec((HALO, c), lambda i: ((i + 1) * (tm // HALO) - 1, 0)),
                  pl.BlockSpec((None, HALO, c), lambda i: (0, 0, 0)),
                  pl.BlockSpec((None, HALO, c), lambda i: (nxt(i) // nt, 0, 0)),
                  row(d), row(d)] + [_const_spec(k.shape) for k in consts],
        out_specs=row(d),
        out_shape=jax.ShapeDtypeStruct((m, d), F32),
        scratch_shapes=[pltpu.VMEM((2, tm, c), BF16), pltpu.VMEM((SUBLANES, HALO + tm, c), F32)],
        compiler_params=pltpu.CompilerParams(dimension_semantics=("arbitrary",), vmem_limit_bytes=VMEM_LIMIT),
        name="mix_ffn_conv",
    )(x, attn, u, u, u, past, past, sga, sgc, *consts)


def _mix_ffn(x, attn, cact, sga, sgc, w):
    m, d = x.shape
    tm = min(ROW_TILE, m)
    assert m % tm == 0
    hidden = w["wg"].shape[1]
    n_chunks = 2 if hidden % (2 * LANES) == 0 else 1
    row = lambda n: pl.BlockSpec((tm, n), lambda i: (i, 0))
    consts = [w[n] for n in ("wao", "wco", "wo", "g_ffn", "wg", "wu", "wd")]
    return pl.pallas_call(
        functools.partial(_mix_ffn_kernel, n_chunks=n_chunks),
        grid=(m // tm,),
        in_specs=[row(d), row(attn.shape[1]), row(cact.shape[1]), row(d), row(d)]
                 + [_const_spec(c.shape) for c in consts],
        out_specs=row(d),
        out_shape=jax.ShapeDtypeStruct((m, d), F32),
        compiler_params=pltpu.CompilerParams(dimension_semantics=("arbitrary",), vmem_limit_bytes=VMEM_LIMIT),
        name="mix_ffn",
    )(x, attn, cact, sga, sgc, *consts)


def _prep_weights(norm_mix_g, w_in, q_norm_g, k_norm_g, idx_k_norm_g, conv_dw_w, conv_dw_b, conv_ln_g, conv_ln_b,
                  w_conv_out, w_attn_out, w_out, norm_ffn_g, w_ffn_gate, w_ffn_up, w_ffn_down):
    d = w_in.shape[0]
    conv_ch = conv_dw_w.shape[-1]
    sizes = (ATTN_W, ATTN_W, ATTN_W, N_IDX_HEADS * IDX_DIM, IDX_DIM, N_IDX_HEADS, 2 * conv_ch, d, d)
    offs = np.concatenate([[0], np.cumsum(sizes)])
    assert offs[-1] == w_in.shape[1]
    col = lambda i: w_in[:, offs[i]:offs[i + 1]]
    wsm = jnp.concatenate([col(4), col(5), jnp.zeros((d, LANES - IDX_DIM - N_IDX_HEADS), w_in.dtype)], axis=1)
    head = np.arange(ATTN_W) // HEAD_DIM
    seg = (head[:, None] == np.arange(LANES)[None, :]).astype(np.float32)
    return dict(
        g_mix=norm_mix_g.reshape(1, d).astype(F32),
        wq=col(0).astype(BF16), wk=col(1).astype(BF16), wv=col(2).astype(BF16), wiq=col(3).astype(BF16),
        wsm=wsm.astype(BF16), wglu=col(6).astype(BF16), wga=col(7).astype(BF16), wgc=col(8).astype(BF16),
        qg=jnp.tile(q_norm_g.astype(F32), N_HEADS).reshape(1, ATTN_W),
        kg=jnp.tile(k_norm_g.astype(F32), N_HEADS).reshape(1, ATTN_W),
        ikg=jnp.concatenate([idx_k_norm_g.astype(F32), jnp.zeros((LANES - IDX_DIM,), F32)]).reshape(1, LANES),
        seg=jnp.asarray(seg, BF16), segt=jnp.asarray(seg.T, BF16),
        conv_w=conv_dw_w.reshape(CONV_K, conv_ch).astype(F32), conv_b=conv_dw_b.reshape(1, conv_ch).astype(F32),
        ln_g=conv_ln_g.reshape(1, conv_ch).astype(F32), ln_b=conv_ln_b.reshape(1, conv_ch).astype(F32),
        wao=w_attn_out.astype(BF16), wco=w_conv_out.astype(BF16), wo=w_out.astype(BF16),
        g_ffn=norm_ffn_g.reshape(1, d).astype(F32),
        wg=w_ffn_gate.astype(BF16), wu=w_ffn_up.astype(BF16), wd=w_ffn_down.astype(BF16),
    )


def _layer(x, pos0, past_k, past_v, past_ik, past_conv, rel_bias, w):
    b, t, d = x.shape
    m = b * t
    prompt = past_k is None
    q, k, k_b, v, v_b, iq, sm, sm_b, u, sga, sgc = _in_proj(x, w, transposed=prompt)
    conv_ch = u.shape[1]
    u = u.reshape(b, t, conv_ch)
    k_b = k_b.reshape(b, t, ATTN_W)
    ik_b = sm_b.reshape(b, t, LANES)[:, :, :IDX_DIM]
    if prompt:
        attn = _sparse_attn(q, iq, sm[:, IDX_DIM:IDX_DIM + N_IDX_HEADS, :], k_b, v_b, ik_b, rel_bias,
                            pos0).swapaxes(1, 2)
        heads = lambda a: a.reshape(b, N_HEADS, HEAD_DIM, t).transpose(0, 3, 1, 2)
        new_k, new_v, new_ik = heads(k), heads(v), sm[:, :IDX_DIM, :].swapaxes(1, 2)
    else:
        sm = sm.reshape(b, t, LANES)
        k, v = k.reshape(b, t, ATTN_W), v.reshape(b, t, ATTN_W)
        attn = _dec_attn(q.reshape(b, t, ATTN_W), iq.reshape(b, t, ATTN_W), sm[:, :, IDX_DIM:IDX_DIM + N_IDX_HEADS],
                         past_k, past_v, past_ik, k, v, ik_b, rel_bias, pos0)
        heads = lambda a: a.reshape(b, t, N_HEADS, HEAD_DIM)
        new_k, new_v, new_ik = heads(k), heads(v), sm[:, :, :IDX_DIM]

    past = jnp.pad(past_conv.astype(F32), ((0, 0), (HALO - (CONV_K - 1), 0), (0, 0)))
    if prompt:
        y = _mix_ffn_conv(x.reshape(m, d), attn.reshape(m, ATTN_W), u.reshape(m, conv_ch), past, sga, sgc, w, t)
    else:
        cact = _conv_mod(u, past, w)
        y = _mix_ffn(x.reshape(m, d), attn.reshape(m, ATTN_W), cact.reshape(m, conv_ch), sga, sgc, w)

    new_conv = jnp.concatenate([past_conv, u], axis=1)[:, -(CONV_K - 1):]
    return y.reshape(b, t, d), new_k, new_v, new_ik, new_conv


def kernel(x_prompt, x_sample, cache_k, cache_v, cache_idx_k, state_conv, rel_bias, norm_mix_g, w_in, q_norm_g, k_norm_g, idx_k_norm_g, conv_dw_w, conv_dw_b, conv_ln_g, conv_ln_b, w_conv_out, w_attn_out, w_out, norm_ffn_g, w_ffn_gate, w_ffn_up, w_ffn_down):
    depth = w_in.shape[0]
    bp = x_prompt.shape[0]
    past_len = cache_k.shape[2]
    conv_ch = conv_dw_w.shape[-1]
    yp, ys = x_prompt, x_sample
    outs_p, outs_s = [], []
    for l in range(depth):
        w = _prep_weights(norm_mix_g[l], w_in[l], q_norm_g[l], k_norm_g[l], idx_k_norm_g[l], conv_dw_w[l],
                          conv_dw_b[l], conv_ln_g[l], conv_ln_b[l], w_conv_out[l], w_attn_out[l], w_out[l],
                          norm_ffn_g[l], w_ffn_gate[l], w_ffn_up[l], w_ffn_down[l])
        zero_conv = jnp.zeros((bp, CONV_K - 1, conv_ch), yp.dtype)
        yp, *rest_p = _layer(yp, 0, None, None, None, zero_conv, rel_bias, w)
        outs_p.append(rest_p)
        ys, *rest_s = _layer(ys, past_len, cache_k[l], cache_v[l], cache_idx_k[l], state_conv[l], rel_bias, w)
        outs_s.append(rest_s)
    stack = lambda outs, i: jnp.stack([o[i] for o in outs])
    return (yp, ys, stack(outs_p, 0), stack(outs_p, 1), stack(outs_p, 2), stack(outs_p, 3),
            stack(outs_s, 0), stack(outs_s, 1), stack(outs_s, 2), stack(outs_s, 3))
```

```python
import functools
import math

import numpy as np
import jax
import jax.numpy as jnp
from jax import lax
from jax.experimental import pallas as pl
from jax.experimental.pallas import tpu as pltpu

CHUNK = 64
CHUNK_SHIFT = 6
N_HEADS = 8
HEAD_DIM = 64
ATTN_W = N_HEADS * HEAD_DIM
N_IDX_HEADS = 8
IDX_DIM = 64
TOPK_MAX = 256
CONV_K = 31
NUM_BUCKETS = 32
MAX_DISTANCE = 128
EPS = 1e-6
NEG = -1e30
INT_MIN = -(2 ** 31)
BF16_INF_BITS = 0x7F80
BF16_MIN_NORMAL_BITS = 0x0080
N_BF16_NEG = BF16_INF_BITS - BF16_MIN_NORMAL_BITS + 1
LOG2E = math.log2(math.e)

LANES = 128
SUBLANES = 8
HALO = 32
CONV_ROWS = 32
KEY_TILE = 256
SCORE_TRIP = 4
SEARCH_TRIP = 2
V_ROWS = HEAD_DIM + 16
ROW_TILE = 256
V7X_VMEM_BYTES = 64 * 1024 * 1024
VMEM_LIMIT = V7X_VMEM_BYTES - 8 * 1024 * 1024

F32 = jnp.float32
BF16 = jnp.bfloat16


def _const_spec(shape):
    return pl.BlockSpec(shape, lambda *_: (0,) * len(shape), pipeline_mode=pl.Buffered(1))


def _split_dot(x, m):
    hi = x.astype(BF16)
    lo = (x - hi.astype(F32)).astype(BF16)
    return jnp.dot(hi, m, preferred_element_type=F32) + jnp.dot(lo, m, preferred_element_type=F32)


def _in_proj_kernel(x_ref, g_ref, wq_ref, wk_ref, wv_ref, wiq_ref, wsm_ref, wglu_ref, wga_ref, wgc_ref,
                    qg_ref, kg_ref, ikg_ref, seg_ref, segt_ref,
                    q_ref, k_ref, kb_ref, v_ref, vb_ref, iq_ref, sm_ref, smb_ref, u_ref, sga_ref, sgc_ref,
                    *, transposed):
    x = x_ref[...]
    ms = jnp.mean(x * x, axis=-1, keepdims=True)
    h = (x * lax.rsqrt(ms + EPS) * g_ref[...]).astype(BF16)

    def put(ref, val):
        ref[...] = (val.T if transposed else val).astype(ref.dtype)

    glu = jnp.dot(h, wglu_ref[...], preferred_element_type=F32)
    c = glu.shape[1] // 2
    u_ref[...] = glu[:, :c] * jax.nn.sigmoid(glu[:, c:])

    def head_rms(y, gain):
        ss = _split_dot(y * y, seg_ref[...])
        r = lax.rsqrt(ss * (1.0 / HEAD_DIM) + EPS)
        return y * _split_dot(r, segt_ref[...]) * gain

    q = head_rms(jnp.dot(h, wq_ref[...], preferred_element_type=F32), qg_ref[...])
    put(q_ref, q * (HEAD_DIM ** -0.5 * LOG2E))
    k = head_rms(jnp.dot(h, wk_ref[...], preferred_element_type=F32), kg_ref[...])
    put(k_ref, k)
    kb_ref[...] = k.astype(BF16)
    v = jnp.dot(h, wv_ref[...], preferred_element_type=F32)
    put(v_ref, v)
    if transposed:
        vt = v.T.astype(BF16)
        pad_rows = lax.broadcasted_iota(jnp.int32, (V_ROWS - HEAD_DIM, vt.shape[1]), 0)
        ones_row = jnp.where(pad_rows == 0, 1.0, 0.0).astype(BF16)
        for hd in range(N_HEADS):
            vb_ref[hd * V_ROWS:hd * V_ROWS + HEAD_DIM, :] = vt[hd * HEAD_DIM:(hd + 1) * HEAD_DIM, :]
            vb_ref[hd * V_ROWS + HEAD_DIM:(hd + 1) * V_ROWS, :] = ones_row
    else:
        vb_ref[...] = v.astype(BF16)
    iq = jnp.dot(h, wiq_ref[...], preferred_element_type=F32)
    put(iq_ref, iq * (IDX_DIM ** -0.5))

    sm = jnp.dot(h, wsm_ref[...], preferred_element_type=F32)
    lane = lax.broadcasted_iota(jnp.int32, sm.shape, 1)
    is_ik = lane < IDX_DIM
    ss = jnp.sum(jnp.where(is_ik, sm * sm, 0.0), axis=-1, keepdims=True)
    ikn = sm * lax.rsqrt(ss * (1.0 / IDX_DIM) + EPS) * ikg_ref[...]
    iw = sm * (N_IDX_HEADS ** -0.5)
    smo = jnp.where(is_ik, ikn, jnp.where(lane < IDX_DIM + N_IDX_HEADS, iw, 0.0))
    put(sm_ref, smo)
    smb_ref[...] = smo.astype(BF16)

    sga_ref[...] = jax.nn.sigmoid(jnp.dot(h, wga_ref[...], preferred_element_type=F32)).astype(BF16)
    sgc_ref[...] = jax.nn.sigmoid(jnp.dot(h, wgc_ref[...], preferred_element_type=F32)).astype(BF16)


def _in_proj(x, w, transposed):
    b, t, d = x.shape
    m = b * t
    tm = min(ROW_TILE, t if transposed else m)
    assert m % tm == 0 and (not transposed or t % tm == 0)
    nt = t // tm
    conv_ch = w["wglu"].shape[1] // 2
    consts = [w[n] for n in ("g_mix", "wq", "wk", "wv", "wiq", "wsm", "wglu", "wga", "wgc",
                             "qg", "kg", "ikg", "seg", "segt")]

    def rows(n, dtype):
        return jax.ShapeDtypeStruct((m, n), dtype), pl.BlockSpec((tm, n), lambda i: (i, 0))

    def cols(n, dtype):
        if not transposed:
            return rows(n, dtype)
        return jax.ShapeDtypeStruct((b, n, t), dtype), pl.BlockSpec((None, n, tm), lambda i: (i // nt, 0, i % nt))

    if transposed:
        v_slab = (jax.ShapeDtypeStruct((b, nt, N_HEADS * V_ROWS, tm), BF16),
                  pl.BlockSpec((None, None, N_HEADS * V_ROWS, tm), lambda i: (i // nt, i % nt, 0, 0)))
    else:
        v_slab = rows(ATTN_W, BF16)
    outs = [
        cols(ATTN_W, BF16),
        cols(ATTN_W, F32),
        rows(ATTN_W, BF16),
        cols(ATTN_W, F32),
        v_slab,
        cols(ATTN_W, BF16),
        cols(LANES, F32),
        rows(LANES, BF16),
        rows(conv_ch, F32),
        rows(d, BF16),
        rows(d, BF16),
    ]
    return pl.pallas_call(
        functools.partial(_in_proj_kernel, transposed=transposed),
        grid=(m // tm,),
        in_specs=[pl.BlockSpec((tm, d), lambda i: (i, 0))] + [_const_spec(c.shape) for c in consts],
        out_specs=[o[1] for o in outs],
        out_shape=[o[0] for o in outs],
        compiler_params=pltpu.CompilerParams(dimension_semantics=("arbitrary",), vmem_limit_bytes=VMEM_LIMIT),
        name="in_proj",
    )(x.reshape(m, d), *consts)


def _fold(x, op):
    return op(x.reshape(x.shape[0] // SUBLANES, SUBLANES, x.shape[1]), axis=0)


def _select_top_k(key_scr, part_scr, n_kt, *, tk, tq, top_k):
    n_trips = (n_kt + SEARCH_TRIP - 1) // SEARCH_TRIP
    for j in range(SEARCH_TRIP - 1):
        @pl.when(n_kt + j < n_trips * SEARCH_TRIP)
        def _():
            key_scr[n_kt + j] = jnp.full((tk, tq), INT_MIN, jnp.int32)
            part_scr[n_kt + j] = jnp.full((tk, tq), -jnp.inf, BF16)

    trip = lambda i: [SEARCH_TRIP * i + j for j in range(SEARCH_TRIP)]

    def count(pred):
        def body(i, acc):
            for kt in trip(i):
                acc = acc + _fold(jnp.where(pred(key_scr[kt], kt), 1.0, 0.0), jnp.sum)
            return acc
        acc = lax.fori_loop(0, n_trips, body, jnp.zeros((SUBLANES, tq), F32))
        return jnp.sum(acc, axis=0, keepdims=True)

    rows16 = 2 * SUBLANES
    one_b, zero_b = jnp.ones((), BF16), jnp.zeros((), BF16)

    def count_part(pred):
        def body(i, acc):
            for kt in trip(i):
                c = jnp.where(pred(part_scr[kt]), one_b, zero_b)
                parts = [c[j * rows16:(j + 1) * rows16] for j in range(4)]
                for j in range(4, tk // rows16):
                    parts[j % 4] = parts[j % 4] + c[j * rows16:(j + 1) * rows16]
                acc = acc + ((parts[0] + parts[1]) + (parts[2] + parts[3])).astype(F32)
            return acc
        acc = lax.fori_loop(0, n_trips, body, jnp.zeros((rows16, tq), F32))
        return jnp.sum(acc, axis=0, keepdims=True)

    def search(n_bits, base, to_image, limit):
        def body(i, t):
            cand = t + (jnp.int32(1) << (n_bits - 1 - i))
            img = to_image(cand)
            c = base + count_part(lambda a: a >= img)
            return jnp.where((c >= top_k) & (cand <= limit), cand, t)
        return lax.fori_loop(0, n_bits, body, jnp.zeros((1, tq), jnp.int32))

    def refine(eq_img, shift):
        def body(i, carry):
            for kt in trip(i):
                nxt = ((key_scr[kt] >> shift) & 0xFF).astype(F32).astype(BF16)
                part_scr[kt] = jnp.where(part_scr[kt] == eq_img, nxt, -one_b)
            return carry
        lax.fori_loop(0, n_trips, body, 0)

    def top_bits(c):
        neg = 0x8000 | (BF16_INF_BITS - c)
        pos = BF16_MIN_NORMAL_BITS + (c - N_BF16_NEG - 1)
        return jnp.where(c < N_BF16_NEG, neg, jnp.where(c == N_BF16_NEG, 0, pos))

    def top_image(c):
        return pltpu.bitcast(top_bits(c) << 16, F32).astype(BF16)

    small_image = lambda c: c.astype(F32).astype(BF16)

    t1 = search(16, 0.0, top_image, 2 * N_BF16_NEG)
    img1 = top_image(t1)
    above = count_part(lambda a: a > img1)
    refine(img1, 8)
    t2 = search(8, above, small_image, 255)
    img2 = small_image(t2)
    above = above + count_part(lambda a: a > img2)
    refine(img2, 0)
    t3 = search(8, above, small_image, 255)
    hi = top_bits(t1)
    hi = hi - ((hi >> 15) << 16)
    hi = hi ^ ((hi >> 15) & 0x7FFF)
    thr = (hi << 16) + (t2 << 8) + t3
    thr = jnp.where(t1 == 0, INT_MIN + 1, jnp.maximum(thr, INT_MIN + 1))

    n_ge = count(lambda key, kt: key >= thr)
    has_tie = jnp.max(n_ge) > top_k

    @pl.when(has_tie)
    def _():
        excess = n_ge - top_k
        upper = (lax.broadcasted_iota(jnp.int32, (tk, tk), 1) >= lax.broadcasted_iota(jnp.int32, (tk, tk), 0))
        upper = jnp.where(upper, 1.0, 0.0).astype(BF16)

        def drop_body(i, later):
            kt = n_kt - 1 - i
            key = key_scr[kt]
            eq = key == thr
            ties_from_here = jnp.dot(upper, jnp.where(eq, 1.0, 0.0).astype(BF16), preferred_element_type=F32)
            key_scr[kt] = jnp.where(eq & (ties_from_here + later <= excess), INT_MIN, key)
            return later + ties_from_here[0:1, :]

        lax.fori_loop(0, n_kt, drop_body, jnp.zeros((1, tq), F32))
    return thr


def _sparse_attn_kernel(qt_ref, iqt_ref, iwt_ref, k_ref, vt_ref, ik_ref, bias_ref, o_ref,
                        key_scr, part_scr, m_scr, acc_scr, s_scr, *, tq, tk, nkt, pos0, n_keys, top_k):
    qi = pl.program_id(1)
    q0 = pos0 + qi * tq
    n_kt = jnp.minimum(nkt, ((q0 + tq - 1) // CHUNK * CHUNK + CHUNK + tk - 1) // tk)
    last = n_kt - 1

    fold = _fold

    iqt = iqt_ref[...]
    iqt_h = [iqt[h * IDX_DIM:(h + 1) * IDX_DIM, :] for h in range(N_IDX_HEADS)]
    iwt = iwt_ref[...]
    iw_h = [iwt[h:h + 1, :] for h in range(N_IDX_HEADS)]

    def score_tile(kt, masked):
        ik = ik_ref[kt]
        s = None
        for h in range(N_IDX_HEADS):
            t = jnp.maximum(jnp.dot(ik, iqt_h[h], preferred_element_type=F32), 0.0) * iw_h[h]
            s = t if s is None else s + t
        s = s + 0.0
        bits = pltpu.bitcast(s, jnp.int32)
        key = bits ^ ((bits >> 31) & 0x7FFFFFFF)
        top = pltpu.bitcast(bits & -65536, F32)
        if masked:
            kp = kt * tk + lax.broadcasted_iota(jnp.int32, (tk, tq), 0)
            qp = q0 + lax.broadcasted_iota(jnp.int32, (tk, tq), 1)
            adm = ((kp >> CHUNK_SHIFT) <= (qp >> CHUNK_SHIFT)) & (kp < n_keys)
            key = jnp.where(adm, key, INT_MIN)
            top = jnp.where(adm, top, -jnp.inf)
        key_scr[kt] = key
        part_scr[kt] = top.astype(BF16)

    def score_body(i, carry):
        for j in range(SCORE_TRIP):
            score_tile(SCORE_TRIP * i + j, False)
        return carry

    lax.fori_loop(0, last // SCORE_TRIP, score_body, 0)
    for j in range(SCORE_TRIP - 1):
        @pl.when(j < last % SCORE_TRIP)
        def _():
            score_tile(last // SCORE_TRIP * SCORE_TRIP + j, False)

    score_tile(last, True)

    thr = _select_top_k(key_scr, part_scr, n_kt, tk=tk, tq=tq, top_k=top_k)

    qt = qt_ref[...]
    slab_row = lax.broadcasted_iota(jnp.int32, (LANES, tq), 0)
    qz = []
    for h in range(N_HEADS):
        slab = qt[(h // 2) * LANES:(h // 2 + 1) * LANES, :]
        mine = (slab_row < HEAD_DIM) if h % 2 == 0 else (slab_row >= HEAD_DIM)
        qz.append(jnp.where(mine, slab, jnp.zeros_like(slab)))
    m_scr[...] = jnp.full(m_scr.shape, NEG, F32)
    acc_scr[...] = jnp.zeros(acc_scr.shape, F32)

    def park(kt, slot, heads, madd):
        for h in heads:
            pair = h // 2
            s = jnp.dot(k_ref[kt, :, pair * LANES:(pair + 1) * LANES], qz[h], preferred_element_type=F32)
            s_scr[slot, h] = s + madd

    def absorb(kt, slot, bias_idx, heads):
        for h in heads:
            s = s_scr[slot, h]
            if bias_idx is not None:
                s = s + bias_ref[bias_idx, h]
            m_new = jnp.maximum(m_scr[h], jnp.max(fold(s, jnp.max), axis=0, keepdims=True))
            alpha = jnp.exp2(m_scr[h] - m_new)
            p = jnp.exp2(s - m_new)
            pv = jnp.dot(vt_ref[kt, h * V_ROWS:(h + 1) * V_ROWS, :], p.astype(BF16), preferred_element_type=F32)
            acc_scr[h] = alpha * acc_scr[h] + pv
            m_scr[h] = m_new

    all_heads = range(N_HEADS)
    head_groups = (range(0, N_HEADS // 2), range(N_HEADS // 2, N_HEADS))

    def step(kt_absorb, slot, bias_idx, kt_park):
        if kt_park is None:
            absorb(kt_absorb, slot, bias_idx, all_heads)
            return
        madd = jnp.where(key_scr[kt_park] >= thr, 0.0, NEG)
        for hs in head_groups:
            park(kt_park, 1 - slot, hs, madd)
            absorb(kt_absorb, slot, bias_idx, hs)

    park(0, 0, all_heads, jnp.where(key_scr[0] >= thr, 0.0, NEG))
    n_trips = (n_kt - 2) // 2

    def far_body(i, carry):
        kt = 2 * i
        step(kt, 0, None, kt + 1)
        step(kt + 1, 1, None, kt + 2)
        return carry

    lax.fori_loop(0, n_trips, far_body, 0)
    done = 2 * jnp.maximum(n_trips, 0)

    @pl.when(n_kt == 1)
    def _():
        step(0, 0, 1, None)

    @pl.when((n_kt >= 2) & (n_kt % 2 == 0))
    def _():
        step(done, 0, 0, done + 1)
        step(done + 1, 1, 1, None)

    @pl.when((n_kt >= 3) & (n_kt % 2 == 1))
    def _():
        step(done, 0, None, done + 1)
        step(done + 1, 1, 0, done + 2)
        step(done + 2, 0, 1, None)

    for h in range(N_HEADS):
        acc = acc_scr[h]
        o_ref[h * HEAD_DIM:(h + 1) * HEAD_DIM, :] = (acc[:HEAD_DIM] / acc[HEAD_DIM:HEAD_DIM + 1]).astype(o_ref.dtype)


def _rel_bucket(rel):
    nb = NUM_BUCKETS // 2
    max_exact = nb // 2
    ret = jnp.where(rel > 0, nb, 0)
    n = jnp.abs(rel)
    nf = jnp.maximum(n, 1).astype(jnp.float32)
    large = max_exact + (jnp.log(nf / max_exact) / math.log(MAX_DISTANCE / max_exact) * (nb - max_exact)).astype(jnp.int32)
    large = jnp.minimum(large, nb - 1)
    return ret + jnp.where(n < max_exact, n, large)


def _bias_tiles(rel_bias, tq, tk):
    j = jnp.arange(tk, dtype=jnp.int32)[:, None]
    i = jnp.arange(tq, dtype=jnp.int32)[None, :]
    return jnp.stack([_shifted_bias(rel_bias, j - i + d) for d in (-tk, 0)])


def _shifted_bias(rel_bias, rel):
    rb = (rel_bias.astype(F32) - rel_bias[_rel_bucket(jnp.int32(-MAX_DISTANCE))].astype(F32)) * LOG2E
    onehot = jax.nn.one_hot(_rel_bucket(rel), NUM_BUCKETS, dtype=F32)
    return jnp.moveaxis(jnp.einsum("...b,bh->...h", onehot, rb, precision=lax.Precision.HIGHEST), -1, 0)


def _check_tiling(t, tq, tk, nkt, pos0, n_keys):
    for qi in range(t // tq):
        q0 = pos0 + qi * tq
        n_kt = min(nkt, ((q0 + tq - 1) // CHUNK * CHUNK + CHUNK + tk - 1) // tk)
        adm_end = min(q0 // CHUNK * CHUNK + CHUNK, n_keys)
        assert (n_kt - 1) * tk <= adm_end, "only the last visited tile may hold inadmissible keys"
        assert n_kt * tk >= min((q0 + tq - 1) // CHUNK * CHUNK + CHUNK, n_keys), "visited tiles cover every admissible key"
        assert (n_kt - 1) * tk == q0, "last tile starts with the query tile"
        assert tk >= MAX_DISTANCE, "tiles before the last two are at least MAX_DISTANCE behind"


def _sparse_attn(qt, iqt, iwt, k_b, vt_t, ik_b, rel_bias, pos0):
    b, _, t = qt.shape
    n_keys = k_b.shape[1]
    top_k = min(TOPK_MAX, n_keys // 4)
    tk = KEY_TILE
    tq = min(tk, t)
    nkt = n_keys // tk
    assert n_keys % tk == 0 and vt_t.shape == (b, nkt, N_HEADS * V_ROWS, tk)
    _check_tiling(t, tq, tk, nkt, pos0, n_keys)
    k_t = k_b.reshape(b, nkt, tk, ATTN_W)
    ik_t = ik_b.reshape(b, nkt, tk, IDX_DIM)
    bias = _bias_tiles(rel_bias, tq, tk)

    qspec = lambda n: pl.BlockSpec((None, n, tq), lambda bi, qi: (bi, 0, qi))
    kspec = lambda r, c: pl.BlockSpec((None, nkt, r, c), lambda bi, qi: (bi, 0, 0, 0))
    kern = functools.partial(_sparse_attn_kernel, tq=tq, tk=tk, nkt=nkt, pos0=pos0, n_keys=n_keys, top_k=top_k)
    return pl.pallas_call(
        kern,
        grid=(b, t // tq),
        in_specs=[qspec(ATTN_W), qspec(ATTN_W), qspec(N_IDX_HEADS),
                  kspec(tk, ATTN_W), kspec(N_HEADS * V_ROWS, tk), kspec(tk, IDX_DIM), _const_spec(bias.shape)],
        out_specs=qspec(ATTN_W),
        out_shape=jax.ShapeDtypeStruct((b, ATTN_W, t), BF16),
        scratch_shapes=[pltpu.VMEM((nkt + SEARCH_TRIP - 1, tk, tq), jnp.int32),
                        pltpu.VMEM((nkt + SEARCH_TRIP - 1, tk, tq), BF16),
                        pltpu.VMEM((N_HEADS, 1, tq), F32),
                        pltpu.VMEM((N_HEADS, V_ROWS, tq), F32),
                        pltpu.VMEM((2, N_HEADS, tk, tq), F32)],
        compiler_params=pltpu.CompilerParams(dimension_semantics=("arbitrary", "arbitrary"),
                                             vmem_limit_bytes=VMEM_LIMIT),
        name="sparse_attn",
    )(qt, iqt, iwt, k_t, vt_t, ik_t, bias)


DEC_ROWS = N_HEADS * 16
DEC_KEY_TILE = 1024
NT_DIMS = (((1,), (1,)), ((), ()))


def _dec_select_kernel(iq_ref, iwc_ref, ikt_ref, iktn_ref, madd_ref, key_scr, *, g, tq, n_keys, n_pad, top_k):
    n_slabs = n_pad // LANES
    col = lax.broadcasted_iota(jnp.int32, (tq, n_pad), 1)
    for s in range(g):
        iq_s = iq_ref[s * tq:(s + 1) * tq, :]
        lhs = jnp.concatenate([iq_s[:, h * IDX_DIM:(h + 1) * IDX_DIM] for h in range(N_IDX_HEADS)], axis=0)
        logit = jnp.concatenate([jnp.dot(lhs, ikt_ref[s].astype(BF16), preferred_element_type=F32),
                                 jnp.dot(lhs, iktn_ref[s], preferred_element_type=F32)], axis=1)
        x = jnp.maximum(logit, 0.0) * iwc_ref[s]
        score = x[0:tq]
        for h in range(1, N_IDX_HEADS):
            score = score + x[h * tq:(h + 1) * tq]
        score = score + 0.0
        bits = pltpu.bitcast(score, jnp.int32)
        key = bits ^ ((bits >> 31) & 0x7FFFFFFF)
        key_scr[s * tq:(s + 1) * tq, :] = jnp.where(col < n_keys, key, INT_MIN)

    rows = g * tq
    lane = lax.broadcasted_iota(jnp.int32, (rows, LANES), 1)

    def count(pred):
        acc = jnp.zeros((rows, LANES), F32)
        for j in range(n_slabs):
            acc = acc + jnp.where(pred(key_scr[:, j * LANES:(j + 1) * LANES], j), 1.0, 0.0)
        return jnp.sum(acc, axis=1, keepdims=True)

    def thr_body(i, t):
        cand = t + (jnp.int32(1) << (31 - i))
        c = count(lambda key, j: key >= cand)
        return jnp.where(c >= top_k, cand, t)

    thr = lax.fori_loop(0, 32, thr_body, jnp.full((rows, 1), INT_MIN, jnp.int32))
    thr = jnp.maximum(thr, INT_MIN + 1)

    n_ge = count(lambda key, j: key >= thr)
    has_tie = jnp.max(n_ge) > top_k

    @pl.when(has_tie)
    def _():
        need_m1 = (top_k - 1) - count(lambda key, j: key > thr)

        def idx_body(i, cut):
            cand = cut + (jnp.int32(1) << (15 - i))
            c = count(lambda key, j: (key == thr) & (j * LANES + lane < cand))
            return jnp.where(c <= need_m1, cand, cut)

        cut = lax.fori_loop(0, 16, idx_body, jnp.zeros((rows, 1), jnp.int32))
        tie_row = n_ge > top_k
        for j in range(n_slabs):
            key = key_scr[:, j * LANES:(j + 1) * LANES]
            drop = tie_row & (key == thr) & (j * LANES + lane > cut)
            key_scr[:, j * LANES:(j + 1) * LANES] = jnp.where(drop, INT_MIN, key)

    madd_ref[...] = jnp.where(key_scr[...] >= thr, 0.0, NEG)


def _dec_attn_kernel(q_ref, maddc_ref, maddn_ref, k_ref, v_ref, kn_ref, vn_ref, biasc_ref, biasn_ref, o_ref,
                     m_scr, l_scr, acc_scr, *, tq):
    kt = pl.program_id(1)
    q = q_ref[...]
    q_h = [q[:, h * HEAD_DIM:(h + 1) * HEAD_DIM] for h in range(N_HEADS)]

    @pl.when(kt == 0)
    def _():
        m_scr[...] = jnp.full(m_scr.shape, NEG, F32)
        l_scr[...] = jnp.zeros(l_scr.shape, F32)
        acc_scr[...] = jnp.zeros(acc_scr.shape, F32)

    def step(ktr, vtr, madd, bias):
        s = jnp.concatenate([jnp.dot(q_h[h], ktr[h].astype(BF16), preferred_element_type=F32)
                             for h in range(N_HEADS)], axis=0)
        s = s + jnp.concatenate([madd] * N_HEADS, axis=0) + bias
        m_prev = m_scr[...]
        m_new = jnp.maximum(m_prev, jnp.max(s, axis=1, keepdims=True))
        alpha = jnp.exp2(m_prev - m_new)
        p = jnp.exp2(s - m_new)
        l_scr[...] = alpha * l_scr[...] + jnp.sum(p, axis=1, keepdims=True)
        pb = p.astype(BF16)
        pv = jnp.concatenate([lax.dot_general(pb[h * tq:(h + 1) * tq, :], vtr[h].astype(BF16), NT_DIMS,
                                              preferred_element_type=F32) for h in range(N_HEADS)], axis=0)
        acc_scr[...] = alpha * acc_scr[...] + pv
        m_scr[...] = m_new

    step(k_ref, v_ref, maddc_ref[...], biasc_ref[...])

    @pl.when(kt == pl.num_programs(1) - 1)
    def _():
        step(kn_ref, vn_ref, maddn_ref[...], biasn_ref[...])
        o = acc_scr[...] / l_scr[...]
        for h in range(N_HEADS):
            o_ref[:, h * HEAD_DIM:(h + 1) * HEAD_DIM] = o[h * tq:(h + 1) * tq, :].astype(o_ref.dtype)


def _dec_attn(q, iq, iw, past_k, past_v, past_ik, k_new, v_new, ik_new, rel_bias, pos0):
    b, t, _ = q.shape
    p_len = past_k.shape[1]
    n_keys = p_len + t
    top_k = min(TOPK_MAX, n_keys // 4)
    tkk = min(DEC_KEY_TILE, p_len)
    n_pad = p_len + LANES
    g = min(LANES // t, b)
    assert N_HEADS * t == DEC_ROWS and b % g == 0 and p_len % tkk == 0 and t <= LANES
    assert (pos0 + t - 1) // CHUNK == pos0 // CHUNK and (n_keys - 1) // CHUNK <= pos0 // CHUNK, "every key admissible"
    assert tkk >= MAX_DISTANCE + LANES >= MAX_DISTANCE + t, "only the last cache tile is within MAX_DISTANCE of a query"

    ikt_new = jnp.pad(ik_new.swapaxes(1, 2), ((0, 0), (0, 0), (0, LANES - t)))
    iw_col = iw.swapaxes(1, 2).reshape(b, DEC_ROWS, 1)
    madd = pl.pallas_call(
        functools.partial(_dec_select_kernel, g=g, tq=t, n_keys=n_keys, n_pad=n_pad, top_k=top_k),
        grid=(b // g,),
        in_specs=[pl.BlockSpec((g * t, ATTN_W), lambda i: (i, 0)),
                  pl.BlockSpec((g, DEC_ROWS, 1), lambda i: (i, 0, 0)),
                  pl.BlockSpec((g, IDX_DIM, p_len), lambda i: (i, 0, 0)),
                  pl.BlockSpec((g, IDX_DIM, LANES), lambda i: (i, 0, 0))],
        out_specs=pl.BlockSpec((g * t, n_pad), lambda i: (i, 0)),
        out_shape=jax.ShapeDtypeStruct((b * t, n_pad), F32),
        scratch_shapes=[pltpu.VMEM((g * t, n_pad), jnp.int32)],
        compiler_params=pltpu.CompilerParams(dimension_semantics=("arbitrary",), vmem_limit_bytes=VMEM_LIMIT),
        name="dec_select",
    )(iq.reshape(b * t, ATTN_W), iw_col, past_ik.swapaxes(1, 2), ikt_new)

    n_steps = p_len // tkk
    qpos = jnp.tile(jnp.arange(t, dtype=jnp.int32), N_HEADS)[:, None]
    hsel = jnp.repeat(jnp.arange(N_HEADS), t)
    rows = jnp.arange(DEC_ROWS)
    near = MAX_DISTANCE + LANES
    rel_c = jnp.arange(near, dtype=jnp.int32)[None, :] - near - qpos
    rel_n = jnp.arange(LANES, dtype=jnp.int32)[None, :] - qpos
    bias_last = jnp.pad(_shifted_bias(rel_bias, rel_c)[hsel, rows], ((0, 0), (tkk - near, 0)))
    bias_c = jnp.concatenate([jnp.zeros((n_steps - 1, DEC_ROWS, tkk), F32), bias_last[None]], axis=0)
    bias_n = _shifted_bias(rel_bias, rel_n)[hsel, rows]
    heads_t = lambda a: a.transpose(0, 2, 3, 1)
    pad_new = lambda a: jnp.pad(heads_t(a.reshape(b, t, N_HEADS, HEAD_DIM)), ((0, 0), (0, 0), (0, 0), (0, LANES - t)))
    kv_tile = pl.BlockSpec((None, N_HEADS, HEAD_DIM, tkk), lambda bi, kt: (bi, 0, 0, kt))
    kv_new = pl.BlockSpec((None, N_HEADS, HEAD_DIM, LANES), lambda bi, kt: (bi, 0, 0, 0))
    return pl.pallas_call(
        functools.partial(_dec_attn_kernel, tq=t),
        grid=(b, n_steps),
        in_specs=[pl.BlockSpec((None, t, ATTN_W), lambda bi, kt: (bi, 0, 0)),
                  pl.BlockSpec((t, tkk), lambda bi, kt: (bi, kt)),
                  pl.BlockSpec((t, LANES), lambda bi, kt: (bi, p_len // LANES)),
                  kv_tile, kv_tile, kv_new, kv_new,
                  pl.BlockSpec((None, DEC_ROWS, tkk), lambda bi, kt: (kt, 0, 0)),
                  _const_spec(bias_n.shape)],
        out_specs=pl.BlockSpec((None, t, ATTN_W), lambda bi, kt: (bi, 0, 0)),
        out_shape=jax.ShapeDtypeStruct((b, t, ATTN_W), BF16),
        scratch_shapes=[pltpu.VMEM((DEC_ROWS, 1), F32), pltpu.VMEM((DEC_ROWS, 1), F32),
                        pltpu.VMEM((DEC_ROWS, HEAD_DIM), F32)],
        compiler_params=pltpu.CompilerParams(dimension_semantics=("arbitrary", "arbitrary"),
                                             vmem_limit_bytes=VMEM_LIMIT),
        name="dec_attn",
    )(q, madd, madd, heads_t(past_k), heads_t(past_v), pad_new(k_new), pad_new(v_new), bias_c, bias_n)


def _conv_kernel(*refs, tm, has_prev):
    if has_prev:
        u_ref, prev_ref, past_ref, w_ref, b_ref, lg_ref, lb_ref, o_ref, ext_scr = refs
        halo = jnp.where(pl.program_id(1) == 0, past_ref[...], prev_ref[...])
    else:
        u_ref, past_ref, w_ref, b_ref, lg_ref, lb_ref, o_ref, ext_scr = refs
        halo = past_ref[...]
    _conv_act(halo, u_ref[...], w_ref, b_ref, lg_ref, lb_ref, o_ref, ext_scr)


def _conv_act(halo, u, w_ref, b_ref, lg_ref, lb_ref, o_ref, ext_scr):
    tm = u.shape[0]
    ext_scr[0, 0:HALO, :] = halo
    ext_scr[0, HALO:HALO + tm, :] = u
    n_rows = tm + HALO - SUBLANES
    for r in range(1, SUBLANES):
        ext_scr[r, 0:n_rows, :] = ext_scr[0, r:r + n_rows, :]
    rb = min(tm, CONV_ROWS)
    first = HALO - (CONV_K - 1)
    for r0 in range(0, tm, rb):
        acc = None
        for j in range(CONV_K):
            a, r = divmod(first + j, SUBLANES)
            t = ext_scr[r, r0 + a * SUBLANES:r0 + a * SUBLANES + rb, :] * w_ref[j:j + 1, :]
            acc = t if acc is None else acc + t
        c = acc + b_ref[...]
        mu = jnp.mean(c, axis=-1, keepdims=True)
        cc = c - mu
        var = jnp.mean(cc * cc, axis=-1, keepdims=True)
        y = cc * lax.rsqrt(var + EPS) * lg_ref[...] + lb_ref[...]
        o_ref[r0:r0 + rb, :] = (y * jax.nn.sigmoid(y)).astype(o_ref.dtype)


def _conv_mod(u, past, w):
    b, t, c = u.shape
    tm = min(ROW_TILE, t)
    assert t % tm == 0 and tm % 8 == 0
    has_prev = t > tm
    tile = pl.BlockSpec((None, tm, c), lambda bi, i: (bi, i, 0))
    halo = pl.BlockSpec((None, HALO, c), lambda bi, i: (bi, 0, 0))
    in_specs = [tile]
    args = [u]
    if has_prev:
        r = tm // HALO
        in_specs.append(pl.BlockSpec((None, HALO, c), lambda bi, i: (bi, jnp.maximum(i * r - 1, 0), 0)))
        args.append(u)
    consts = [w["conv_w"], w["conv_b"], w["ln_g"], w["ln_b"]]
    in_specs += [halo] + [_const_spec(x.shape) for x in consts]
    args += [past] + consts
    return pl.pallas_call(
        functools.partial(_conv_kernel, tm=tm, has_prev=has_prev),
        grid=(b, t // tm),
        in_specs=in_specs,
        out_specs=tile,
        out_shape=jax.ShapeDtypeStruct((b, t, c), BF16),
        scratch_shapes=[pltpu.VMEM((SUBLANES, HALO + tm, c), F32)],
        compiler_params=pltpu.CompilerParams(dimension_semantics=("arbitrary", "arbitrary"),
                                             vmem_limit_bytes=VMEM_LIMIT),
        name="conv_mod",
    )(*args)


def _mix_ffn_conv_kernel(x_ref, attn_ref, u0_ref, unext_ref, utail_ref, past0_ref, pastn_ref, sga_ref, sgc_ref,
                         wao_ref, wco_ref, wo_ref, g_ref, wg_ref, wu_ref, wd_ref, cw_ref, cb_ref, lg_ref, lb_ref,
                         y_ref, cact_scr, ext_scr, *, n_chunks, tiles_per_seq):
    i = pl.program_id(0)

    @pl.when(i == 0)
    def _():
        _conv_act(past0_ref[...], u0_ref[...], cw_ref, cb_ref, lg_ref, lb_ref, cact_scr.at[0], ext_scr)

    _mix_ffn_kernel(x_ref, attn_ref, cact_scr.at[i % 2], sga_ref, sgc_ref, wao_ref, wco_ref, wo_ref, g_ref,
                    wg_ref, wu_ref, wd_ref, y_ref, n_chunks=n_chunks)
    halo = jnp.where((i + 1) % tiles_per_seq == 0, pastn_ref[...], utail_ref[...])
    _conv_act(halo, unext_ref[...], cw_ref, cb_ref, lg_ref, lb_ref, cact_scr.at[(i + 1) % 2], ext_scr)


def _mix_ffn_kernel(x_ref, attn_ref, cact_ref, sga_ref, sgc_ref, wao_ref, wco_ref, wo_ref, g_ref,
                    wg_ref, wu_ref, wd_ref, y_ref, *, n_chunks):
    attn_out = jnp.dot(attn_ref[...], wao_ref[...], preferred_element_type=F32)
    conv_out = jnp.dot(cact_ref[...], wco_ref[...], preferred_element_type=F32)
    merged = sga_ref[...].astype(F32) * attn_out + sgc_ref[...].astype(F32) * conv_out
    x1 = x_ref[...] + jnp.dot(merged.astype(BF16), wo_ref[...], preferred_element_type=F32)
    ms = jnp.mean(x1 * x1, axis=-1, keepdims=True)
    h2 = (x1 * lax.rsqrt(ms + EPS) * g_ref[...]).astype(BF16)
    hc = wg_ref.shape[1] // n_chunks
    y = x1
    for ci in range(n_chunks):
        sl = slice(ci * hc, (ci + 1) * hc)
        gate = jnp.dot(h2, wg_ref[:, sl], preferred_element_type=F32)
        up = jnp.dot(h2, wu_ref[:, sl], preferred_element_type=F32)
        act = (gate * jax.nn.sigmoid(gate) * up).astype(BF16)
        y = y + jnp.dot(act, wd_ref[sl, :], preferred_element_type=F32)
    y_ref[...] = y


def _mix_ffn_conv(x, attn, u, past, sga, sgc, w, seq_len):
    m, d = x.shape
    c = u.shape[1]
    tm = min(ROW_TILE, seq_len)
    assert m % tm == 0 and seq_len % tm == 0 and tm % HALO == 0
    nt, n = seq_len // tm, m // tm
    hidden = w["wg"].shape[1]
    n_chunks = 2 if hidden % (2 * LANES) == 0 else 1
    row = lambda k: pl.BlockSpec((tm, k), lambda i: (i, 0))
    nxt = lambda i: jnp.minimum(i + 1, n - 1)
    consts = [w[k] for k in ("wao", "wco", "wo", "g_ffn", "wg", "wu", "wd", "conv_w", "conv_b", "ln_g", "ln_b")]
    return pl.pallas_call(
        functools.partial(_mix_ffn_conv_kernel, n_chunks=n_chunks, tiles_per_seq=nt),
        grid=(n,),
        in_specs=[row(d), row(attn.shape[1]),
                  pl.BlockSpec((tm, c), lambda i: (0, 0)),
                  pl.BlockSpec((tm, c), lambda i: (nxt(i), 0)),
                  pl.BlockSpec((HALO, c), lambda i: ((i + 1) * (tm // HALO) - 1, 0)),
                  pl.BlockSpec((None, HALO, c), lambda i: (0, 0, 0)),
                  pl.BlockSpec((None, HALO, c), lambda i: (nxt(i) // nt, 0, 0)),
                  row(d), row(d)] + [_const_spec(k.shape) for k in consts],
        out_specs=row(d),
        out_shape=jax.ShapeDtypeStruct((m, d), F32),
        scratch_shapes=[pltpu.VMEM((2, tm, c), BF16), pltpu.VMEM((SUBLANES, HALO + tm, c), F32)],
        compiler_params=pltpu.CompilerParams(dimension_semantics=("arbitrary",), vmem_limit_bytes=VMEM_LIMIT),
        name="mix_ffn_conv",
    )(x, attn, u, u, u, past, past, sga, sgc, *consts)


def _mix_ffn(x, attn, cact, sga, sgc, w):
    m, d = x.shape
    tm = min(ROW_TILE, m)
    assert m % tm == 0
    hidden = w["wg"].shape[1]
    n_chunks = 2 if hidden % (2 * LANES) == 0 else 1
    row = lambda n: pl.BlockSpec((tm, n), lambda i: (i, 0))
    consts = [w[n] for n in ("wao", "wco", "wo", "g_ffn", "wg", "wu", "wd")]
    return pl.pallas_call(
        functools.partial(_mix_ffn_kernel, n_chunks=n_chunks),
        grid=(m // tm,),
        in_specs=[row(d), row(attn.shape[1]), row(cact.shape[1]), row(d), row(d)]
                 + [_const_spec(c.shape) for c in consts],
        out_specs=row(d),
        out_shape=jax.ShapeDtypeStruct((m, d), F32),
        compiler_params=pltpu.CompilerParams(dimension_semantics=("arbitrary",), vmem_limit_bytes=VMEM_LIMIT),
        name="mix_ffn",
    )(x, attn, cact, sga, sgc, *consts)


def _prep_weights(norm_mix_g, w_in, q_norm_g, k_norm_g, idx_k_norm_g, conv_dw_w, conv_dw_b, conv_ln_g, conv_ln_b,
                  w_conv_out, w_attn_out, w_out, norm_ffn_g, w_ffn_gate, w_ffn_up, w_ffn_down):
    d = w_in.shape[0]
    conv_ch = conv_dw_w.shape[-1]
    sizes = (ATTN_W, ATTN_W, ATTN_W, N_IDX_HEADS * IDX_DIM, IDX_DIM, N_IDX_HEADS, 2 * conv_ch, d, d)
    offs = np.concatenate([[0], np.cumsum(sizes)])
    assert offs[-1] == w_in.shape[1]
    col = lambda i: w_in[:, offs[i]:offs[i + 1]]
    wsm = jnp.concatenate([col(4), col(5), jnp.zeros((d, LANES - IDX_DIM - N_IDX_HEADS), w_in.dtype)], axis=1)
    head = np.arange(ATTN_W) // HEAD_DIM
    seg = (head[:, None] == np.arange(LANES)[None, :]).astype(np.float32)
    return dict(
        g_mix=norm_mix_g.reshape(1, d).astype(F32),
        wq=col(0).astype(BF16), wk=col(1).astype(BF16), wv=col(2).astype(BF16), wiq=col(3).astype(BF16),
        wsm=wsm.astype(BF16), wglu=col(6).astype(BF16), wga=col(7).astype(BF16), wgc=col(8).astype(BF16),
        qg=jnp.tile(q_norm_g.astype(F32), N_HEADS).reshape(1, ATTN_W),
        kg=jnp.tile(k_norm_g.astype(F32), N_HEADS).reshape(1, ATTN_W),
        ikg=jnp.concatenate([idx_k_norm_g.astype(F32), jnp.zeros((LANES - IDX_DIM,), F32)]).reshape(1, LANES),
        seg=jnp.asarray(seg, BF16), segt=jnp.asarray(seg.T, BF16),
        conv_w=conv_dw_w.reshape(CONV_K, conv_ch).astype(F32), conv_b=conv_dw_b.reshape(1, conv_ch).astype(F32),
        ln_g=conv_ln_g.reshape(1, conv_ch).astype(F32), ln_b=conv_ln_b.reshape(1, conv_ch).astype(F32),
        wao=w_attn_out.astype(BF16), wco=w_conv_out.astype(BF16), wo=w_out.astype(BF16),
        g_ffn=norm_ffn_g.reshape(1, d).astype(F32),
        wg=w_ffn_gate.astype(BF16), wu=w_ffn_up.astype(BF16), wd=w_ffn_down.astype(BF16),
    )


def _layer(x, pos0, past_k, past_v, past_ik, past_conv, rel_bias, w):
    b, t, d = x.shape
    m = b * t
    prompt = past_k is None
    q, k, k_b, v, v_b, iq, sm, sm_b, u, sga, sgc = _in_proj(x, w, transposed=prompt)
    conv_ch = u.shape[1]
    u = u.reshape(b, t, conv_ch)
    k_b = k_b.reshape(b, t, ATTN_W)
    ik_b = sm_b.reshape(b, t, LANES)[:, :, :IDX_DIM]
    if prompt:
        attn = _sparse_attn(q, iq, sm[:, IDX_DIM:IDX_DIM + N_IDX_HEADS, :], k_b, v_b, ik_b, rel_bias,
                            pos0).swapaxes(1, 2)
        heads = lambda a: a.reshape(b, N_HEADS, HEAD_DIM, t).transpose(0, 3, 1, 2)
        new_k, new_v, new_ik = heads(k), heads(v), sm[:, :IDX_DIM, :].swapaxes(1, 2)
    else:
        sm = sm.reshape(b, t, LANES)
        k, v = k.reshape(b, t, ATTN_W), v.reshape(b, t, ATTN_W)
        attn = _dec_attn(q.reshape(b, t, ATTN_W), iq.reshape(b, t, ATTN_W), sm[:, :, IDX_DIM:IDX_DIM + N_IDX_HEADS],
                         past_k, past_v, past_ik, k, v, ik_b, rel_bias, pos0)
        heads = lambda a: a.reshape(b, t, N_HEADS, HEAD_DIM)
        new_k, new_v, new_ik = heads(k), heads(v), sm[:, :, :IDX_DIM]

    past = jnp.pad(past_conv.astype(F32), ((0, 0), (HALO - (CONV_K - 1), 0), (0, 0)))
    if prompt:
        y = _mix_ffn_conv(x.reshape(m, d), attn.reshape(m, ATTN_W), u.reshape(m, conv_ch), past, sga, sgc, w, t)
    else:
        cact = _conv_mod(u, past, w)
        y = _mix_ffn(x.reshape(m, d), attn.reshape(m, ATTN_W), cact.reshape(m, conv_ch), sga, sgc, w)

    new_conv = jnp.concatenate([past_conv, u], axis=1)[:, -(CONV_K - 1):]
    return y.reshape(b, t, d), new_k, new_v, new_ik, new_conv


def kernel(x_prompt, x_sample, cache_k, cache_v, cache_idx_k, state_conv, rel_bias, norm_mix_g, w_in, q_norm_g, k_norm_g, idx_k_norm_g, conv_dw_w, conv_dw_b, conv_ln_g, conv_ln_b, w_conv_out, w_attn_out, w_out, norm_ffn_g, w_ffn_gate, w_ffn_up, w_ffn_down):
    depth = w_in.shape[0]
    bp = x_prompt.shape[0]
    past_len = cache_k.shape[2]
    conv_ch = conv_dw_w.shape[-1]
    yp, ys = x_prompt, x_sample
    outs_p, outs_s = [], []
    for l in range(depth):
        w = _prep_weights(norm_mix_g[l], w_in[l], q_norm_g[l], k_norm_g[l], idx_k_norm_g[l], conv_dw_w[l],
                          conv_dw_b[l], conv_ln_g[l], conv_ln_b[l], w_conv_out[l], w_attn_out[l], w_out[l],
                          norm_ffn_g[l], w_ffn_gate[l], w_ffn_up[l], w_ffn_down[l])
        zero_conv = jnp.zeros((bp, CONV_K - 1, conv_ch), yp.dtype)
        yp, *rest_p = _layer(yp, 0, None, None, None, zero_conv, rel_bias, w)
        outs_p.append(rest_p)
        ys, *rest_s = _layer(ys, past_len, cache_k[l], cache_v[l], cache_idx_k[l], state_conv[l], rel_bias, w)
        outs_s.append(rest_s)
    stack = lambda outs, i: jnp.stack([o[i] for o in outs])
    return (yp, ys, stack(outs_p, 0), stack(outs_p, 1), stack(outs_p, 2), stack(outs_p, 3),
            stack(outs_s, 0), stack(outs_s, 1), stack(outs_s, 2), stack(outs_s, 3))
```

```python
import functools
import math

import numpy as np
import jax
import jax.numpy as jnp
from jax import lax
from jax.experimental import pallas as pl
from jax.experimental.pallas import tpu as pltpu

CHUNK = 64
CHUNK_SHIFT = 6
N_HEADS = 8
HEAD_DIM = 64
ATTN_W = N_HEADS * HEAD_DIM
N_IDX_HEADS = 8
IDX_DIM = 64
TOPK_MAX = 256
CONV_K = 31
NUM_BUCKETS = 32
MAX_DISTANCE = 128
EPS = 1e-6
NEG = -1e30
INT_MIN = -(2 ** 31)
BF16_INF_BITS = 0x7F80
BF16_MIN_NORMAL_BITS = 0x0080
N_BF16_NEG = BF16_INF_BITS - BF16_MIN_NORMAL_BITS + 1
LOG2E = math.log2(math.e)

LANES = 128
SUBLANES = 8
HALO = 32
CONV_ROWS = 32
KEY_TILE = 256
SCORE_TRIP = 4
SEARCH_TRIP = 2
V_ROWS = HEAD_DIM + 16
ROW_TILE = 256
V7X_VMEM_BYTES = 64 * 1024 * 1024
VMEM_LIMIT = V7X_VMEM_BYTES - 8 * 1024 * 1024

F32 = jnp.float32
BF16 = jnp.bfloat16


def _const_spec(shape):
    return pl.BlockSpec(shape, lambda *_: (0,) * len(shape), pipeline_mode=pl.Buffered(1))


def _split_dot(x, m):
    hi = x.astype(BF16)
    lo = (x - hi.astype(F32)).astype(BF16)
    return jnp.dot(hi, m, preferred_element_type=F32) + jnp.dot(lo, m, preferred_element_type=F32)


def _in_proj_kernel(x_ref, g_ref, wq_ref, wk_ref, wv_ref, wiq_ref, wsm_ref, wglu_ref, wga_ref, wgc_ref,
                    qg_ref, kg_ref, ikg_ref, seg_ref, segt_ref,
                    q_ref, k_ref, kb_ref, v_ref, vb_ref, iq_ref, sm_ref, smb_ref, u_ref, sga_ref, sgc_ref,
                    *, transposed):
    x = x_ref[...]
    ms = jnp.mean(x * x, axis=-1, keepdims=True)
    h = (x * lax.rsqrt(ms + EPS) * g_ref[...]).astype(BF16)

    def put(ref, val):
        ref[...] = (val.T if transposed else val).astype(ref.dtype)

    glu = jnp.dot(h, wglu_ref[...], preferred_element_type=F32)
    c = glu.shape[1] // 2
    u_ref[...] = glu[:, :c] * jax.nn.sigmoid(glu[:, c:])

    def head_rms(y, gain):
        ss = _split_dot(y * y, seg_ref[...])
        r = lax.rsqrt(ss * (1.0 / HEAD_DIM) + EPS)
        return y * _split_dot(r, segt_ref[...]) * gain

    q = head_rms(jnp.dot(h, wq_ref[...], preferred_element_type=F32), qg_ref[...])
    put(q_ref, q * (HEAD_DIM ** -0.5 * LOG2E))
    k = head_rms(jnp.dot(h, wk_ref[...], preferred_element_type=F32), kg_ref[...])
    put(k_ref, k)
    kb_ref[...] = k.astype(BF16)
    v = jnp.dot(h, wv_ref[...], preferred_element_type=F32)
    put(v_ref, v)
    if transposed:
        vt = v.T.astype(BF16)
        pad_rows = lax.broadcasted_iota(jnp.int32, (V_ROWS - HEAD_DIM, vt.shape[1]), 0)
        ones_row = jnp.where(pad_rows == 0, 1.0, 0.0).astype(BF16)
        for hd in range(N_HEADS):
            vb_ref[hd * V_ROWS:hd * V_ROWS + HEAD_DIM, :] = vt[hd * HEAD_DIM:(hd + 1) * HEAD_DIM, :]
            vb_ref[hd * V_ROWS + HEAD_DIM:(hd + 1) * V_ROWS, :] = ones_row
    else:
        vb_ref[...] = v.astype(BF16)
    iq = jnp.dot(h, wiq_ref[...], preferred_element_type=F32)
    put(iq_ref, iq * (IDX_DIM ** -0.5))

    sm = jnp.dot(h, wsm_ref[...], preferred_element_type=F32)
    lane = lax.broadcasted_iota(jnp.int32, sm.shape, 1)
    is_ik = lane < IDX_DIM
    ss = jnp.sum(jnp.where(is_ik, sm * sm, 0.0), axis=-1, keepdims=True)
    ikn = sm * lax.rsqrt(ss * (1.0 / IDX_DIM) + EPS) * ikg_ref[...]
    iw = sm * (N_IDX_HEADS ** -0.5)
    smo = jnp.where(is_ik, ikn, jnp.where(lane < IDX_DIM + N_IDX_HEADS, iw, 0.0))
    put(sm_ref, smo)
    smb_ref[...] = smo.astype(BF16)

    sga_ref[...] = jax.nn.sigmoid(jnp.dot(h, wga_ref[...], preferred_element_type=F32)).astype(BF16)
    sgc_ref[...] = jax.nn.sigmoid(jnp.dot(h, wgc_ref[...], preferred_element_type=F32)).astype(BF16)


def _in_proj(x, w, transposed):
    b, t, d = x.shape
    m = b * t
    tm = min(ROW_TILE, t if transposed else m)
    assert m % tm == 0 and (not transposed or t % tm == 0)
    nt = t // tm
    conv_ch = w["wglu"].shape[1] // 2
    consts = [w[n] for n in ("g_mix", "wq", "wk", "wv", "wiq", "wsm", "wglu", "wga", "wgc",
                             "qg", "kg", "ikg", "seg", "segt")]

    def rows(n, dtype):
        return jax.ShapeDtypeStruct((m, n), dtype), pl.BlockSpec((tm, n), lambda i: (i, 0))

    def cols(n, dtype):
        if not transposed:
            return rows(n, dtype)
        return jax.ShapeDtypeStruct((b, n, t), dtype), pl.BlockSpec((None, n, tm), lambda i: (i // nt, 0, i % nt))

    if transposed:
        v_slab = (jax.ShapeDtypeStruct((b, nt, N_HEADS * V_ROWS, tm), BF16),
                  pl.BlockSpec((None, None, N_HEADS * V_ROWS, tm), lambda i: (i // nt, i % nt, 0, 0)))
    else:
        v_slab = rows(ATTN_W, BF16)
    outs = [
        cols(ATTN_W, BF16),
        cols(ATTN_W, F32),
        rows(ATTN_W, BF16),
        cols(ATTN_W, F32),
        v_slab,
        cols(ATTN_W, BF16),
        cols(LANES, F32),
        rows(LANES, BF16),
        rows(conv_ch, F32),
        rows(d, BF16),
        rows(d, BF16),
    ]
    return pl.pallas_call(
        functools.partial(_in_proj_kernel, transposed=transposed),
        grid=(m // tm,),
        in_specs=[pl.BlockSpec((tm, d), lambda i: (i, 0))] + [_const_spec(c.shape) for c in consts],
        out_specs=[o[1] for o in outs],
        out_shape=[o[0] for o in outs],
        compiler_params=pltpu.CompilerParams(dimension_semantics=("arbitrary",), vmem_limit_bytes=VMEM_LIMIT),
        name="in_proj",
    )(x.reshape(m, d), *consts)


def _fold(x, op):
    return op(x.reshape(x.shape[0] // SUBLANES, SUBLANES, x.shape[1]), axis=0)


def _select_top_k(key_scr, part_scr, n_kt, *, tk, tq, top_k):
    n_trips = (n_kt + SEARCH_TRIP - 1) // SEARCH_TRIP
    for j in range(SEARCH_TRIP - 1):
        @pl.when(n_kt + j < n_trips * SEARCH_TRIP)
        def _():
            key_scr[n_kt + j] = jnp.full((tk, tq), INT_MIN, jnp.int32)
            part_scr[n_kt + j] = jnp.full((tk, tq), -jnp.inf, BF16)

    trip = lambda i: [SEARCH_TRIP * i + j for j in range(SEARCH_TRIP)]

    rows16 = 2 * SUBLANES
    one_b, zero_b = jnp.ones((), BF16), jnp.zeros((), BF16)

    def count_part(pred):
        def body(i, acc):
            for kt in trip(i):
                c = jnp.where(pred(part_scr[kt]), one_b, zero_b)
                parts = [c[j * rows16:(j + 1) * rows16] for j in range(4)]
                for j in range(4, tk // rows16):
                    parts[j % 4] = parts[j % 4] + c[j * rows16:(j + 1) * rows16]
                acc = acc + ((parts[0] + parts[1]) + (parts[2] + parts[3])).astype(F32)
            return acc
        acc = lax.fori_loop(0, n_trips, body, jnp.zeros((rows16, tq), F32))
        return jnp.sum(acc, axis=0, keepdims=True)

    def search(n_bits, base, to_image, limit):
        def body(i, t):
            cand = t + (jnp.int32(1) << (n_bits - 1 - i))
            img = to_image(cand)
            c = base + count_part(lambda a: a >= img)
            return jnp.where((c >= top_k) & (cand <= limit), cand, t)
        return lax.fori_loop(0, n_bits, body, jnp.zeros((1, tq), jnp.int32))

    def refine(eq_img, shift):
        def body(i, carry):
            for kt in trip(i):
                nxt = ((key_scr[kt] >> shift) & 0xFF).astype(F32).astype(BF16)
                part_scr[kt] = jnp.where(part_scr[kt] == eq_img, nxt, -one_b)
            return carry
        lax.fori_loop(0, n_trips, body, 0)

    def top_bits(c):
        neg = 0x8000 | (BF16_INF_BITS - c)
        pos = BF16_MIN_NORMAL_BITS + (c - N_BF16_NEG - 1)
        return jnp.where(c < N_BF16_NEG, neg, jnp.where(c == N_BF16_NEG, 0, pos))

    def top_image(c):
        return pltpu.bitcast(top_bits(c) << 16, F32).astype(BF16)

    small_image = lambda c: c.astype(F32).astype(BF16)

    t1 = search(16, 0.0, top_image, 2 * N_BF16_NEG)
    img1 = top_image(t1)
    above = count_part(lambda a: a > img1)
    refine(img1, 8)
    t2 = search(8, above, small_image, 255)
    img2 = small_image(t2)
    above = above + count_part(lambda a: a > img2)
    refine(img2, 0)
    t3 = search(8, above, small_image, 255)
    img3 = small_image(t3)
    n_ge = above + count_part(lambda a: a >= img3)
    hi = top_bits(t1)
    hi = hi - ((hi >> 15) << 16)
    hi = hi ^ ((hi >> 15) & 0x7FFF)
    thr = (hi << 16) + (t2 << 8) + t3
    few = t1 == 0
    thr = jnp.where(few, INT_MIN + 1, jnp.maximum(thr, INT_MIN + 1))

    excess = jnp.where(few, 0.0, n_ge - top_k)
    has_tie = jnp.max(excess) > 0

    @pl.when(has_tie)
    def _():
        upper = (lax.broadcasted_iota(jnp.int32, (tk, tk), 1) >= lax.broadcasted_iota(jnp.int32, (tk, tk), 0))
        upper = jnp.where(upper, 1.0, 0.0).astype(BF16)

        def drop_body(i, later):
            kt = n_kt - 1 - i
            key = key_scr[kt]
            eq = key == thr
            ties_from_here = jnp.dot(upper, jnp.where(eq, 1.0, 0.0).astype(BF16), preferred_element_type=F32)
            key_scr[kt] = jnp.where(eq & (ties_from_here + later <= excess), INT_MIN, key)
            return later + ties_from_here[0:1, :]

        lax.fori_loop(0, n_kt, drop_body, jnp.zeros((1, tq), F32))
    return thr


def _sparse_attn_kernel(qt_ref, iqt_ref, iwt_ref, k_ref, vt_ref, ik_ref, bias_ref, o_ref,
                        key_scr, part_scr, m_scr, acc_scr, s_scr, *, tq, tk, nkt, pos0, n_keys, top_k):
    qi = pl.program_id(1)
    q0 = pos0 + qi * tq
    n_kt = jnp.minimum(nkt, ((q0 + tq - 1) // CHUNK * CHUNK + CHUNK + tk - 1) // tk)
    last = n_kt - 1

    fold = _fold

    iqt = iqt_ref[...]
    iqt_h = [iqt[h * IDX_DIM:(h + 1) * IDX_DIM, :] for h in range(N_IDX_HEADS)]
    iwt = iwt_ref[...]
    iw_h = [iwt[h:h + 1, :] for h in range(N_IDX_HEADS)]

    def score_tile(kt, masked):
        ik = ik_ref[kt]
        s = None
        for h in range(N_IDX_HEADS):
            t = jnp.maximum(jnp.dot(ik, iqt_h[h], preferred_element_type=F32), 0.0) * iw_h[h]
            s = t if s is None else s + t
        s = s + 0.0
        bits = pltpu.bitcast(s, jnp.int32)
        key = bits ^ ((bits >> 31) & 0x7FFFFFFF)
        top = pltpu.bitcast(bits & -65536, F32)
        if masked:
            kp = kt * tk + lax.broadcasted_iota(jnp.int32, (tk, tq), 0)
            qp = q0 + lax.broadcasted_iota(jnp.int32, (tk, tq), 1)
            adm = ((kp >> CHUNK_SHIFT) <= (qp >> CHUNK_SHIFT)) & (kp < n_keys)
            key = jnp.where(adm, key, INT_MIN)
            top = jnp.where(adm, top, -jnp.inf)
        key_scr[kt] = key
        part_scr[kt] = top.astype(BF16)

    def score_body(i, carry):
        for j in range(SCORE_TRIP):
            score_tile(SCORE_TRIP * i + j, False)
        return carry

    lax.fori_loop(0, last // SCORE_TRIP, score_body, 0)
    for j in range(SCORE_TRIP - 1):
        @pl.when(j < last % SCORE_TRIP)
        def _():
            score_tile(last // SCORE_TRIP * SCORE_TRIP + j, False)

    score_tile(last, True)

    thr = _select_top_k(key_scr, part_scr, n_kt, tk=tk, tq=tq, top_k=top_k)

    qt = qt_ref[...]
    slab_row = lax.broadcasted_iota(jnp.int32, (LANES, tq), 0)
    qz = []
    for h in range(N_HEADS):
        slab = qt[(h // 2) * LANES:(h // 2 + 1) * LANES, :]
        mine = (slab_row < HEAD_DIM) if h % 2 == 0 else (slab_row >= HEAD_DIM)
        qz.append(jnp.where(mine, slab, jnp.zeros_like(slab)))
    m_scr[...] = jnp.full(m_scr.shape, NEG, F32)
    acc_scr[...] = jnp.zeros(acc_scr.shape, F32)

    def park(kt, slot, heads, madd):
        for h in heads:
            pair = h // 2
            s = jnp.dot(k_ref[kt, :, pair * LANES:(pair + 1) * LANES], qz[h], preferred_element_type=F32)
            s_scr[slot, h] = s + madd

    def absorb(kt, slot, bias_idx, heads):
        for h in heads:
            s = s_scr[slot, h]
            if bias_idx is not None:
                s = s + bias_ref[bias_idx, h]
            m_new = jnp.maximum(m_scr[h], jnp.max(fold(s, jnp.max), axis=0, keepdims=True))
            alpha = jnp.exp2(m_scr[h] - m_new)
            p = jnp.exp2(s - m_new)
            pv = jnp.dot(vt_ref[kt, h * V_ROWS:(h + 1) * V_ROWS, :], p.astype(BF16), preferred_element_type=F32)
            acc_scr[h] = alpha * acc_scr[h] + pv
            m_scr[h] = m_new

    all_heads = range(N_HEADS)
    head_groups = (range(0, N_HEADS // 2), range(N_HEADS // 2, N_HEADS))

    def step(kt_absorb, slot, bias_idx, kt_park):
        if kt_park is None:
            absorb(kt_absorb, slot, bias_idx, all_heads)
            return
        madd = jnp.where(key_scr[kt_park] >= thr, 0.0, NEG)
        for hs in head_groups:
            park(kt_park, 1 - slot, hs, madd)
            absorb(kt_absorb, slot, bias_idx, hs)

    park(0, 0, all_heads, jnp.where(key_scr[0] >= thr, 0.0, NEG))
    n_trips = (n_kt - 2) // 2

    def far_body(i, carry):
        kt = 2 * i
        step(kt, 0, None, kt + 1)
        step(kt + 1, 1, None, kt + 2)
        return carry

    lax.fori_loop(0, n_trips, far_body, 0)
    done = 2 * jnp.maximum(n_trips, 0)

    @pl.when(n_kt == 1)
    def _():
        step(0, 0, 1, None)

    @pl.when((n_kt >= 2) & (n_kt % 2 == 0))
    def _():
        step(done, 0, 0, done + 1)
        step(done + 1, 1, 1, None)

    @pl.when((n_kt >= 3) & (n_kt % 2 == 1))
    def _():
        step(done, 0, None, done + 1)
        step(done + 1, 1, 0, done + 2)
        step(done + 2, 0, 1, None)

    for h in range(N_HEADS):
        acc = acc_scr[h]
        o_ref[h * HEAD_DIM:(h + 1) * HEAD_DIM, :] = (acc[:HEAD_DIM] / acc[HEAD_DIM:HEAD_DIM + 1]).astype(o_ref.dtype)


def _rel_bucket(rel):
    nb = NUM_BUCKETS // 2
    max_exact = nb // 2
    ret = jnp.where(rel > 0, nb, 0)
    n = jnp.abs(rel)
    nf = jnp.maximum(n, 1).astype(jnp.float32)
    large = max_exact + (jnp.log(nf / max_exact) / math.log(MAX_DISTANCE / max_exact) * (nb - max_exact)).astype(jnp.int32)
    large = jnp.minimum(large, nb - 1)
    return ret + jnp.where(n < max_exact, n, large)


def _bias_tiles(rel_bias, tq, tk):
    j = jnp.arange(tk, dtype=jnp.int32)[:, None]
    i = jnp.arange(tq, dtype=jnp.int32)[None, :]
    return jnp.stack([_shifted_bias(rel_bias, j - i + d) for d in (-tk, 0)])


def _shifted_bias(rel_bias, rel):
    rb = (rel_bias.astype(F32) - rel_bias[_rel_bucket(jnp.int32(-MAX_DISTANCE))].astype(F32)) * LOG2E
    onehot = jax.nn.one_hot(_rel_bucket(rel), NUM_BUCKETS, dtype=F32)
    return jnp.moveaxis(jnp.einsum("...b,bh->...h", onehot, rb, precision=lax.Precision.HIGHEST), -1, 0)


def _check_tiling(t, tq, tk, nkt, pos0, n_keys):
    for qi in range(t // tq):
        q0 = pos0 + qi * tq
        n_kt = min(nkt, ((q0 + tq - 1) // CHUNK * CHUNK + CHUNK + tk - 1) // tk)
        adm_end = min(q0 // CHUNK * CHUNK + CHUNK, n_keys)
        assert (n_kt - 1) * tk <= adm_end, "only the last visited tile may hold inadmissible keys"
        assert n_kt * tk >= min((q0 + tq - 1) // CHUNK * CHUNK + CHUNK, n_keys), "visited tiles cover every admissible key"
        assert (n_kt - 1) * tk == q0, "last tile starts with the query tile"
        assert tk >= MAX_DISTANCE, "tiles before the last two are at least MAX_DISTANCE behind"


def _sparse_attn(qt, iqt, iwt, k_b, vt_t, ik_b, rel_bias, pos0):
    b, _, t = qt.shape
    n_keys = k_b.shape[1]
    top_k = min(TOPK_MAX, n_keys // 4)
    tk = KEY_TILE
    tq = min(tk, t)
    nkt = n_keys // tk
    assert n_keys % tk == 0 and vt_t.shape == (b, nkt, N_HEADS * V_ROWS, tk)
    _check_tiling(t, tq, tk, nkt, pos0, n_keys)
    k_t = k_b.reshape(b, nkt, tk, ATTN_W)
    ik_t = ik_b.reshape(b, nkt, tk, IDX_DIM)
    bias = _bias_tiles(rel_bias, tq, tk)

    qspec = lambda n: pl.BlockSpec((None, n, tq), lambda bi, qi: (bi, 0, qi))
    kspec = lambda r, c: pl.BlockSpec((None, nkt, r, c), lambda bi, qi: (bi, 0, 0, 0))
    kern = functools.partial(_sparse_attn_kernel, tq=tq, tk=tk, nkt=nkt, pos0=pos0, n_keys=n_keys, top_k=top_k)
    return pl.pallas_call(
        kern,
        grid=(b, t // tq),
        in_specs=[qspec(ATTN_W), qspec(ATTN_W), qspec(N_IDX_HEADS),
                  kspec(tk, ATTN_W), kspec(N_HEADS * V_ROWS, tk), kspec(tk, IDX_DIM), _const_spec(bias.shape)],
        out_specs=qspec(ATTN_W),
        out_shape=jax.ShapeDtypeStruct((b, ATTN_W, t), BF16),
        scratch_shapes=[pltpu.VMEM((nkt + SEARCH_TRIP - 1, tk, tq), jnp.int32),
                        pltpu.VMEM((nkt + SEARCH_TRIP - 1, tk, tq), BF16),
                        pltpu.VMEM((N_HEADS, 1, tq), F32),
                        pltpu.VMEM((N_HEADS, V_ROWS, tq), F32),
                        pltpu.VMEM((2, N_HEADS, tk, tq), F32)],
        compiler_params=pltpu.CompilerParams(dimension_semantics=("arbitrary", "arbitrary"),
                                             vmem_limit_bytes=VMEM_LIMIT),
        name="sparse_attn",
    )(qt, iqt, iwt, k_t, vt_t, ik_t, bias)


DEC_ROWS = N_HEADS * 16
DEC_KEY_TILE = 1024
NT_DIMS = (((1,), (1,)), ((), ()))


def _dec_select_kernel(iq_ref, iwc_ref, ikt_ref, iktn_ref, madd_ref, key_scr, *, g, tq, n_keys, n_pad, top_k):
    n_slabs = n_pad // LANES
    col = lax.broadcasted_iota(jnp.int32, (tq, n_pad), 1)
    for s in range(g):
        iq_s = iq_ref[s * tq:(s + 1) * tq, :]
        lhs = jnp.concatenate([iq_s[:, h * IDX_DIM:(h + 1) * IDX_DIM] for h in range(N_IDX_HEADS)], axis=0)
        logit = jnp.concatenate([jnp.dot(lhs, ikt_ref[s].astype(BF16), preferred_element_type=F32),
                                 jnp.dot(lhs, iktn_ref[s], preferred_element_type=F32)], axis=1)
        x = jnp.maximum(logit, 0.0) * iwc_ref[s]
        score = x[0:tq]
        for h in range(1, N_IDX_HEADS):
            score = score + x[h * tq:(h + 1) * tq]
        score = score + 0.0
        bits = pltpu.bitcast(score, jnp.int32)
        key = bits ^ ((bits >> 31) & 0x7FFFFFFF)
        key_scr[s * tq:(s + 1) * tq, :] = jnp.where(col < n_keys, key, INT_MIN)

    rows = g * tq
    lane = lax.broadcasted_iota(jnp.int32, (rows, LANES), 1)

    def count(pred):
        acc = jnp.zeros((rows, LANES), F32)
        for j in range(n_slabs):
            acc = acc + jnp.where(pred(key_scr[:, j * LANES:(j + 1) * LANES], j), 1.0, 0.0)
        return jnp.sum(acc, axis=1, keepdims=True)

    def thr_body(i, t):
        cand = t + (jnp.int32(1) << (31 - i))
        c = count(lambda key, j: key >= cand)
        return jnp.where(c >= top_k, cand, t)

    thr = lax.fori_loop(0, 32, thr_body, jnp.full((rows, 1), INT_MIN, jnp.int32))
    thr = jnp.maximum(thr, INT_MIN + 1)

    n_ge = count(lambda key, j: key >= thr)
    has_tie = jnp.max(n_ge) > top_k

    @pl.when(has_tie)
    def _():
        need_m1 = (top_k - 1) - count(lambda key, j: key > thr)

        def idx_body(i, cut):
            cand = cut + (jnp.int32(1) << (15 - i))
            c = count(lambda key, j: (key == thr) & (j * LANES + lane < cand))
            return jnp.where(c <= need_m1, cand, cut)

        cut = lax.fori_loop(0, 16, idx_body, jnp.zeros((rows, 1), jnp.int32))
        tie_row = n_ge > top_k
        for j in range(n_slabs):
            key = key_scr[:, j * LANES:(j + 1) * LANES]
            drop = tie_row & (key == thr) & (j * LANES + lane > cut)
            key_scr[:, j * LANES:(j + 1) * LANES] = jnp.where(drop, INT_MIN, key)

    madd_ref[...] = jnp.where(key_scr[...] >= thr, 0.0, NEG)


def _dec_attn_kernel(q_ref, maddc_ref, maddn_ref, k_ref, v_ref, kn_ref, vn_ref, biasc_ref, biasn_ref, o_ref,
                     m_scr, l_scr, acc_scr, *, tq):
    kt = pl.program_id(1)
    q = q_ref[...]
    q_h = [q[:, h * HEAD_DIM:(h + 1) * HEAD_DIM] for h in range(N_HEADS)]

    @pl.when(kt == 0)
    def _():
        m_scr[...] = jnp.full(m_scr.shape, NEG, F32)
        l_scr[...] = jnp.zeros(l_scr.shape, F32)
        acc_scr[...] = jnp.zeros(acc_scr.shape, F32)

    def step(ktr, vtr, madd, bias):
        s = jnp.concatenate([jnp.dot(q_h[h], ktr[h].astype(BF16), preferred_element_type=F32)
                             for h in range(N_HEADS)], axis=0)
        s = s + jnp.concatenate([madd] * N_HEADS, axis=0) + bias
        m_prev = m_scr[...]
        m_new = jnp.maximum(m_prev, jnp.max(s, axis=1, keepdims=True))
        alpha = jnp.exp2(m_prev - m_new)
        p = jnp.exp2(s - m_new)
        l_scr[...] = alpha * l_scr[...] + jnp.sum(p, axis=1, keepdims=True)
        pb = p.astype(BF16)
        pv = jnp.concatenate([lax.dot_general(pb[h * tq:(h + 1) * tq, :], vtr[h].astype(BF16), NT_DIMS,
                                              preferred_element_type=F32) for h in range(N_HEADS)], axis=0)
        acc_scr[...] = alpha * acc_scr[...] + pv
        m_scr[...] = m_new

    step(k_ref, v_ref, maddc_ref[...], biasc_ref[...])

    @pl.when(kt == pl.num_programs(1) - 1)
    def _():
        step(kn_ref, vn_ref, maddn_ref[...], biasn_ref[...])
        o = acc_scr[...] / l_scr[...]
        for h in range(N_HEADS):
            o_ref[:, h * HEAD_DIM:(h + 1) * HEAD_DIM] = o[h * tq:(h + 1) * tq, :].astype(o_ref.dtype)


def _dec_attn(q, iq, iw, past_k, past_v, past_ik, k_new, v_new, ik_new, rel_bias, pos0):
    b, t, _ = q.shape
    p_len = past_k.shape[1]
    n_keys = p_len + t
    top_k = min(TOPK_MAX, n_keys // 4)
    tkk = min(DEC_KEY_TILE, p_len)
    n_pad = p_len + LANES
    g = min(LANES // t, b)
    assert N_HEADS * t == DEC_ROWS and b % g == 0 and p_len % tkk == 0 and t <= LANES
    assert (pos0 + t - 1) // CHUNK == pos0 // CHUNK and (n_keys - 1) // CHUNK <= pos0 // CHUNK, "every key admissible"
    assert tkk >= MAX_DISTANCE + LANES >= MAX_DISTANCE + t, "only the last cache tile is within MAX_DISTANCE of a query"

    ikt_new = jnp.pad(ik_new.swapaxes(1, 2), ((0, 0), (0, 0), (0, LANES - t)))
    iw_col = iw.swapaxes(1, 2).reshape(b, DEC_ROWS, 1)
    madd = pl.pallas_call(
        functools.partial(_dec_select_kernel, g=g, tq=t, n_keys=n_keys, n_pad=n_pad, top_k=top_k),
        grid=(b // g,),
        in_specs=[pl.BlockSpec((g * t, ATTN_W), lambda i: (i, 0)),
                  pl.BlockSpec((g, DEC_ROWS, 1), lambda i: (i, 0, 0)),
                  pl.BlockSpec((g, IDX_DIM, p_len), lambda i: (i, 0, 0)),
                  pl.BlockSpec((g, IDX_DIM, LANES), lambda i: (i, 0, 0))],
        out_specs=pl.BlockSpec((g * t, n_pad), lambda i: (i, 0)),
        out_shape=jax.ShapeDtypeStruct((b * t, n_pad), F32),
        scratch_shapes=[pltpu.VMEM((g * t, n_pad), jnp.int32)],
        compiler_params=pltpu.CompilerParams(dimension_semantics=("arbitrary",), vmem_limit_bytes=VMEM_LIMIT),
        name="dec_select",
    )(iq.reshape(b * t, ATTN_W), iw_col, past_ik.swapaxes(1, 2), ikt_new)

    n_steps = p_len // tkk
    qpos = jnp.tile(jnp.arange(t, dtype=jnp.int32), N_HEADS)[:, None]
    hsel = jnp.repeat(jnp.arange(N_HEADS), t)
    rows = jnp.arange(DEC_ROWS)
    near = MAX_DISTANCE + LANES
    rel_c = jnp.arange(near, dtype=jnp.int32)[None, :] - near - qpos
    rel_n = jnp.arange(LANES, dtype=jnp.int32)[None, :] - qpos
    bias_last = jnp.pad(_shifted_bias(rel_bias, rel_c)[hsel, rows], ((0, 0), (tkk - near, 0)))
    bias_c = jnp.concatenate([jnp.zeros((n_steps - 1, DEC_ROWS, tkk), F32), bias_last[None]], axis=0)
    bias_n = _shifted_bias(rel_bias, rel_n)[hsel, rows]
    heads_t = lambda a: a.transpose(0, 2, 3, 1)
    pad_new = lambda a: jnp.pad(heads_t(a.reshape(b, t, N_HEADS, HEAD_DIM)), ((0, 0), (0, 0), (0, 0), (0, LANES - t)))
    kv_tile = pl.BlockSpec((None, N_HEADS, HEAD_DIM, tkk), lambda bi, kt: (bi, 0, 0, kt))
    kv_new = pl.BlockSpec((None, N_HEADS, HEAD_DIM, LANES), lambda bi, kt: (bi, 0, 0, 0))
    return pl.pallas_call(
        functools.partial(_dec_attn_kernel, tq=t),
        grid=(b, n_steps),
        in_specs=[pl.BlockSpec((None, t, ATTN_W), lambda bi, kt: (bi, 0, 0)),
                  pl.BlockSpec((t, tkk), lambda bi, kt: (bi, kt)),
                  pl.BlockSpec((t, LANES), lambda bi, kt: (bi, p_len // LANES)),
                  kv_tile, kv_tile, kv_new, kv_new,
                  pl.BlockSpec((None, DEC_ROWS, tkk), lambda bi, kt: (kt, 0, 0)),
                  _const_spec(bias_n.shape)],
        out_specs=pl.BlockSpec((None, t, ATTN_W), lambda bi, kt: (bi, 0, 0)),
        out_shape=jax.ShapeDtypeStruct((b, t, ATTN_W), BF16),
        scratch_shapes=[pltpu.VMEM((DEC_ROWS, 1), F32), pltpu.VMEM((DEC_ROWS, 1), F32),
                        pltpu.VMEM((DEC_ROWS, HEAD_DIM), F32)],
        compiler_params=pltpu.CompilerParams(dimension_semantics=("arbitrary", "arbitrary"),
                                             vmem_limit_bytes=VMEM_LIMIT),
        name="dec_attn",
    )(q, madd, madd, heads_t(past_k), heads_t(past_v), pad_new(k_new), pad_new(v_new), bias_c, bias_n)


def _conv_kernel(*refs, tm, has_prev):
    if has_prev:
        u_ref, prev_ref, past_ref, w_ref, b_ref, lg_ref, lb_ref, o_ref, ext_scr = refs
        halo = jnp.where(pl.program_id(1) == 0, past_ref[...], prev_ref[...])
    else:
        u_ref, past_ref, w_ref, b_ref, lg_ref, lb_ref, o_ref, ext_scr = refs
        halo = past_ref[...]
    _conv_act(halo, u_ref[...], w_ref, b_ref, lg_ref, lb_ref, o_ref, ext_scr)


def _conv_act(halo, u, w_ref, b_ref, lg_ref, lb_ref, o_ref, ext_scr):
    tm = u.shape[0]
    ext_scr[0, 0:HALO, :] = halo
    ext_scr[0, HALO:HALO + tm, :] = u
    n_rows = tm + HALO - SUBLANES
    for r in range(1, SUBLANES):
        ext_scr[r, 0:n_rows, :] = ext_scr[0, r:r + n_rows, :]
    rb = min(tm, CONV_ROWS)
    first = HALO - (CONV_K - 1)
    for r0 in range(0, tm, rb):
        acc = None
        for j in range(CONV_K):
            a, r = divmod(first + j, SUBLANES)
            t = ext_scr[r, r0 + a * SUBLANES:r0 + a * SUBLANES + rb, :] * w_ref[j:j + 1, :]
            acc = t if acc is None else acc + t
        c = acc + b_ref[...]
        mu = jnp.mean(c, axis=-1, keepdims=True)
        cc = c - mu
        var = jnp.mean(cc * cc, axis=-1, keepdims=True)
        y = cc * lax.rsqrt(var + EPS) * lg_ref[...] + lb_ref[...]
        o_ref[r0:r0 + rb, :] = (y * jax.nn.sigmoid(y)).astype(o_ref.dtype)


def _conv_mod(u, past, w):
    b, t, c = u.shape
    tm = min(ROW_TILE, t)
    assert t % tm == 0 and tm % 8 == 0
    has_prev = t > tm
    tile = pl.BlockSpec((None, tm, c), lambda bi, i: (bi, i, 0))
    halo = pl.BlockSpec((None, HALO, c), lambda bi, i: (bi, 0, 0))
    in_specs = [tile]
    args = [u]
    if has_prev:
        r = tm // HALO
        in_specs.append(pl.BlockSpec((None, HALO, c), lambda bi, i: (bi, jnp.maximum(i * r - 1, 0), 0)))
        args.append(u)
    consts = [w["conv_w"], w["conv_b"], w["ln_g"], w["ln_b"]]
    in_specs += [halo] + [_const_spec(x.shape) for x in consts]
    args += [past] + consts
    return pl.pallas_call(
        functools.partial(_conv_kernel, tm=tm, has_prev=has_prev),
        grid=(b, t // tm),
        in_specs=in_specs,
        out_specs=tile,
        out_shape=jax.ShapeDtypeStruct((b, t, c), BF16),
        scratch_shapes=[pltpu.VMEM((SUBLANES, HALO + tm, c), F32)],
        compiler_params=pltpu.CompilerParams(dimension_semantics=("arbitrary", "arbitrary"),
                                             vmem_limit_bytes=VMEM_LIMIT),
        name="conv_mod",
    )(*args)


def _mix_ffn_conv_kernel(x_ref, attn_ref, u0_ref, unext_ref, utail_ref, past0_ref, pastn_ref, sga_ref, sgc_ref,
                         wao_ref, wco_ref, wo_ref, g_ref, wg_ref, wu_ref, wd_ref, cw_ref, cb_ref, lg_ref, lb_ref,
                         y_ref, cact_scr, ext_scr, *, n_chunks, tiles_per_seq):
    i = pl.program_id(0)

    @pl.when(i == 0)
    def _():
        _conv_act(past0_ref[...], u0_ref[...], cw_ref, cb_ref, lg_ref, lb_ref, cact_scr.at[0], ext_scr)

    _mix_ffn_kernel(x_ref, attn_ref, cact_scr.at[i % 2], sga_ref, sgc_ref, wao_ref, wco_ref, wo_ref, g_ref,
                    wg_ref, wu_ref, wd_ref, y_ref, n_chunks=n_chunks)
    halo = jnp.where((i + 1) % tiles_per_seq == 0, pastn_ref[...], utail_ref[...])
    _conv_act(halo, unext_ref[...], cw_ref, cb_ref, lg_ref, lb_ref, cact_scr.at[(i + 1) % 2], ext_scr)


def _mix_ffn_kernel(x_ref, attn_ref, cact_ref, sga_ref, sgc_ref, wao_ref, wco_ref, wo_ref, g_ref,
                    wg_ref, wu_ref, wd_ref, y_ref, *, n_chunks):
    attn_out = jnp.dot(attn_ref[...], wao_ref[...], preferred_element_type=F32)
    conv_out = jnp.dot(cact_ref[...], wco_ref[...], preferred_element_type=F32)
    merged = sga_ref[...].astype(F32) * attn_out + sgc_ref[...].astype(F32) * conv_out
    x1 = x_ref[...] + jnp.dot(merged.astype(BF16), wo_ref[...], preferred_element_type=F32)
    ms = jnp.mean(x1 * x1, axis=-1, keepdims=True)
    h2 = (x1 * lax.rsqrt(ms + EPS) * g_ref[...]).astype(BF16)
    hc = wg_ref.shape[1] // n_chunks
    y = x1
    for ci in range(n_chunks):
        sl = slice(ci * hc, (ci + 1) * hc)
        gate = jnp.dot(h2, wg_ref[:, sl], preferred_element_type=F32)
        up = jnp.dot(h2, wu_ref[:, sl], preferred_element_type=F32)
        act = (gate * jax.nn.sigmoid(gate) * up).astype(BF16)
        y = y + jnp.dot(act, wd_ref[sl, :], preferred_element_type=F32)
    y_ref[...] = y


def _mix_ffn_conv(x, attn, u, past, sga, sgc, w, seq_len):
    m, d = x.shape
    c = u.shape[1]
    tm = min(ROW_TILE, seq_len)
    assert m % tm == 0 and seq_len % tm == 0 and tm % HALO == 0
    nt, n = seq_len // tm, m // tm
    hidden = w["wg"].shape[1]
    n_chunks = 2 if hidden % (2 * LANES) == 0 else 1
    row = lambda k: pl.BlockSpec((tm, k), lambda i: (i, 0))
    nxt = lambda i: jnp.minimum(i + 1, n - 1)
    consts = [w[k] for k in ("wao", "wco", "wo", "g_ffn", "wg", "wu", "wd", "conv_w", "conv_b", "ln_g", "ln_b")]
    return pl.pallas_call(
        functools.partial(_mix_ffn_conv_kernel, n_chunks=n_chunks, tiles_per_seq=nt),
        grid=(n,),
        in_specs=[row(d), row(attn.shape[1]),
                  pl.BlockSpec((tm, c), lambda i: (0, 0)),
                  pl.BlockSpec((tm, c), lambda i: (nxt(i), 0)),
                  pl.BlockSpec((HALO, c), lambda i: ((i + 1) * (tm // HALO) - 1, 0)),
                  pl.BlockSpec((None, HALO, c), lambda i: (0, 0, 0)),
                  pl.BlockSpec((None, HALO, c), lambda i: (nxt(i) // nt, 0, 0)),
                  row(d), row(d)] + [_const_spec(k.shape) for k in consts],
        out_specs=row(d),
        out_shape=jax.ShapeDtypeStruct((m, d), F32),
        scratch_shapes=[pltpu.VMEM((2, tm, c), BF16), pltpu.VMEM((SUBLANES, HALO + tm, c), F32)],
        compiler_params=pltpu.CompilerParams(dimension_semantics=("arbitrary",), vmem_limit_bytes=VMEM_LIMIT),
        name="mix_ffn_conv",
    )(x, attn, u, u, u, past, past, sga, sgc, *consts)


def _mix_ffn(x, attn, cact, sga, sgc, w):
    m, d = x.shape
    tm = min(ROW_TILE, m)
    assert m % tm == 0
    hidden = w["wg"].shape[1]
    n_chunks = 2 if hidden % (2 * LANES) == 0 else 1
    row = lambda n: pl.BlockSpec((tm, n), lambda i: (i, 0))
    consts = [w[n] for n in ("wao", "wco", "wo", "g_ffn", "wg", "wu", "wd")]
    return pl.pallas_call(
        functools.partial(_mix_ffn_kernel, n_chunks=n_chunks),
        grid=(m // tm,),
        in_specs=[row(d), row(attn.shape[1]), row(cact.shape[1]), row(d), row(d)]
                 + [_const_spec(c.shape) for c in consts],
        out_specs=row(d),
        out_shape=jax.ShapeDtypeStruct((m, d), F32),
        compiler_params=pltpu.CompilerParams(dimension_semantics=("arbitrary",), vmem_limit_bytes=VMEM_LIMIT),
        name="mix_ffn",
    )(x, attn, cact, sga, sgc, *consts)


def _prep_weights(norm_mix_g, w_in, q_norm_g, k_norm_g, idx_k_norm_g, conv_dw_w, conv_dw_b, conv_ln_g, conv_ln_b,
                  w_conv_out, w_attn_out, w_out, norm_ffn_g, w_ffn_gate, w_ffn_up, w_ffn_down):
    d = w_in.shape[0]
    conv_ch = conv_dw_w.shape[-1]
    sizes = (ATTN_W, ATTN_W, ATTN_W, N_IDX_HEADS * IDX_DIM, IDX_DIM, N_IDX_HEADS, 2 * conv_ch, d, d)
    offs = np.concatenate([[0], np.cumsum(sizes)])
    assert offs[-1] == w_in.shape[1]
    col = lambda i: w_in[:, offs[i]:offs[i + 1]]
    wsm = jnp.concatenate([col(4), col(5), jnp.zeros((d, LANES - IDX_DIM - N_IDX_HEADS), w_in.dtype)], axis=1)
    head = np.arange(ATTN_W) // HEAD_DIM
    seg = (head[:, None] == np.arange(LANES)[None, :]).astype(np.float32)
    return dict(
        g_mix=norm_mix_g.reshape(1, d).astype(F32),
        wq=col(0).astype(BF16), wk=col(1).astype(BF16), wv=col(2).astype(BF16), wiq=col(3).astype(BF16),
        wsm=wsm.astype(BF16), wglu=col(6).astype(BF16), wga=col(7).astype(BF16), wgc=col(8).astype(BF16),
        qg=jnp.tile(q_norm_g.astype(F32), N_HEADS).reshape(1, ATTN_W),
        kg=jnp.tile(k_norm_g.astype(F32), N_HEADS).reshape(1, ATTN_W),
        ikg=jnp.concatenate([idx_k_norm_g.astype(F32), jnp.zeros((LANES - IDX_DIM,), F32)]).reshape(1, LANES),
        seg=jnp.asarray(seg, BF16), segt=jnp.asarray(seg.T, BF16),
        conv_w=conv_dw_w.reshape(CONV_K, conv_ch).astype(F32), conv_b=conv_dw_b.reshape(1, conv_ch).astype(F32),
        ln_g=conv_ln_g.reshape(1, conv_ch).astype(F32), ln_b=conv_ln_b.reshape(1, conv_ch).astype(F32),
        wao=w_attn_out.astype(BF16), wco=w_conv_out.astype(BF16), wo=w_out.astype(BF16),
        g_ffn=norm_ffn_g.reshape(1, d).astype(F32),
        wg=w_ffn_gate.astype(BF16), wu=w_ffn_up.astype(BF16), wd=w_ffn_down.astype(BF16),
    )


def _layer(x, pos0, past_k, past_v, past_ik, past_conv, rel_bias, w):
    b, t, d = x.shape
    m = b * t
    prompt = past_k is None
    q, k, k_b, v, v_b, iq, sm, sm_b, u, sga, sgc = _in_proj(x, w, transposed=prompt)
    conv_ch = u.shape[1]
    u = u.reshape(b, t, conv_ch)
    k_b = k_b.reshape(b, t, ATTN_W)
    ik_b = sm_b.reshape(b, t, LANES)[:, :, :IDX_DIM]
    if prompt:
        attn = _sparse_attn(q, iq, sm[:, IDX_DIM:IDX_DIM + N_IDX_HEADS, :], k_b, v_b, ik_b, rel_bias,
                            pos0).swapaxes(1, 2)
        heads = lambda a: a.reshape(b, N_HEADS, HEAD_DIM, t).transpose(0, 3, 1, 2)
        new_k, new_v, new_ik = heads(k), heads(v), sm[:, :IDX_DIM, :].swapaxes(1, 2)
    else:
        sm = sm.reshape(b, t, LANES)
        k, v = k.reshape(b, t, ATTN_W), v.reshape(b, t, ATTN_W)
        attn = _dec_attn(q.reshape(b, t, ATTN_W), iq.reshape(b, t, ATTN_W), sm[:, :, IDX_DIM:IDX_DIM + N_IDX_HEADS],
                         past_k, past_v, past_ik, k, v, ik_b, rel_bias, pos0)
        heads = lambda a: a.reshape(b, t, N_HEADS, HEAD_DIM)
        new_k, new_v, new_ik = heads(k), heads(v), sm[:, :, :IDX_DIM]

    past = jnp.pad(past_conv.astype(F32), ((0, 0), (HALO - (CONV_K - 1), 0), (0, 0)))
    if prompt:
        y = _mix_ffn_conv(x.reshape(m, d), attn.reshape(m, ATTN_W), u.reshape(m, conv_ch), past, sga, sgc, w, t)
    else:
        cact = _conv_mod(u, past, w)
        y = _mix_ffn(x.reshape(m, d), attn.reshape(m, ATTN_W), cact.reshape(m, conv_ch), sga, sgc, w)

    new_conv = jnp.concatenate([past_conv, u], axis=1)[:, -(CONV_K - 1):]
    return y.reshape(b, t, d), new_k, new_v, new_ik, new_conv


def kernel(x_prompt, x_sample, cache_k, cache_v, cache_idx_k, state_conv, rel_bias, norm_mix_g, w_in, q_norm_g, k_norm_g, idx_k_norm_g, conv_dw_w, conv_dw_b, conv_ln_g, conv_ln_b, w_conv_out, w_attn_out, w_out, norm_ffn_g, w_ffn_gate, w_ffn_up, w_ffn_down):
    depth = w_in.shape[0]
    bp = x_prompt.shape[0]
    past_len = cache_k.shape[2]
    conv_ch = conv_dw_w.shape[-1]
    yp, ys = x_prompt, x_sample
    outs_p, outs_s = [], []
    for l in range(depth):
        w = _prep_weights(norm_mix_g[l], w_in[l], q_norm_g[l], k_norm_g[l], idx_k_norm_g[l], conv_dw_w[l],
                          conv_dw_b[l], conv_ln_g[l], conv_ln_b[l], w_conv_out[l], w_attn_out[l], w_out[l],
                          norm_ffn_g[l], w_ffn_gate[l], w_ffn_up[l], w_ffn_down[l])
        zero_conv = jnp.zeros((bp, CONV_K - 1, conv_ch), yp.dtype)
        yp, *rest_p = _layer(yp, 0, None, None, None, zero_conv, rel_bias, w)
        outs_p.append(rest_p)
        ys, *rest_s = _layer(ys, past_len, cache_k[l], cache_v[l], cache_idx_k[l], state_conv[l], rel_bias, w)
        outs_s.append(rest_s)
    stack = lambda outs, i: jnp.stack([o[i] for o in outs])
    return (yp, ys, stack(outs_p, 0), stack(outs_p, 1), stack(outs_p, 2), stack(outs_p, 3),
            stack(outs_s, 0), stack(outs_s, 1), stack(outs_s, 2), stack(outs_s, 3))
```

```python
import functools
import math

import numpy as np
import jax
import jax.numpy as jnp
from jax import lax
from jax.experimental import pallas as pl
from jax.experimental.pallas import tpu as pltpu

CHUNK = 64
CHUNK_SHIFT = 6
N_HEADS = 8
HEAD_DIM = 64
ATTN_W = N_HEADS * HEAD_DIM
N_IDX_HEADS = 8
IDX_DIM = 64
TOPK_MAX = 256
CONV_K = 31
NUM_BUCKETS = 32
MAX_DISTANCE = 128
EPS = 1e-6
NEG = -1e30
INT_MIN = -(2 ** 31)
BF16_INF_BITS = 0x7F80
BF16_MIN_NORMAL_BITS = 0x0080
N_BF16_NEG = BF16_INF_BITS - BF16_MIN_NORMAL_BITS + 1
LOG2E = math.log2(math.e)

LANES = 128
SUBLANES = 8
HALO = 32
CONV_ROWS = 32
KEY_TILE = 256
SCORE_TRIP = 4
assert SCORE_TRIP in (2, 4)
SEARCH_TRIP = 2
V_ROWS = HEAD_DIM + 16
ROW_TILE = 256
V7X_VMEM_BYTES = 64 * 1024 * 1024
VMEM_LIMIT = V7X_VMEM_BYTES - 8 * 1024 * 1024

F32 = jnp.float32
BF16 = jnp.bfloat16


def _const_spec(shape):
    return pl.BlockSpec(shape, lambda *_: (0,) * len(shape), pipeline_mode=pl.Buffered(1))


def _split_dot(x, m):
    hi = x.astype(BF16)
    lo = (x - hi.astype(F32)).astype(BF16)
    return jnp.dot(hi, m, preferred_element_type=F32) + jnp.dot(lo, m, preferred_element_type=F32)


def _in_proj_kernel(x_ref, g_ref, wq_ref, wk_ref, wv_ref, wiq_ref, wsm_ref, wglu_ref, wga_ref, wgc_ref,
                    qg_ref, kg_ref, ikg_ref, seg_ref, segt_ref,
                    q_ref, k_ref, kb_ref, v_ref, vb_ref, iq_ref, sm_ref, smb_ref, u_ref, sga_ref, sgc_ref,
                    *, transposed):
    x = x_ref[...]
    ms = jnp.mean(x * x, axis=-1, keepdims=True)
    h = (x * lax.rsqrt(ms + EPS) * g_ref[...]).astype(BF16)

    def put(ref, val):
        ref[...] = (val.T if transposed else val).astype(ref.dtype)

    glu = jnp.dot(h, wglu_ref[...], preferred_element_type=F32)
    c = glu.shape[1] // 2
    u_ref[...] = glu[:, :c] * jax.nn.sigmoid(glu[:, c:])

    def head_rms(y, gain):
        ss = _split_dot(y * y, seg_ref[...])
        r = lax.rsqrt(ss * (1.0 / HEAD_DIM) + EPS)
        return y * _split_dot(r, segt_ref[...]) * gain

    q = head_rms(jnp.dot(h, wq_ref[...], preferred_element_type=F32), qg_ref[...])
    put(q_ref, q * (HEAD_DIM ** -0.5 * LOG2E))
    k = head_rms(jnp.dot(h, wk_ref[...], preferred_element_type=F32), kg_ref[...])
    put(k_ref, k)
    kb_ref[...] = k.astype(BF16)
    v = jnp.dot(h, wv_ref[...], preferred_element_type=F32)
    put(v_ref, v)
    if transposed:
        vt = v.T.astype(BF16)
        pad_rows = lax.broadcasted_iota(jnp.int32, (V_ROWS - HEAD_DIM, vt.shape[1]), 0)
        ones_row = jnp.where(pad_rows == 0, 1.0, 0.0).astype(BF16)
        for hd in range(N_HEADS):
            vb_ref[hd * V_ROWS:hd * V_ROWS + HEAD_DIM, :] = vt[hd * HEAD_DIM:(hd + 1) * HEAD_DIM, :]
            vb_ref[hd * V_ROWS + HEAD_DIM:(hd + 1) * V_ROWS, :] = ones_row
    else:
        vb_ref[...] = v.astype(BF16)
    iq = jnp.dot(h, wiq_ref[...], preferred_element_type=F32)
    put(iq_ref, iq * (IDX_DIM ** -0.5))

    sm = jnp.dot(h, wsm_ref[...], preferred_element_type=F32)
    lane = lax.broadcasted_iota(jnp.int32, sm.shape, 1)
    is_ik = lane < IDX_DIM
    ss = jnp.sum(jnp.where(is_ik, sm * sm, 0.0), axis=-1, keepdims=True)
    ikn = sm * lax.rsqrt(ss * (1.0 / IDX_DIM) + EPS) * ikg_ref[...]
    iw = sm * (N_IDX_HEADS ** -0.5)
    smo = jnp.where(is_ik, ikn, jnp.where(lane < IDX_DIM + N_IDX_HEADS, iw, 0.0))
    put(sm_ref, smo)
    smb_ref[...] = smo.astype(BF16)

    sga_ref[...] = jax.nn.sigmoid(jnp.dot(h, wga_ref[...], preferred_element_type=F32)).astype(BF16)
    sgc_ref[...] = jax.nn.sigmoid(jnp.dot(h, wgc_ref[...], preferred_element_type=F32)).astype(BF16)


def _in_proj(x, w, transposed):
    b, t, d = x.shape
    m = b * t
    tm = min(ROW_TILE, t if transposed else m)
    assert m % tm == 0 and (not transposed or t % tm == 0)
    nt = t // tm
    conv_ch = w["wglu"].shape[1] // 2
    consts = [w[n] for n in ("g_mix", "wq", "wk", "wv", "wiq", "wsm", "wglu", "wga", "wgc",
                             "qg", "kg", "ikg", "seg", "segt")]

    def rows(n, dtype):
        return jax.ShapeDtypeStruct((m, n), dtype), pl.BlockSpec((tm, n), lambda i: (i, 0))

    def cols(n, dtype):
        if not transposed:
            return rows(n, dtype)
        return jax.ShapeDtypeStruct((b, n, t), dtype), pl.BlockSpec((None, n, tm), lambda i: (i // nt, 0, i % nt))

    if transposed:
        v_slab = (jax.ShapeDtypeStruct((b, nt, N_HEADS * V_ROWS, tm), BF16),
                  pl.BlockSpec((None, None, N_HEADS * V_ROWS, tm), lambda i: (i // nt, i % nt, 0, 0)))
    else:
        v_slab = rows(ATTN_W, BF16)
    outs = [
        cols(ATTN_W, BF16),
        cols(ATTN_W, F32),
        rows(ATTN_W, BF16),
        cols(ATTN_W, F32),
        v_slab,
        cols(ATTN_W, BF16),
        cols(LANES, F32),
        rows(LANES, BF16),
        rows(conv_ch, F32),
        rows(d, BF16),
        rows(d, BF16),
    ]
    return pl.pallas_call(
        functools.partial(_in_proj_kernel, transposed=transposed),
        grid=(m // tm,),
        in_specs=[pl.BlockSpec((tm, d), lambda i: (i, 0))] + [_const_spec(c.shape) for c in consts],
        out_specs=[o[1] for o in outs],
        out_shape=[o[0] for o in outs],
        compiler_params=pltpu.CompilerParams(dimension_semantics=("arbitrary",), vmem_limit_bytes=VMEM_LIMIT),
        name="in_proj",
    )(x.reshape(m, d), *consts)


def _fold(x, op):
    return op(x.reshape(x.shape[0] // SUBLANES, SUBLANES, x.shape[1]), axis=0)


def _select_top_k(key_scr, part_scr, n_kt, *, tk, tq, top_k):
    n_trips = (n_kt + SEARCH_TRIP - 1) // SEARCH_TRIP
    for j in range(SEARCH_TRIP - 1):
        @pl.when(n_kt + j < n_trips * SEARCH_TRIP)
        def _():
            key_scr[n_kt + j] = jnp.full((tk, tq), INT_MIN, jnp.int32)
            part_scr[n_kt + j] = jnp.full((tk, tq), -jnp.inf, BF16)

    trip = lambda i: [SEARCH_TRIP * i + j for j in range(SEARCH_TRIP)]

    rows16 = 2 * SUBLANES
    one_b, zero_b = jnp.ones((), BF16), jnp.zeros((), BF16)

    def count_part(pred):
        def body(i, acc):
            for kt in trip(i):
                c = jnp.where(pred(part_scr[kt]), one_b, zero_b)
                parts = [c[j * rows16:(j + 1) * rows16] for j in range(4)]
                for j in range(4, tk // rows16):
                    parts[j % 4] = parts[j % 4] + c[j * rows16:(j + 1) * rows16]
                acc = acc + ((parts[0] + parts[1]) + (parts[2] + parts[3])).astype(F32)
            return acc
        acc = lax.fori_loop(0, n_trips, body, jnp.zeros((rows16, tq), F32))
        return jnp.sum(acc, axis=0, keepdims=True)

    def search(n_bits, base, to_image, limit):
        def body(i, t):
            cand = t + (jnp.int32(1) << (n_bits - 1 - i))
            img = to_image(cand)
            c = base + count_part(lambda a: a >= img)
            return jnp.where((c >= top_k) & (cand <= limit), cand, t)
        return lax.fori_loop(0, n_bits, body, jnp.zeros((1, tq), jnp.int32))

    def refine(eq_img, shift):
        def body(i, carry):
            for kt in trip(i):
                nxt = ((key_scr[kt] >> shift) & 0xFF).astype(F32).astype(BF16)
                part_scr[kt] = jnp.where(part_scr[kt] == eq_img, nxt, -one_b)
            return carry
        lax.fori_loop(0, n_trips, body, 0)

    def top_bits(c):
        neg = 0x8000 | (BF16_INF_BITS - c)
        pos = BF16_MIN_NORMAL_BITS + (c - N_BF16_NEG - 1)
        return jnp.where(c < N_BF16_NEG, neg, jnp.where(c == N_BF16_NEG, 0, pos))

    def top_image(c):
        return pltpu.bitcast(top_bits(c) << 16, F32).astype(BF16)

    small_image = lambda c: c.astype(F32).astype(BF16)

    t1 = search(16, 0.0, top_image, 2 * N_BF16_NEG)
    img1 = top_image(t1)
    above = count_part(lambda a: a > img1)
    refine(img1, 8)
    t2 = search(8, above, small_image, 255)
    img2 = small_image(t2)
    above = above + count_part(lambda a: a > img2)
    refine(img2, 0)
    t3 = search(8, above, small_image, 255)
    img3 = small_image(t3)
    n_ge = above + count_part(lambda a: a >= img3)
    hi = top_bits(t1)
    hi = hi - ((hi >> 15) << 16)
    hi = hi ^ ((hi >> 15) & 0x7FFF)
    thr = (hi << 16) + (t2 << 8) + t3
    few = t1 == 0
    thr = jnp.where(few, INT_MIN + 1, jnp.maximum(thr, INT_MIN + 1))

    excess = jnp.where(few, 0.0, n_ge - top_k)
    has_tie = jnp.max(excess) > 0

    @pl.when(has_tie)
    def _():
        upper = (lax.broadcasted_iota(jnp.int32, (tk, tk), 1) >= lax.broadcasted_iota(jnp.int32, (tk, tk), 0))
        upper = jnp.where(upper, 1.0, 0.0).astype(BF16)

        def drop_body(i, later):
            kt = n_kt - 1 - i
            key = key_scr[kt]
            eq = key == thr
            ties_from_here = jnp.dot(upper, jnp.where(eq, 1.0, 0.0).astype(BF16), preferred_element_type=F32)
            key_scr[kt] = jnp.where(eq & (ties_from_here + later <= excess), INT_MIN, key)
            return later + ties_from_here[0:1, :]

        lax.fori_loop(0, n_kt, drop_body, jnp.zeros((1, tq), F32))
    return thr


def _sparse_attn_kernel(qt_ref, iqt_ref, iwt_ref, k_ref, vt_ref, ik_ref, bias_ref, o_ref,
                        key_scr, part_scr, m_scr, acc_scr, s_scr, *, tq, tk, nkt, pos0, n_keys, top_k):
    qi = pl.program_id(1)
    q0 = pos0 + qi * tq
    n_kt = jnp.minimum(nkt, ((q0 + tq - 1) // CHUNK * CHUNK + CHUNK + tk - 1) // tk)
    last = n_kt - 1

    fold = _fold

    iqt = iqt_ref[...]
    iqt_h = [iqt[h * IDX_DIM:(h + 1) * IDX_DIM, :] for h in range(N_IDX_HEADS)]
    iwt = iwt_ref[...]
    iw_h = [iwt[h:h + 1, :] for h in range(N_IDX_HEADS)]

    def score_tile(kt, masked):
        ik = ik_ref[kt]
        s = None
        for h in range(N_IDX_HEADS):
            t = jnp.maximum(jnp.dot(ik, iqt_h[h], preferred_element_type=F32), 0.0) * iw_h[h]
            s = t if s is None else s + t
        s = s + 0.0
        bits = pltpu.bitcast(s, jnp.int32)
        key = bits ^ ((bits >> 31) & 0x7FFFFFFF)
        top = pltpu.bitcast(bits & -65536, F32)
        if masked:
            kp = kt * tk + lax.broadcasted_iota(jnp.int32, (tk, tq), 0)
            qp = q0 + lax.broadcasted_iota(jnp.int32, (tk, tq), 1)
            adm = ((kp >> CHUNK_SHIFT) <= (qp >> CHUNK_SHIFT)) & (kp < n_keys)
            key = jnp.where(adm, key, INT_MIN)
            top = jnp.where(adm, top, -jnp.inf)
        key_scr[kt] = key
        part_scr[kt] = top.astype(BF16)

    def score_body(i, carry):
        for j in range(SCORE_TRIP):
            score_tile(SCORE_TRIP * i + j, False)
        return carry

    lax.fori_loop(0, last // SCORE_TRIP, score_body, 0)
    left, rest = last // SCORE_TRIP * SCORE_TRIP, last % SCORE_TRIP

    @pl.when(rest >= 2)
    def _():
        score_tile(left, False)
        score_tile(left + 1, False)

    @pl.when(rest % 2 == 1)
    def _():
        score_tile(last - 1, False)

    score_tile(last, True)

    thr = _select_top_k(key_scr, part_scr, n_kt, tk=tk, tq=tq, top_k=top_k)

    qt = qt_ref[...]
    slab_row = lax.broadcasted_iota(jnp.int32, (LANES, tq), 0)
    qz = []
    for h in range(N_HEADS):
        slab = qt[(h // 2) * LANES:(h // 2 + 1) * LANES, :]
        mine = (slab_row < HEAD_DIM) if h % 2 == 0 else (slab_row >= HEAD_DIM)
        qz.append(jnp.where(mine, slab, jnp.zeros_like(slab)))
    m_scr[...] = jnp.full(m_scr.shape, NEG, F32)
    acc_scr[...] = jnp.zeros(acc_scr.shape, F32)

    def park(kt, slot, heads, madd):
        for h in heads:
            pair = h // 2
            s = jnp.dot(k_ref[kt, :, pair * LANES:(pair + 1) * LANES], qz[h], preferred_element_type=F32)
            s_scr[slot, h] = s + madd

    def absorb(kt, slot, bias_idx, heads):
        for h in heads:
            s = s_scr[slot, h]
            if bias_idx is not None:
                s = s + bias_ref[bias_idx, h]
            m_new = jnp.maximum(m_scr[h], jnp.max(fold(s, jnp.max), axis=0, keepdims=True))
            alpha = jnp.exp2(m_scr[h] - m_new)
            p = jnp.exp2(s - m_new)
            pv = jnp.dot(vt_ref[kt, h * V_ROWS:(h + 1) * V_ROWS, :], p.astype(BF16), preferred_element_type=F32)
            acc_scr[h] = alpha * acc_scr[h] + pv
            m_scr[h] = m_new

    all_heads = range(N_HEADS)
    head_groups = (range(0, N_HEADS // 2), range(N_HEADS // 2, N_HEADS))

    def step(kt_absorb, slot, bias_idx, kt_park):
        if kt_park is None:
            absorb(kt_absorb, slot, bias_idx, all_heads)
            return
        madd = jnp.where(key_scr[kt_park] >= thr, 0.0, NEG)
        for hs in head_groups:
            park(kt_park, 1 - slot, hs, madd)
            absorb(kt_absorb, slot, bias_idx, hs)

    park(0, 0, all_heads, jnp.where(key_scr[0] >= thr, 0.0, NEG))
    n_trips = (n_kt - 2) // 2

    def far_body(i, carry):
        kt = 2 * i
        step(kt, 0, None, kt + 1)
        step(kt + 1, 1, None, kt + 2)
        return carry

    lax.fori_loop(0, n_trips, far_body, 0)
    done = 2 * jnp.maximum(n_trips, 0)

    @pl.when(n_kt == 1)
    def _():
        step(0, 0, 1, None)

    @pl.when((n_kt >= 2) & (n_kt % 2 == 0))
    def _():
        step(done, 0, 0, done + 1)
        step(done + 1, 1, 1, None)

    @pl.when((n_kt >= 3) & (n_kt % 2 == 1))
    def _():
        step(done, 0, None, done + 1)
        step(done + 1, 1, 0, done + 2)
        step(done + 2, 0, 1, None)

    for h in range(N_HEADS):
        acc = acc_scr[h]
        o_ref[h * HEAD_DIM:(h + 1) * HEAD_DIM, :] = (acc[:HEAD_DIM] / acc[HEAD_DIM:HEAD_DIM + 1]).astype(o_ref.dtype)


def _rel_bucket(rel):
    nb = NUM_BUCKETS // 2
    max_exact = nb // 2
    ret = jnp.where(rel > 0, nb, 0)
    n = jnp.abs(rel)
    nf = jnp.maximum(n, 1).astype(jnp.float32)
    large = max_exact + (jnp.log(nf / max_exact) / math.log(MAX_DISTANCE / max_exact) * (nb - max_exact)).astype(jnp.int32)
    large = jnp.minimum(large, nb - 1)
    return ret + jnp.where(n < max_exact, n, large)


def _bias_tiles(rel_bias, tq, tk):
    j = jnp.arange(tk, dtype=jnp.int32)[:, None]
    i = jnp.arange(tq, dtype=jnp.int32)[None, :]
    return jnp.stack([_shifted_bias(rel_bias, j - i + d) for d in (-tk, 0)])


def _shifted_bias(rel_bias, rel):
    rb = (rel_bias.astype(F32) - rel_bias[_rel_bucket(jnp.int32(-MAX_DISTANCE))].astype(F32)) * LOG2E
    onehot = jax.nn.one_hot(_rel_bucket(rel), NUM_BUCKETS, dtype=F32)
    return jnp.moveaxis(jnp.einsum("...b,bh->...h", onehot, rb, precision=lax.Precision.HIGHEST), -1, 0)


def _check_tiling(t, tq, tk, nkt, pos0, n_keys):
    for qi in range(t // tq):
        q0 = pos0 + qi * tq
        n_kt = min(nkt, ((q0 + tq - 1) // CHUNK * CHUNK + CHUNK + tk - 1) // tk)
        adm_end = min(q0 // CHUNK * CHUNK + CHUNK, n_keys)
        assert (n_kt - 1) * tk <= adm_end, "only the last visited tile may hold inadmissible keys"
        assert n_kt * tk >= min((q0 + tq - 1) // CHUNK * CHUNK + CHUNK, n_keys), "visited tiles cover every admissible key"
        assert (n_kt - 1) * tk == q0, "last tile starts with the query tile"
        assert tk >= MAX_DISTANCE, "tiles before the last two are at least MAX_DISTANCE behind"


def _sparse_attn(qt, iqt, iwt, k_b, vt_t, ik_b, rel_bias, pos0):
    b, _, t = qt.shape
    n_keys = k_b.shape[1]
    top_k = min(TOPK_MAX, n_keys // 4)
    tk = KEY_TILE
    tq = min(tk, t)
    nkt = n_keys // tk
    assert n_keys % tk == 0 and vt_t.shape == (b, nkt, N_HEADS * V_ROWS, tk)
    _check_tiling(t, tq, tk, nkt, pos0, n_keys)
    k_t = k_b.reshape(b, nkt, tk, ATTN_W)
    ik_t = ik_b.reshape(b, nkt, tk, IDX_DIM)
    bias = _bias_tiles(rel_bias, tq, tk)

    qspec = lambda n: pl.BlockSpec((None, n, tq), lambda bi, qi: (bi, 0, qi))
    kspec = lambda r, c: pl.BlockSpec((None, nkt, r, c), lambda bi, qi: (bi, 0, 0, 0))
    kern = functools.partial(_sparse_attn_kernel, tq=tq, tk=tk, nkt=nkt, pos0=pos0, n_keys=n_keys, top_k=top_k)
    return pl.pallas_call(
        kern,
        grid=(b, t // tq),
        in_specs=[qspec(ATTN_W), qspec(ATTN_W), qspec(N_IDX_HEADS),
                  kspec(tk, ATTN_W), kspec(N_HEADS * V_ROWS, tk), kspec(tk, IDX_DIM), _const_spec(bias.shape)],
        out_specs=qspec(ATTN_W),
        out_shape=jax.ShapeDtypeStruct((b, ATTN_W, t), BF16),
        scratch_shapes=[pltpu.VMEM((nkt + SEARCH_TRIP - 1, tk, tq), jnp.int32),
                        pltpu.VMEM((nkt + SEARCH_TRIP - 1, tk, tq), BF16),
                        pltpu.VMEM((N_HEADS, 1, tq), F32),
                        pltpu.VMEM((N_HEADS, V_ROWS, tq), F32),
                        pltpu.VMEM((2, N_HEADS, tk, tq), F32)],
        compiler_params=pltpu.CompilerParams(dimension_semantics=("arbitrary", "arbitrary"),
                                             vmem_limit_bytes=VMEM_LIMIT),
        name="sparse_attn",
    )(qt, iqt, iwt, k_t, vt_t, ik_t, bias)


DEC_ROWS = N_HEADS * 16
DEC_KEY_TILE = 2048
NT_DIMS = (((1,), (1,)), ((), ()))


def _dec_select_kernel(iq_ref, iwc_ref, ikt_ref, iktn_ref, madd_ref, key_scr, *, g, tq, n_keys, n_pad, top_k):
    n_slabs = n_pad // LANES
    col = lax.broadcasted_iota(jnp.int32, (tq, n_pad), 1)
    for s in range(g):
        iq_s = iq_ref[s * tq:(s + 1) * tq, :]
        lhs = jnp.concatenate([iq_s[:, h * IDX_DIM:(h + 1) * IDX_DIM] for h in range(N_IDX_HEADS)], axis=0)
        logit = jnp.concatenate([jnp.dot(lhs, ikt_ref[s].astype(BF16), preferred_element_type=F32),
                                 jnp.dot(lhs, iktn_ref[s], preferred_element_type=F32)], axis=1)
        x = jnp.maximum(logit, 0.0) * iwc_ref[s]
        score = x[0:tq]
        for h in range(1, N_IDX_HEADS):
            score = score + x[h * tq:(h + 1) * tq]
        score = score + 0.0
        bits = pltpu.bitcast(score, jnp.int32)
        key = bits ^ ((bits >> 31) & 0x7FFFFFFF)
        key_scr[s * tq:(s + 1) * tq, :] = jnp.where(col < n_keys, key, INT_MIN)

    rows = g * tq
    lane = lax.broadcasted_iota(jnp.int32, (rows, LANES), 1)

    def count(pred):
        acc = jnp.zeros((rows, LANES), F32)
        for j in range(n_slabs):
            acc = acc + jnp.where(pred(key_scr[:, j * LANES:(j + 1) * LANES], j), 1.0, 0.0)
        return jnp.sum(acc, axis=1, keepdims=True)

    def thr_body(i, t):
        cand = t + (jnp.int32(1) << (31 - i))
        c = count(lambda key, j: key >= cand)
        return jnp.where(c >= top_k, cand, t)

    thr = lax.fori_loop(0, 32, thr_body, jnp.full((rows, 1), INT_MIN, jnp.int32))
    thr = jnp.maximum(thr, INT_MIN + 1)

    n_ge = count(lambda key, j: key >= thr)
    has_tie = jnp.max(n_ge) > top_k

    @pl.when(has_tie)
    def _():
        need_m1 = (top_k - 1) - count(lambda key, j: key > thr)

        def idx_body(i, cut):
            cand = cut + (jnp.int32(1) << (15 - i))
            c = count(lambda key, j: (key == thr) & (j * LANES + lane < cand))
            return jnp.where(c <= need_m1, cand, cut)

        cut = lax.fori_loop(0, 16, idx_body, jnp.zeros((rows, 1), jnp.int32))
        tie_row = n_ge > top_k
        for j in range(n_slabs):
            key = key_scr[:, j * LANES:(j + 1) * LANES]
            drop = tie_row & (key == thr) & (j * LANES + lane > cut)
            key_scr[:, j * LANES:(j + 1) * LANES] = jnp.where(drop, INT_MIN, key)

    madd_ref[...] = jnp.where(key_scr[...] >= thr, 0.0, NEG)


def _dec_attn_kernel(q_ref, maddc_ref, maddn_ref, k_ref, v_ref, kn_ref, vn_ref, biasc_ref, biasn_ref, o_ref,
                     m_scr, l_scr, acc_scr, *, tq):
    kt = pl.program_id(1)
    q = q_ref[...]
    q_h = [q[:, h * HEAD_DIM:(h + 1) * HEAD_DIM] for h in range(N_HEADS)]

    @pl.when(kt == 0)
    def _():
        m_scr[...] = jnp.full(m_scr.shape, NEG, F32)
        l_scr[...] = jnp.zeros(l_scr.shape, F32)
        acc_scr[...] = jnp.zeros(acc_scr.shape, F32)

    def step(ktr, vtr, madd, bias):
        s = jnp.concatenate([jnp.dot(q_h[h], ktr[h].astype(BF16), preferred_element_type=F32)
                             for h in range(N_HEADS)], axis=0)
        s = s + jnp.concatenate([madd] * N_HEADS, axis=0) + bias
        m_prev = m_scr[...]
        m_new = jnp.maximum(m_prev, jnp.max(s, axis=1, keepdims=True))
        alpha = jnp.exp2(m_prev - m_new)
        p = jnp.exp2(s - m_new)
        l_scr[...] = alpha * l_scr[...] + jnp.sum(p, axis=1, keepdims=True)
        pb = p.astype(BF16)
        pv = jnp.concatenate([lax.dot_general(pb[h * tq:(h + 1) * tq, :], vtr[h].astype(BF16), NT_DIMS,
                                              preferred_element_type=F32) for h in range(N_HEADS)], axis=0)
        acc_scr[...] = alpha * acc_scr[...] + pv
        m_scr[...] = m_new

    step(k_ref, v_ref, maddc_ref[...], biasc_ref[...])

    @pl.when(kt == pl.num_programs(1) - 1)
    def _():
        step(kn_ref, vn_ref, maddn_ref[...], biasn_ref[...])
        o = acc_scr[...] / l_scr[...]
        for h in range(N_HEADS):
            o_ref[:, h * HEAD_DIM:(h + 1) * HEAD_DIM] = o[h * tq:(h + 1) * tq, :].astype(o_ref.dtype)


def _dec_attn(q, iq, iw, past_k, past_v, past_ik, k_new, v_new, ik_new, rel_bias, pos0):
    b, t, _ = q.shape
    p_len = past_k.shape[1]
    n_keys = p_len + t
    top_k = min(TOPK_MAX, n_keys // 4)
    tkk = min(DEC_KEY_TILE, p_len)
    n_pad = p_len + LANES
    g = min(LANES // t, b)
    assert N_HEADS * t == DEC_ROWS and b % g == 0 and p_len % tkk == 0 and t <= LANES
    assert (pos0 + t - 1) // CHUNK == pos0 // CHUNK and (n_keys - 1) // CHUNK <= pos0 // CHUNK, "every key admissible"
    assert tkk >= MAX_DISTANCE + LANES >= MAX_DISTANCE + t, "only the last cache tile is within MAX_DISTANCE of a query"

    ikt_new = jnp.pad(ik_new.swapaxes(1, 2), ((0, 0), (0, 0), (0, LANES - t)))
    iw_col = iw.swapaxes(1, 2).reshape(b, DEC_ROWS, 1)
    madd = pl.pallas_call(
        functools.partial(_dec_select_kernel, g=g, tq=t, n_keys=n_keys, n_pad=n_pad, top_k=top_k),
        grid=(b // g,),
        in_specs=[pl.BlockSpec((g * t, ATTN_W), lambda i: (i, 0)),
                  pl.BlockSpec((g, DEC_ROWS, 1), lambda i: (i, 0, 0)),
                  pl.BlockSpec((g, IDX_DIM, p_len), lambda i: (i, 0, 0)),
                  pl.BlockSpec((g, IDX_DIM, LANES), lambda i: (i, 0, 0))],
        out_specs=pl.BlockSpec((g * t, n_pad), lambda i: (i, 0)),
        out_shape=jax.ShapeDtypeStruct((b * t, n_pad), F32),
        scratch_shapes=[pltpu.VMEM((g * t, n_pad), jnp.int32)],
        compiler_params=pltpu.CompilerParams(dimension_semantics=("arbitrary",), vmem_limit_bytes=VMEM_LIMIT),
        name="dec_select",
    )(iq.reshape(b * t, ATTN_W), iw_col, past_ik.swapaxes(1, 2), ikt_new)

    n_steps = p_len // tkk
    qpos = jnp.tile(jnp.arange(t, dtype=jnp.int32), N_HEADS)[:, None]
    hsel = jnp.repeat(jnp.arange(N_HEADS), t)
    rows = jnp.arange(DEC_ROWS)
    near = MAX_DISTANCE + LANES
    rel_c = jnp.arange(near, dtype=jnp.int32)[None, :] - near - qpos
    rel_n = jnp.arange(LANES, dtype=jnp.int32)[None, :] - qpos
    bias_last = jnp.pad(_shifted_bias(rel_bias, rel_c)[hsel, rows], ((0, 0), (tkk - near, 0)))
    bias_c = jnp.concatenate([jnp.zeros((n_steps - 1, DEC_ROWS, tkk), F32), bias_last[None]], axis=0)
    bias_n = _shifted_bias(rel_bias, rel_n)[hsel, rows]
    heads_t = lambda a: a.transpose(0, 2, 3, 1)
    pad_new = lambda a: jnp.pad(heads_t(a.reshape(b, t, N_HEADS, HEAD_DIM)), ((0, 0), (0, 0), (0, 0), (0, LANES - t)))
    kv_tile = pl.BlockSpec((None, N_HEADS, HEAD_DIM, tkk), lambda bi, kt: (bi, 0, 0, kt))
    kv_new = pl.BlockSpec((None, N_HEADS, HEAD_DIM, LANES), lambda bi, kt: (bi, 0, 0, 0))
    return pl.pallas_call(
        functools.partial(_dec_attn_kernel, tq=t),
        grid=(b, n_steps),
        in_specs=[pl.BlockSpec((None, t, ATTN_W), lambda bi, kt: (bi, 0, 0)),
                  pl.BlockSpec((t, tkk), lambda bi, kt: (bi, kt)),
                  pl.BlockSpec((t, LANES), lambda bi, kt: (bi, p_len // LANES)),
                  kv_tile, kv_tile, kv_new, kv_new,
                  pl.BlockSpec((None, DEC_ROWS, tkk), lambda bi, kt: (kt, 0, 0)),
                  _const_spec(bias_n.shape)],
        out_specs=pl.BlockSpec((None, t, ATTN_W), lambda bi, kt: (bi, 0, 0)),
        out_shape=jax.ShapeDtypeStruct((b, t, ATTN_W), BF16),
        scratch_shapes=[pltpu.VMEM((DEC_ROWS, 1), F32), pltpu.VMEM((DEC_ROWS, 1), F32),
                        pltpu.VMEM((DEC_ROWS, HEAD_DIM), F32)],
        compiler_params=pltpu.CompilerParams(dimension_semantics=("arbitrary", "arbitrary"),
                                             vmem_limit_bytes=VMEM_LIMIT),
        name="dec_attn",
    )(q, madd, madd, heads_t(past_k), heads_t(past_v), pad_new(k_new), pad_new(v_new), bias_c, bias_n)


def _conv_kernel(*refs, tm, has_prev):
    if has_prev:
        u_ref, prev_ref, past_ref, w_ref, b_ref, lg_ref, lb_ref, o_ref, ext_scr = refs
        halo = jnp.where(pl.program_id(1) == 0, past_ref[...], prev_ref[...])
    else:
        u_ref, past_ref, w_ref, b_ref, lg_ref, lb_ref, o_ref, ext_scr = refs
        halo = past_ref[...]
    _conv_act(halo, u_ref[...], w_ref, b_ref, lg_ref, lb_ref, o_ref, ext_scr)


def _conv_act(halo, u, w_ref, b_ref, lg_ref, lb_ref, o_ref, ext_scr):
    tm = u.shape[0]
    ext_scr[0, 0:HALO, :] = halo
    ext_scr[0, HALO:HALO + tm, :] = u
    n_rows = tm + HALO - SUBLANES
    for r in range(1, SUBLANES):
        ext_scr[r, 0:n_rows, :] = ext_scr[0, r:r + n_rows, :]
    rb = min(tm, CONV_ROWS)
    first = HALO - (CONV_K - 1)
    for r0 in range(0, tm, rb):
        acc = None
        for j in range(CONV_K):
            a, r = divmod(first + j, SUBLANES)
            t = ext_scr[r, r0 + a * SUBLANES:r0 + a * SUBLANES + rb, :] * w_ref[j:j + 1, :]
            acc = t if acc is None else acc + t
        c = acc + b_ref[...]
        mu = jnp.mean(c, axis=-1, keepdims=True)
        cc = c - mu
        var = jnp.mean(cc * cc, axis=-1, keepdims=True)
        y = cc * lax.rsqrt(var + EPS) * lg_ref[...] + lb_ref[...]
        o_ref[r0:r0 + rb, :] = (y * jax.nn.sigmoid(y)).astype(o_ref.dtype)


def _conv_mod(u, past, w):
    b, t, c = u.shape
    tm = min(ROW_TILE, t)
    assert t % tm == 0 and tm % 8 == 0
    has_prev = t > tm
    tile = pl.BlockSpec((None, tm, c), lambda bi, i: (bi, i, 0))
    halo = pl.BlockSpec((None, HALO, c), lambda bi, i: (bi, 0, 0))
    in_specs = [tile]
    args = [u]
    if has_prev:
        r = tm // HALO
        in_specs.append(pl.BlockSpec((None, HALO, c), lambda bi, i: (bi, jnp.maximum(i * r - 1, 0), 0)))
        args.append(u)
    consts = [w["conv_w"], w["conv_b"], w["ln_g"], w["ln_b"]]
    in_specs += [halo] + [_const_spec(x.shape) for x in consts]
    args += [past] + consts
    return pl.pallas_call(
        functools.partial(_conv_kernel, tm=tm, has_prev=has_prev),
        grid=(b, t // tm),
        in_specs=in_specs,
        out_specs=tile,
        out_shape=jax.ShapeDtypeStruct((b, t, c), BF16),
        scratch_shapes=[pltpu.VMEM((SUBLANES, HALO + tm, c), F32)],
        compiler_params=pltpu.CompilerParams(dimension_semantics=("arbitrary", "arbitrary"),
                                             vmem_limit_bytes=VMEM_LIMIT),
        name="conv_mod",
    )(*args)


def _mix_ffn_conv_kernel(x_ref, attn_ref, u0_ref, unext_ref, utail_ref, past0_ref, pastn_ref, sga_ref, sgc_ref,
                         wao_ref, wco_ref, wo_ref, g_ref, wg_ref, wu_ref, wd_ref, cw_ref, cb_ref, lg_ref, lb_ref,
                         y_ref, cact_scr, ext_scr, *, n_chunks, tiles_per_seq):
    i = pl.program_id(0)

    @pl.when(i == 0)
    def _():
        _conv_act(past0_ref[...], u0_ref[...], cw_ref, cb_ref, lg_ref, lb_ref, cact_scr.at[0], ext_scr)

    _mix_ffn_kernel(x_ref, attn_ref, cact_scr.at[i % 2], sga_ref, sgc_ref, wao_ref, wco_ref, wo_ref, g_ref,
                    wg_ref, wu_ref, wd_ref, y_ref, n_chunks=n_chunks)
    halo = jnp.where((i + 1) % tiles_per_seq == 0, pastn_ref[...], utail_ref[...])
    _conv_act(halo, unext_ref[...], cw_ref, cb_ref, lg_ref, lb_ref, cact_scr.at[(i + 1) % 2], ext_scr)


def _mix_ffn_kernel(x_ref, attn_ref, cact_ref, sga_ref, sgc_ref, wao_ref, wco_ref, wo_ref, g_ref,
                    wg_ref, wu_ref, wd_ref, y_ref, *, n_chunks):
    attn_out = jnp.dot(attn_ref[...], wao_ref[...], preferred_element_type=F32)
    conv_out = jnp.dot(cact_ref[...], wco_ref[...], preferred_element_type=F32)
    merged = sga_ref[...].astype(F32) * attn_out + sgc_ref[...].astype(F32) * conv_out
    x1 = x_ref[...] + jnp.dot(merged.astype(BF16), wo_ref[...], preferred_element_type=F32)
    ms = jnp.mean(x1 * x1, axis=-1, keepdims=True)
    h2 = (x1 * lax.rsqrt(ms + EPS) * g_ref[...]).astype(BF16)
    hc = wg_ref.shape[1] // n_chunks
    y = x1
    for ci in range(n_chunks):
        sl = slice(ci * hc, (ci + 1) * hc)
        gate = jnp.dot(h2, wg_ref[:, sl], preferred_element_type=F32)
        up = jnp.dot(h2, wu_ref[:, sl], preferred_element_type=F32)
        act = (gate * jax.nn.sigmoid(gate) * up).astype(BF16)
        y = y + jnp.dot(act, wd_ref[sl, :], preferred_element_type=F32)
    y_ref[...] = y


def _mix_ffn_conv(x, attn, u, past, sga, sgc, w, seq_len):
    m, d = x.shape
    c = u.shape[1]
    tm = min(ROW_TILE, seq_len)
    assert m % tm == 0 and seq_len % tm == 0 and tm % HALO == 0
    nt, n = seq_len // tm, m // tm
    hidden = w["wg"].shape[1]
    n_chunks = 2 if hidden % (2 * LANES) == 0 else 1
    row = lambda k: pl.BlockSpec((tm, k), lambda i: (i, 0))
    nxt = lambda i: jnp.minimum(i + 1, n - 1)
    consts = [w[k] for k in ("wao", "wco", "wo", "g_ffn", "wg", "wu", "wd", "conv_w", "conv_b", "ln_g", "ln_b")]
    return pl.pallas_call(
        functools.partial(_mix_ffn_conv_kernel, n_chunks=n_chunks, tiles_per_seq=nt),
        grid=(n,),
        in_specs=[row(d), row(attn.shape[1]),
                  pl.BlockSpec((tm, c), lambda i: (0, 0)),
                  pl.BlockSpec((tm, c), lambda i: (nxt(i), 0)),
                  pl.BlockSpec((HALO, c), lambda i: ((i + 1) * (tm // HALO) - 1, 0)),
                  pl.BlockSpec((None, HALO, c), lambda i: (0, 0, 0)),
                  pl.BlockSpec((None, HALO, c), lambda i: (nxt(i) // nt, 0, 0)),
                  row(d), row(d)] + [_const_spec(k.shape) for k in consts],
        out_specs=row(d),
        out_shape=jax.ShapeDtypeStruct((m, d), F32),
        scratch_shapes=[pltpu.VMEM((2, tm, c), BF16), pltpu.VMEM((SUBLANES, HALO + tm, c), F32)],
        compiler_params=pltpu.CompilerParams(dimension_semantics=("arbitrary",), vmem_limit_bytes=VMEM_LIMIT),
        name="mix_ffn_conv",
    )(x, attn, u, u, u, past, past, sga, sgc, *consts)


def _mix_ffn(x, attn, cact, sga, sgc, w):
    m, d = x.shape
    tm = min(ROW_TILE, m)
    assert m % tm == 0
    hidden = w["wg"].shape[1]
    n_chunks = 2 if hidden % (2 * LANES) == 0 else 1
    row = lambda n: pl.BlockSpec((tm, n), lambda i: (i, 0))
    consts = [w[n] for n in ("wao", "wco", "wo", "g_ffn", "wg", "wu", "wd")]
    return pl.pallas_call(
        functools.partial(_mix_ffn_kernel, n_chunks=n_chunks),
        grid=(m // tm,),
        in_specs=[row(d), row(attn.shape[1]), row(cact.shape[1]), row(d), row(d)]
                 + [_const_spec(c.shape) for c in consts],
        out_specs=row(d),
        out_shape=jax.ShapeDtypeStruct((m, d), F32),
        compiler_params=pltpu.CompilerParams(dimension_semantics=("arbitrary",), vmem_limit_bytes=VMEM_LIMIT),
        name="mix_ffn",
    )(x, attn, cact, sga, sgc, *consts)


def _prep_weights(norm_mix_g, w_in, q_norm_g, k_norm_g, idx_k_norm_g, conv_dw_w, conv_dw_b, conv_ln_g, conv_ln_b,
                  w_conv_out, w_attn_out, w_out, norm_ffn_g, w_ffn_gate, w_ffn_up, w_ffn_down):
    d = w_in.shape[0]
    conv_ch = conv_dw_w.shape[-1]
    sizes = (ATTN_W, ATTN_W, ATTN_W, N_IDX_HEADS * IDX_DIM, IDX_DIM, N_IDX_HEADS, 2 * conv_ch, d, d)
    offs = np.concatenate([[0], np.cumsum(sizes)])
    assert offs[-1] == w_in.shape[1]
    col = lambda i: w_in[:, offs[i]:offs[i + 1]]
    wsm = jnp.concatenate([col(4), col(5), jnp.zeros((d, LANES - IDX_DIM - N_IDX_HEADS), w_in.dtype)], axis=1)
    head = np.arange(ATTN_W) // HEAD_DIM
    seg = (head[:, None] == np.arange(LANES)[None, :]).astype(np.float32)
    return dict(
        g_mix=norm_mix_g.reshape(1, d).astype(F32),
        wq=col(0).astype(BF16), wk=col(1).astype(BF16), wv=col(2).astype(BF16), wiq=col(3).astype(BF16),
        wsm=wsm.astype(BF16), wglu=col(6).astype(BF16), wga=col(7).astype(BF16), wgc=col(8).astype(BF16),
        qg=jnp.tile(q_norm_g.astype(F32), N_HEADS).reshape(1, ATTN_W),
        kg=jnp.tile(k_norm_g.astype(F32), N_HEADS).reshape(1, ATTN_W),
        ikg=jnp.concatenate([idx_k_norm_g.astype(F32), jnp.zeros((LANES - IDX_DIM,), F32)]).reshape(1, LANES),
        seg=jnp.asarray(seg, BF16), segt=jnp.asarray(seg.T, BF16),
        conv_w=conv_dw_w.reshape(CONV_K, conv_ch).astype(F32), conv_b=conv_dw_b.reshape(1, conv_ch).astype(F32),
        ln_g=conv_ln_g.reshape(1, conv_ch).astype(F32), ln_b=conv_ln_b.reshape(1, conv_ch).astype(F32),
        wao=w_attn_out.astype(BF16), wco=w_conv_out.astype(BF16), wo=w_out.astype(BF16),
        g_ffn=norm_ffn_g.reshape(1, d).astype(F32),
        wg=w_ffn_gate.astype(BF16), wu=w_ffn_up.astype(BF16), wd=w_ffn_down.astype(BF16),
    )


def _layer(x, pos0, past_k, past_v, past_ik, past_conv, rel_bias, w):
    b, t, d = x.shape
    m = b * t
    prompt = past_k is None
    q, k, k_b, v, v_b, iq, sm, sm_b, u, sga, sgc = _in_proj(x, w, transposed=prompt)
    conv_ch = u.shape[1]
    u = u.reshape(b, t, conv_ch)
    k_b = k_b.reshape(b, t, ATTN_W)
    ik_b = sm_b.reshape(b, t, LANES)[:, :, :IDX_DIM]
    if prompt:
        attn = _sparse_attn(q, iq, sm[:, IDX_DIM:IDX_DIM + N_IDX_HEADS, :], k_b, v_b, ik_b, rel_bias,
                            pos0).swapaxes(1, 2)
        heads = lambda a: a.reshape(b, N_HEADS, HEAD_DIM, t).transpose(0, 3, 1, 2)
        new_k, new_v, new_ik = heads(k), heads(v), sm[:, :IDX_DIM, :].swapaxes(1, 2)
    else:
        sm = sm.reshape(b, t, LANES)
        k, v = k.reshape(b, t, ATTN_W), v.reshape(b, t, ATTN_W)
        attn = _dec_attn(q.reshape(b, t, ATTN_W), iq.reshape(b, t, ATTN_W), sm[:, :, IDX_DIM:IDX_DIM + N_IDX_HEADS],
                         past_k, past_v, past_ik, k, v, ik_b, rel_bias, pos0)
        heads = lambda a: a.reshape(b, t, N_HEADS, HEAD_DIM)
        new_k, new_v, new_ik = heads(k), heads(v), sm[:, :, :IDX_DIM]

    past = jnp.pad(past_conv.astype(F32), ((0, 0), (HALO - (CONV_K - 1), 0), (0, 0)))
    if prompt:
        y = _mix_ffn_conv(x.reshape(m, d), attn.reshape(m, ATTN_W), u.reshape(m, conv_ch), past, sga, sgc, w, t)
    else:
        cact = _conv_mod(u, past, w)
        y = _mix_ffn(x.reshape(m, d), attn.reshape(m, ATTN_W), cact.reshape(m, conv_ch), sga, sgc, w)

    new_conv = jnp.concatenate([past_conv, u], axis=1)[:, -(CONV_K - 1):]
    return y.reshape(b, t, d), new_k, new_v, new_ik, new_conv


def kernel(x_prompt, x_sample, cache_k, cache_v, cache_idx_k, state_conv, rel_bias, norm_mix_g, w_in, q_norm_g, k_norm_g, idx_k_norm_g, conv_dw_w, conv_dw_b, conv_ln_g, conv_ln_b, w_conv_out, w_attn_out, w_out, norm_ffn_g, w_ffn_gate, w_ffn_up, w_ffn_down):
    depth = w_in.shape[0]
    bp = x_prompt.shape[0]
    past_len = cache_k.shape[2]
    conv_ch = conv_dw_w.shape[-1]
    yp, ys = x_prompt, x_sample
    outs_p, outs_s = [], []
    for l in range(depth):
        w = _prep_weights(norm_mix_g[l], w_in[l], q_norm_g[l], k_norm_g[l], idx_k_norm_g[l], conv_dw_w[l],
                          conv_dw_b[l], conv_ln_g[l], conv_ln_b[l], w_conv_out[l], w_attn_out[l], w_out[l],
                          norm_ffn_g[l], w_ffn_gate[l], w_ffn_up[l], w_ffn_down[l])
        zero_conv = jnp.zeros((bp, CONV_K - 1, conv_ch), yp.dtype)
        yp, *rest_p = _layer(yp, 0, None, None, None, zero_conv, rel_bias, w)
        outs_p.append(rest_p)
        ys, *rest_s = _layer(ys, past_len, cache_k[l], cache_v[l], cache_idx_k[l], state_conv[l], rel_bias, w)
        outs_s.append(rest_s)
    stack = lambda outs, i: jnp.stack([o[i] for o in outs])
    return (yp, ys, stack(outs_p, 0), stack(outs_p, 1), stack(outs_p, 2), stack(outs_p, 3),
            stack(outs_s, 0), stack(outs_s, 1), stack(outs_s, 2), stack(outs_s, 3))
```

```python
import functools
import math

import numpy as np
import jax
import jax.numpy as jnp
from jax import lax
from jax.experimental import pallas as pl
from jax.experimental.pallas import tpu as pltpu

CHUNK = 64
CHUNK_SHIFT = 6
N_HEADS = 8
HEAD_DIM = 64
ATTN_W = N_HEADS * HEAD_DIM
N_IDX_HEADS = 8
IDX_DIM = 64
TOPK_MAX = 256
CONV_K = 31
NUM_BUCKETS = 32
MAX_DISTANCE = 128
EPS = 1e-6
NEG = -1e30
INT_MIN = -(2 ** 31)
BF16_INF_BITS = 0x7F80
BF16_MIN_NORMAL_BITS = 0x0080
N_BF16_NEG = BF16_INF_BITS - BF16_MIN_NORMAL_BITS + 1
LOG2E = math.log2(math.e)

LANES = 128
SUBLANES = 8
HALO = 32
CONV_ROWS = 32
KEY_TILE = 256
SCORE_TRIP = 4
assert SCORE_TRIP in (2, 4)
SEARCH_TRIP = 2
V_ROWS = HEAD_DIM + 16
ROW_TILE = 256
V7X_VMEM_BYTES = 64 * 1024 * 1024
VMEM_LIMIT = V7X_VMEM_BYTES - 8 * 1024 * 1024

F32 = jnp.float32
BF16 = jnp.bfloat16


def _const_spec(shape):
    return pl.BlockSpec(shape, lambda *_: (0,) * len(shape), pipeline_mode=pl.Buffered(1))


def _split_dot(x, m):
    hi = x.astype(BF16)
    lo = (x - hi.astype(F32)).astype(BF16)
    return jnp.dot(hi, m, preferred_element_type=F32) + jnp.dot(lo, m, preferred_element_type=F32)


def _in_proj_kernel(x_ref, g_ref, wq_ref, wk_ref, wv_ref, wiq_ref, wsm_ref, wglu_ref, wga_ref, wgc_ref,
                    qg_ref, kg_ref, ikg_ref, seg_ref, segt_ref,
                    q_ref, k_ref, kb_ref, v_ref, vb_ref, iq_ref, sm_ref, smb_ref, u_ref, sga_ref, sgc_ref,
                    *, transposed):
    x = x_ref[...]
    ms = jnp.mean(x * x, axis=-1, keepdims=True)
    h = (x * lax.rsqrt(ms + EPS) * g_ref[...]).astype(BF16)

    def put(ref, val):
        ref[...] = (val.T if transposed else val).astype(ref.dtype)

    glu = jnp.dot(h, wglu_ref[...], preferred_element_type=F32)
    c = glu.shape[1] // 2
    u_ref[...] = glu[:, :c] * jax.nn.sigmoid(glu[:, c:])

    def head_rms(y, gain):
        ss = _split_dot(y * y, seg_ref[...])
        r = lax.rsqrt(ss * (1.0 / HEAD_DIM) + EPS)
        return y * _split_dot(r, segt_ref[...]) * gain

    q = head_rms(jnp.dot(h, wq_ref[...], preferred_element_type=F32), qg_ref[...])
    put(q_ref, q * (HEAD_DIM ** -0.5 * LOG2E))
    k = head_rms(jnp.dot(h, wk_ref[...], preferred_element_type=F32), kg_ref[...])
    put(k_ref, k)
    kb_ref[...] = k.astype(BF16)
    v = jnp.dot(h, wv_ref[...], preferred_element_type=F32)
    put(v_ref, v)
    if transposed:
        vt = v.T.astype(BF16)
        pad_rows = lax.broadcasted_iota(jnp.int32, (V_ROWS - HEAD_DIM, vt.shape[1]), 0)
        ones_row = jnp.where(pad_rows == 0, 1.0, 0.0).astype(BF16)
        for hd in range(N_HEADS):
            vb_ref[hd * V_ROWS:hd * V_ROWS + HEAD_DIM, :] = vt[hd * HEAD_DIM:(hd + 1) * HEAD_DIM, :]
            vb_ref[hd * V_ROWS + HEAD_DIM:(hd + 1) * V_ROWS, :] = ones_row
    else:
        vb_ref[...] = v.astype(BF16)
    iq = jnp.dot(h, wiq_ref[...], preferred_element_type=F32)
    put(iq_ref, iq * (IDX_DIM ** -0.5))

    sm = jnp.dot(h, wsm_ref[...], preferred_element_type=F32)
    lane = lax.broadcasted_iota(jnp.int32, sm.shape, 1)
    is_ik = lane < IDX_DIM
    ss = jnp.sum(jnp.where(is_ik, sm * sm, 0.0), axis=-1, keepdims=True)
    ikn = sm * lax.rsqrt(ss * (1.0 / IDX_DIM) + EPS) * ikg_ref[...]
    iw = sm * (N_IDX_HEADS ** -0.5)
    smo = jnp.where(is_ik, ikn, jnp.where(lane < IDX_DIM + N_IDX_HEADS, iw, 0.0))
    put(sm_ref, smo)
    smb_ref[...] = smo.astype(BF16)

    sga_ref[...] = jax.nn.sigmoid(jnp.dot(h, wga_ref[...], preferred_element_type=F32)).astype(BF16)
    sgc_ref[...] = jax.nn.sigmoid(jnp.dot(h, wgc_ref[...], preferred_element_type=F32)).astype(BF16)


def _in_proj(x, w, transposed):
    b, t, d = x.shape
    m = b * t
    tm = min(ROW_TILE, t if transposed else m)
    assert m % tm == 0 and (not transposed or t % tm == 0)
    nt = t // tm
    conv_ch = w["wglu"].shape[1] // 2
    consts = [w[n] for n in ("g_mix", "wq", "wk", "wv", "wiq", "wsm", "wglu", "wga", "wgc",
                             "qg", "kg", "ikg", "seg", "segt")]

    def rows(n, dtype):
        return jax.ShapeDtypeStruct((m, n), dtype), pl.BlockSpec((tm, n), lambda i: (i, 0))

    def cols(n, dtype):
        if not transposed:
            return rows(n, dtype)
        return jax.ShapeDtypeStruct((b, n, t), dtype), pl.BlockSpec((None, n, tm), lambda i: (i // nt, 0, i % nt))

    if transposed:
        v_slab = (jax.ShapeDtypeStruct((b, nt, N_HEADS * V_ROWS, tm), BF16),
                  pl.BlockSpec((None, None, N_HEADS * V_ROWS, tm), lambda i: (i // nt, i % nt, 0, 0)))
    else:
        v_slab = rows(ATTN_W, BF16)
    outs = [
        cols(ATTN_W, BF16),
        cols(ATTN_W, F32),
        rows(ATTN_W, BF16),
        cols(ATTN_W, F32),
        v_slab,
        cols(ATTN_W, BF16),
        cols(LANES, F32),
        rows(LANES, BF16),
        rows(conv_ch, F32),
        rows(d, BF16),
        rows(d, BF16),
    ]
    return pl.pallas_call(
        functools.partial(_in_proj_kernel, transposed=transposed),
        grid=(m // tm,),
        in_specs=[pl.BlockSpec((tm, d), lambda i: (i, 0))] + [_const_spec(c.shape) for c in consts],
        out_specs=[o[1] for o in outs],
        out_shape=[o[0] for o in outs],
        compiler_params=pltpu.CompilerParams(dimension_semantics=("arbitrary",), vmem_limit_bytes=VMEM_LIMIT),
        name="in_proj",
    )(x.reshape(m, d), *consts)


def _fold(x, op):
    return op(x.reshape(x.shape[0] // SUBLANES, SUBLANES, x.shape[1]), axis=0)


def _select_top_k(key_scr, part_scr, n_kt, *, tk, tq, top_k):
    n_trips = (n_kt + SEARCH_TRIP - 1) // SEARCH_TRIP
    for j in range(SEARCH_TRIP - 1):
        @pl.when(n_kt + j < n_trips * SEARCH_TRIP)
        def _():
            key_scr[n_kt + j] = jnp.full((tk, tq), INT_MIN, jnp.int32)
            part_scr[n_kt + j] = jnp.full((tk, tq), -jnp.inf, BF16)

    trip = lambda i: [SEARCH_TRIP * i + j for j in range(SEARCH_TRIP)]

    rows16 = 2 * SUBLANES
    one_b, zero_b = jnp.ones((), BF16), jnp.zeros((), BF16)

    def count_part(pred):
        def body(i, acc):
            for kt in trip(i):
                c = jnp.where(pred(part_scr[kt]), one_b, zero_b)
                parts = [c[j * rows16:(j + 1) * rows16] for j in range(4)]
                for j in range(4, tk // rows16):
                    parts[j % 4] = parts[j % 4] + c[j * rows16:(j + 1) * rows16]
                acc = acc + ((parts[0] + parts[1]) + (parts[2] + parts[3])).astype(F32)
            return acc
        acc = lax.fori_loop(0, n_trips, body, jnp.zeros((rows16, tq), F32))
        return jnp.sum(acc, axis=0, keepdims=True)

    def search(n_bits, base, to_image, limit):
        def body(i, t):
            cand = t + (jnp.int32(1) << (n_bits - 1 - i))
            img = to_image(cand)
            c = base + count_part(lambda a: a >= img)
            return jnp.where((c >= top_k) & (cand <= limit), cand, t)
        return lax.fori_loop(0, n_bits, body, jnp.zeros((1, tq), jnp.int32))

    def refine(eq_img, shift):
        def body(i, carry):
            for kt in trip(i):
                nxt = ((key_scr[kt] >> shift) & 0xFF).astype(F32).astype(BF16)
                part_scr[kt] = jnp.where(part_scr[kt] == eq_img, nxt, -one_b)
            return carry
        lax.fori_loop(0, n_trips, body, 0)

    def top_bits(c):
        neg = 0x8000 | (BF16_INF_BITS - c)
        pos = BF16_MIN_NORMAL_BITS + (c - N_BF16_NEG - 1)
        return jnp.where(c < N_BF16_NEG, neg, jnp.where(c == N_BF16_NEG, 0, pos))

    def top_image(c):
        return pltpu.bitcast(top_bits(c) << 16, F32).astype(BF16)

    small_image = lambda c: c.astype(F32).astype(BF16)

    t1 = search(16, 0.0, top_image, 2 * N_BF16_NEG)
    img1 = top_image(t1)
    above = count_part(lambda a: a > img1)
    refine(img1, 8)
    t2 = search(8, above, small_image, 255)
    img2 = small_image(t2)
    above = above + count_part(lambda a: a > img2)
    refine(img2, 0)
    t3 = search(8, above, small_image, 255)
    img3 = small_image(t3)
    n_ge = above + count_part(lambda a: a >= img3)
    hi = top_bits(t1)
    hi = hi - ((hi >> 15) << 16)
    hi = hi ^ ((hi >> 15) & 0x7FFF)
    thr = (hi << 16) + (t2 << 8) + t3
    few = t1 == 0
    thr = jnp.where(few, INT_MIN + 1, jnp.maximum(thr, INT_MIN + 1))

    excess = jnp.where(few, 0.0, n_ge - top_k)
    has_tie = jnp.max(excess) > 0

    @pl.when(has_tie)
    def _():
        upper = (lax.broadcasted_iota(jnp.int32, (tk, tk), 1) >= lax.broadcasted_iota(jnp.int32, (tk, tk), 0))
        upper = jnp.where(upper, 1.0, 0.0).astype(BF16)

        def drop_body(i, later):
            kt = n_kt - 1 - i
            key = key_scr[kt]
            eq = key == thr
            ties_from_here = jnp.dot(upper, jnp.where(eq, 1.0, 0.0).astype(BF16), preferred_element_type=F32)
            key_scr[kt] = jnp.where(eq & (ties_from_here + later <= excess), INT_MIN, key)
            return later + ties_from_here[0:1, :]

        lax.fori_loop(0, n_kt, drop_body, jnp.zeros((1, tq), F32))
    return thr


def _sparse_attn_kernel(qt_ref, iqt_ref, iwt_ref, k_ref, vt_ref, ik_ref, bias_ref, o_ref,
                        key_scr, part_scr, m_scr, acc_scr, s_scr, *, tq, tk, nkt, pos0, n_keys, top_k):
    qi = pl.program_id(1)
    q0 = pos0 + qi * tq
    n_kt = jnp.minimum(nkt, ((q0 + tq - 1) // CHUNK * CHUNK + CHUNK + tk - 1) // tk)
    last = n_kt - 1

    fold = _fold

    iqt = iqt_ref[...]
    iqt_h = [iqt[h * IDX_DIM:(h + 1) * IDX_DIM, :] for h in range(N_IDX_HEADS)]
    iwt = iwt_ref[...]
    iw_h = [iwt[h:h + 1, :] for h in range(N_IDX_HEADS)]

    def score_tile(kt, masked):
        ik = ik_ref[kt]
        s = None
        for h in range(N_IDX_HEADS):
            t = jnp.maximum(jnp.dot(ik, iqt_h[h], preferred_element_type=F32), 0.0) * iw_h[h]
            s = t if s is None else s + t
        s = s + 0.0
        bits = pltpu.bitcast(s, jnp.int32)
        key = bits ^ ((bits >> 31) & 0x7FFFFFFF)
        top = pltpu.bitcast(bits & -65536, F32)
        if masked:
            kp = kt * tk + lax.broadcasted_iota(jnp.int32, (tk, tq), 0)
            qp = q0 + lax.broadcasted_iota(jnp.int32, (tk, tq), 1)
            adm = ((kp >> CHUNK_SHIFT) <= (qp >> CHUNK_SHIFT)) & (kp < n_keys)
            key = jnp.where(adm, key, INT_MIN)
            top = jnp.where(adm, top, -jnp.inf)
        key_scr[kt] = key
        part_scr[kt] = top.astype(BF16)

    def score_body(i, carry):
        for j in range(SCORE_TRIP):
            score_tile(SCORE_TRIP * i + j, False)
        return carry

    lax.fori_loop(0, last // SCORE_TRIP, score_body, 0)
    left, rest = last // SCORE_TRIP * SCORE_TRIP, last % SCORE_TRIP

    @pl.when(rest >= 2)
    def _():
        score_tile(left, False)
        score_tile(left + 1, False)

    @pl.when(rest % 2 == 1)
    def _():
        score_tile(last - 1, False)

    score_tile(last, True)

    thr = _select_top_k(key_scr, part_scr, n_kt, tk=tk, tq=tq, top_k=top_k)

    qt = qt_ref[...]
    slab_row = lax.broadcasted_iota(jnp.int32, (LANES, tq), 0)
    qz = []
    for h in range(N_HEADS):
        slab = qt[(h // 2) * LANES:(h // 2 + 1) * LANES, :]
        mine = (slab_row < HEAD_DIM) if h % 2 == 0 else (slab_row >= HEAD_DIM)
        qz.append(jnp.where(mine, slab, jnp.zeros_like(slab)))
    m_scr[...] = jnp.full(m_scr.shape, NEG, F32)
    acc_scr[...] = jnp.zeros(acc_scr.shape, F32)

    def park(kt, slot, heads, madd):
        for h in heads:
            pair = h // 2
            s = jnp.dot(k_ref[kt, :, pair * LANES:(pair + 1) * LANES], qz[h], preferred_element_type=F32)
            s_scr[slot, h] = s + madd

    def absorb(kt, slot, bias_idx, heads):
        for h in heads:
            s = s_scr[slot, h]
            if bias_idx is not None:
                s = s + bias_ref[bias_idx, h]
            m_new = jnp.maximum(m_scr[h], jnp.max(fold(s, jnp.max), axis=0, keepdims=True))
            alpha = jnp.exp2(m_scr[h] - m_new)
            p = jnp.exp2(s - m_new)
            pv = jnp.dot(vt_ref[kt, h * V_ROWS:(h + 1) * V_ROWS, :], p.astype(BF16), preferred_element_type=F32)
            acc_scr[h] = alpha * acc_scr[h] + pv
            m_scr[h] = m_new

    all_heads = range(N_HEADS)
    head_groups = (range(0, N_HEADS // 2), range(N_HEADS // 2, N_HEADS))

    def step(kt_absorb, slot, bias_idx, kt_park):
        if kt_park is None:
            absorb(kt_absorb, slot, bias_idx, all_heads)
            return
        madd = jnp.where(key_scr[kt_park] >= thr, 0.0, NEG)
        for hs in head_groups:
            park(kt_park, 1 - slot, hs, madd)
            absorb(kt_absorb, slot, bias_idx, hs)

    park(0, 0, all_heads, jnp.where(key_scr[0] >= thr, 0.0, NEG))
    n_trips = (n_kt - 2) // 2

    def far_body(i, carry):
        kt = 2 * i
        step(kt, 0, None, kt + 1)
        step(kt + 1, 1, None, kt + 2)
        return carry

    lax.fori_loop(0, n_trips, far_body, 0)
    done = 2 * jnp.maximum(n_trips, 0)

    @pl.when(n_kt == 1)
    def _():
        step(0, 0, 1, None)

    @pl.when((n_kt >= 2) & (n_kt % 2 == 0))
    def _():
        step(done, 0, 0, done + 1)
        step(done + 1, 1, 1, None)

    @pl.when((n_kt >= 3) & (n_kt % 2 == 1))
    def _():
        step(done, 0, None, done + 1)
        step(done + 1, 1, 0, done + 2)
        step(done + 2, 0, 1, None)

    for h in range(N_HEADS):
        acc = acc_scr[h]
        o_ref[h * HEAD_DIM:(h + 1) * HEAD_DIM, :] = (acc[:HEAD_DIM] / acc[HEAD_DIM:HEAD_DIM + 1]).astype(o_ref.dtype)


def _rel_bucket(rel):
    nb = NUM_BUCKETS // 2
    max_exact = nb // 2
    ret = jnp.where(rel > 0, nb, 0)
    n = jnp.abs(rel)
    nf = jnp.maximum(n, 1).astype(jnp.float32)
    large = max_exact + (jnp.log(nf / max_exact) / math.log(MAX_DISTANCE / max_exact) * (nb - max_exact)).astype(jnp.int32)
    large = jnp.minimum(large, nb - 1)
    return ret + jnp.where(n < max_exact, n, large)


def _bias_tiles(rel_bias, tq, tk):
    j = jnp.arange(tk, dtype=jnp.int32)[:, None]
    i = jnp.arange(tq, dtype=jnp.int32)[None, :]
    return jnp.stack([_shifted_bias(rel_bias, j - i + d) for d in (-tk, 0)])


def _shifted_bias(rel_bias, rel):
    rb = (rel_bias.astype(F32) - rel_bias[_rel_bucket(jnp.int32(-MAX_DISTANCE))].astype(F32)) * LOG2E
    onehot = jax.nn.one_hot(_rel_bucket(rel), NUM_BUCKETS, dtype=F32)
    return jnp.moveaxis(jnp.einsum("...b,bh->...h", onehot, rb, precision=lax.Precision.HIGHEST), -1, 0)


def _check_tiling(t, tq, tk, nkt, pos0, n_keys):
    for qi in range(t // tq):
        q0 = pos0 + qi * tq
        n_kt = min(nkt, ((q0 + tq - 1) // CHUNK * CHUNK + CHUNK + tk - 1) // tk)
        adm_end = min(q0 // CHUNK * CHUNK + CHUNK, n_keys)
        assert (n_kt - 1) * tk <= adm_end, "only the last visited tile may hold inadmissible keys"
        assert n_kt * tk >= min((q0 + tq - 1) // CHUNK * CHUNK + CHUNK, n_keys), "visited tiles cover every admissible key"
        assert (n_kt - 1) * tk == q0, "last tile starts with the query tile"
        assert tk >= MAX_DISTANCE, "tiles before the last two are at least MAX_DISTANCE behind"


def _sparse_attn(qt, iqt, iwt, k_b, vt_t, ik_b, rel_bias, pos0):
    b, _, t = qt.shape
    n_keys = k_b.shape[1]
    top_k = min(TOPK_MAX, n_keys // 4)
    tk = KEY_TILE
    tq = min(tk, t)
    nkt = n_keys // tk
    assert n_keys % tk == 0 and vt_t.shape == (b, nkt, N_HEADS * V_ROWS, tk)
    _check_tiling(t, tq, tk, nkt, pos0, n_keys)
    k_t = k_b.reshape(b, nkt, tk, ATTN_W)
    ik_t = ik_b.reshape(b, nkt, tk, IDX_DIM)
    bias = _bias_tiles(rel_bias, tq, tk)

    qspec = lambda n: pl.BlockSpec((None, n, tq), lambda bi, qi: (bi, 0, qi))
    kspec = lambda r, c: pl.BlockSpec((None, nkt, r, c), lambda bi, qi: (bi, 0, 0, 0))
    kern = functools.partial(_sparse_attn_kernel, tq=tq, tk=tk, nkt=nkt, pos0=pos0, n_keys=n_keys, top_k=top_k)
    return pl.pallas_call(
        kern,
        grid=(b, t // tq),
        in_specs=[qspec(ATTN_W), qspec(ATTN_W), qspec(N_IDX_HEADS),
                  kspec(tk, ATTN_W), kspec(N_HEADS * V_ROWS, tk), kspec(tk, IDX_DIM), _const_spec(bias.shape)],
        out_specs=qspec(ATTN_W),
        out_shape=jax.ShapeDtypeStruct((b, ATTN_W, t), BF16),
        scratch_shapes=[pltpu.VMEM((nkt + SEARCH_TRIP - 1, tk, tq), jnp.int32),
                        pltpu.VMEM((nkt + SEARCH_TRIP - 1, tk, tq), BF16),
                        pltpu.VMEM((N_HEADS, 1, tq), F32),
                        pltpu.VMEM((N_HEADS, V_ROWS, tq), F32),
                        pltpu.VMEM((2, N_HEADS, tk, tq), F32)],
        compiler_params=pltpu.CompilerParams(dimension_semantics=("arbitrary", "arbitrary"),
                                             vmem_limit_bytes=VMEM_LIMIT),
        name="sparse_attn",
    )(qt, iqt, iwt, k_t, vt_t, ik_t, bias)


DEC_ROWS = N_HEADS * 16
DEC_KEY_TILE = 4096
NT_DIMS = (((1,), (1,)), ((), ()))


def _dec_select_kernel(iq_ref, iwc_ref, ikt_ref, iktn_ref, madd_ref, key_scr, *, g, tq, n_keys, n_pad, top_k):
    n_slabs = n_pad // LANES
    col = lax.broadcasted_iota(jnp.int32, (tq, n_pad), 1)
    for s in range(g):
        iq_s = iq_ref[s * tq:(s + 1) * tq, :]
        lhs = jnp.concatenate([iq_s[:, h * IDX_DIM:(h + 1) * IDX_DIM] for h in range(N_IDX_HEADS)], axis=0)
        logit = jnp.concatenate([jnp.dot(lhs, ikt_ref[s].astype(BF16), preferred_element_type=F32),
                                 jnp.dot(lhs, iktn_ref[s], preferred_element_type=F32)], axis=1)
        x = jnp.maximum(logit, 0.0) * iwc_ref[s]
        score = x[0:tq]
        for h in range(1, N_IDX_HEADS):
            score = score + x[h * tq:(h + 1) * tq]
        score = score + 0.0
        bits = pltpu.bitcast(score, jnp.int32)
        key = bits ^ ((bits >> 31) & 0x7FFFFFFF)
        key_scr[s * tq:(s + 1) * tq, :] = jnp.where(col < n_keys, key, INT_MIN)

    rows = g * tq
    lane = lax.broadcasted_iota(jnp.int32, (rows, LANES), 1)

    def count(pred):
        acc = jnp.zeros((rows, LANES), F32)
        for j in range(n_slabs):
            acc = acc + jnp.where(pred(key_scr[:, j * LANES:(j + 1) * LANES], j), 1.0, 0.0)
        return jnp.sum(acc, axis=1, keepdims=True)

    def thr_body(i, t):
        cand = t + (jnp.int32(1) << (31 - i))
        c = count(lambda key, j: key >= cand)
        return jnp.where(c >= top_k, cand, t)

    thr = lax.fori_loop(0, 32, thr_body, jnp.full((rows, 1), INT_MIN, jnp.int32))
    thr = jnp.maximum(thr, INT_MIN + 1)

    n_ge = count(lambda key, j: key >= thr)
    has_tie = jnp.max(n_ge) > top_k

    @pl.when(has_tie)
    def _():
        need_m1 = (top_k - 1) - count(lambda key, j: key > thr)

        def idx_body(i, cut):
            cand = cut + (jnp.int32(1) << (15 - i))
            c = count(lambda key, j: (key == thr) & (j * LANES + lane < cand))
            return jnp.where(c <= need_m1, cand, cut)

        cut = lax.fori_loop(0, 16, idx_body, jnp.zeros((rows, 1), jnp.int32))
        tie_row = n_ge > top_k
        for j in range(n_slabs):
            key = key_scr[:, j * LANES:(j + 1) * LANES]
            drop = tie_row & (key == thr) & (j * LANES + lane > cut)
            key_scr[:, j * LANES:(j + 1) * LANES] = jnp.where(drop, INT_MIN, key)

    madd_ref[...] = jnp.where(key_scr[...] >= thr, 0.0, NEG)


def _dec_attn_kernel(q_ref, maddc_ref, maddn_ref, k_ref, v_ref, kn_ref, vn_ref, biasc_ref, biasn_ref, o_ref,
                     m_scr, l_scr, acc_scr, *, tq):
    kt = pl.program_id(1)
    q = q_ref[...]
    q_h = [q[:, h * HEAD_DIM:(h + 1) * HEAD_DIM] for h in range(N_HEADS)]

    @pl.when(kt == 0)
    def _():
        m_scr[...] = jnp.full(m_scr.shape, NEG, F32)
        l_scr[...] = jnp.zeros(l_scr.shape, F32)
        acc_scr[...] = jnp.zeros(acc_scr.shape, F32)

    def step(ktr, vtr, madd, bias):
        s = jnp.concatenate([jnp.dot(q_h[h], ktr[h].astype(BF16), preferred_element_type=F32)
                             for h in range(N_HEADS)], axis=0)
        s = s + jnp.concatenate([madd] * N_HEADS, axis=0) + bias
        m_prev = m_scr[...]
        m_new = jnp.maximum(m_prev, jnp.max(s, axis=1, keepdims=True))
        alpha = jnp.exp2(m_prev - m_new)
        p = jnp.exp2(s - m_new)
        l_scr[...] = alpha * l_scr[...] + jnp.sum(p, axis=1, keepdims=True)
        pb = p.astype(BF16)
        pv = jnp.concatenate([lax.dot_general(pb[h * tq:(h + 1) * tq, :], vtr[h].astype(BF16), NT_DIMS,
                                              preferred_element_type=F32) for h in range(N_HEADS)], axis=0)
        acc_scr[...] = alpha * acc_scr[...] + pv
        m_scr[...] = m_new

    step(k_ref, v_ref, maddc_ref[...], biasc_ref[...])

    @pl.when(kt == pl.num_programs(1) - 1)
    def _():
        step(kn_ref, vn_ref, maddn_ref[...], biasn_ref[...])
        o = acc_scr[...] / l_scr[...]
        for h in range(N_HEADS):
            o_ref[:, h * HEAD_DIM:(h + 1) * HEAD_DIM] = o[h * tq:(h + 1) * tq, :].astype(o_ref.dtype)


def _dec_attn(q, iq, iw, past_k, past_v, past_ik, k_new, v_new, ik_new, rel_bias, pos0):
    b, t, _ = q.shape
    p_len = past_k.shape[1]
    n_keys = p_len + t
    top_k = min(TOPK_MAX, n_keys // 4)
    tkk = min(DEC_KEY_TILE, p_len)
    n_pad = p_len + LANES
    g = min(LANES // t, b)
    assert N_HEADS * t == DEC_ROWS and b % g == 0 and p_len % tkk == 0 and t <= LANES
    assert (pos0 + t - 1) // CHUNK == pos0 // CHUNK and (n_keys - 1) // CHUNK <= pos0 // CHUNK, "every key admissible"
    assert tkk >= MAX_DISTANCE + LANES >= MAX_DISTANCE + t, "only the last cache tile is within MAX_DISTANCE of a query"

    ikt_new = jnp.pad(ik_new.swapaxes(1, 2), ((0, 0), (0, 0), (0, LANES - t)))
    iw_col = iw.swapaxes(1, 2).reshape(b, DEC_ROWS, 1)
    madd = pl.pallas_call(
        functools.partial(_dec_select_kernel, g=g, tq=t, n_keys=n_keys, n_pad=n_pad, top_k=top_k),
        grid=(b // g,),
        in_specs=[pl.BlockSpec((g * t, ATTN_W), lambda i: (i, 0)),
                  pl.BlockSpec((g, DEC_ROWS, 1), lambda i: (i, 0, 0)),
                  pl.BlockSpec((g, IDX_DIM, p_len), lambda i: (i, 0, 0)),
                  pl.BlockSpec((g, IDX_DIM, LANES), lambda i: (i, 0, 0))],
        out_specs=pl.BlockSpec((g * t, n_pad), lambda i: (i, 0)),
        out_shape=jax.ShapeDtypeStruct((b * t, n_pad), F32),
        scratch_shapes=[pltpu.VMEM((g * t, n_pad), jnp.int32)],
        compiler_params=pltpu.CompilerParams(dimension_semantics=("arbitrary",), vmem_limit_bytes=VMEM_LIMIT),
        name="dec_select",
    )(iq.reshape(b * t, ATTN_W), iw_col, past_ik.swapaxes(1, 2), ikt_new)

    n_steps = p_len // tkk
    qpos = jnp.tile(jnp.arange(t, dtype=jnp.int32), N_HEADS)[:, None]
    hsel = jnp.repeat(jnp.arange(N_HEADS), t)
    rows = jnp.arange(DEC_ROWS)
    near = MAX_DISTANCE + LANES
    rel_c = jnp.arange(near, dtype=jnp.int32)[None, :] - near - qpos
    rel_n = jnp.arange(LANES, dtype=jnp.int32)[None, :] - qpos
    bias_last = jnp.pad(_shifted_bias(rel_bias, rel_c)[hsel, rows], ((0, 0), (tkk - near, 0)))
    bias_c = jnp.concatenate([jnp.zeros((n_steps - 1, DEC_ROWS, tkk), F32), bias_last[None]], axis=0)
    bias_n = _shifted_bias(rel_bias, rel_n)[hsel, rows]
    heads_t = lambda a: a.transpose(0, 2, 3, 1)
    pad_new = lambda a: jnp.pad(heads_t(a.reshape(b, t, N_HEADS, HEAD_DIM)), ((0, 0), (0, 0), (0, 0), (0, LANES - t)))
    kv_tile = pl.BlockSpec((None, N_HEADS, HEAD_DIM, tkk), lambda bi, kt: (bi, 0, 0, kt))
    kv_new = pl.BlockSpec((None, N_HEADS, HEAD_DIM, LANES), lambda bi, kt: (bi, 0, 0, 0))
    return pl.pallas_call(
        functools.partial(_dec_attn_kernel, tq=t),
        grid=(b, n_steps),
        in_specs=[pl.BlockSpec((None, t, ATTN_W), lambda bi, kt: (bi, 0, 0)),
                  pl.BlockSpec((t, tkk), lambda bi, kt: (bi, kt)),
                  pl.BlockSpec((t, LANES), lambda bi, kt: (bi, p_len // LANES)),
                  kv_tile, kv_tile, kv_new, kv_new,
                  pl.BlockSpec((None, DEC_ROWS, tkk), lambda bi, kt: (kt, 0, 0)),
                  _const_spec(bias_n.shape)],
        out_specs=pl.BlockSpec((None, t, ATTN_W), lambda bi, kt: (bi, 0, 0)),
        out_shape=jax.ShapeDtypeStruct((b, t, ATTN_W), BF16),
        scratch_shapes=[pltpu.VMEM((DEC_ROWS, 1), F32), pltpu.VMEM((DEC_ROWS, 1), F32),
                        pltpu.VMEM((DEC_ROWS, HEAD_DIM), F32)],
        compiler_params=pltpu.CompilerParams(dimension_semantics=("arbitrary", "arbitrary"),
                                             vmem_limit_bytes=VMEM_LIMIT),
        name="dec_attn",
    )(q, madd, madd, heads_t(past_k), heads_t(past_v), pad_new(k_new), pad_new(v_new), bias_c, bias_n)


def _conv_kernel(*refs, tm, has_prev):
    if has_prev:
        u_ref, prev_ref, past_ref, w_ref, b_ref, lg_ref, lb_ref, o_ref, ext_scr = refs
        halo = jnp.where(pl.program_id(1) == 0, past_ref[...], prev_ref[...])
    else:
        u_ref, past_ref, w_ref, b_ref, lg_ref, lb_ref, o_ref, ext_scr = refs
        halo = past_ref[...]
    _conv_act(halo, u_ref[...], w_ref, b_ref, lg_ref, lb_ref, o_ref, ext_scr)


def _conv_act(halo, u, w_ref, b_ref, lg_ref, lb_ref, o_ref, ext_scr):
    tm = u.shape[0]
    ext_scr[0, 0:HALO, :] = halo
    ext_scr[0, HALO:HALO + tm, :] = u
    n_rows = tm + HALO - SUBLANES
    for r in range(1, SUBLANES):
        ext_scr[r, 0:n_rows, :] = ext_scr[0, r:r + n_rows, :]
    rb = min(tm, CONV_ROWS)
    first = HALO - (CONV_K - 1)
    for r0 in range(0, tm, rb):
        acc = None
        for j in range(CONV_K):
            a, r = divmod(first + j, SUBLANES)
            t = ext_scr[r, r0 + a * SUBLANES:r0 + a * SUBLANES + rb, :] * w_ref[j:j + 1, :]
            acc = t if acc is None else acc + t
        c = acc + b_ref[...]
        mu = jnp.mean(c, axis=-1, keepdims=True)
        cc = c - mu
        var = jnp.mean(cc * cc, axis=-1, keepdims=True)
        y = cc * lax.rsqrt(var + EPS) * lg_ref[...] + lb_ref[...]
        o_ref[r0:r0 + rb, :] = (y * jax.nn.sigmoid(y)).astype(o_ref.dtype)


def _conv_mod(u, past, w):
    b, t, c = u.shape
    tm = min(ROW_TILE, t)
    assert t % tm == 0 and tm % 8 == 0
    has_prev = t > tm
    tile = pl.BlockSpec((None, tm, c), lambda bi, i: (bi, i, 0))
    halo = pl.BlockSpec((None, HALO, c), lambda bi, i: (bi, 0, 0))
    in_specs = [tile]
    args = [u]
    if has_prev:
        r = tm // HALO
        in_specs.append(pl.BlockSpec((None, HALO, c), lambda bi, i: (bi, jnp.maximum(i * r - 1, 0), 0)))
        args.append(u)
    consts = [w["conv_w"], w["conv_b"], w["ln_g"], w["ln_b"]]
    in_specs += [halo] + [_const_spec(x.shape) for x in consts]
    args += [past] + consts
    return pl.pallas_call(
        functools.partial(_conv_kernel, tm=tm, has_prev=has_prev),
        grid=(b, t // tm),
        in_specs=in_specs,
        out_specs=tile,
        out_shape=jax.ShapeDtypeStruct((b, t, c), BF16),
        scratch_shapes=[pltpu.VMEM((SUBLANES, HALO + tm, c), F32)],
        compiler_params=pltpu.CompilerParams(dimension_semantics=("arbitrary", "arbitrary"),
                                             vmem_limit_bytes=VMEM_LIMIT),
        name="conv_mod",
    )(*args)


def _mix_ffn_conv_kernel(x_ref, attn_ref, u0_ref, unext_ref, utail_ref, past0_ref, pastn_ref, sga_ref, sgc_ref,
                         wao_ref, wco_ref, wo_ref, g_ref, wg_ref, wu_ref, wd_ref, cw_ref, cb_ref, lg_ref, lb_ref,
                         y_ref, cact_scr, ext_scr, *, n_chunks, tiles_per_seq):
    i = pl.program_id(0)

    @pl.when(i == 0)
    def _():
        _conv_act(past0_ref[...], u0_ref[...], cw_ref, cb_ref, lg_ref, lb_ref, cact_scr.at[0], ext_scr)

    _mix_ffn_kernel(x_ref, attn_ref, cact_scr.at[i % 2], sga_ref, sgc_ref, wao_ref, wco_ref, wo_ref, g_ref,
                    wg_ref, wu_ref, wd_ref, y_ref, n_chunks=n_chunks)
    halo = jnp.where((i + 1) % tiles_per_seq == 0, pastn_ref[...], utail_ref[...])
    _conv_act(halo, unext_ref[...], cw_ref, cb_ref, lg_ref, lb_ref, cact_scr.at[(i + 1) % 2], ext_scr)


def _mix_ffn_kernel(x_ref, attn_ref, cact_ref, sga_ref, sgc_ref, wao_ref, wco_ref, wo_ref, g_ref,
                    wg_ref, wu_ref, wd_ref, y_ref, *, n_chunks):
    attn_out = jnp.dot(attn_ref[...], wao_ref[...], preferred_element_type=F32)
    conv_out = jnp.dot(cact_ref[...], wco_ref[...], preferred_element_type=F32)
    merged = sga_ref[...].astype(F32) * attn_out + sgc_ref[...].astype(F32) * conv_out
    x1 = x_ref[...] + jnp.dot(merged.astype(BF16), wo_ref[...], preferred_element_type=F32)
    ms = jnp.mean(x1 * x1, axis=-1, keepdims=True)
    h2 = (x1 * lax.rsqrt(ms + EPS) * g_ref[...]).astype(BF16)
    hc = wg_ref.shape[1] // n_chunks
    y = x1
    for ci in range(n_chunks):
        sl = slice(ci * hc, (ci + 1) * hc)
        gate = jnp.dot(h2, wg_ref[:, sl], preferred_element_type=F32)
        up = jnp.dot(h2, wu_ref[:, sl], preferred_element_type=F32)
        act = (gate * jax.nn.sigmoid(gate) * up).astype(BF16)
        y = y + jnp.dot(act, wd_ref[sl, :], preferred_element_type=F32)
    y_ref[...] = y


def _mix_ffn_conv(x, attn, u, past, sga, sgc, w, seq_len):
    m, d = x.shape
    c = u.shape[1]
    tm = min(ROW_TILE, seq_len)
    assert m % tm == 0 and seq_len % tm == 0 and tm % HALO == 0
    nt, n = seq_len // tm, m // tm
    hidden = w["wg"].shape[1]
    n_chunks = 2 if hidden % (2 * LANES) == 0 else 1
    row = lambda k: pl.BlockSpec((tm, k), lambda i: (i, 0))
    nxt = lambda i: jnp.minimum(i + 1, n - 1)
    consts = [w[k] for k in ("wao", "wco", "wo", "g_ffn", "wg", "wu", "wd", "conv_w", "conv_b", "ln_g", "ln_b")]
    return pl.pallas_call(
        functools.partial(_mix_ffn_conv_kernel, n_chunks=n_chunks, tiles_per_seq=nt),
        grid=(n,),
        in_specs=[row(d), row(attn.shape[1]),
                  pl.BlockSpec((tm, c), lambda i: (0, 0)),
                  pl.BlockSpec((tm, c), lambda i: (nxt(i), 0)),
                  pl.BlockSpec((HALO, c), lambda i: ((i + 1) * (tm // HALO) - 1, 0)),
                  pl.BlockSpec((None, HALO, c), lambda i: (0, 0, 0)),
                  pl.BlockSpec((None, HALO, c), lambda i: (nxt(i) // nt, 0, 0)),
                  row(d), row(d)] + [_const_spec(k.shape) for k in consts],
        out_specs=row(d),
        out_shape=jax.ShapeDtypeStruct((m, d), F32),
        scratch_shapes=[pltpu.VMEM((2, tm, c), BF16), pltpu.VMEM((SUBLANES, HALO + tm, c), F32)],
        compiler_params=pltpu.CompilerParams(dimension_semantics=("arbitrary",), vmem_limit_bytes=VMEM_LIMIT),
        name="mix_ffn_conv",
    )(x, attn, u, u, u, past, past, sga, sgc, *consts)


def _mix_ffn(x, attn, cact, sga, sgc, w):
    m, d = x.shape
    tm = min(ROW_TILE, m)
    assert m % tm == 0
    hidden = w["wg"].shape[1]
    n_chunks = 2 if hidden % (2 * LANES) == 0 else 1
    row = lambda n: pl.BlockSpec((tm, n), lambda i: (i, 0))
    consts = [w[n] for n in ("wao", "wco", "wo", "g_ffn", "wg", "wu", "wd")]
    return pl.pallas_call(
        functools.partial(_mix_ffn_kernel, n_chunks=n_chunks),
        grid=(m // tm,),
        in_specs=[row(d), row(attn.shape[1]), row(cact.shape[1]), row(d), row(d)]
                 + [_const_spec(c.shape) for c in consts],
        out_specs=row(d),
        out_shape=jax.ShapeDtypeStruct((m, d), F32),
        compiler_params=pltpu.CompilerParams(dimension_semantics=("arbitrary",), vmem_limit_bytes=VMEM_LIMIT),
        name="mix_ffn",
    )(x, attn, cact, sga, sgc, *consts)


def _prep_weights(norm_mix_g, w_in, q_norm_g, k_norm_g, idx_k_norm_g, conv_dw_w, conv_dw_b, conv_ln_g, conv_ln_b,
                  w_conv_out, w_attn_out, w_out, norm_ffn_g, w_ffn_gate, w_ffn_up, w_ffn_down):
    d = w_in.shape[0]
    conv_ch = conv_dw_w.shape[-1]
    sizes = (ATTN_W, ATTN_W, ATTN_W, N_IDX_HEADS * IDX_DIM, IDX_DIM, N_IDX_HEADS, 2 * conv_ch, d, d)
    offs = np.concatenate([[0], np.cumsum(sizes)])
    assert offs[-1] == w_in.shape[1]
    col = lambda i: w_in[:, offs[i]:offs[i + 1]]
    wsm = jnp.concatenate([col(4), col(5), jnp.zeros((d, LANES - IDX_DIM - N_IDX_HEADS), w_in.dtype)], axis=1)
    head = np.arange(ATTN_W) // HEAD_DIM
    seg = (head[:, None] == np.arange(LANES)[None, :]).astype(np.float32)
    return dict(
        g_mix=norm_mix_g.reshape(1, d).astype(F32),
        wq=col(0).astype(BF16), wk=col(1).astype(BF16), wv=col(2).astype(BF16), wiq=col(3).astype(BF16),
        wsm=wsm.astype(BF16), wglu=col(6).astype(BF16), wga=col(7).astype(BF16), wgc=col(8).astype(BF16),
        qg=jnp.tile(q_norm_g.astype(F32), N_HEADS).reshape(1, ATTN_W),
        kg=jnp.tile(k_norm_g.astype(F32), N_HEADS).reshape(1, ATTN_W),
        ikg=jnp.concatenate([idx_k_norm_g.astype(F32), jnp.zeros((LANES - IDX_DIM,), F32)]).reshape(1, LANES),
        seg=jnp.asarray(seg, BF16), segt=jnp.asarray(seg.T, BF16),
        conv_w=conv_dw_w.reshape(CONV_K, conv_ch).astype(F32), conv_b=conv_dw_b.reshape(1, conv_ch).astype(F32),
        ln_g=conv_ln_g.reshape(1, conv_ch).astype(F32), ln_b=conv_ln_b.reshape(1, conv_ch).astype(F32),
        wao=w_attn_out.astype(BF16), wco=w_conv_out.astype(BF16), wo=w_out.astype(BF16),
        g_ffn=norm_ffn_g.reshape(1, d).astype(F32),
        wg=w_ffn_gate.astype(BF16), wu=w_ffn_up.astype(BF16), wd=w_ffn_down.astype(BF16),
    )


def _layer(x, pos0, past_k, past_v, past_ik, past_conv, rel_bias, w):
    b, t, d = x.shape
    m = b * t
    prompt = past_k is None
    q, k, k_b, v, v_b, iq, sm, sm_b, u, sga, sgc = _in_proj(x, w, transposed=prompt)
    conv_ch = u.shape[1]
    u = u.reshape(b, t, conv_ch)
    k_b = k_b.reshape(b, t, ATTN_W)
    ik_b = sm_b.reshape(b, t, LANES)[:, :, :IDX_DIM]
    if prompt:
        attn = _sparse_attn(q, iq, sm[:, IDX_DIM:IDX_DIM + N_IDX_HEADS, :], k_b, v_b, ik_b, rel_bias,
                            pos0).swapaxes(1, 2)
        heads = lambda a: a.reshape(b, N_HEADS, HEAD_DIM, t).transpose(0, 3, 1, 2)
        new_k, new_v, new_ik = heads(k), heads(v), sm[:, :IDX_DIM, :].swapaxes(1, 2)
    else:
        sm = sm.reshape(b, t, LANES)
        k, v = k.reshape(b, t, ATTN_W), v.reshape(b, t, ATTN_W)
        attn = _dec_attn(q.reshape(b, t, ATTN_W), iq.reshape(b, t, ATTN_W), sm[:, :, IDX_DIM:IDX_DIM + N_IDX_HEADS],
                         past_k, past_v, past_ik, k, v, ik_b, rel_bias, pos0)
        heads = lambda a: a.reshape(b, t, N_HEADS, HEAD_DIM)
        new_k, new_v, new_ik = heads(k), heads(v), sm[:, :, :IDX_DIM]

    past = jnp.pad(past_conv.astype(F32), ((0, 0), (HALO - (CONV_K - 1), 0), (0, 0)))
    if prompt:
        y = _mix_ffn_conv(x.reshape(m, d), attn.reshape(m, ATTN_W), u.reshape(m, conv_ch), past, sga, sgc, w, t)
    else:
        cact = _conv_mod(u, past, w)
        y = _mix_ffn(x.reshape(m, d), attn.reshape(m, ATTN_W), cact.reshape(m, conv_ch), sga, sgc, w)

    new_conv = jnp.concatenate([past_conv, u], axis=1)[:, -(CONV_K - 1):]
    return y.reshape(b, t, d), new_k, new_v, new_ik, new_conv


def kernel(x_prompt, x_sample, cache_k, cache_v, cache_idx_k, state_conv, rel_bias, norm_mix_g, w_in, q_norm_g, k_norm_g, idx_k_norm_g, conv_dw_w, conv_dw_b, conv_ln_g, conv_ln_b, w_conv_out, w_attn_out, w_out, norm_ffn_g, w_ffn_gate, w_ffn_up, w_ffn_down):
    depth = w_in.shape[0]
    bp = x_prompt.shape[0]
    past_len = cache_k.shape[2]
    conv_ch = conv_dw_w.shape[-1]
    yp, ys = x_prompt, x_sample
    outs_p, outs_s = [], []
    for l in range(depth):
        w = _prep_weights(norm_mix_g[l], w_in[l], q_norm_g[l], k_norm_g[l], idx_k_norm_g[l], conv_dw_w[l],
                          conv_dw_b[l], conv_ln_g[l], conv_ln_b[l], w_conv_out[l], w_attn_out[l], w_out[l],
                          norm_ffn_g[l], w_ffn_gate[l], w_ffn_up[l], w_ffn_down[l])
        zero_conv = jnp.zeros((bp, CONV_K - 1, conv_ch), yp.dtype)
        yp, *rest_p = _layer(yp, 0, None, None, None, zero_conv, rel_bias, w)
        outs_p.append(rest_p)
        ys, *rest_s = _layer(ys, past_len, cache_k[l], cache_v[l], cache_idx_k[l], state_conv[l], rel_bias, w)
        outs_s.append(rest_s)
    stack = lambda outs, i: jnp.stack([o[i] for o in outs])
    return (yp, ys, stack(outs_p, 0), stack(outs_p, 1), stack(outs_p, 2), stack(outs_p, 3),
            stack(outs_s, 0), stack(outs_s, 1), stack(outs_s, 2), stack(outs_s, 3))
```

```python
import functools
import math

import numpy as np
import jax
import jax.numpy as jnp
from jax import lax
from jax.experimental import pallas as pl
from jax.experimental.pallas import tpu as pltpu

CHUNK = 64
CHUNK_SHIFT = 6
N_HEADS = 8
HEAD_DIM = 64
ATTN_W = N_HEADS * HEAD_DIM
N_IDX_HEADS = 8
IDX_DIM = 64
TOPK_MAX = 256
CONV_K = 31
NUM_BUCKETS = 32
MAX_DISTANCE = 128
EPS = 1e-6
NEG = -1e30
INT_MIN = -(2 ** 31)
BF16_INF_BITS = 0x7F80
BF16_MIN_NORMAL_BITS = 0x0080
N_BF16_NEG = BF16_INF_BITS - BF16_MIN_NORMAL_BITS + 1
LOG2E = math.log2(math.e)

LANES = 128
SUBLANES = 8
HALO = 32
CONV_ROWS = 32
KEY_TILE = 256
SCORE_TRIP = 4
assert SCORE_TRIP in (2, 4)
SEARCH_TRIP = 2
V_ROWS = HEAD_DIM + 16
ROW_TILE = 256
V7X_VMEM_BYTES = 64 * 1024 * 1024
VMEM_LIMIT = V7X_VMEM_BYTES - 8 * 1024 * 1024

F32 = jnp.float32
BF16 = jnp.bfloat16


def _const_spec(shape):
    return pl.BlockSpec(shape, lambda *_: (0,) * len(shape), pipeline_mode=pl.Buffered(1))


def _split_dot(x, m):
    hi = x.astype(BF16)
    lo = (x - hi.astype(F32)).astype(BF16)
    return jnp.dot(hi, m, preferred_element_type=F32) + jnp.dot(lo, m, preferred_element_type=F32)


def _in_proj_kernel(x_ref, g_ref, wq_ref, wk_ref, wv_ref, wiq_ref, wsm_ref, wglu_ref, wga_ref, wgc_ref,
                    qg_ref, kg_ref, ikg_ref, seg_ref, segt_ref,
                    q_ref, k_ref, kb_ref, v_ref, vb_ref, iq_ref, sm_ref, smb_ref, u_ref, sga_ref, sgc_ref,
                    *, transposed):
    x = x_ref[...]
    ms = jnp.mean(x * x, axis=-1, keepdims=True)
    h = (x * lax.rsqrt(ms + EPS) * g_ref[...]).astype(BF16)

    def put(ref, val):
        ref[...] = (val.T if transposed else val).astype(ref.dtype)

    glu = jnp.dot(h, wglu_ref[...], preferred_element_type=F32)
    c = glu.shape[1] // 2
    u_ref[...] = glu[:, :c] * jax.nn.sigmoid(glu[:, c:])

    def head_rms(y, gain):
        ss = _split_dot(y * y, seg_ref[...])
        r = lax.rsqrt(ss * (1.0 / HEAD_DIM) + EPS)
        return y * _split_dot(r, segt_ref[...]) * gain

    q = head_rms(jnp.dot(h, wq_ref[...], preferred_element_type=F32), qg_ref[...])
    put(q_ref, q * (HEAD_DIM ** -0.5 * LOG2E))
    k = head_rms(jnp.dot(h, wk_ref[...], preferred_element_type=F32), kg_ref[...])
    put(k_ref, k)
    kb_ref[...] = k.astype(BF16)
    v = jnp.dot(h, wv_ref[...], preferred_element_type=F32)
    put(v_ref, v)
    if transposed:
        vt = v.T.astype(BF16)
        pad_rows = lax.broadcasted_iota(jnp.int32, (V_ROWS - HEAD_DIM, vt.shape[1]), 0)
        ones_row = jnp.where(pad_rows == 0, 1.0, 0.0).astype(BF16)
        for hd in range(N_HEADS):
            vb_ref[hd * V_ROWS:hd * V_ROWS + HEAD_DIM, :] = vt[hd * HEAD_DIM:(hd + 1) * HEAD_DIM, :]
            vb_ref[hd * V_ROWS + HEAD_DIM:(hd + 1) * V_ROWS, :] = ones_row
    else:
        vb_ref[...] = v.astype(BF16)
    iq = jnp.dot(h, wiq_ref[...], preferred_element_type=F32)
    put(iq_ref, iq * (IDX_DIM ** -0.5))

    sm = jnp.dot(h, wsm_ref[...], preferred_element_type=F32)
    lane = lax.broadcasted_iota(jnp.int32, sm.shape, 1)
    is_ik = lane < IDX_DIM
    ss = jnp.sum(jnp.where(is_ik, sm * sm, 0.0), axis=-1, keepdims=True)
    ikn = sm * lax.rsqrt(ss * (1.0 / IDX_DIM) + EPS) * ikg_ref[...]
    iw = sm * (N_IDX_HEADS ** -0.5)
    smo = jnp.where(is_ik, ikn, jnp.where(lane < IDX_DIM + N_IDX_HEADS, iw, 0.0))
    put(sm_ref, smo)
    smb_ref[...] = smo.astype(BF16)

    sga_ref[...] = jax.nn.sigmoid(jnp.dot(h, wga_ref[...], preferred_element_type=F32)).astype(BF16)
    sgc_ref[...] = jax.nn.sigmoid(jnp.dot(h, wgc_ref[...], preferred_element_type=F32)).astype(BF16)


def _in_proj(x, w, transposed):
    b, t, d = x.shape
    m = b * t
    tm = min(ROW_TILE, t if transposed else m)
    assert m % tm == 0 and (not transposed or t % tm == 0)
    nt = t // tm
    conv_ch = w["wglu"].shape[1] // 2
    consts = [w[n] for n in ("g_mix", "wq", "wk", "wv", "wiq", "wsm", "wglu", "wga", "wgc",
                             "qg", "kg", "ikg", "seg", "segt")]

    def rows(n, dtype):
        return jax.ShapeDtypeStruct((m, n), dtype), pl.BlockSpec((tm, n), lambda i: (i, 0))

    def cols(n, dtype):
        if not transposed:
            return rows(n, dtype)
        return jax.ShapeDtypeStruct((b, n, t), dtype), pl.BlockSpec((None, n, tm), lambda i: (i // nt, 0, i % nt))

    if transposed:
        v_slab = (jax.ShapeDtypeStruct((b, nt, N_HEADS * V_ROWS, tm), BF16),
                  pl.BlockSpec((None, None, N_HEADS * V_ROWS, tm), lambda i: (i // nt, i % nt, 0, 0)))
    else:
        v_slab = rows(ATTN_W, BF16)
    outs = [
        cols(ATTN_W, BF16),
        cols(ATTN_W, F32),
        rows(ATTN_W, BF16),
        cols(ATTN_W, F32),
        v_slab,
        cols(ATTN_W, BF16),
        cols(LANES, F32),
        rows(LANES, BF16),
        rows(conv_ch, F32),
        rows(d, BF16),
        rows(d, BF16),
    ]
    return pl.pallas_call(
        functools.partial(_in_proj_kernel, transposed=transposed),
        grid=(m // tm,),
        in_specs=[pl.BlockSpec((tm, d), lambda i: (i, 0))] + [_const_spec(c.shape) for c in consts],
        out_specs=[o[1] for o in outs],
        out_shape=[o[0] for o in outs],
        compiler_params=pltpu.CompilerParams(dimension_semantics=("arbitrary",), vmem_limit_bytes=VMEM_LIMIT),
        name="in_proj",
    )(x.reshape(m, d), *consts)


def _fold(x, op):
    return op(x.reshape(x.shape[0] // SUBLANES, SUBLANES, x.shape[1]), axis=0)


def _select_top_k(key_scr, part_scr, n_kt, *, tk, tq, top_k):
    n_trips = (n_kt + SEARCH_TRIP - 1) // SEARCH_TRIP
    for j in range(SEARCH_TRIP - 1):
        @pl.when(n_kt + j < n_trips * SEARCH_TRIP)
        def _():
            key_scr[n_kt + j] = jnp.full((tk, tq), INT_MIN, jnp.int32)
            part_scr[n_kt + j] = jnp.full((tk, tq), -jnp.inf, BF16)

    trip = lambda i: [SEARCH_TRIP * i + j for j in range(SEARCH_TRIP)]

    rows16 = 2 * SUBLANES
    one_b, zero_b = jnp.ones((), BF16), jnp.zeros((), BF16)

    def count_part(pred):
        def body(i, acc):
            for kt in trip(i):
                c = jnp.where(pred(part_scr[kt]), one_b, zero_b)
                parts = [c[j * rows16:(j + 1) * rows16] for j in range(4)]
                for j in range(4, tk // rows16):
                    parts[j % 4] = parts[j % 4] + c[j * rows16:(j + 1) * rows16]
                acc = acc + ((parts[0] + parts[1]) + (parts[2] + parts[3])).astype(F32)
            return acc
        acc = lax.fori_loop(0, n_trips, body, jnp.zeros((rows16, tq), F32))
        return jnp.sum(acc, axis=0, keepdims=True)

    def search(n_bits, base, to_image, limit):
        def body(i, t):
            cand = t + (jnp.int32(1) << (n_bits - 1 - i))
            img = to_image(cand)
            c = base + count_part(lambda a: a >= img)
            return jnp.where((c >= top_k) & (cand <= limit), cand, t)
        return lax.fori_loop(0, n_bits, body, jnp.zeros((1, tq), jnp.int32))

    def refine(eq_img, shift):
        def body(i, carry):
            for kt in trip(i):
                nxt = ((key_scr[kt] >> shift) & 0xFF).astype(F32).astype(BF16)
                part_scr[kt] = jnp.where(part_scr[kt] == eq_img, nxt, -one_b)
            return carry
        lax.fori_loop(0, n_trips, body, 0)

    def top_bits(c):
        neg = 0x8000 | (BF16_INF_BITS - c)
        pos = BF16_MIN_NORMAL_BITS + (c - N_BF16_NEG - 1)
        return jnp.where(c < N_BF16_NEG, neg, jnp.where(c == N_BF16_NEG, 0, pos))

    def top_image(c):
        return pltpu.bitcast(top_bits(c) << 16, F32).astype(BF16)

    small_image = lambda c: c.astype(F32).astype(BF16)

    t1 = search(16, 0.0, top_image, 2 * N_BF16_NEG)
    img1 = top_image(t1)
    above = count_part(lambda a: a > img1)
    refine(img1, 8)
    t2 = search(8, above, small_image, 255)
    img2 = small_image(t2)
    above = above + count_part(lambda a: a > img2)
    refine(img2, 0)
    t3 = search(8, above, small_image, 255)
    img3 = small_image(t3)
    n_ge = above + count_part(lambda a: a >= img3)
    hi = top_bits(t1)
    hi = hi - ((hi >> 15) << 16)
    hi = hi ^ ((hi >> 15) & 0x7FFF)
    thr = (hi << 16) + (t2 << 8) + t3
    few = t1 == 0
    thr = jnp.where(few, INT_MIN + 1, jnp.maximum(thr, INT_MIN + 1))

    excess = jnp.where(few, 0.0, n_ge - top_k)
    has_tie = jnp.max(excess) > 0

    @pl.when(has_tie)
    def _():
        upper = (lax.broadcasted_iota(jnp.int32, (tk, tk), 1) >= lax.broadcasted_iota(jnp.int32, (tk, tk), 0))
        upper = jnp.where(upper, 1.0, 0.0).astype(BF16)

        def drop_body(i, later):
            kt = n_kt - 1 - i
            key = key_scr[kt]
            eq = key == thr
            ties_from_here = jnp.dot(upper, jnp.where(eq, 1.0, 0.0).astype(BF16), preferred_element_type=F32)
            key_scr[kt] = jnp.where(eq & (ties_from_here + later <= excess), INT_MIN, key)
            return later + ties_from_here[0:1, :]

        lax.fori_loop(0, n_kt, drop_body, jnp.zeros((1, tq), F32))
    return thr


def _sparse_attn_kernel(qt_ref, iqt_ref, iwt_ref, k_ref, vt_ref, ik_ref, bias_ref, o_ref,
                        key_scr, part_scr, m_scr, acc_scr, s_scr, *, tq, tk, nkt, pos0, n_keys, top_k):
    qi = pl.program_id(1)
    q0 = pos0 + qi * tq
    n_kt = jnp.minimum(nkt, ((q0 + tq - 1) // CHUNK * CHUNK + CHUNK + tk - 1) // tk)
    last = n_kt - 1

    fold = _fold

    iqt = iqt_ref[...]
    iqt_h = [iqt[h * IDX_DIM:(h + 1) * IDX_DIM, :] for h in range(N_IDX_HEADS)]
    iwt = iwt_ref[...]
    iw_h = [iwt[h:h + 1, :] for h in range(N_IDX_HEADS)]

    def score_tile(kt, masked):
        ik = ik_ref[kt]
        s = None
        for h in range(N_IDX_HEADS):
            t = jnp.maximum(jnp.dot(ik, iqt_h[h], preferred_element_type=F32), 0.0) * iw_h[h]
            s = t if s is None else s + t
        s = s + 0.0
        bits = pltpu.bitcast(s, jnp.int32)
        key = bits ^ ((bits >> 31) & 0x7FFFFFFF)
        top = pltpu.bitcast(bits & -65536, F32)
        if masked:
            kp = kt * tk + lax.broadcasted_iota(jnp.int32, (tk, tq), 0)
            qp = q0 + lax.broadcasted_iota(jnp.int32, (tk, tq), 1)
            adm = ((kp >> CHUNK_SHIFT) <= (qp >> CHUNK_SHIFT)) & (kp < n_keys)
            key = jnp.where(adm, key, INT_MIN)
            top = jnp.where(adm, top, -jnp.inf)
        key_scr[kt] = key
        part_scr[kt] = top.astype(BF16)

    def score_body(i, carry):
        for j in range(SCORE_TRIP):
            score_tile(SCORE_TRIP * i + j, False)
        return carry

    lax.fori_loop(0, last // SCORE_TRIP, score_body, 0)
    left, rest = last // SCORE_TRIP * SCORE_TRIP, last % SCORE_TRIP

    @pl.when(rest >= 2)
    def _():
        score_tile(left, False)
        score_tile(left + 1, False)

    @pl.when(rest % 2 == 1)
    def _():
        score_tile(last - 1, False)

    score_tile(last, True)

    thr = _select_top_k(key_scr, part_scr, n_kt, tk=tk, tq=tq, top_k=top_k)

    qt = qt_ref[...]
    slab_row = lax.broadcasted_iota(jnp.int32, (LANES, tq), 0)
    qz = []
    for h in range(N_HEADS):
        slab = qt[(h // 2) * LANES:(h // 2 + 1) * LANES, :]
        mine = (slab_row < HEAD_DIM) if h % 2 == 0 else (slab_row >= HEAD_DIM)
        qz.append(jnp.where(mine, slab, jnp.zeros_like(slab)))
    m_scr[...] = jnp.full(m_scr.shape, NEG, F32)
    acc_scr[...] = jnp.zeros(acc_scr.shape, F32)

    def park(kt, slot, heads, madd):
        for h in heads:
            pair = h // 2
            s = jnp.dot(k_ref[kt, :, pair * LANES:(pair + 1) * LANES], qz[h], preferred_element_type=F32)
            s_scr[slot, h] = s + madd

    def absorb(kt, slot, bias_idx, heads):
        for h in heads:
            s = s_scr[slot, h]
            if bias_idx is not None:
                s = s + bias_ref[bias_idx, h]
            m_new = jnp.maximum(m_scr[h], jnp.max(fold(s, jnp.max), axis=0, keepdims=True))
            alpha = jnp.exp2(m_scr[h] - m_new)
            p = jnp.exp2(s - m_new)
            pv = jnp.dot(vt_ref[kt, h * V_ROWS:(h + 1) * V_ROWS, :], p.astype(BF16), preferred_element_type=F32)
            acc_scr[h] = alpha * acc_scr[h] + pv
            m_scr[h] = m_new

    all_heads = range(N_HEADS)
    head_groups = (range(0, N_HEADS // 2), range(N_HEADS // 2, N_HEADS))

    def step(kt_absorb, slot, bias_idx, kt_park):
        if kt_park is None:
            absorb(kt_absorb, slot, bias_idx, all_heads)
            return
        madd = jnp.where(key_scr[kt_park] >= thr, 0.0, NEG)
        for hs in head_groups:
            park(kt_park, 1 - slot, hs, madd)
            absorb(kt_absorb, slot, bias_idx, hs)

    park(0, 0, all_heads, jnp.where(key_scr[0] >= thr, 0.0, NEG))
    n_trips = (n_kt - 2) // 2

    def far_body(i, carry):
        kt = 2 * i
        step(kt, 0, None, kt + 1)
        step(kt + 1, 1, None, kt + 2)
        return carry

    lax.fori_loop(0, n_trips, far_body, 0)
    done = 2 * jnp.maximum(n_trips, 0)

    @pl.when(n_kt == 1)
    def _():
        step(0, 0, 1, None)

    @pl.when((n_kt >= 2) & (n_kt % 2 == 0))
    def _():
        step(done, 0, 0, done + 1)
        step(done + 1, 1, 1, None)

    @pl.when((n_kt >= 3) & (n_kt % 2 == 1))
    def _():
        step(done, 0, None, done + 1)
        step(done + 1, 1, 0, done + 2)
        step(done + 2, 0, 1, None)

    for h in range(N_HEADS):
        acc = acc_scr[h]
        o_ref[h * HEAD_DIM:(h + 1) * HEAD_DIM, :] = (acc[:HEAD_DIM] / acc[HEAD_DIM:HEAD_DIM + 1]).astype(o_ref.dtype)


def _rel_bucket(rel):
    nb = NUM_BUCKETS // 2
    max_exact = nb // 2
    ret = jnp.where(rel > 0, nb, 0)
    n = jnp.abs(rel)
    nf = jnp.maximum(n, 1).astype(jnp.float32)
    large = max_exact + (jnp.log(nf / max_exact) / math.log(MAX_DISTANCE / max_exact) * (nb - max_exact)).astype(jnp.int32)
    large = jnp.minimum(large, nb - 1)
    return ret + jnp.where(n < max_exact, n, large)


def _bias_tiles(rel_bias, tq, tk):
    j = jnp.arange(tk, dtype=jnp.int32)[:, None]
    i = jnp.arange(tq, dtype=jnp.int32)[None, :]
    return jnp.stack([_shifted_bias(rel_bias, j - i + d) for d in (-tk, 0)])


def _shifted_bias(rel_bias, rel):
    rb = (rel_bias.astype(F32) - rel_bias[_rel_bucket(jnp.int32(-MAX_DISTANCE))].astype(F32)) * LOG2E
    onehot = jax.nn.one_hot(_rel_bucket(rel), NUM_BUCKETS, dtype=F32)
    return jnp.moveaxis(jnp.einsum("...b,bh->...h", onehot, rb, precision=lax.Precision.HIGHEST), -1, 0)


def _check_tiling(t, tq, tk, nkt, pos0, n_keys):
    for qi in range(t // tq):
        q0 = pos0 + qi * tq
        n_kt = min(nkt, ((q0 + tq - 1) // CHUNK * CHUNK + CHUNK + tk - 1) // tk)
        adm_end = min(q0 // CHUNK * CHUNK + CHUNK, n_keys)
        assert (n_kt - 1) * tk <= adm_end, "only the last visited tile may hold inadmissible keys"
        assert n_kt * tk >= min((q0 + tq - 1) // CHUNK * CHUNK + CHUNK, n_keys), "visited tiles cover every admissible key"
        assert (n_kt - 1) * tk == q0, "last tile starts with the query tile"
        assert tk >= MAX_DISTANCE, "tiles before the last two are at least MAX_DISTANCE behind"


def _sparse_attn(qt, iqt, iwt, k_b, vt_t, ik_b, rel_bias, pos0):
    b, _, t = qt.shape
    n_keys = k_b.shape[1]
    top_k = min(TOPK_MAX, n_keys // 4)
    tk = KEY_TILE
    tq = min(tk, t)
    nkt = n_keys // tk
    assert n_keys % tk == 0 and vt_t.shape == (b, nkt, N_HEADS * V_ROWS, tk)
    _check_tiling(t, tq, tk, nkt, pos0, n_keys)
    k_t = k_b.reshape(b, nkt, tk, ATTN_W)
    ik_t = ik_b.reshape(b, nkt, tk, IDX_DIM)
    bias = _bias_tiles(rel_bias, tq, tk)

    qspec = lambda n: pl.BlockSpec((None, n, tq), lambda bi, qi: (bi, 0, qi))
    kspec = lambda r, c: pl.BlockSpec((None, nkt, r, c), lambda bi, qi: (bi, 0, 0, 0))
    kern = functools.partial(_sparse_attn_kernel, tq=tq, tk=tk, nkt=nkt, pos0=pos0, n_keys=n_keys, top_k=top_k)
    return pl.pallas_call(
        kern,
        grid=(b, t // tq),
        in_specs=[qspec(ATTN_W), qspec(ATTN_W), qspec(N_IDX_HEADS),
                  kspec(tk, ATTN_W), kspec(N_HEADS * V_ROWS, tk), kspec(tk, IDX_DIM), _const_spec(bias.shape)],
        out_specs=qspec(ATTN_W),
        out_shape=jax.ShapeDtypeStruct((b, ATTN_W, t), BF16),
        scratch_shapes=[pltpu.VMEM((nkt + SEARCH_TRIP - 1, tk, tq), jnp.int32),
                        pltpu.VMEM((nkt + SEARCH_TRIP - 1, tk, tq), BF16),
                        pltpu.VMEM((N_HEADS, 1, tq), F32),
                        pltpu.VMEM((N_HEADS, V_ROWS, tq), F32),
                        pltpu.VMEM((2, N_HEADS, tk, tq), F32)],
        compiler_params=pltpu.CompilerParams(dimension_semantics=("arbitrary", "arbitrary"),
                                             vmem_limit_bytes=VMEM_LIMIT),
        name="sparse_attn",
    )(qt, iqt, iwt, k_t, vt_t, ik_t, bias)


DEC_ROWS = N_HEADS * 16
DEC_KEY_TILE = 4096
NT_DIMS = (((1,), (1,)), ((), ()))


def _dec_select_kernel(iq_ref, iwc_ref, ikt_ref, iktn_ref, madd_ref, key_scr, *, g, tq, n_keys, n_pad, top_k):
    n_slabs = n_pad // LANES
    col = lax.broadcasted_iota(jnp.int32, (tq, n_pad), 1)
    for s in range(g):
        iq_s = iq_ref[s * tq:(s + 1) * tq, :]
        lhs = jnp.concatenate([iq_s[:, h * IDX_DIM:(h + 1) * IDX_DIM] for h in range(N_IDX_HEADS)], axis=0)
        logit = jnp.concatenate([jnp.dot(lhs, ikt_ref[s].astype(BF16), preferred_element_type=F32),
                                 jnp.dot(lhs, iktn_ref[s], preferred_element_type=F32)], axis=1)
        x = jnp.maximum(logit, 0.0) * iwc_ref[s]
        score = x[0:tq]
        for h in range(1, N_IDX_HEADS):
            score = score + x[h * tq:(h + 1) * tq]
        score = score + 0.0
        bits = pltpu.bitcast(score, jnp.int32)
        key = bits ^ ((bits >> 31) & 0x7FFFFFFF)
        key_scr[s * tq:(s + 1) * tq, :] = jnp.where(col < n_keys, key, INT_MIN)

    rows = g * tq
    lane = lax.broadcasted_iota(jnp.int32, (rows, LANES), 1)

    def count(pred):
        acc = jnp.zeros((rows, LANES), F32)
        for j in range(n_slabs):
            acc = acc + jnp.where(pred(key_scr[:, j * LANES:(j + 1) * LANES], j), 1.0, 0.0)
        return jnp.sum(acc, axis=1, keepdims=True)

    def thr_body(i, t):
        cand = t + (jnp.int32(1) << (31 - i))
        c = count(lambda key, j: key >= cand)
        return jnp.where(c >= top_k, cand, t)

    thr = lax.fori_loop(0, 32, thr_body, jnp.full((rows, 1), INT_MIN, jnp.int32))
    thr = jnp.maximum(thr, INT_MIN + 1)

    n_ge = count(lambda key, j: key >= thr)
    has_tie = jnp.max(n_ge) > top_k

    @pl.when(has_tie)
    def _():
        need_m1 = (top_k - 1) - count(lambda key, j: key > thr)

        def idx_body(i, cut):
            cand = cut + (jnp.int32(1) << (15 - i))
            c = count(lambda key, j: (key == thr) & (j * LANES + lane < cand))
            return jnp.where(c <= need_m1, cand, cut)

        cut = lax.fori_loop(0, 16, idx_body, jnp.zeros((rows, 1), jnp.int32))
        tie_row = n_ge > top_k
        for j in range(n_slabs):
            key = key_scr[:, j * LANES:(j + 1) * LANES]
            drop = tie_row & (key == thr) & (j * LANES + lane > cut)
            key_scr[:, j * LANES:(j + 1) * LANES] = jnp.where(drop, INT_MIN, key)

    madd_ref[...] = jnp.where(key_scr[...] >= thr, 0.0, NEG)


def _dec_attn_kernel(q_ref, maddc_ref, maddn_ref, k_ref, v_ref, kn_ref, vn_ref, biasc_ref, biasn_ref, o_ref,
                     m_scr, l_scr, acc_scr, *, tq):
    kt = pl.program_id(1)
    q = q_ref[...]
    q_h = [q[:, h * HEAD_DIM:(h + 1) * HEAD_DIM] for h in range(N_HEADS)]

    @pl.when(kt == 0)
    def _():
        m_scr[...] = jnp.full(m_scr.shape, NEG, F32)
        l_scr[...] = jnp.zeros(l_scr.shape, F32)
        acc_scr[...] = jnp.zeros(acc_scr.shape, F32)

    def step(ktr, vtr, madd, bias):
        s = jnp.concatenate([jnp.dot(q_h[h], ktr[h].astype(BF16), preferred_element_type=F32)
                             for h in range(N_HEADS)], axis=0)
        s = s + jnp.concatenate([madd] * N_HEADS, axis=0) + bias
        m_prev = m_scr[...]
        m_new = jnp.maximum(m_prev, jnp.max(s, axis=1, keepdims=True))
        alpha = jnp.exp2(m_prev - m_new)
        p = jnp.exp2(s - m_new)
        l_scr[...] = alpha * l_scr[...] + jnp.sum(p, axis=1, keepdims=True)
        pb = p.astype(BF16)
        pv = jnp.concatenate([lax.dot_general(pb[h * tq:(h + 1) * tq, :], vtr[h].astype(BF16), NT_DIMS,
                                              preferred_element_type=F32) for h in range(N_HEADS)], axis=0)
        acc_scr[...] = alpha * acc_scr[...] + pv
        m_scr[...] = m_new

    step(k_ref, v_ref, maddc_ref[...], biasc_ref[...])

    @pl.when(kt == pl.num_programs(1) - 1)
    def _():
        step(kn_ref, vn_ref, maddn_ref[...], biasn_ref[...])
        o = acc_scr[...] / l_scr[...]
        for h in range(N_HEADS):
            o_ref[:, h * HEAD_DIM:(h + 1) * HEAD_DIM] = o[h * tq:(h + 1) * tq, :].astype(o_ref.dtype)


def _dec_attn(q, iq, iw, past_k, past_v, past_ik, k_new, v_new, ik_new, rel_bias, pos0):
    b, t, _ = q.shape
    p_len = past_k.shape[1]
    n_keys = p_len + t
    top_k = min(TOPK_MAX, n_keys // 4)
    tkk = min(DEC_KEY_TILE, p_len)
    n_pad = p_len + LANES
    g = min(LANES // t, b)
    assert N_HEADS * t == DEC_ROWS and b % g == 0 and p_len % tkk == 0 and t <= LANES
    assert (pos0 + t - 1) // CHUNK == pos0 // CHUNK and (n_keys - 1) // CHUNK <= pos0 // CHUNK, "every key admissible"
    assert tkk >= MAX_DISTANCE + LANES >= MAX_DISTANCE + t, "only the last cache tile is within MAX_DISTANCE of a query"

    ikt_new = jnp.pad(ik_new.swapaxes(1, 2), ((0, 0), (0, 0), (0, LANES - t)))
    iw_col = iw.swapaxes(1, 2).reshape(b, DEC_ROWS, 1)
    madd = pl.pallas_call(
        functools.partial(_dec_select_kernel, g=g, tq=t, n_keys=n_keys, n_pad=n_pad, top_k=top_k),
        grid=(b // g,),
        in_specs=[pl.BlockSpec((g * t, ATTN_W), lambda i: (i, 0)),
                  pl.BlockSpec((g, DEC_ROWS, 1), lambda i: (i, 0, 0)),
                  pl.BlockSpec((g, IDX_DIM, p_len), lambda i: (i, 0, 0)),
                  pl.BlockSpec((g, IDX_DIM, LANES), lambda i: (i, 0, 0))],
        out_specs=pl.BlockSpec((g * t, n_pad), lambda i: (i, 0)),
        out_shape=jax.ShapeDtypeStruct((b * t, n_pad), F32),
        scratch_shapes=[pltpu.VMEM((g * t, n_pad), jnp.int32)],
        compiler_params=pltpu.CompilerParams(dimension_semantics=("arbitrary",), vmem_limit_bytes=VMEM_LIMIT),
        name="dec_select",
    )(iq.reshape(b * t, ATTN_W), iw_col, past_ik.swapaxes(1, 2), ikt_new)

    n_steps = p_len // tkk
    qpos = jnp.tile(jnp.arange(t, dtype=jnp.int32), N_HEADS)[:, None]
    hsel = jnp.repeat(jnp.arange(N_HEADS), t)
    rows = jnp.arange(DEC_ROWS)
    near = MAX_DISTANCE + LANES
    rel_c = jnp.arange(near, dtype=jnp.int32)[None, :] - near - qpos
    rel_n = jnp.arange(LANES, dtype=jnp.int32)[None, :] - qpos
    bias_last = jnp.pad(_shifted_bias(rel_bias, rel_c)[hsel, rows], ((0, 0), (tkk - near, 0)))
    bias_c = jnp.concatenate([jnp.zeros((n_steps - 1, DEC_ROWS, tkk), F32), bias_last[None]], axis=0)
    bias_n = _shifted_bias(rel_bias, rel_n)[hsel, rows]
    heads_t = lambda a: a.transpose(0, 2, 3, 1)
    pad_new = lambda a: jnp.pad(heads_t(a.reshape(b, t, N_HEADS, HEAD_DIM)), ((0, 0), (0, 0), (0, 0), (0, LANES - t)))
    kv_tile = pl.BlockSpec((None, N_HEADS, HEAD_DIM, tkk), lambda bi, kt: (bi, 0, 0, kt))
    kv_new = pl.BlockSpec((None, N_HEADS, HEAD_DIM, LANES), lambda bi, kt: (bi, 0, 0, 0))
    return pl.pallas_call(
        functools.partial(_dec_attn_kernel, tq=t),
        grid=(b, n_steps),
        in_specs=[pl.BlockSpec((None, t, ATTN_W), lambda bi, kt: (bi, 0, 0)),
                  pl.BlockSpec((t, tkk), lambda bi, kt: (bi, kt)),
                  pl.BlockSpec((t, LANES), lambda bi, kt: (bi, p_len // LANES)),
                  kv_tile, kv_tile, kv_new, kv_new,
                  pl.BlockSpec((None, DEC_ROWS, tkk), lambda bi, kt: (kt, 0, 0)),
                  _const_spec(bias_n.shape)],
        out_specs=pl.BlockSpec((None, t, ATTN_W), lambda bi, kt: (bi, 0, 0)),
        out_shape=jax.ShapeDtypeStruct((b, t, ATTN_W), BF16),
        scratch_shapes=[pltpu.VMEM((DEC_ROWS, 1), F32), pltpu.VMEM((DEC_ROWS, 1), F32),
                        pltpu.VMEM((DEC_ROWS, HEAD_DIM), F32)],
        compiler_params=pltpu.CompilerParams(dimension_semantics=("arbitrary", "arbitrary"),
                                             vmem_limit_bytes=VMEM_LIMIT),
        name="dec_attn",
    )(q, madd, madd, heads_t(past_k), heads_t(past_v), pad_new(k_new), pad_new(v_new), bias_c, bias_n)


def _conv_kernel(*refs, tm, has_prev):
    if has_prev:
        u_ref, prev_ref, past_ref, w_ref, b_ref, lg_ref, lb_ref, o_ref, ext_scr = refs
        halo = jnp.where(pl.program_id(1) == 0, past_ref[...], prev_ref[...])
    else:
        u_ref, past_ref, w_ref, b_ref, lg_ref, lb_ref, o_ref, ext_scr = refs
        halo = past_ref[...]
    _conv_act(halo, u_ref[...], w_ref, b_ref, lg_ref, lb_ref, o_ref, ext_scr)


def _conv_act(halo, u, w_ref, b_ref, lg_ref, lb_ref, o_ref, ext_scr):
    tm = u.shape[0]
    ext_scr[0, 0:HALO, :] = halo
    ext_scr[0, HALO:HALO + tm, :] = u
    n_rows = tm + HALO - SUBLANES
    for r in range(1, SUBLANES):
        ext_scr[r, 0:n_rows, :] = ext_scr[0, r:r + n_rows, :]
    rb = min(tm, CONV_ROWS)
    first = HALO - (CONV_K - 1)
    for r0 in range(0, tm, rb):
        acc = None
        for j in range(CONV_K):
            a, r = divmod(first + j, SUBLANES)
            t = ext_scr[r, r0 + a * SUBLANES:r0 + a * SUBLANES + rb, :] * w_ref[j:j + 1, :]
            acc = t if acc is None else acc + t
        c = acc + b_ref[...]
        mu = jnp.mean(c, axis=-1, keepdims=True)
        cc = c - mu
        var = jnp.mean(cc * cc, axis=-1, keepdims=True)
        y = cc * lax.rsqrt(var + EPS) * lg_ref[...] + lb_ref[...]
        o_ref[r0:r0 + rb, :] = (y * jax.nn.sigmoid(y)).astype(o_ref.dtype)


def _conv_mod(u, past, w):
    b, t, c = u.shape
    tm = min(ROW_TILE, t)
    assert t % tm == 0 and tm % 8 == 0
    has_prev = t > tm
    tile = pl.BlockSpec((None, tm, c), lambda bi, i: (bi, i, 0))
    halo = pl.BlockSpec((None, HALO, c), lambda bi, i: (bi, 0, 0))
    in_specs = [tile]
    args = [u]
    if has_prev:
        r = tm // HALO
        in_specs.append(pl.BlockSpec((None, HALO, c), lambda bi, i: (bi, jnp.maximum(i * r - 1, 0), 0)))
        args.append(u)
    consts = [w["conv_w"], w["conv_b"], w["ln_g"], w["ln_b"]]
    in_specs += [halo] + [_const_spec(x.shape) for x in consts]
    args += [past] + consts
    return pl.pallas_call(
        functools.partial(_conv_kernel, tm=tm, has_prev=has_prev),
        grid=(b, t // tm),
        in_specs=in_specs,
        out_specs=tile,
        out_shape=jax.ShapeDtypeStruct((b, t, c), BF16),
        scratch_shapes=[pltpu.VMEM((SUBLANES, HALO + tm, c), F32)],
        compiler_params=pltpu.CompilerParams(dimension_semantics=("arbitrary", "arbitrary"),
                                             vmem_limit_bytes=VMEM_LIMIT),
        name="conv_mod",
    )(*args)


def _mix_ffn_conv_kernel(x_ref, attn_ref, u0_ref, unext_ref, utail_ref, past0_ref, pastn_ref, sga_ref, sgc_ref,
                         wao_ref, wco_ref, wo_ref, g_ref, wg_ref, wu_ref, wd_ref, cw_ref, cb_ref, lg_ref, lb_ref,
                         y_ref, cact_scr, ext_scr, *, n_chunks, tiles_per_seq):
    i = pl.program_id(0)

    @pl.when(i == 0)
    def _():
        _conv_act(past0_ref[...], u0_ref[...], cw_ref, cb_ref, lg_ref, lb_ref, cact_scr.at[0], ext_scr)

    _mix_ffn_kernel(x_ref, attn_ref, cact_scr.at[i % 2], sga_ref, sgc_ref, wao_ref, wco_ref, wo_ref, g_ref,
                    wg_ref, wu_ref, wd_ref, y_ref, n_chunks=n_chunks)
    halo = jnp.where((i + 1) % tiles_per_seq == 0, pastn_ref[...], utail_ref[...])
    _conv_act(halo, unext_ref[...], cw_ref, cb_ref, lg_ref, lb_ref, cact_scr.at[(i + 1) % 2], ext_scr)


def _mix_ffn_kernel(x_ref, attn_ref, cact_ref, sga_ref, sgc_ref, wao_ref, wco_ref, wo_ref, g_ref,
                    wg_ref, wu_ref, wd_ref, y_ref, *, n_chunks):
    attn_out = jnp.dot(attn_ref[...], wao_ref[...], preferred_element_type=F32)
    conv_out = jnp.dot(cact_ref[...], wco_ref[...], preferred_element_type=F32)
    merged = sga_ref[...].astype(F32) * attn_out + sgc_ref[...].astype(F32) * conv_out
    x1 = x_ref[...] + jnp.dot(merged.astype(BF16), wo_ref[...], preferred_element_type=F32)
    ms = jnp.mean(x1 * x1, axis=-1, keepdims=True)
    h2 = (x1 * lax.rsqrt(ms + EPS) * g_ref[...]).astype(BF16)
    hc = wg_ref.shape[1] // n_chunks
    y = x1
    for ci in range(n_chunks):
        sl = slice(ci * hc, (ci + 1) * hc)
        gate = jnp.dot(h2, wg_ref[:, sl], preferred_element_type=F32)
        up = jnp.dot(h2, wu_ref[:, sl], preferred_element_type=F32)
        act = (gate * jax.nn.sigmoid(gate) * up).astype(BF16)
        y = y + jnp.dot(act, wd_ref[sl, :], preferred_element_type=F32)
    y_ref[...] = y


def _mix_ffn_conv(x, attn, u, past, sga, sgc, w, seq_len):
    m, d = x.shape
    c = u.shape[1]
    tm = min(ROW_TILE, seq_len)
    assert m % tm == 0 and seq_len % tm == 0 and tm % HALO == 0
    nt, n = seq_len // tm, m // tm
    hidden = w["wg"].shape[1]
    n_chunks = 1
    row = lambda k: pl.BlockSpec((tm, k), lambda i: (i, 0))
    nxt = lambda i: jnp.minimum(i + 1, n - 1)
    consts = [w[k] for k in ("wao", "wco", "wo", "g_ffn", "wg", "wu", "wd", "conv_w", "conv_b", "ln_g", "ln_b")]
    return pl.pallas_call(
        functools.partial(_mix_ffn_conv_kernel, n_chunks=n_chunks, tiles_per_seq=nt),
        grid=(n,),
        in_specs=[row(d), row(attn.shape[1]),
                  pl.BlockSpec((tm, c), lambda i: (0, 0)),
                  pl.BlockSpec((tm, c), lambda i: (nxt(i), 0)),
                  pl.BlockSpec((HALO, c), lambda i: ((i + 1) * (tm // HALO) - 1, 0)),
                  pl.BlockSpec((None, HALO, c), lambda i: (0, 0, 0)),
                  pl.BlockSpec((None, HALO, c), lambda i: (nxt(i) // nt, 0, 0)),
                  row(d), row(d)] + [_const_spec(k.shape) for k in consts],
        out_specs=row(d),
        out_shape=jax.ShapeDtypeStruct((m, d), F32),
        scratch_shapes=[pltpu.VMEM((2, tm, c), BF16), pltpu.VMEM((SUBLANES, HALO + tm, c), F32)],
        compiler_params=pltpu.CompilerParams(dimension_semantics=("arbitrary",), vmem_limit_bytes=VMEM_LIMIT),
        name="mix_ffn_conv",
    )(x, attn, u, u, u, past, past, sga, sgc, *consts)


def _mix_ffn(x, attn, cact, sga, sgc, w):
    m, d = x.shape
    tm = min(ROW_TILE, m)
    assert m % tm == 0
    hidden = w["wg"].shape[1]
    n_chunks = 2 if hidden % (2 * LANES) == 0 else 1
    row = lambda n: pl.BlockSpec((tm, n), lambda i: (i, 0))
    consts = [w[n] for n in ("wao", "wco", "wo", "g_ffn", "wg", "wu", "wd")]
    return pl.pallas_call(
        functools.partial(_mix_ffn_kernel, n_chunks=n_chunks),
        grid=(m // tm,),
        in_specs=[row(d), row(attn.shape[1]), row(cact.shape[1]), row(d), row(d)]
                 + [_const_spec(c.shape) for c in consts],
        out_specs=row(d),
        out_shape=jax.ShapeDtypeStruct((m, d), F32),
        compiler_params=pltpu.CompilerParams(dimension_semantics=("arbitrary",), vmem_limit_bytes=VMEM_LIMIT),
        name="mix_ffn",
    )(x, attn, cact, sga, sgc, *consts)


def _prep_weights(norm_mix_g, w_in, q_norm_g, k_norm_g, idx_k_norm_g, conv_dw_w, conv_dw_b, conv_ln_g, conv_ln_b,
                  w_conv_out, w_attn_out, w_out, norm_ffn_g, w_ffn_gate, w_ffn_up, w_ffn_down):
    d = w_in.shape[0]
    conv_ch = conv_dw_w.shape[-1]
    sizes = (ATTN_W, ATTN_W, ATTN_W, N_IDX_HEADS * IDX_DIM, IDX_DIM, N_IDX_HEADS, 2 * conv_ch, d, d)
    offs = np.concatenate([[0], np.cumsum(sizes)])
    assert offs[-1] == w_in.shape[1]
    col = lambda i: w_in[:, offs[i]:offs[i + 1]]
    wsm = jnp.concatenate([col(4), col(5), jnp.zeros((d, LANES - IDX_DIM - N_IDX_HEADS), w_in.dtype)], axis=1)
    head = np.arange(ATTN_W) // HEAD_DIM
    seg = (head[:, None] == np.arange(LANES)[None, :]).astype(np.float32)
    return dict(
        g_mix=norm_mix_g.reshape(1, d).astype(F32),
        wq=col(0).astype(BF16), wk=col(1).astype(BF16), wv=col(2).astype(BF16), wiq=col(3).astype(BF16),
        wsm=wsm.astype(BF16), wglu=col(6).astype(BF16), wga=col(7).astype(BF16), wgc=col(8).astype(BF16),
        qg=jnp.tile(q_norm_g.astype(F32), N_HEADS).reshape(1, ATTN_W),
        kg=jnp.tile(k_norm_g.astype(F32), N_HEADS).reshape(1, ATTN_W),
        ikg=jnp.concatenate([idx_k_norm_g.astype(F32), jnp.zeros((LANES - IDX_DIM,), F32)]).reshape(1, LANES),
        seg=jnp.asarray(seg, BF16), segt=jnp.asarray(seg.T, BF16),
        conv_w=conv_dw_w.reshape(CONV_K, conv_ch).astype(F32), conv_b=conv_dw_b.reshape(1, conv_ch).astype(F32),
        ln_g=conv_ln_g.reshape(1, conv_ch).astype(F32), ln_b=conv_ln_b.reshape(1, conv_ch).astype(F32),
        wao=w_attn_out.astype(BF16), wco=w_conv_out.astype(BF16), wo=w_out.astype(BF16),
        g_ffn=norm_ffn_g.reshape(1, d).astype(F32),
        wg=w_ffn_gate.astype(BF16), wu=w_ffn_up.astype(BF16), wd=w_ffn_down.astype(BF16),
    )


def _layer(x, pos0, past_k, past_v, past_ik, past_conv, rel_bias, w):
    b, t, d = x.shape
    m = b * t
    prompt = past_k is None
    q, k, k_b, v, v_b, iq, sm, sm_b, u, sga, sgc = _in_proj(x, w, transposed=prompt)
    conv_ch = u.shape[1]
    u = u.reshape(b, t, conv_ch)
    k_b = k_b.reshape(b, t, ATTN_W)
    ik_b = sm_b.reshape(b, t, LANES)[:, :, :IDX_DIM]
    if prompt:
        attn = _sparse_attn(q, iq, sm[:, IDX_DIM:IDX_DIM + N_IDX_HEADS, :], k_b, v_b, ik_b, rel_bias,
                            pos0).swapaxes(1, 2)
        heads = lambda a: a.reshape(b, N_HEADS, HEAD_DIM, t).transpose(0, 3, 1, 2)
        new_k, new_v, new_ik = heads(k), heads(v), sm[:, :IDX_DIM, :].swapaxes(1, 2)
    else:
        sm = sm.reshape(b, t, LANES)
        k, v = k.reshape(b, t, ATTN_W), v.reshape(b, t, ATTN_W)
        attn = _dec_attn(q.reshape(b, t, ATTN_W), iq.reshape(b, t, ATTN_W), sm[:, :, IDX_DIM:IDX_DIM + N_IDX_HEADS],
                         past_k, past_v, past_ik, k, v, ik_b, rel_bias, pos0)
        heads = lambda a: a.reshape(b, t, N_HEADS, HEAD_DIM)
        new_k, new_v, new_ik = heads(k), heads(v), sm[:, :, :IDX_DIM]

    past = jnp.pad(past_conv.astype(F32), ((0, 0), (HALO - (CONV_K - 1), 0), (0, 0)))
    if prompt:
        y = _mix_ffn_conv(x.reshape(m, d), attn.reshape(m, ATTN_W), u.reshape(m, conv_ch), past, sga, sgc, w, t)
    else:
        cact = _conv_mod(u, past, w)
        y = _mix_ffn(x.reshape(m, d), attn.reshape(m, ATTN_W), cact.reshape(m, conv_ch), sga, sgc, w)

    new_conv = jnp.concatenate([past_conv, u], axis=1)[:, -(CONV_K - 1):]
    return y.reshape(b, t, d), new_k, new_v, new_ik, new_conv


def kernel(x_prompt, x_sample, cache_k, cache_v, cache_idx_k, state_conv, rel_bias, norm_mix_g, w_in, q_norm_g, k_norm_g, idx_k_norm_g, conv_dw_w, conv_dw_b, conv_ln_g, conv_ln_b, w_conv_out, w_attn_out, w_out, norm_ffn_g, w_ffn_gate, w_ffn_up, w_ffn_down):
    depth = w_in.shape[0]
    bp = x_prompt.shape[0]
    past_len = cache_k.shape[2]
    conv_ch = conv_dw_w.shape[-1]
    yp, ys = x_prompt, x_sample
    outs_p, outs_s = [], []
    for l in range(depth):
        w = _prep_weights(norm_mix_g[l], w_in[l], q_norm_g[l], k_norm_g[l], idx_k_norm_g[l], conv_dw_w[l],
                          conv_dw_b[l], conv_ln_g[l], conv_ln_b[l], w_conv_out[l], w_attn_out[l], w_out[l],
                          norm_ffn_g[l], w_ffn_gate[l], w_ffn_up[l], w_ffn_down[l])
        zero_conv = jnp.zeros((bp, CONV_K - 1, conv_ch), yp.dtype)
        yp, *rest_p = _layer(yp, 0, None, None, None, zero_conv, rel_bias, w)
        outs_p.append(rest_p)
        ys, *rest_s = _layer(ys, past_len, cache_k[l], cache_v[l], cache_idx_k[l], state_conv[l], rel_bias, w)
        outs_s.append(rest_s)
    stack = lambda outs, i: jnp.stack([o[i] for o in outs])
    return (yp, ys, stack(outs_p, 0), stack(outs_p, 1), stack(outs_p, 2), stack(outs_p, 3),
            stack(outs_s, 0), stack(outs_s, 1), stack(outs_s, 2), stack(outs_s, 3))
```
